```python
import math
import jax, jax.numpy as jnp
from jax import lax
import numpy as np

D_MODEL = 1024
BATCH = 8
SEQ = 2048
DEPTH = 1

CHUNK = 64
Q_BLOCK = 128
SSM_WIDTH = D_MODEL // 2
SSM_GROUP = 16
SSM_GROUPS = SSM_WIDTH // SSM_GROUP
SSM_STATE = 64
ATT_HEADS = 4
ATT_SUB_DIM = 64
ATT_V_DIM = 2 * ATT_SUB_DIM
ATT_QK_WIDTH = ATT_HEADS * 2 * ATT_SUB_DIM
ATT_V_WIDTH = ATT_HEADS * ATT_V_DIM
IN_WIDTH = SSM_WIDTH + 2 * ATT_QK_WIDTH + ATT_V_WIDTH + 2 * D_MODEL
N_EXPERTS = 32
TOP_K = 4
D_EXPERT = D_MODEL
SWIGLU_ALPHA = 1.702
SWIGLU_LIMIT = 7.0
MOE_BLOCK = 256
RMS_EPS = 1e-6

kernel_name = "hybrid_s5_diffattn_moe_block"


def rms_norm(x, g):
    xf = x.astype(jnp.float32)
    y = xf * lax.rsqrt(jnp.mean(xf * xf, axis=-1, keepdims=True) + RMS_EPS)
    return (y * g.astype(jnp.float32)).astype(x.dtype)


def _complex_linear_combine(left, right):
    a1r, a1i, b1r, b1i = left
    a2r, a2i, b2r, b2i = right
    ar = a2r * a1r - a2i * a1i
    ai = a2r * a1i + a2i * a1r
    br = a2r * b1r - a2i * b1i + b2r
    bi = a2r * b1i + a2i * b1r + b2i
    return (ar, ai, br, bi)


def s5_branch(u, a_re, a_im, log_dt, b_re, b_im, c_re, c_im, d_skip, w_glu, b_glu):
    f32 = jnp.float32
    bsz, seq, _ = u.shape
    ug = u.astype(f32).reshape(bsz, seq, SSM_GROUPS, SSM_GROUP)
    dt = jnp.exp(log_dt.astype(f32))[:, None]
    lr = a_re.astype(f32)
    li = a_im.astype(f32)
    mag = jnp.exp(lr * dt)
    abar_r = mag * jnp.cos(li * dt)
    abar_i = mag * jnp.sin(li * dt)
    den = lr * lr + li * li
    nr = abar_r - 1.0
    ni = abar_i
    coef_r = (nr * lr + ni * li) / den
    coef_i = (ni * lr - nr * li) / den
    br_ = b_re.astype(f32)
    bi_ = b_im.astype(f32)
    bbar_r = coef_r[..., None] * br_ - coef_i[..., None] * bi_
    bbar_i = coef_r[..., None] * bi_ + coef_i[..., None] * br_
    inp_r = jnp.einsum('blgh,gph->blgp', ug, bbar_r)
    inp_i = jnp.einsum('blgh,gph->blgp', ug, bbar_i)
    a_r = jnp.broadcast_to(abar_r, inp_r.shape)
    a_i = jnp.broadcast_to(abar_i, inp_i.shape)
    _, _, st_r, st_i = lax.associative_scan(
        _complex_linear_combine, (a_r, a_i, inp_r, inp_i), axis=1)
    y = (jnp.einsum('blgp,ghp->blgh', st_r, c_re.astype(f32))
         - jnp.einsum('blgp,ghp->blgh', st_i, c_im.astype(f32))
         + d_skip.astype(f32).reshape(SSM_GROUPS, SSM_GROUP) * ug)
    y = jax.nn.gelu(y.reshape(bsz, seq, SSM_WIDTH))
    y = y * jax.nn.sigmoid(y @ w_glu.astype(f32) + b_glu.astype(f32))
    return y.astype(u.dtype)


def diff_attention(q, k, v, q_g, k_g, lq1, lk1, lq2, lk2, subln_g, lambda_init):
    f32 = jnp.float32
    bsz, seq, _ = q.shape
    q = rms_norm(q.reshape(bsz, seq, ATT_HEADS, 2, ATT_SUB_DIM), q_g)
    k = rms_norm(k.reshape(bsz, seq, ATT_HEADS, 2, ATT_SUB_DIM), k_g)
    v = v.reshape(bsz, seq, ATT_HEADS, ATT_V_DIM)
    lam = (jnp.exp(jnp.sum(lq1.astype(f32) * lk1.astype(f32)))
           - jnp.exp(jnp.sum(lq2.astype(f32) * lk2.astype(f32))) + lambda_init)
    n_blocks = seq // Q_BLOCK
    q_blocks = q.reshape(bsz, n_blocks, Q_BLOCK, ATT_HEADS, 2, ATT_SUB_DIM).transpose(1, 0, 2, 3, 4, 5)
    key_chunk = jnp.arange(seq) // CHUNK
    scale = ATT_SUB_DIM ** -0.5

    def one_block(args):
        qblk, bi = args
        qpos = bi * Q_BLOCK + jnp.arange(Q_BLOCK)
        allowed = key_chunk[None, :] <= (qpos // CHUNK)[:, None]
        s = jnp.einsum('bqhtd,bkhtd->bhtqk', qblk, k).astype(f32) * scale
        p = jax.nn.softmax(jnp.where(allowed, s, -jnp.inf), axis=-1)
        w = p[:, :, 0] - lam * p[:, :, 1]
        return jnp.einsum('bhqk,bkhe->bqhe', w.astype(v.dtype), v)

    o = lax.map(one_block, (q_blocks, jnp.arange(n_blocks)))
    o = o.transpose(1, 0, 2, 3, 4).reshape(bsz, seq, ATT_HEADS, ATT_V_DIM)
    o = rms_norm(o, subln_g) * (1.0 - lambda_init)
    return o.reshape(bsz, seq, ATT_V_WIDTH)


def moe_ffn(hn, w_router, b_router, w_up, b_up, w_down, b_down):
    bsz, seq, dm = hn.shape
    xf = hn.reshape(-1, dm)
    n_tok = xf.shape[0]
    logits = (xf @ w_router + b_router).astype(jnp.float32)
    top_v, top_i = lax.top_k(logits, TOP_K)
    top_w = jax.nn.softmax(top_v, axis=-1)
    m = n_tok * TOP_K
    flat_e = top_i.reshape(-1)
    flat_tok = jnp.repeat(jnp.arange(n_tok, dtype=jnp.int32), TOP_K)
    flat_w = top_w.reshape(-1)
    order = jnp.argsort(flat_e)
    sorted_e = flat_e[order]
    counts = jnp.bincount(flat_e, length=N_EXPERTS)
    padded = ((counts + MOE_BLOCK - 1) // MOE_BLOCK) * MOE_BLOCK
    pad_end = jnp.cumsum(padded)
    pad_start = pad_end - padded
    start = jnp.cumsum(counts) - counts
    dest = pad_start[sorted_e] + jnp.arange(m) - start[sorted_e]
    n_rows = ((m + MOE_BLOCK - 1) // MOE_BLOCK) * MOE_BLOCK + N_EXPERTS * MOE_BLOCK
    n_blocks = n_rows // MOE_BLOCK
    row_tok = jnp.zeros((n_rows,), jnp.int32).at[dest].set(flat_tok[order])
    row_w = jnp.zeros((n_rows,), jnp.float32).at[dest].set(flat_w[order])
    block_e = jnp.minimum(
        jnp.searchsorted(pad_end, jnp.arange(n_blocks) * MOE_BLOCK, side='right'), N_EXPERTS - 1)
    xs = xf[row_tok].reshape(n_blocks, MOE_BLOCK, dm)

    def expert_block(args):
        xb, e = args
        hup = xb @ w_up[e] + b_up[e]
        glu = jnp.minimum(hup[:, 0::2], SWIGLU_LIMIT)
        lin = jnp.clip(hup[:, 1::2], -SWIGLU_LIMIT, SWIGLU_LIMIT)
        act = glu * jax.nn.sigmoid(SWIGLU_ALPHA * glu) * (lin + 1.0)
        return act @ w_down[e] + b_down[e]

    ys = lax.map(expert_block, (xs, block_e)).reshape(n_rows, dm)
    out = jnp.zeros_like(xf).at[row_tok].add(ys * row_w[:, None].astype(ys.dtype))
    return out.reshape(bsz, seq, dm)


def setup_inputs(seed: int = 0) -> dict:
    key = jax.random.key(seed)
    ks = jax.random.split(key, 32)
    f32 = jnp.float32

    def nrm(k, shape, scale):
        return jax.random.normal(k, shape, f32) * scale

    G, P, H, S = SSM_GROUPS, SSM_STATE, SSM_GROUP, SSM_WIDTH
    a_im_init = jnp.pi * jnp.arange(P, dtype=f32)
    return {
        "x": nrm(ks[0], (BATCH, SEQ, D_MODEL), 1.0),
        "norm1_g": 1.0 + nrm(ks[1], (DEPTH, D_MODEL), 0.02),
        "w_in": nrm(ks[2], (DEPTH, D_MODEL, IN_WIDTH), D_MODEL ** -0.5),
        "b_gate": nrm(ks[3], (DEPTH, 2 * D_MODEL), 0.01),
        "ssm_a_re": -0.5 + nrm(ks[4], (DEPTH, G, P), 0.01),
        "ssm_a_im": a_im_init + nrm(ks[5], (DEPTH, G, P), 0.01),
        "ssm_log_dt": jax.random.uniform(ks[6], (DEPTH, G), f32, math.log(1e-3), math.log(1e-1)),
        "ssm_b_re": nrm(ks[7], (DEPTH, G, P, H), (2 * H) ** -0.5),
        "ssm_b_im": nrm(ks[8], (DEPTH, G, P, H), (2 * H) ** -0.5),
        "ssm_c_re": nrm(ks[9], (DEPTH, G, H, P), (2 * P) ** -0.5),
        "ssm_c_im": nrm(ks[10], (DEPTH, G, H, P), (2 * P) ** -0.5),
        "ssm_d": nrm(ks[11], (DEPTH, S), 1.0),
        "ssm_w_glu": nrm(ks[12], (DEPTH, S, S), S ** -0.5),
        "ssm_b_glu": nrm(ks[13], (DEPTH, S), 0.01),
        "q_norm_g": 1.0 + nrm(ks[14], (DEPTH, ATT_SUB_DIM), 0.02),
        "k_norm_g": 1.0 + nrm(ks[15], (DEPTH, ATT_SUB_DIM), 0.02),
        "lambda_q1": nrm(ks[16], (DEPTH, ATT_SUB_DIM), 0.1),
        "lambda_k1": nrm(ks[17], (DEPTH, ATT_SUB_DIM), 0.1),
        "lambda_q2": nrm(ks[18], (DEPTH, ATT_SUB_DIM), 0.1),
        "lambda_k2": nrm(ks[19], (DEPTH, ATT_SUB_DIM), 0.1),
        "subln_g": 1.0 + nrm(ks[20], (DEPTH, ATT_V_DIM), 0.02),
        "w_proj_ssm": nrm(ks[21], (DEPTH, S, D_MODEL), S ** -0.5),
        "w_proj_att": nrm(ks[22], (DEPTH, ATT_V_WIDTH, D_MODEL), ATT_V_WIDTH ** -0.5),
        "w_out": nrm(ks[23], (DEPTH, D_MODEL, D_MODEL), D_MODEL ** -0.5),
        "norm2_g": 1.0 + nrm(ks[24], (DEPTH, D_MODEL), 0.02),
        "w_router": nrm(ks[25], (DEPTH, D_MODEL, N_EXPERTS), D_MODEL ** -0.5),
        "b_router": nrm(ks[26], (DEPTH, N_EXPERTS), 0.01),
        "w_up": nrm(ks[27], (DEPTH, N_EXPERTS, D_MODEL, 2 * D_EXPERT), D_MODEL ** -0.5),
        "b_up": nrm(ks[28], (DEPTH, N_EXPERTS, 2 * D_EXPERT), 0.01),
        "w_down": nrm(ks[29], (DEPTH, N_EXPERTS, D_EXPERT, D_MODEL), D_EXPERT ** -0.5),
        "b_down": nrm(ks[30], (DEPTH, N_EXPERTS, D_MODEL), 0.01),
    }


def reference(x, norm1_g, w_in, b_gate, ssm_a_re, ssm_a_im, ssm_log_dt, ssm_b_re, ssm_b_im,
              ssm_c_re, ssm_c_im, ssm_d, ssm_w_glu, ssm_b_glu, q_norm_g, k_norm_g,
              lambda_q1, lambda_k1, lambda_q2, lambda_k2, subln_g, w_proj_ssm, w_proj_att,
              w_out, norm2_g, w_router, b_router, w_up, b_up, w_down, b_down):
    split_at = [SSM_WIDTH,
                SSM_WIDTH + ATT_QK_WIDTH,
                SSM_WIDTH + 2 * ATT_QK_WIDTH,
                SSM_WIDTH + 2 * ATT_QK_WIDTH + ATT_V_WIDTH,
                SSM_WIDTH + 2 * ATT_QK_WIDTH + ATT_V_WIDTH + D_MODEL]
    h = x
    for l in range(DEPTH):
        lambda_init = 0.8 - 0.6 * math.exp(-0.3 * l)
        xn = rms_norm(h, norm1_g[l])
        z = xn @ w_in[l]
        u, q, k, v, g_ssm, g_att = jnp.split(z, split_at, axis=-1)
        y_ssm = s5_branch(u, ssm_a_re[l], ssm_a_im[l], ssm_log_dt[l], ssm_b_re[l], ssm_b_im[l],
                          ssm_c_re[l], ssm_c_im[l], ssm_d[l], ssm_w_glu[l], ssm_b_glu[l])
        y_att = diff_attention(q, k, v, q_norm_g[l], k_norm_g[l], lambda_q1[l], lambda_k1[l],
                               lambda_q2[l], lambda_k2[l], subln_g[l], lambda_init)
        gate_ssm = jax.nn.sigmoid(g_ssm + b_gate[l, :D_MODEL])
        gate_att = jax.nn.sigmoid(g_att + b_gate[l, D_MODEL:])
        mixed = gate_ssm * (y_ssm @ w_proj_ssm[l]) + gate_att * (y_att @ w_proj_att[l])
        h = h + mixed @ w_out[l]
        hn = rms_norm(h, norm2_g[l])
        h = h + moe_ffn(hn, w_router[l], b_router[l], w_up[l], b_up[l], w_down[l], b_down[l])
    return h
```

```python
import functools
import math

import jax
import jax.numpy as jnp
from jax import lax
from jax.experimental import pallas as pl
from jax.experimental.pallas import tpu as pltpu

F32 = jnp.float32
BF16 = jnp.bfloat16

RMS_EPS = 1e-6
CHUNK = 64
SSM_GROUP = 16
SSM_STATE = 64
ATT_HEADS = 4
ATT_SUB_DIM = 64
ATT_V_DIM = 128
N_EXPERTS = 32
TOP_K = 4
SWIGLU_ALPHA = 1.702
SWIGLU_LIMIT = 7.0
LAMBDA_INIT = 0.8 - 0.6 * math.exp(-0.3 * 0)

V7X_SUBLANES = 8
V7X_LANES = 128
V7X_MXU_DIM = 256

TL_PROJ = 512
S5_STEPS = 64
TQ = 256
TK = 256
MOE_ROWS = 512
TC_COMBINE = 256
NEG = -1e30
VMEM_LIMIT = 56 * 1024 * 1024


def _dot(a, b):
    return jnp.dot(a, b, preferred_element_type=F32)


def _dot_nt(a, b):
    return lax.dot_general(a, b, (((1,), (1,)), ((), ())), preferred_element_type=F32)


def _params(sem, vmem=VMEM_LIMIT):
    return pltpu.CompilerParams(dimension_semantics=sem, vmem_limit_bytes=vmem)


def _inproj_kernel(x_ref, g1_ref, w_ref, bg_ref, qg_ref, kg_ref, ones_ref,
                   u_ref, q_ref, k_ref, vt_ref, gate_ref):
    x = x_ref[...]
    ms = jnp.mean(x * x, axis=-1, keepdims=True)
    xn = (x * lax.rsqrt(ms + RMS_EPS) * g1_ref[...]).astype(BF16)

    u_ref[...] = _dot(xn, w_ref[:, 0:512]).astype(BF16)

    def head_norm(z, g):
        sq = (z * z).astype(BF16)
        ss = jnp.concatenate([_dot(sq[:, 0:256], ones_ref[...]),
                              _dot(sq[:, 256:512], ones_ref[...])], axis=1)
        return z * lax.rsqrt(ss * (1.0 / ATT_SUB_DIM) + RMS_EPS) * g

    q_ref[...] = head_norm(_dot(xn, w_ref[:, 512:1024]), qg_ref[...]).astype(BF16)
    k_ref[...] = head_norm(_dot(xn, w_ref[:, 1024:1536]), kg_ref[...]).astype(BF16)
    vt_ref[...] = _dot(xn, w_ref[:, 1536:2048]).T.astype(BF16)
    for c in range(4):
        lo = 2048 + 512 * c
        z = _dot(xn, w_ref[:, lo:lo + 512]) + bg_ref[:, 512 * c:512 * (c + 1)]
        gate_ref[:, 512 * c:512 * (c + 1)] = jax.nn.sigmoid(z).astype(BF16)


def _inproj(x, g1, w_in_bf, b_gate, qg, kg, ones_bd):
    B, L, D = x.shape
    nl = L // TL_PROJ
    full = lambda shape: pl.BlockSpec(shape, lambda b, l: (0,) * len(shape))
    return pl.pallas_call(
        _inproj_kernel,
        grid=(B, nl),
        in_specs=[
            pl.BlockSpec((None, TL_PROJ, D), lambda b, l: (b, l, 0)),
            full((1, D)), full(w_in_bf.shape), full((1, 2 * D)),
            full((1, 512)), full((1, 512)), full((256, 256)),
        ],
        out_specs=[
            pl.BlockSpec((TL_PROJ, 512), lambda b, l: (l, b)),
            pl.BlockSpec((None, TL_PROJ, 512), lambda b, l: (b, l, 0)),
            pl.BlockSpec((None, TL_PROJ, 512), lambda b, l: (b, l, 0)),
            pl.BlockSpec((None, 512, TL_PROJ), lambda b, l: (b, 0, l)),
            pl.BlockSpec((None, TL_PROJ, 2 * D), lambda b, l: (b, l, 0)),
        ],
        out_shape=[
            jax.ShapeDtypeStruct((L, B * 512), BF16),
            jax.ShapeDtypeStruct((B, L, 512), BF16),
            jax.ShapeDtypeStruct((B, L, 512), BF16),
            jax.ShapeDtypeStruct((B, 512, L), BF16),
            jax.ShapeDtypeStruct((B, L, 2 * D), BF16),
        ],
        compiler_params=_params(("arbitrary", "arbitrary")),
        name="inproj",
    )(x, g1, w_in_bf, b_gate, qg, kg, ones_bd)


def _s5_kernel(u_ref, bm_ref, ar_ref, ai_ref, cm_ref, d_ref, wg_ref, bgl_ref,
               y_ref, hs_ref, h_ref):
    i = pl.program_id(0)

    @pl.when(i == 0)
    def _():
        h_ref[...] = jnp.zeros_like(h_ref)

    u = u_ref[...]
    for hh in range(2):
        hs_ref[:, 2048 * hh:2048 * (hh + 1)] = _dot(u[:, 256 * hh:256 * (hh + 1)], bm_ref[hh])

    for hh in range(2):
        re0 = 2048 * hh
        im0 = re0 + 1024
        ar = jnp.broadcast_to(ar_ref[:, 1024 * hh:1024 * (hh + 1)], (V7X_SUBLANES, 1024))
        ai = jnp.broadcast_to(ai_ref[:, 1024 * hh:1024 * (hh + 1)], (V7X_SUBLANES, 1024))

        def step(t, carry):
            hr, hi = carry
            r0 = pl.multiple_of(t * V7X_SUBLANES, V7X_SUBLANES)
            xr = hs_ref[pl.ds(r0, V7X_SUBLANES), re0:re0 + 1024]
            xi = hs_ref[pl.ds(r0, V7X_SUBLANES), im0:im0 + 1024]
            nr = ar * hr - ai * hi + xr
            ni = ar * hi + ai * hr + xi
            hs_ref[pl.ds(r0, V7X_SUBLANES), re0:re0 + 1024] = nr
            hs_ref[pl.ds(r0, V7X_SUBLANES), im0:im0 + 1024] = ni
            return nr, ni

        hr, hi = lax.fori_loop(0, S5_STEPS, step,
                               (h_ref[:, re0:re0 + 1024], h_ref[:, im0:im0 + 1024]), unroll=2)
        h_ref[:, re0:re0 + 1024] = hr
        h_ref[:, im0:im0 + 1024] = hi

    y = jnp.concatenate(
        [_dot(hs_ref[:, 2048 * hh:2048 * (hh + 1)].astype(BF16), cm_ref[hh]) for hh in range(2)], axis=1)
    y = y + d_ref[...] * u.astype(F32)
    y = 0.5 * y * (1.0 + jnp.tanh(math.sqrt(2.0 / math.pi) * (y + 0.044715 * (y * y * y))))
    z = _dot(y.astype(BF16), wg_ref[...]) + bgl_ref[...]
    y_ref[...] = (y * jax.nn.sigmoid(z)).astype(BF16)


def _s5(u2, bmat, ar, ai, cmat, d_skip, w_glu_bf, b_glu):
    rows = u2.shape[0]
    R = S5_STEPS * V7X_SUBLANES
    full = lambda shape: pl.BlockSpec(shape, lambda i: (0,) * len(shape))
    return pl.pallas_call(
        _s5_kernel,
        grid=(rows // R,),
        in_specs=[
            pl.BlockSpec((R, 512), lambda i: (i, 0)),
            full(bmat.shape), full(ar.shape), full(ai.shape), full(cmat.shape),
            full(d_skip.shape), full(w_glu_bf.shape), full(b_glu.shape),
        ],
        out_specs=pl.BlockSpec((R, 512), lambda i: (i, 0)),
        out_shape=jax.ShapeDtypeStruct((rows, 512), BF16),
        scratch_shapes=[pltpu.VMEM((R, 4096), F32), pltpu.VMEM((V7X_SUBLANES, 4096), F32)],
        compiler_params=_params(("arbitrary",)),
        name="s5",
    )(u2, bmat, ar, ai, cmat, d_skip, w_glu_bf, b_glu)


def _attn_kernel(q_ref, k_ref, vt_ref, lam_ref, sg_ref, o_ref, acc_ref):
    i = pl.program_id(1)
    lamv = lam_ref[...]
    lam = (jnp.exp(jnp.sum(lamv[0:1] * lamv[1:2], axis=-1, keepdims=True))
           - jnp.exp(jnp.sum(lamv[2:3] * lamv[3:4], axis=-1, keepdims=True)) + LAMBDA_INIT)
    lane = lax.broadcasted_iota(jnp.int32, (TQ, 128), 1)
    key_chunk = lax.broadcasted_iota(jnp.int32, (TK, TQ), 0) // CHUNK
    qry_chunk = lax.broadcasted_iota(jnp.int32, (TK, TQ), 1) // CHUNK
    diag_ok = key_chunk <= qry_chunk

    for hd in range(ATT_HEADS):
        c0 = 128 * hd
        qh = q_ref[:, c0:c0 + 128]
        zero = jnp.zeros_like(qh)
        qs = (jnp.where(lane < ATT_SUB_DIM, qh, zero), jnp.where(lane >= ATT_SUB_DIM, qh, zero))
        acc_ref[...] = jnp.zeros_like(acc_ref)

        def tile(j, carry, masked):
            off = pl.multiple_of(j * TK, TK)
            ks = k_ref[pl.ds(off, TK), c0:c0 + 128]
            vts = vt_ref[c0:c0 + 128, pl.ds(off, TK)]
            new = []
            for s in range(2):
                m, l = carry[2 * s], carry[2 * s + 1]
                st = _dot_nt(ks, qs[s])
                if masked:
                    st = jnp.where(diag_ok, st, NEG)
                mn = jnp.maximum(m, jnp.max(st, axis=0, keepdims=True))
                alpha = jnp.exp2(m - mn)
                p = jnp.exp2(st - mn)
                l = alpha * l + jnp.sum(p, axis=0, keepdims=True)
                acc_ref[s] = acc_ref[s] * alpha + _dot(vts, p.astype(BF16))
                new += [mn, l]
            return tuple(new)

        init = (jnp.full((1, TQ), NEG, F32), jnp.zeros((1, TQ), F32),
                jnp.full((1, TQ), NEG, F32), jnp.zeros((1, TQ), F32))
        carry = lax.fori_loop(0, i, lambda j, c: tile(j, c, False), init)
        _, l1, _, l2 = tile(i, carry, True)

        o = acc_ref[0] / l1 - lam * (acc_ref[1] / l2)
        ms = jnp.mean(o * o, axis=0, keepdims=True)
        on = (o * lax.rsqrt(ms + RMS_EPS)).T
        o_ref[:, c0:c0 + 128] = (on * sg_ref[:, c0:c0 + 128] * (1.0 - LAMBDA_INIT)).astype(BF16)


def _attn(q, k, vt, lam4, sg):
    B, L, W = q.shape
    return pl.pallas_call(
        _attn_kernel,
        grid=(B, L // TQ),
        in_specs=[
            pl.BlockSpec((None, TQ, W), lambda b, i: (b, i, 0)),
            pl.BlockSpec((None, L, W), lambda b, i: (b, 0, 0)),
            pl.BlockSpec((None, W, L), lambda b, i: (b, 0, 0)),
            pl.BlockSpec((4, ATT_SUB_DIM), lambda b, i: (0, 0)),
            pl.BlockSpec((1, W), lambda b, i: (0, 0)),
        ],
        out_specs=pl.BlockSpec((None, TQ, W), lambda b, i: (b, i, 0)),
        out_shape=jax.ShapeDtypeStruct((B, L, W), BF16),
        scratch_shapes=[pltpu.VMEM((2, ATT_V_DIM, TQ), F32)],
        compiler_params=_params(("arbitrary", "arbitrary")),
        name="diff_attn",
    )(q, k, vt, lam4, sg)


def _mix_kernel(x_ref, ys_ref, ya_ref, gate_ref, wps_ref, wpa_ref, wo_ref, g2_ref, wrt_ref, br_ref,
                sut_ref, h_ref, hn_ref, e_ref, w_ref, rank_ref, cnt_out_ref, cnt_ref):
    first = jnp.logical_and(pl.program_id(0) == 0, pl.program_id(1) == 0)

    @pl.when(first)
    def _():
        cnt_ref[...] = jnp.zeros_like(cnt_ref)

    D = x_ref.shape[-1]
    tl = x_ref.shape[0]
    gs = gate_ref[:, 0:D].astype(F32)
    ga = gate_ref[:, D:2 * D].astype(F32)
    mixed = gs * _dot(ys_ref[...], wps_ref[...]) + ga * _dot(ya_ref[...], wpa_ref[...])
    h = x_ref[...] + _dot(mixed.astype(BF16), wo_ref[...])
    h_ref[...] = h
    ms = jnp.mean(h * h, axis=-1, keepdims=True)
    hn = h * lax.rsqrt(ms + RMS_EPS) * g2_ref[...]
    hn_ref[...] = hn

    lg = _dot_nt(wrt_ref[...], hn.astype(BF16)) + br_ref[...]
    eio = lax.broadcasted_iota(jnp.int32, (N_EXPERTS, tl), 0)
    vals, sels = [], []
    for k in range(TOP_K):
        m = jnp.max(lg, axis=0, keepdims=True)
        idx = jnp.min(jnp.where(lg == m, eio, N_EXPERTS), axis=0, keepdims=True)
        sel = eio == idx
        e_ref[k:k + 1, :] = idx
        vals.append(m)
        sels.append(sel)
        lg = jnp.where(sel, -jnp.inf, lg)
    ex = [jnp.exp(v - vals[0]) for v in vals]
    den = ex[0] + ex[1] + ex[2] + ex[3]
    for k in range(TOP_K):
        w_ref[k:k + 1, :] = ex[k] / den

    mtot = jnp.zeros((N_EXPERTS, tl), F32)
    for k in range(TOP_K):
        mtot = mtot + jnp.where(sels[k], 1.0, 0.0)
    base = _dot(mtot.astype(BF16), sut_ref[...]) + cnt_ref[:, 0:1]
    for k in range(TOP_K):
        rank_ref[k:k + 1, :] = jnp.sum(jnp.where(sels[k], base, 0.0), axis=0, keepdims=True).astype(jnp.int32)
    cnt_ref[...] = cnt_ref[...] + jnp.sum(mtot, axis=1, keepdims=True)
    cnt_out_ref[...] = cnt_ref[...]


def _mix(x, ys_tm, ya, gates, wps, wpa, wo, g2, wrt, br, sut):
    B, L, D = x.shape
    nl = L // TL_PROJ
    n_tok = B * L
    full = lambda shape: pl.BlockSpec(shape, lambda b, l: (0,) * len(shape))
    tok_spec = pl.BlockSpec((TOP_K, TL_PROJ), lambda b, l: (0, b * nl + l))
    return pl.pallas_call(
        _mix_kernel,
        grid=(B, nl),
        in_specs=[
            pl.BlockSpec((None, TL_PROJ, D), lambda b, l: (b, l, 0)),
            pl.BlockSpec((TL_PROJ, 512), lambda b, l: (l, b)),
            pl.BlockSpec((None, TL_PROJ, 512), lambda b, l: (b, l, 0)),
            pl.BlockSpec((None, TL_PROJ, 2 * D), lambda b, l: (b, l, 0)),
            full(wps.shape), full(wpa.shape), full(wo.shape), full((1, D)),
            full(wrt.shape), full(br.shape), full(sut.shape),
        ],
        out_specs=[
            pl.BlockSpec((None, TL_PROJ, D), lambda b, l: (b, l, 0)),
            pl.BlockSpec((None, TL_PROJ, D), lambda b, l: (b, l, 0)),
            tok_spec, tok_spec, tok_spec,
            full((N_EXPERTS, V7X_LANES)),
        ],
        out_shape=[
            jax.ShapeDtypeStruct((B, L, D), F32),
            jax.ShapeDtypeStruct((B, L, D), F32),
            jax.ShapeDtypeStruct((TOP_K, n_tok), jnp.int32),
            jax.ShapeDtypeStruct((TOP_K, n_tok), F32),
            jax.ShapeDtypeStruct((TOP_K, n_tok), jnp.int32),
            jax.ShapeDtypeStruct((N_EXPERTS, V7X_LANES), F32),
        ],
        scratch_shapes=[pltpu.VMEM((N_EXPERTS, V7X_LANES), F32)],
        compiler_params=_params(("arbitrary", "arbitrary")),
        name="mix_router",
    )(x, ys_tm, ya, gates, wps, wpa, wo, g2, wrt, br, sut)


def _row_gather(idx_ref, n_rows, src_hbm, dst, sem):
    def body(r, c):
        pltpu.make_async_copy(src_hbm.at[pl.ds(idx_ref[0, r], 1), :], dst.at[pl.ds(r, 1), :], sem).start()
        return c
    lax.fori_loop(0, n_rows, body, 0, unroll=8)


def _expert_kernel(be_ref, nv_ref, idx_cur_ref, idx_nxt_ref, hn_hbm, wup_ref, bup_ref, wdn_ref, bdn_ref,
                   perm_ref, y_ref, xbuf, wup_bf, wdn_bf, act_ref, sem):
    i = pl.program_id(0)
    nv = nv_ref[0]
    slot = lax.rem(i, 2)

    @pl.when(i == 0)
    def _():
        _row_gather(idx_cur_ref, MOE_ROWS, hn_hbm, xbuf.at[0], sem.at[0])

    @pl.when(i + 1 < nv)
    def _():
        _row_gather(idx_nxt_ref, MOE_ROWS, hn_hbm, xbuf.at[1 - slot], sem.at[1 - slot])

    @pl.when(i < nv)
    def _():
        pltpu.make_async_copy(hn_hbm.at[pl.ds(0, MOE_ROWS), :], xbuf.at[slot], sem.at[slot]).wait()

        changed = jnp.logical_or(i == 0, be_ref[i] != be_ref[jnp.maximum(i - 1, 0)])

        @pl.when(changed)
        def _():
            for cb in range(8):
                blk = wup_ref[:, 256 * cb:256 * (cb + 1)].astype(BF16)
                wup_bf[:, 256 * cb:256 * (cb + 1)] = _dot(blk, perm_ref[...]).astype(BF16)
            wdn_bf[...] = wdn_ref[...].astype(BF16)

        x = xbuf[slot].astype(BF16)
        for cb in range(8):
            hp = _dot(x, wup_bf[:, 256 * cb:256 * (cb + 1)]) + bup_ref[:, 256 * cb:256 * (cb + 1)]
            glu = jnp.minimum(hp[:, 0:128], SWIGLU_LIMIT)
            lin = jnp.clip(hp[:, 128:256], -SWIGLU_LIMIT, SWIGLU_LIMIT)
            act = glu * jax.nn.sigmoid(SWIGLU_ALPHA * glu) * (lin + 1.0)
            act_ref[:, 128 * cb:128 * (cb + 1)] = act.astype(BF16)
        y_ref[...] = _dot(act_ref[...], wdn_bf[...]) + bdn_ref[...]

    @pl.when(i >= nv)
    def _():
        y_ref[...] = jnp.zeros_like(y_ref)


def _experts(block_e, n_valid, row_tok3, hn2, w_up, b_up_p, w_down, b_down, perm):
    n_blocks = row_tok3.shape[0]
    D = hn2.shape[1]
    DE2 = w_up.shape[2]
    grid_spec = pltpu.PrefetchScalarGridSpec(
        num_scalar_prefetch=2,
        grid=(n_blocks,),
        in_specs=[
            pl.BlockSpec((None, 1, MOE_ROWS), lambda i, be, nv: (i, 0, 0), memory_space=pltpu.SMEM),
            pl.BlockSpec((None, 1, MOE_ROWS), lambda i, be, nv: (jnp.minimum(i + 1, n_blocks - 1), 0, 0),
                         memory_space=pltpu.SMEM),
            pl.BlockSpec(memory_space=pl.ANY),
            pl.BlockSpec((None, D, DE2), lambda i, be, nv: (be[i], 0, 0)),
            pl.BlockSpec((None, 1, DE2), lambda i, be, nv: (be[i], 0, 0)),
            pl.BlockSpec((None, DE2 // 2, D), lambda i, be, nv: (be[i], 0, 0)),
            pl.BlockSpec((None, 1, D), lambda i, be, nv: (be[i], 0, 0)),
            pl.BlockSpec((256, 256), lambda i, be, nv: (0, 0)),
        ],
        out_specs=pl.BlockSpec((MOE_ROWS, D), lambda i, be, nv: (i, 0)),
        scratch_shapes=[
            pltpu.VMEM((2, MOE_ROWS, D), F32),
            pltpu.VMEM((D, DE2), BF16),
            pltpu.VMEM((DE2 // 2, D), BF16),
            pltpu.VMEM((MOE_ROWS, DE2 // 2), BF16),
            pltpu.SemaphoreType.DMA((2,)),
        ],
    )
    return pl.pallas_call(
        _expert_kernel,
        grid_spec=grid_spec,
        out_shape=jax.ShapeDtypeStruct((n_blocks * MOE_ROWS, D), F32),
        compiler_params=_params(("arbitrary",)),
        name="experts",
    )(block_e, n_valid, row_tok3, row_tok3, hn2, w_up, b_up_p, w_down, b_down, perm)


def _combine_kernel(pos_cur_ref, pos_nxt_ref, h_ref, w_ref, ys_hbm, o_ref, buf, sem):
    i = pl.program_id(0)
    n = pl.num_programs(0)
    slot = lax.rem(i, 2)

    def gather(pos_ref, s):
        for k in range(TOP_K):
            def body(r, c):
                pltpu.make_async_copy(ys_hbm.at[pl.ds(pos_ref[k, r], 1), :],
                                      buf.at[s, k, pl.ds(r, 1), :], sem.at[s]).start()
                return c
            lax.fori_loop(0, TC_COMBINE, body, 0, unroll=8)

    @pl.when(i == 0)
    def _():
        gather(pos_cur_ref, 0)

    @pl.when(i + 1 < n)
    def _():
        gather(pos_nxt_ref, 1 - slot)

    for k in range(TOP_K):
        pltpu.make_async_copy(ys_hbm.at[pl.ds(0, TC_COMBINE), :], buf.at[slot, k], sem.at[slot]).wait()
    acc = h_ref[...]
    for k in range(TOP_K):
        acc = acc + w_ref[:, k:k + 1] * buf[slot, k]
    o_ref[...] = acc


def _combine(pos3, h2, w_tok, ys):
    n_tok, D = h2.shape
    n_steps = n_tok // TC_COMBINE
    return pl.pallas_call(
        _combine_kernel,
        grid=(n_steps,),
        in_specs=[
            pl.BlockSpec((None, TOP_K, TC_COMBINE), lambda i: (i, 0, 0), memory_space=pltpu.SMEM),
            pl.BlockSpec((None, TOP_K, TC_COMBINE), lambda i: (jnp.minimum(i + 1, n_steps - 1), 0, 0),
                         memory_space=pltpu.SMEM),
            pl.BlockSpec((TC_COMBINE, D), lambda i: (i, 0)),
            pl.BlockSpec((TC_COMBINE, TOP_K), lambda i: (i, 0)),
            pl.BlockSpec(memory_space=pl.ANY),
        ],
        out_specs=pl.BlockSpec((TC_COMBINE, D), lambda i: (i, 0)),
        out_shape=jax.ShapeDtypeStruct((n_tok, D), F32),
        scratch_shapes=[pltpu.VMEM((2, TOP_K, TC_COMBINE, D), F32), pltpu.SemaphoreType.DMA((2,))],
        compiler_params=_params(("arbitrary",)),
        name="combine",
    )(pos3, pos3, h2, w_tok, ys)


def _s5_matrices(a_re, a_im, log_dt, b_re, b_im, c_re, c_im):
    G, P = a_re.shape
    dt = jnp.exp(log_dt.astype(F32))[:, None]
    lr = a_re.astype(F32)
    li = a_im.astype(F32)
    mag = jnp.exp(lr * dt)
    abar_r = mag * jnp.cos(li * dt)
    abar_i = mag * jnp.sin(li * dt)
    den = lr * lr + li * li
    nr = abar_r - 1.0
    ni = abar_i
    coef_r = (nr * lr + ni * li) / den
    coef_i = (ni * lr - nr * li) / den
    br_ = b_re.astype(F32)
    bi_ = b_im.astype(F32)
    bbar_r = coef_r[..., None] * br_ - coef_i[..., None] * bi_
    bbar_i = coef_r[..., None] * bi_ + coef_i[..., None] * br_
    eye = jnp.eye(G // 2, dtype=F32)

    def in_mat(bb):
        bb = bb.reshape(2, G // 2, P, SSM_GROUP)
        return jnp.einsum('xgph,gk->xghkp', bb, eye).reshape(2, (G // 2) * SSM_GROUP, (G // 2) * P)

    def out_mat(cc):
        cc = cc.reshape(2, G // 2, SSM_GROUP, P)
        return jnp.einsum('xghp,gk->xgpkh', cc, eye).reshape(2, (G // 2) * P, (G // 2) * SSM_GROUP)

    bmat = jnp.concatenate([in_mat(bbar_r), in_mat(bbar_i)], axis=-1).astype(BF16)
    cmat = jnp.concatenate([out_mat(c_re.astype(F32)), -out_mat(c_im.astype(F32))], axis=1).astype(BF16)
    return bmat, abar_r.reshape(1, G * P), abar_i.reshape(1, G * P), cmat


def kernel(x, norm1_g, w_in, b_gate, ssm_a_re, ssm_a_im, ssm_log_dt, ssm_b_re, ssm_b_im, ssm_c_re, ssm_c_im, ssm_d, ssm_w_glu, ssm_b_glu, q_norm_g, k_norm_g, lambda_q1, lambda_k1, lambda_q2, lambda_k2, subln_g, w_proj_ssm, w_proj_att, w_out, norm2_g, w_router, b_router, w_up, b_up, w_down, b_down):
    B, L, D = x.shape
    assert B == V7X_SUBLANES and D == 1024 and L % TL_PROJ == 0 and w_in.shape[0] == 1
    n_tok = B * L
    l = 0

    scale = ATT_SUB_DIM ** -0.5
    qg = (jnp.tile(q_norm_g[l].astype(F32), 2 * ATT_HEADS) * (scale * math.log2(math.e))).reshape(1, 512)
    kg = jnp.tile(k_norm_g[l].astype(F32), 2 * ATT_HEADS).reshape(1, 512)
    blk = jnp.arange(256) // ATT_SUB_DIM
    ones_bd = (blk[:, None] == blk[None, :]).astype(BF16)
    u_tm, q, k, vt, gates = _inproj(x, norm1_g[l].reshape(1, D), w_in[l].astype(BF16),
                                    b_gate[l].reshape(1, 2 * D), qg, kg, ones_bd)

    bmat, ar, ai, cmat = _s5_matrices(ssm_a_re[l], ssm_a_im[l], ssm_log_dt[l], ssm_b_re[l], ssm_b_im[l],
                                      ssm_c_re[l], ssm_c_im[l])
    y_ssm = _s5(u_tm.reshape(L * B, 512), bmat, ar, ai, cmat, ssm_d[l].reshape(1, 512).astype(F32),
                ssm_w_glu[l].astype(BF16), ssm_b_glu[l].reshape(1, 512).astype(F32))

    lam4 = jnp.stack([lambda_q1[l], lambda_k1[l], lambda_q2[l], lambda_k2[l]]).astype(F32)
    sg = jnp.tile(subln_g[l].astype(F32), ATT_HEADS).reshape(1, 512)
    y_att = _attn(q, k, vt, lam4, sg)

    tpos = jnp.arange(TL_PROJ)
    sut = (tpos[:, None] < tpos[None, :]).astype(BF16)
    h, hn, top_e, top_w, rank, cnt = _mix(
        x, y_ssm.reshape(L, B * 512), y_att, gates,
        w_proj_ssm[l].astype(BF16), w_proj_att[l].astype(BF16), w_out[l].astype(BF16),
        norm2_g[l].reshape(1, D), w_router[l].T.astype(BF16), b_router[l].reshape(N_EXPERTS, 1).astype(F32), sut)

    counts = cnt[:, 0].astype(jnp.int32)
    padded = ((counts + MOE_ROWS - 1) // MOE_ROWS) * MOE_ROWS
    pad_end = jnp.cumsum(padded)
    pad_start = pad_end - padded
    pos = pad_start[top_e] + rank
    n_blocks = (n_tok * TOP_K) // MOE_ROWS + N_EXPERTS
    n_valid = (pad_end[-1] // MOE_ROWS).astype(jnp.int32)
    blk_start = jnp.arange(n_blocks, dtype=jnp.int32) * MOE_ROWS
    blk_e = jnp.minimum(jnp.sum(blk_start[:, None] >= pad_end[None, :], axis=1), N_EXPERTS - 1).astype(jnp.int32)
    last_e = blk_e[jnp.maximum(n_valid - 1, 0)]
    blk_e = jnp.where(jnp.arange(n_blocks) < n_valid, blk_e, last_e)
    tok_ids = jnp.broadcast_to(jnp.arange(n_tok, dtype=jnp.int32)[None, :], (TOP_K, n_tok))
    row_tok = jnp.zeros((n_blocks * MOE_ROWS,), jnp.int32).at[pos.reshape(-1)].set(
        tok_ids.reshape(-1), unique_indices=True)

    de2 = w_up.shape[-1]
    b_up_p = b_up[l].reshape(N_EXPERTS, de2 // 256, 128, 2).transpose(0, 1, 3, 2).reshape(N_EXPERTS, 1, de2)
    pr = jnp.arange(256)
    src = jnp.where(pr < 128, 2 * pr, 2 * (pr - 128) + 1)
    perm = (jnp.arange(256)[:, None] == src[None, :]).astype(BF16)
    ys = _experts(blk_e, n_valid.reshape(1), row_tok.reshape(n_blocks, 1, MOE_ROWS), hn.reshape(n_tok, D),
                  w_up[l], b_up_p, w_down[l], b_down[l].reshape(N_EXPERTS, 1, D), perm)

    pos3 = pos.reshape(TOP_K, n_tok // TC_COMBINE, TC_COMBINE).transpose(1, 0, 2)
    out = _combine(pos3, h.reshape(n_tok, D), top_w.T, ys)
    return out.reshape(B, L, D)
```

```python
import math

import jax
import jax.numpy as jnp
from jax import lax
from jax.experimental import pallas as pl
from jax.experimental.pallas import tpu as pltpu

F32 = jnp.float32
BF16 = jnp.bfloat16
I32 = jnp.int32

RMS_EPS = 1e-6
CHUNK = 64
SSM_GROUP = 16
SSM_STATE = 64
ATT_HEADS = 4
ATT_SUB_DIM = 64
ATT_V_DIM = 128
N_EXPERTS = 32
TOP_K = 4
SWIGLU_ALPHA = 1.702
SWIGLU_LIMIT = 7.0
LAMBDA_INIT = 0.8 - 0.6 * math.exp(-0.3 * 0)

V7X_SUBLANES = 8
V7X_LANES = 128
V7X_MXU_DIM = 256

TL_PROJ = 512
S5_STEPS = 64
TQ = 256
TK = 256
MOE_ROWS = 512
OCT = V7X_SUBLANES
BLK_OCT = MOE_ROWS // OCT
SLOTS = ((TL_PROJ * TOP_K + N_EXPERTS * (OCT - 1) + V7X_MXU_DIM - 1) // V7X_MXU_DIM) * V7X_MXU_DIM
SL_OCT = SLOTS // OCT
NEG = -1e30
VMEM_LIMIT = 56 * 1024 * 1024


def _dot(a, b):
    return jnp.dot(a, b, preferred_element_type=F32)


def _dot_nt(a, b):
    return lax.dot_general(a, b, (((1,), (1,)), ((), ())), preferred_element_type=F32)


def _params(sem, vmem=VMEM_LIMIT):
    return pltpu.CompilerParams(dimension_semantics=sem, vmem_limit_bytes=vmem)


def _inproj_kernel(x_ref, g1_ref, w_ref, bg_ref, qg_ref, kg_ref, ones_ref,
                   u_ref, q_ref, k_ref, vt_ref, gate_ref):
    x = x_ref[...]
    ms = jnp.mean(x * x, axis=-1, keepdims=True)
    xn = (x * lax.rsqrt(ms + RMS_EPS) * g1_ref[...]).astype(BF16)

    u_ref[...] = _dot(xn, w_ref[:, 0:512]).astype(BF16)

    def head_norm(z, g):
        sq = (z * z).astype(BF16)
        ss = jnp.concatenate([_dot(sq[:, 0:256], ones_ref[...]),
                              _dot(sq[:, 256:512], ones_ref[...])], axis=1)
        return z * lax.rsqrt(ss * (1.0 / ATT_SUB_DIM) + RMS_EPS) * g

    q_ref[...] = head_norm(_dot(xn, w_ref[:, 512:1024]), qg_ref[...]).astype(BF16)
    k_ref[...] = head_norm(_dot(xn, w_ref[:, 1024:1536]), kg_ref[...]).astype(BF16)
    vt_ref[...] = _dot(xn, w_ref[:, 1536:2048]).T.astype(BF16)
    for c in range(4):
        lo = 2048 + 512 * c
        z = _dot(xn, w_ref[:, lo:lo + 512]) + bg_ref[:, 512 * c:512 * (c + 1)]
        gate_ref[:, 512 * c:512 * (c + 1)] = jax.nn.sigmoid(z).astype(BF16)


def _inproj(x, g1, w_in_bf, b_gate, qg, kg, ones_bd):
    B, L, D = x.shape
    nl = L // TL_PROJ
    full = lambda shape: pl.BlockSpec(shape, lambda b, l: (0,) * len(shape))
    return pl.pallas_call(
        _inproj_kernel,
        grid=(B, nl),
        in_specs=[
            pl.BlockSpec((None, TL_PROJ, D), lambda b, l: (b, l, 0)),
            full((1, D)), full(w_in_bf.shape), full((1, 2 * D)),
            full((1, 512)), full((1, 512)), full((256, 256)),
        ],
        out_specs=[
            pl.BlockSpec((TL_PROJ, 512), lambda b, l: (l, b)),
            pl.BlockSpec((None, TL_PROJ, 512), lambda b, l: (b, l, 0)),
            pl.BlockSpec((None, TL_PROJ, 512), lambda b, l: (b, l, 0)),
            pl.BlockSpec((None, 512, TL_PROJ), lambda b, l: (b, 0, l)),
            pl.BlockSpec((None, TL_PROJ, 2 * D), lambda b, l: (b, l, 0)),
        ],
        out_shape=[
            jax.ShapeDtypeStruct((L, B * 512), BF16),
            jax.ShapeDtypeStruct((B, L, 512), BF16),
            jax.ShapeDtypeStruct((B, L, 512), BF16),
            jax.ShapeDtypeStruct((B, 512, L), BF16),
            jax.ShapeDtypeStruct((B, L, 2 * D), BF16),
        ],
        compiler_params=_params(("arbitrary", "arbitrary")),
        name="inproj",
    )(x, g1, w_in_bf, b_gate, qg, kg, ones_bd)


def _s5_kernel(u_ref, bm_ref, ar_ref, ai_ref, cm_ref, d_ref, wg_ref, bgl_ref,
               y_ref, hs_ref, h_ref):
    i = pl.program_id(0)

    @pl.when(i == 0)
    def _():
        h_ref[...] = jnp.zeros_like(h_ref)

    u = u_ref[...]
    for hh in range(2):
        hs_ref[:, 2048 * hh:2048 * (hh + 1)] = _dot(u[:, 256 * hh:256 * (hh + 1)], bm_ref[hh])

    for hh in range(2):
        re0 = 2048 * hh
        im0 = re0 + 1024
        ar = jnp.broadcast_to(ar_ref[:, 1024 * hh:1024 * (hh + 1)], (V7X_SUBLANES, 1024))
        ai = jnp.broadcast_to(ai_ref[:, 1024 * hh:1024 * (hh + 1)], (V7X_SUBLANES, 1024))

        def step(t, carry):
            hr, hi = carry
            r0 = pl.multiple_of(t * V7X_SUBLANES, V7X_SUBLANES)
            xr = hs_ref[pl.ds(r0, V7X_SUBLANES), re0:re0 + 1024]
            xi = hs_ref[pl.ds(r0, V7X_SUBLANES), im0:im0 + 1024]
            nr = ar * hr - ai * hi + xr
            ni = ar * hi + ai * hr + xi
            hs_ref[pl.ds(r0, V7X_SUBLANES), re0:re0 + 1024] = nr
            hs_ref[pl.ds(r0, V7X_SUBLANES), im0:im0 + 1024] = ni
            return nr, ni

        hr, hi = lax.fori_loop(0, S5_STEPS, step,
                               (h_ref[:, re0:re0 + 1024], h_ref[:, im0:im0 + 1024]), unroll=2)
        h_ref[:, re0:re0 + 1024] = hr
        h_ref[:, im0:im0 + 1024] = hi

    y = jnp.concatenate(
        [_dot(hs_ref[:, 2048 * hh:2048 * (hh + 1)].astype(BF16), cm_ref[hh]) for hh in range(2)], axis=1)
    y = y + d_ref[...] * u.astype(F32)
    y = 0.5 * y * (1.0 + jnp.tanh(math.sqrt(2.0 / math.pi) * (y + 0.044715 * (y * y * y))))
    z = _dot(y.astype(BF16), wg_ref[...]) + bgl_ref[...]
    y_ref[...] = (y * jax.nn.sigmoid(z)).astype(BF16)


def _s5(u2, bmat, ar, ai, cmat, d_skip, w_glu_bf, b_glu):
    rows = u2.shape[0]
    R = S5_STEPS * V7X_SUBLANES
    full = lambda shape: pl.BlockSpec(shape, lambda i: (0,) * len(shape))
    return pl.pallas_call(
        _s5_kernel,
        grid=(rows // R,),
        in_specs=[
            pl.BlockSpec((R, 512), lambda i: (i, 0)),
            full(bmat.shape), full(ar.shape), full(ai.shape), full(cmat.shape),
            full(d_skip.shape), full(w_glu_bf.shape), full(b_glu.shape),
        ],
        out_specs=pl.BlockSpec((R, 512), lambda i: (i, 0)),
        out_shape=jax.ShapeDtypeStruct((rows, 512), BF16),
        scratch_shapes=[pltpu.VMEM((R, 4096), F32), pltpu.VMEM((V7X_SUBLANES, 4096), F32)],
        compiler_params=_params(("arbitrary",)),
        name="s5",
    )(u2, bmat, ar, ai, cmat, d_skip, w_glu_bf, b_glu)


def _attn_kernel(q_ref, k_ref, vt_ref, lam_ref, sg_ref, o_ref, acc_ref):
    i = pl.program_id(1)
    lamv = lam_ref[...]
    lam = (jnp.exp(jnp.sum(lamv[0:1] * lamv[1:2], axis=-1, keepdims=True))
           - jnp.exp(jnp.sum(lamv[2:3] * lamv[3:4], axis=-1, keepdims=True)) + LAMBDA_INIT)
    lane = lax.broadcasted_iota(I32, (TQ, 128), 1)
    key_chunk = lax.broadcasted_iota(I32, (TK, TQ), 0) // CHUNK
    qry_chunk = lax.broadcasted_iota(I32, (TK, TQ), 1) // CHUNK
    diag_ok = key_chunk <= qry_chunk

    for hd in range(ATT_HEADS):
        c0 = 128 * hd
        qh = q_ref[:, c0:c0 + 128]
        zero = jnp.zeros_like(qh)
        qs = (jnp.where(lane < ATT_SUB_DIM, qh, zero), jnp.where(lane >= ATT_SUB_DIM, qh, zero))
        acc_ref[...] = jnp.zeros_like(acc_ref)

        def tile(j, carry, masked):
            off = pl.multiple_of(j * TK, TK)
            ks = k_ref[pl.ds(off, TK), c0:c0 + 128]
            vts = vt_ref[c0:c0 + 128, pl.ds(off, TK)]
            new = []
            for s in range(2):
                m, l = carry[2 * s], carry[2 * s + 1]
                st = _dot_nt(ks, qs[s])
                if masked:
                    st = jnp.where(diag_ok, st, NEG)
                mn = jnp.maximum(m, jnp.max(st, axis=0, keepdims=True))
                alpha = jnp.exp2(m - mn)
                p = jnp.exp2(st - mn)
                l = alpha * l + jnp.sum(p, axis=0, keepdims=True)
                acc_ref[s] = acc_ref[s] * alpha + _dot(vts, p.astype(BF16))
                new += [mn, l]
            return tuple(new)

        init = (jnp.full((1, TQ), NEG, F32), jnp.zeros((1, TQ), F32),
                jnp.full((1, TQ), NEG, F32), jnp.zeros((1, TQ), F32))
        carry = lax.fori_loop(0, i, lambda j, c: tile(j, c, False), init)
        _, l1, _, l2 = tile(i, carry, True)

        o = acc_ref[0] / l1 - lam * (acc_ref[1] / l2)
        ms = jnp.mean(o * o, axis=0, keepdims=True)
        on = (o * lax.rsqrt(ms + RMS_EPS)).T
        o_ref[:, c0:c0 + 128] = (on * sg_ref[:, c0:c0 + 128] * (1.0 - LAMBDA_INIT)).astype(BF16)


def _attn(q, k, vt, lam4, sg):
    B, L, W = q.shape
    return pl.pallas_call(
        _attn_kernel,
        grid=(B, L // TQ),
        in_specs=[
            pl.BlockSpec((None, TQ, W), lambda b, i: (b, i, 0)),
            pl.BlockSpec((None, L, W), lambda b, i: (b, 0, 0)),
            pl.BlockSpec((None, W, L), lambda b, i: (b, 0, 0)),
            pl.BlockSpec((4, ATT_SUB_DIM), lambda b, i: (0, 0)),
            pl.BlockSpec((1, W), lambda b, i: (0, 0)),
        ],
        out_specs=pl.BlockSpec((None, TQ, W), lambda b, i: (b, i, 0)),
        out_shape=jax.ShapeDtypeStruct((B, L, W), BF16),
        scratch_shapes=[pltpu.VMEM((2, ATT_V_DIM, TQ), F32)],
        compiler_params=_params(("arbitrary", "arbitrary")),
        name="diff_attn",
    )(q, k, vt, lam4, sg)


def _mix_kernel(x_ref, ys_ref, ya_ref, gate_ref, wps_ref, wpa_ref, wo_ref, g2_ref, wrt_ref, br_ref,
                sut_ref, lt_ref, h_ref, xs_ref, slot_ref, w_ref, oct_ref):
    tile = pl.program_id(0) * pl.num_programs(1) + pl.program_id(1)

    D = x_ref.shape[-1]
    tl = x_ref.shape[0]
    gs = gate_ref[:, 0:D].astype(F32)
    ga = gate_ref[:, D:2 * D].astype(F32)
    mixed = gs * _dot(ys_ref[...], wps_ref[...]) + ga * _dot(ya_ref[...], wpa_ref[...])
    h = x_ref[...] + _dot(mixed.astype(BF16), wo_ref[...])
    h_ref[...] = h
    ms = jnp.mean(h * h, axis=-1, keepdims=True)
    hb = (h * lax.rsqrt(ms + RMS_EPS) * g2_ref[...]).astype(BF16)

    lg = _dot_nt(wrt_ref[...], hb) + br_ref[...]
    eio = lax.broadcasted_iota(I32, (N_EXPERTS, tl), 0)
    vals, sels = [], []
    for k in range(TOP_K):
        m = jnp.max(lg, axis=0, keepdims=True)
        idx = jnp.min(jnp.where(lg == m, eio, N_EXPERTS), axis=0, keepdims=True)
        sel = eio == idx
        vals.append(m)
        sels.append(sel)
        lg = jnp.where(sel, -jnp.inf, lg)
    ex = [jnp.exp(v - vals[0]) for v in vals]
    den = ex[0] + ex[1] + ex[2] + ex[3]
    for k in range(TOP_K):
        w_ref[k:k + 1, :] = ex[k] / den

    mtot = jnp.zeros((N_EXPERTS, tl), F32)
    for k in range(TOP_K):
        mtot = mtot + jnp.where(sels[k], 1.0, 0.0)
    pre = _dot(mtot.astype(BF16), sut_ref[...])
    cnt = jnp.sum(mtot, axis=1, keepdims=True)
    seg_oct = jnp.floor((cnt + (OCT - 1.0)) * (1.0 / OCT))
    seg_b = jnp.broadcast_to(seg_oct, (N_EXPERTS, V7X_LANES))
    start_oct = _dot(lt_ref[...], seg_b.astype(BF16))
    base = pre + start_oct[:, 0:1] * float(OCT)
    slots = []
    for k in range(TOP_K):
        s_k = jnp.sum(jnp.where(sels[k], base, 0.0), axis=0, keepdims=True).astype(I32)
        slot_ref[k:k + 1, :] = s_k
        slots.append(s_k)

    for c in range(SLOTS // V7X_MXU_DIM):
        sio = lax.broadcasted_iota(I32, (V7X_MXU_DIM, tl), 0) + V7X_MXU_DIM * c
        p = jnp.where(sio == slots[0], 1.0, 0.0)
        for k in range(1, TOP_K):
            p = jnp.where(sio == slots[k], 1.0, p)
        xs_ref[V7X_MXU_DIM * c:V7X_MXU_DIM * (c + 1), :] = _dot(p.astype(BF16), hb)

    lane = lax.broadcasted_iota(I32, (N_EXPERTS, V7X_LANES), 1)

    @pl.when(tile == 0)
    def _():
        oct_ref[...] = jnp.zeros_like(oct_ref)

    oct_ref[...] = jnp.where(lane == tile, seg_b, oct_ref[...])


def _mix(x, ys_tm, ya, gates, wps, wpa, wo, g2, wrt, br, sut, lt):
    B, L, D = x.shape
    nl = L // TL_PROJ
    n_tok = B * L
    full = lambda shape: pl.BlockSpec(shape, lambda b, l: (0,) * len(shape))
    tok_spec = pl.BlockSpec((TOP_K, TL_PROJ), lambda b, l: (0, b * nl + l))
    return pl.pallas_call(
        _mix_kernel,
        grid=(B, nl),
        in_specs=[
            pl.BlockSpec((None, TL_PROJ, D), lambda b, l: (b, l, 0)),
            pl.BlockSpec((TL_PROJ, 512), lambda b, l: (l, b)),
            pl.BlockSpec((None, TL_PROJ, 512), lambda b, l: (b, l, 0)),
            pl.BlockSpec((None, TL_PROJ, 2 * D), lambda b, l: (b, l, 0)),
            full(wps.shape), full(wpa.shape), full(wo.shape), full((1, D)),
            full(wrt.shape), full(br.shape), full(sut.shape), full(lt.shape),
        ],
        out_specs=[
            pl.BlockSpec((None, TL_PROJ, D), lambda b, l: (b, l, 0)),
            pl.BlockSpec((SLOTS, D), lambda b, l: (b * nl + l, 0)),
            tok_spec, tok_spec,
            full((N_EXPERTS, V7X_LANES)),
        ],
        out_shape=[
            jax.ShapeDtypeStruct((B, L, D), F32),
            jax.ShapeDtypeStruct((B * nl * SLOTS, D), F32),
            jax.ShapeDtypeStruct((TOP_K, n_tok), I32),
            jax.ShapeDtypeStruct((TOP_K, n_tok), F32),
            jax.ShapeDtypeStruct((N_EXPERTS, V7X_LANES), F32),
        ],
        compiler_params=_params(("arbitrary", "arbitrary")),
        name="mix_router",
    )(x, ys_tm, ya, gates, wps, wpa, wo, g2, wrt, br, sut, lt)


def _moe_tables(oct_tab, n_t, n_blocks):
    E = N_EXPERTS
    O = oct_tab[:, :n_t].astype(I32)
    lstart = jnp.cumsum(O, axis=0) - O
    ecum = jnp.cumsum(O, axis=1) - O
    tot = jnp.sum(O, axis=1)
    nb = (tot + BLK_OCT - 1) // BLK_OCT
    bend = jnp.cumsum(nb)
    bstart = bend - nb
    n_valid = bend[-1]
    bi = jnp.arange(n_blocks, dtype=I32)
    er = jnp.arange(E, dtype=I32)
    tr = jnp.arange(n_t, dtype=I32)
    be = jnp.minimum(jnp.sum((bi[:, None] >= bend[None, :]).astype(I32), axis=1), E - 1)
    last_e = jnp.sum(jnp.where(bi == jnp.maximum(n_valid - 1, 0), be, 0))
    be = jnp.where(bi < n_valid, be, last_e)
    oh_e = be[:, None] == er[None, :]
    pick = lambda tab: jnp.sum(jnp.where(oh_e[:, :, None], tab[None, :, :], 0), axis=1)
    ecum_i, o_i, lst_i = pick(ecum), pick(O), pick(lstart)
    tot_i = jnp.sum(jnp.where(oh_e, tot[None, :], 0), axis=1)
    bst_i = jnp.sum(jnp.where(oh_e, bstart[None, :], 0), axis=1)
    g = (bi - bst_i)[:, None] * BLK_OCT + jnp.arange(BLK_OCT, dtype=I32)[None, :]
    tau = jnp.minimum(jnp.sum((g[:, :, None] >= (ecum_i + o_i)[:, None, :]).astype(I32), axis=2), n_t - 1)
    off = tr[None, :] * SL_OCT + lst_i - ecum_i
    src = jnp.sum(jnp.where(tau[:, :, None] == tr[None, None, :], off[:, None, :], 0), axis=2) + g
    src = jnp.where((g < tot_i[:, None]) & (bi[:, None] < n_valid), src, 0)
    s = jnp.arange(SL_OCT, dtype=I32)
    lend_t = (lstart + O).T
    e_s = jnp.minimum(jnp.sum((s[None, :, None] >= lend_t[:, None, :]).astype(I32), axis=2), E - 1)
    offc = (bstart[:, None] * BLK_OCT + ecum - lstart).T
    csrc = jnp.sum(jnp.where(e_s[:, :, None] == er[None, None, :], offc[:, None, :], 0), axis=2) + s[None, :]
    csrc = jnp.where(s[None, :] < jnp.sum(O, axis=0)[:, None], csrc, 0)
    return be, n_valid.astype(I32), src, csrc


def _gather_octets(src_ref, n_oct, src_hbm, dst, sem):
    for q in range(n_oct):
        row = pl.multiple_of(src_ref[0, q] * OCT, OCT)
        pltpu.make_async_copy(src_hbm.at[pl.ds(row, OCT), :], dst.at[pl.ds(OCT * q, OCT), :], sem).start()


def _expert_kernel(be_ref, nv_ref, src_cur_ref, src_nxt_ref, xs_hbm, wup_ref, bup_ref, wdn_ref, bdn_ref,
                   perm_ref, y_ref, xbuf, wup_bf, wdn_bf, act_ref, sem):
    i = pl.program_id(0)
    nv = nv_ref[0]
    slot = lax.rem(i, 2)

    @pl.when(i == 0)
    def _():
        _gather_octets(src_cur_ref, BLK_OCT, xs_hbm, xbuf.at[0], sem.at[0])

    @pl.when(i + 1 < nv)
    def _():
        _gather_octets(src_nxt_ref, BLK_OCT, xs_hbm, xbuf.at[1 - slot], sem.at[1 - slot])

    @pl.when(i < nv)
    def _():
        pltpu.make_async_copy(xs_hbm.at[pl.ds(0, MOE_ROWS), :], xbuf.at[slot], sem.at[slot]).wait()

        changed = jnp.logical_or(i == 0, be_ref[i] != be_ref[jnp.maximum(i - 1, 0)])

        @pl.when(changed)
        def _():
            for cb in range(8):
                blk = wup_ref[:, 256 * cb:256 * (cb + 1)].astype(BF16)
                wup_bf[:, 256 * cb:256 * (cb + 1)] = _dot(blk, perm_ref[...]).astype(BF16)
            wdn_bf[...] = wdn_ref[...].astype(BF16)

        x = xbuf[slot].astype(BF16)
        for cb in range(8):
            hp = _dot(x, wup_bf[:, 256 * cb:256 * (cb + 1)]) + bup_ref[:, 256 * cb:256 * (cb + 1)]
            glu = jnp.minimum(hp[:, 0:128], SWIGLU_LIMIT)
            lin = jnp.clip(hp[:, 128:256], -SWIGLU_LIMIT, SWIGLU_LIMIT)
            act = glu * jax.nn.sigmoid(SWIGLU_ALPHA * glu) * (lin + 1.0)
            act_ref[:, 128 * cb:128 * (cb + 1)] = act.astype(BF16)
        y_ref[...] = _dot(act_ref[...], wdn_bf[...]) + bdn_ref[...]

    @pl.when(i >= nv)
    def _():
        y_ref[...] = jnp.zeros_like(y_ref)


def _experts(block_e, n_valid, src3, xs, w_up, b_up_p, w_down, b_down, perm):
    n_blocks = src3.shape[0]
    D = xs.shape[1]
    DE2 = w_up.shape[2]
    tab = src3.shape[2]
    grid_spec = pltpu.PrefetchScalarGridSpec(
        num_scalar_prefetch=2,
        grid=(n_blocks,),
        in_specs=[
            pl.BlockSpec((None, 1, tab), lambda i, be, nv: (i, 0, 0), memory_space=pltpu.SMEM),
            pl.BlockSpec((None, 1, tab), lambda i, be, nv: (jnp.minimum(i + 1, n_blocks - 1), 0, 0),
                         memory_space=pltpu.SMEM),
            pl.BlockSpec(memory_space=pl.ANY),
            pl.BlockSpec((None, D, DE2), lambda i, be, nv: (be[i], 0, 0)),
            pl.BlockSpec((None, 1, DE2), lambda i, be, nv: (be[i], 0, 0)),
            pl.BlockSpec((None, DE2 // 2, D), lambda i, be, nv: (be[i], 0, 0)),
            pl.BlockSpec((None, 1, D), lambda i, be, nv: (be[i], 0, 0)),
            pl.BlockSpec((256, 256), lambda i, be, nv: (0, 0)),
        ],
        out_specs=pl.BlockSpec((MOE_ROWS, D), lambda i, be, nv: (i, 0)),
        scratch_shapes=[
            pltpu.VMEM((2, MOE_ROWS, D), F32),
            pltpu.VMEM((D, DE2), BF16),
            pltpu.VMEM((DE2 // 2, D), BF16),
            pltpu.VMEM((MOE_ROWS, DE2 // 2), BF16),
            pltpu.SemaphoreType.DMA((2,)),
        ],
    )
    return pl.pallas_call(
        _expert_kernel,
        grid_spec=grid_spec,
        out_shape=jax.ShapeDtypeStruct((n_blocks * MOE_ROWS, D), F32),
        compiler_params=_params(("arbitrary",)),
        name="experts",
    )(block_e, n_valid, src3, src3, xs, w_up, b_up_p, w_down, b_down, perm)


def _combine_kernel(src_cur_ref, src_nxt_ref, h_ref, slot_ref, w_ref, ys_hbm, o_ref, ybuf, wmat, sem):
    i = pl.program_id(0)
    n = pl.num_programs(0)
    slot = lax.rem(i, 2)

    @pl.when(i == 0)
    def _():
        _gather_octets(src_cur_ref, SL_OCT, ys_hbm, ybuf.at[0], sem.at[0])

    @pl.when(i + 1 < n)
    def _():
        _gather_octets(src_nxt_ref, SL_OCT, ys_hbm, ybuf.at[1 - slot], sem.at[1 - slot])

    tl = h_ref.shape[0]
    for c in range(SLOTS // V7X_MXU_DIM):
        sio = lax.broadcasted_iota(I32, (tl, V7X_MXU_DIM), 1) + V7X_MXU_DIM * c
        wm = jnp.where(sio == slot_ref[:, 0:1], w_ref[:, 0:1], 0.0)
        for k in range(1, TOP_K):
            wm = jnp.where(sio == slot_ref[:, k:k + 1], w_ref[:, k:k + 1], wm)
        wmat[:, V7X_MXU_DIM * c:V7X_MXU_DIM * (c + 1)] = wm.astype(BF16)

    pltpu.make_async_copy(ys_hbm.at[pl.ds(0, SLOTS), :], ybuf.at[slot], sem.at[slot]).wait()
    o_ref[...] = h_ref[...] + _dot(wmat[...], ybuf[slot].astype(BF16))


def _combine(csrc3, h2, slot_tok, w_tok, ys):
    n_tok, D = h2.shape
    n_steps = n_tok // TL_PROJ
    tab = csrc3.shape[2]
    return pl.pallas_call(
        _combine_kernel,
        grid=(n_steps,),
        in_specs=[
            pl.BlockSpec((None, 1, tab), lambda i: (i, 0, 0), memory_space=pltpu.SMEM),
            pl.BlockSpec((None, 1, tab), lambda i: (jnp.minimum(i + 1, n_steps - 1), 0, 0),
                         memory_space=pltpu.SMEM),
            pl.BlockSpec((TL_PROJ, D), lambda i: (i, 0)),
            pl.BlockSpec((TL_PROJ, TOP_K), lambda i: (i, 0)),
            pl.BlockSpec((TL_PROJ, TOP_K), lambda i: (i, 0)),
            pl.BlockSpec(memory_space=pl.ANY),
        ],
        out_specs=pl.BlockSpec((TL_PROJ, D), lambda i: (i, 0)),
        out_shape=jax.ShapeDtypeStruct((n_tok, D), F32),
        scratch_shapes=[pltpu.VMEM((2, SLOTS, D), F32), pltpu.VMEM((TL_PROJ, SLOTS), BF16),
                        pltpu.SemaphoreType.DMA((2,))],
        compiler_params=_params(("arbitrary",)),
        name="combine",
    )(csrc3, csrc3, h2, slot_tok, w_tok, ys)


def _s5_matrices(a_re, a_im, log_dt, b_re, b_im, c_re, c_im):
    G, P = a_re.shape
    dt = jnp.exp(log_dt.astype(F32))[:, None]
    lr = a_re.astype(F32)
    li = a_im.astype(F32)
    mag = jnp.exp(lr * dt)
    abar_r = mag * jnp.cos(li * dt)
    abar_i = mag * jnp.sin(li * dt)
    den = lr * lr + li * li
    nr = abar_r - 1.0
    ni = abar_i
    coef_r = (nr * lr + ni * li) / den
    coef_i = (ni * lr - nr * li) / den
    br_ = b_re.astype(F32)
    bi_ = b_im.astype(F32)
    bbar_r = coef_r[..., None] * br_ - coef_i[..., None] * bi_
    bbar_i = coef_r[..., None] * bi_ + coef_i[..., None] * br_
    eye = jnp.eye(G // 2, dtype=F32)

    def in_mat(bb):
        bb = bb.reshape(2, G // 2, P, SSM_GROUP)
        return jnp.einsum('xgph,gk->xghkp', bb, eye).reshape(2, (G // 2) * SSM_GROUP, (G // 2) * P)

    def out_mat(cc):
        cc = cc.reshape(2, G // 2, SSM_GROUP, P)
        return jnp.einsum('xghp,gk->xgpkh', cc, eye).reshape(2, (G // 2) * P, (G // 2) * SSM_GROUP)

    bmat = jnp.concatenate([in_mat(bbar_r), in_mat(bbar_i)], axis=-1).astype(BF16)
    cmat = jnp.concatenate([out_mat(c_re.astype(F32)), -out_mat(c_im.astype(F32))], axis=1).astype(BF16)
    return bmat, abar_r.reshape(1, G * P), abar_i.reshape(1, G * P), cmat


def _pad_lanes(t):
    n, w = t.shape
    wp = ((w + V7X_LANES - 1) // V7X_LANES) * V7X_LANES
    return jnp.pad(t, ((0, 0), (0, wp - w))).reshape(n, 1, wp)


def kernel(x, norm1_g, w_in, b_gate, ssm_a_re, ssm_a_im, ssm_log_dt, ssm_b_re, ssm_b_im, ssm_c_re, ssm_c_im, ssm_d, ssm_w_glu, ssm_b_glu, q_norm_g, k_norm_g, lambda_q1, lambda_k1, lambda_q2, lambda_k2, subln_g, w_proj_ssm, w_proj_att, w_out, norm2_g, w_router, b_router, w_up, b_up, w_down, b_down):
    B, L, D = x.shape
    assert B == V7X_SUBLANES and D == 1024 and L % TL_PROJ == 0 and w_in.shape[0] == 1
    n_tok = B * L
    n_t = n_tok // TL_PROJ
    assert n_t <= V7X_LANES
    l = 0

    scale = ATT_SUB_DIM ** -0.5
    qg = (jnp.tile(q_norm_g[l].astype(F32), 2 * ATT_HEADS) * (scale * math.log2(math.e))).reshape(1, 512)
    kg = jnp.tile(k_norm_g[l].astype(F32), 2 * ATT_HEADS).reshape(1, 512)
    blk = jnp.arange(256) // ATT_SUB_DIM
    ones_bd = (blk[:, None] == blk[None, :]).astype(BF16)
    u_tm, q, k, vt, gates = _inproj(x, norm1_g[l].reshape(1, D), w_in[l].astype(BF16),
                                    b_gate[l].reshape(1, 2 * D), qg, kg, ones_bd)

    bmat, ar, ai, cmat = _s5_matrices(ssm_a_re[l], ssm_a_im[l], ssm_log_dt[l], ssm_b_re[l], ssm_b_im[l],
                                      ssm_c_re[l], ssm_c_im[l])
    y_ssm = _s5(u_tm.reshape(L * B, 512), bmat, ar, ai, cmat, ssm_d[l].reshape(1, 512).astype(F32),
                ssm_w_glu[l].astype(BF16), ssm_b_glu[l].reshape(1, 512).astype(F32))

    lam4 = jnp.stack([lambda_q1[l], lambda_k1[l], lambda_q2[l], lambda_k2[l]]).astype(F32)
    sg = jnp.tile(subln_g[l].astype(F32), ATT_HEADS).reshape(1, 512)
    y_att = _attn(q, k, vt, lam4, sg)

    tpos = jnp.arange(TL_PROJ)
    sut = (tpos[:, None] < tpos[None, :]).astype(BF16)
    epos = jnp.arange(N_EXPERTS)
    lt = (epos[None, :] < epos[:, None]).astype(BF16)
    h, xs, slot, top_w, oct_tab = _mix(
        x, y_ssm.reshape(L, B * 512), y_att, gates,
        w_proj_ssm[l].astype(BF16), w_proj_att[l].astype(BF16), w_out[l].astype(BF16),
        norm2_g[l].reshape(1, D), w_router[l].T.astype(BF16), b_router[l].reshape(N_EXPERTS, 1).astype(F32),
        sut, lt)

    n_blocks = (n_tok * TOP_K + n_t * N_EXPERTS * (OCT - 1) + MOE_ROWS - 1) // MOE_ROWS + N_EXPERTS
    blk_e, n_valid, src, csrc = _moe_tables(oct_tab, n_t, n_blocks)

    de2 = w_up.shape[-1]
    b_up_p = b_up[l].reshape(N_EXPERTS, de2 // 256, 128, 2).transpose(0, 1, 3, 2).reshape(N_EXPERTS, 1, de2)
    pr = jnp.arange(256)
    col_src = jnp.where(pr < 128, 2 * pr, 2 * (pr - 128) + 1)
    perm = (jnp.arange(256)[:, None] == col_src[None, :]).astype(BF16)
    ys = _experts(blk_e, n_valid.reshape(1), _pad_lanes(src), xs,
                  w_up[l], b_up_p, w_down[l], b_down[l].reshape(N_EXPERTS, 1, D), perm)

    out = _combine(_pad_lanes(csrc), h.reshape(n_tok, D), slot.T, top_w.T, ys)
    return out.reshape(B, L, D)
```

```python
import math

import jax
import jax.numpy as jnp
from jax import lax
from jax.experimental import pallas as pl
from jax.experimental.pallas import tpu as pltpu

F32 = jnp.float32
BF16 = jnp.bfloat16
I32 = jnp.int32

RMS_EPS = 1e-6
CHUNK = 64
SSM_GROUP = 16
SSM_STATE = 64
ATT_HEADS = 4
ATT_SUB_DIM = 64
ATT_V_DIM = 128
N_EXPERTS = 32
TOP_K = 4
SWIGLU_ALPHA = 1.702
SWIGLU_LIMIT = 7.0
LAMBDA_INIT = 0.8 - 0.6 * math.exp(-0.3 * 0)

V7X_SUBLANES = 8
V7X_LANES = 128
V7X_MXU_DIM = 256

TL_PROJ = 512
S5_STEPS = 64
TQ = 256
TK = 256
MOE_ROWS = 512
OCT = V7X_SUBLANES
BLK_OCT = MOE_ROWS // OCT
SLOTS = ((TL_PROJ * TOP_K + N_EXPERTS * (OCT - 1) + V7X_MXU_DIM - 1) // V7X_MXU_DIM) * V7X_MXU_DIM
SL_OCT = SLOTS // OCT
NEG = -1e30
VMEM_LIMIT = 56 * 1024 * 1024


def _dot(a, b):
    return jnp.dot(a, b, preferred_element_type=F32)


def _dot_nt(a, b):
    return lax.dot_general(a, b, (((1,), (1,)), ((), ())), preferred_element_type=F32)


def _params(sem, vmem=VMEM_LIMIT):
    return pltpu.CompilerParams(dimension_semantics=sem, vmem_limit_bytes=vmem)


def _inproj_kernel(x_ref, g1_ref, w_ref, bg_ref, qg_ref, kg_ref, ones_ref,
                   u_ref, q_ref, k_ref, vt_ref, gate_ref):
    x = x_ref[...]
    ms = jnp.mean(x * x, axis=-1, keepdims=True)
    xn = (x * lax.rsqrt(ms + RMS_EPS) * g1_ref[...]).astype(BF16)

    u_ref[...] = _dot(xn, w_ref[:, 0:512]).astype(BF16)

    def head_norm(z, g):
        sq = (z * z).astype(BF16)
        ss = jnp.concatenate([_dot(sq[:, 0:256], ones_ref[...]),
                              _dot(sq[:, 256:512], ones_ref[...])], axis=1)
        return z * lax.rsqrt(ss * (1.0 / ATT_SUB_DIM) + RMS_EPS) * g

    q_ref[...] = head_norm(_dot(xn, w_ref[:, 512:1024]), qg_ref[...]).astype(BF16)
    k_ref[...] = head_norm(_dot(xn, w_ref[:, 1024:1536]), kg_ref[...]).astype(BF16)
    vt_ref[...] = _dot(xn, w_ref[:, 1536:2048]).T.astype(BF16)
    for c in range(4):
        lo = 2048 + 512 * c
        z = _dot(xn, w_ref[:, lo:lo + 512]) + bg_ref[:, 512 * c:512 * (c + 1)]
        gate_ref[:, 512 * c:512 * (c + 1)] = jax.nn.sigmoid(z).astype(BF16)


def _inproj(x, g1, w_in_bf, b_gate, qg, kg, ones_bd):
    B, L, D = x.shape
    nl = L // TL_PROJ
    full = lambda shape: pl.BlockSpec(shape, lambda b, l: (0,) * len(shape))
    return pl.pallas_call(
        _inproj_kernel,
        grid=(B, nl),
        in_specs=[
            pl.BlockSpec((None, TL_PROJ, D), lambda b, l: (b, l, 0)),
            full((1, D)), full(w_in_bf.shape), full((1, 2 * D)),
            full((1, 512)), full((1, 512)), full((256, 256)),
        ],
        out_specs=[
            pl.BlockSpec((TL_PROJ, 512), lambda b, l: (l, b)),
            pl.BlockSpec((None, TL_PROJ, 512), lambda b, l: (b, l, 0)),
            pl.BlockSpec((None, TL_PROJ, 512), lambda b, l: (b, l, 0)),
            pl.BlockSpec((None, 512, TL_PROJ), lambda b, l: (b, 0, l)),
            pl.BlockSpec((None, TL_PROJ, 2 * D), lambda b, l: (b, l, 0)),
        ],
        out_shape=[
            jax.ShapeDtypeStruct((L, B * 512), BF16),
            jax.ShapeDtypeStruct((B, L, 512), BF16),
            jax.ShapeDtypeStruct((B, L, 512), BF16),
            jax.ShapeDtypeStruct((B, 512, L), BF16),
            jax.ShapeDtypeStruct((B, L, 2 * D), BF16),
        ],
        compiler_params=_params(("arbitrary", "arbitrary")),
        name="inproj",
    )(x, g1, w_in_bf, b_gate, qg, kg, ones_bd)


def _s5_kernel(u_ref, bm_ref, ar_ref, ai_ref, cm_ref, d_ref, wg_ref, bgl_ref,
               y_ref, hs_ref, h_ref):
    i = pl.program_id(0)

    @pl.when(i == 0)
    def _():
        h_ref[...] = jnp.zeros_like(h_ref)

    u = u_ref[...]
    for hh in range(2):
        hs_ref[:, 2048 * hh:2048 * (hh + 1)] = _dot(u[:, 256 * hh:256 * (hh + 1)], bm_ref[hh])

    for hh in range(2):
        re0 = 2048 * hh
        im0 = re0 + 1024
        ar = jnp.broadcast_to(ar_ref[:, 1024 * hh:1024 * (hh + 1)], (V7X_SUBLANES, 1024))
        ai = jnp.broadcast_to(ai_ref[:, 1024 * hh:1024 * (hh + 1)], (V7X_SUBLANES, 1024))

        def step(t, carry):
            hr, hi = carry
            r0 = pl.multiple_of(t * V7X_SUBLANES, V7X_SUBLANES)
            xr = hs_ref[pl.ds(r0, V7X_SUBLANES), re0:re0 + 1024]
            xi = hs_ref[pl.ds(r0, V7X_SUBLANES), im0:im0 + 1024]
            nr = ar * hr - ai * hi + xr
            ni = ar * hi + ai * hr + xi
            hs_ref[pl.ds(r0, V7X_SUBLANES), re0:re0 + 1024] = nr
            hs_ref[pl.ds(r0, V7X_SUBLANES), im0:im0 + 1024] = ni
            return nr, ni

        hr, hi = lax.fori_loop(0, S5_STEPS, step,
                               (h_ref[:, re0:re0 + 1024], h_ref[:, im0:im0 + 1024]), unroll=2)
        h_ref[:, re0:re0 + 1024] = hr
        h_ref[:, im0:im0 + 1024] = hi

    y = jnp.concatenate(
        [_dot(hs_ref[:, 2048 * hh:2048 * (hh + 1)].astype(BF16), cm_ref[hh]) for hh in range(2)], axis=1)
    y = y + d_ref[...] * u.astype(F32)
    y = 0.5 * y * (1.0 + jnp.tanh(math.sqrt(2.0 / math.pi) * (y + 0.044715 * (y * y * y))))
    z = _dot(y.astype(BF16), wg_ref[...]) + bgl_ref[...]
    y_ref[...] = (y * jax.nn.sigmoid(z)).astype(BF16)


def _s5(u2, bmat, ar, ai, cmat, d_skip, w_glu_bf, b_glu):
    rows = u2.shape[0]
    R = S5_STEPS * V7X_SUBLANES
    full = lambda shape: pl.BlockSpec(shape, lambda i: (0,) * len(shape))
    return pl.pallas_call(
        _s5_kernel,
        grid=(rows // R,),
        in_specs=[
            pl.BlockSpec((R, 512), lambda i: (i, 0)),
            full(bmat.shape), full(ar.shape), full(ai.shape), full(cmat.shape),
            full(d_skip.shape), full(w_glu_bf.shape), full(b_glu.shape),
        ],
        out_specs=pl.BlockSpec((R, 512), lambda i: (i, 0)),
        out_shape=jax.ShapeDtypeStruct((rows, 512), BF16),
        scratch_shapes=[pltpu.VMEM((R, 4096), F32), pltpu.VMEM((V7X_SUBLANES, 4096), F32)],
        compiler_params=_params(("arbitrary",)),
        name="s5",
    )(u2, bmat, ar, ai, cmat, d_skip, w_glu_bf, b_glu)


def _attn_kernel(q_ref, k_ref, vt_ref, lam_ref, sg_ref, o_ref, qm_ref, acc_ref, ml_ref):
    i = pl.program_id(1)
    n_chain = 2 * ATT_HEADS
    lamv = lam_ref[...]
    lam = (jnp.exp(jnp.sum(lamv[0:1] * lamv[1:2], axis=-1, keepdims=True))
           - jnp.exp(jnp.sum(lamv[2:3] * lamv[3:4], axis=-1, keepdims=True)) + LAMBDA_INIT)
    lane = lax.broadcasted_iota(I32, (TQ, 128), 1)
    key_chunk = lax.broadcasted_iota(I32, (TK, TQ), 0) // CHUNK
    qry_chunk = lax.broadcasted_iota(I32, (TK, TQ), 1) // CHUNK
    diag_ok = key_chunk <= qry_chunk

    for hd in range(ATT_HEADS):
        qh = q_ref[:, 128 * hd:128 * (hd + 1)]
        zero = jnp.zeros_like(qh)
        qm_ref[2 * hd] = jnp.where(lane < ATT_SUB_DIM, qh, zero)
        qm_ref[2 * hd + 1] = jnp.where(lane >= ATT_SUB_DIM, qh, zero)
    acc_ref[...] = jnp.zeros_like(acc_ref)
    row = lax.broadcasted_iota(I32, (2 * n_chain, TQ), 0)
    ml_ref[...] = jnp.where(row % 2 == 0, NEG, 0.0)

    def tile(j, masked):
        off = pl.multiple_of(j * TK, TK)
        sts = []
        for hd in range(ATT_HEADS):
            ks = k_ref[pl.ds(off, TK), 128 * hd:128 * (hd + 1)]
            for s in range(2):
                sts.append(_dot_nt(ks, qm_ref[2 * hd + s]))
        ps, alphas = [], []
        for c in range(n_chain):
            st = jnp.where(diag_ok, sts[c], NEG) if masked else sts[c]
            m = ml_ref[2 * c:2 * c + 1, :]
            l = ml_ref[2 * c + 1:2 * c + 2, :]
            mn = jnp.maximum(m, jnp.max(st, axis=0, keepdims=True))
            alpha = jnp.exp2(m - mn)
            p = jnp.exp2(st - mn)
            ml_ref[2 * c:2 * c + 1, :] = mn
            ml_ref[2 * c + 1:2 * c + 2, :] = alpha * l + jnp.sum(p, axis=0, keepdims=True)
            ps.append(p.astype(BF16))
            alphas.append(alpha)
        for c in range(n_chain):
            hd = c // 2
            vts = vt_ref[128 * hd:128 * (hd + 1), pl.ds(off, TK)]
            acc_ref[c] = acc_ref[c] * alphas[c] + _dot(vts, ps[c])

    def body(j, carry):
        tile(j, False)
        return carry

    lax.fori_loop(0, i, body, 0)
    tile(i, True)

    for hd in range(ATT_HEADS):
        c0 = 128 * hd
        l1 = ml_ref[4 * hd + 1:4 * hd + 2, :]
        l2 = ml_ref[4 * hd + 3:4 * hd + 4, :]
        o = acc_ref[2 * hd] / l1 - lam * (acc_ref[2 * hd + 1] / l2)
        ms = jnp.mean(o * o, axis=0, keepdims=True)
        on = (o * lax.rsqrt(ms + RMS_EPS)).T
        o_ref[:, c0:c0 + 128] = (on * sg_ref[:, c0:c0 + 128] * (1.0 - LAMBDA_INIT)).astype(BF16)


def _attn(q, k, vt, lam4, sg):
    B, L, W = q.shape
    n_chain = 2 * ATT_HEADS
    return pl.pallas_call(
        _attn_kernel,
        grid=(B, L // TQ),
        in_specs=[
            pl.BlockSpec((None, TQ, W), lambda b, i: (b, i, 0)),
            pl.BlockSpec((None, L, W), lambda b, i: (b, 0, 0)),
            pl.BlockSpec((None, W, L), lambda b, i: (b, 0, 0)),
            pl.BlockSpec((4, ATT_SUB_DIM), lambda b, i: (0, 0)),
            pl.BlockSpec((1, W), lambda b, i: (0, 0)),
        ],
        out_specs=pl.BlockSpec((None, TQ, W), lambda b, i: (b, i, 0)),
        out_shape=jax.ShapeDtypeStruct((B, L, W), BF16),
        scratch_shapes=[pltpu.VMEM((n_chain, TQ, 128), BF16),
                        pltpu.VMEM((n_chain, ATT_V_DIM, TQ), F32),
                        pltpu.VMEM((2 * n_chain, TQ), F32)],
        compiler_params=_params(("arbitrary", "arbitrary")),
        name="diff_attn",
    )(q, k, vt, lam4, sg)


def _mix_kernel(x_ref, ys_ref, ya_ref, gate_ref, wps_ref, wpa_ref, wo_ref, g2_ref, wrt_ref, br_ref,
                sut_ref, lt_ref, h_ref, xs_ref, slot_ref, w_ref, oct_ref):
    tile = pl.program_id(0) * pl.num_programs(1) + pl.program_id(1)

    D = x_ref.shape[-1]
    tl = x_ref.shape[0]
    gs = gate_ref[:, 0:D].astype(F32)
    ga = gate_ref[:, D:2 * D].astype(F32)
    mixed = gs * _dot(ys_ref[...], wps_ref[...]) + ga * _dot(ya_ref[...], wpa_ref[...])
    h = x_ref[...] + _dot(mixed.astype(BF16), wo_ref[...])
    h_ref[...] = h
    ms = jnp.mean(h * h, axis=-1, keepdims=True)
    hb = (h * lax.rsqrt(ms + RMS_EPS) * g2_ref[...]).astype(BF16)

    lg = _dot_nt(wrt_ref[...], hb) + br_ref[...]
    eio = lax.broadcasted_iota(I32, (N_EXPERTS, tl), 0)
    vals, sels = [], []
    for k in range(TOP_K):
        m = jnp.max(lg, axis=0, keepdims=True)
        idx = jnp.min(jnp.where(lg == m, eio, N_EXPERTS), axis=0, keepdims=True)
        sel = eio == idx
        vals.append(m)
        sels.append(sel)
        lg = jnp.where(sel, -jnp.inf, lg)
    ex = [jnp.exp(v - vals[0]) for v in vals]
    den = ex[0] + ex[1] + ex[2] + ex[3]
    for k in range(TOP_K):
        w_ref[k:k + 1, :] = ex[k] / den

    mtot = jnp.zeros((N_EXPERTS, tl), F32)
    for k in range(TOP_K):
        mtot = mtot + jnp.where(sels[k], 1.0, 0.0)
    pre = _dot(mtot.astype(BF16), sut_ref[...])
    cnt = jnp.sum(mtot, axis=1, keepdims=True)
    seg_oct = jnp.floor((cnt + (OCT - 1.0)) * (1.0 / OCT))
    seg_b = jnp.broadcast_to(seg_oct, (N_EXPERTS, V7X_LANES))
    start_oct = _dot(lt_ref[...], seg_b.astype(BF16))
    base = pre + start_oct[:, 0:1] * float(OCT)
    slots = []
    for k in range(TOP_K):
        s_k = jnp.sum(jnp.where(sels[k], base, 0.0), axis=0, keepdims=True).astype(I32)
        slot_ref[k:k + 1, :] = s_k
        slots.append(s_k)

    for c in range(SLOTS // V7X_MXU_DIM):
        sio = lax.broadcasted_iota(I32, (V7X_MXU_DIM, tl), 0) + V7X_MXU_DIM * c
        p = jnp.where(sio == slots[0], 1.0, 0.0)
        for k in range(1, TOP_K):
            p = jnp.where(sio == slots[k], 1.0, p)
        xs_ref[V7X_MXU_DIM * c:V7X_MXU_DIM * (c + 1), :] = _dot(p.astype(BF16), hb)

    lane = lax.broadcasted_iota(I32, (N_EXPERTS, V7X_LANES), 1)

    @pl.when(tile == 0)
    def _():
        oct_ref[...] = jnp.zeros_like(oct_ref)

    oct_ref[...] = jnp.where(lane == tile, seg_b, oct_ref[...])


def _mix(x, ys_tm, ya, gates, wps, wpa, wo, g2, wrt, br, sut, lt):
    B, L, D = x.shape
    nl = L // TL_PROJ
    n_tok = B * L
    full = lambda shape: pl.BlockSpec(shape, lambda b, l: (0,) * len(shape))
    tok_spec = pl.BlockSpec((TOP_K, TL_PROJ), lambda b, l: (0, b * nl + l))
    return pl.pallas_call(
        _mix_kernel,
        grid=(B, nl),
        in_specs=[
            pl.BlockSpec((None, TL_PROJ, D), lambda b, l: (b, l, 0)),
            pl.BlockSpec((TL_PROJ, 512), lambda b, l: (l, b)),
            pl.BlockSpec((None, TL_PROJ, 512), lambda b, l: (b, l, 0)),
            pl.BlockSpec((None, TL_PROJ, 2 * D), lambda b, l: (b, l, 0)),
            full(wps.shape), full(wpa.shape), full(wo.shape), full((1, D)),
            full(wrt.shape), full(br.shape), full(sut.shape), full(lt.shape),
        ],
        out_specs=[
            pl.BlockSpec((None, TL_PROJ, D), lambda b, l: (b, l, 0)),
            pl.BlockSpec((SLOTS, D), lambda b, l: (b * nl + l, 0)),
            tok_spec, tok_spec,
            full((N_EXPERTS, V7X_LANES)),
        ],
        out_shape=[
            jax.ShapeDtypeStruct((B, L, D), F32),
            jax.ShapeDtypeStruct((B * nl * SLOTS, D), F32),
            jax.ShapeDtypeStruct((TOP_K, n_tok), I32),
            jax.ShapeDtypeStruct((TOP_K, n_tok), F32),
            jax.ShapeDtypeStruct((N_EXPERTS, V7X_LANES), F32),
        ],
        compiler_params=_params(("arbitrary", "arbitrary")),
        name="mix_router",
    )(x, ys_tm, ya, gates, wps, wpa, wo, g2, wrt, br, sut, lt)


def _moe_tables(oct_tab, n_t, n_blocks):
    E = N_EXPERTS
    O = oct_tab[:, :n_t].astype(I32)
    lstart = jnp.cumsum(O, axis=0) - O
    ecum = jnp.cumsum(O, axis=1) - O
    tot = jnp.sum(O, axis=1)
    nb = (tot + BLK_OCT - 1) // BLK_OCT
    bend = jnp.cumsum(nb)
    bstart = bend - nb
    n_valid = bend[-1]
    bi = jnp.arange(n_blocks, dtype=I32)
    er = jnp.arange(E, dtype=I32)
    tr = jnp.arange(n_t, dtype=I32)
    be = jnp.minimum(jnp.sum((bi[:, None] >= bend[None, :]).astype(I32), axis=1), E - 1)
    last_e = jnp.sum(jnp.where(bi == jnp.maximum(n_valid - 1, 0), be, 0))
    be = jnp.where(bi < n_valid, be, last_e)
    oh_e = be[:, None] == er[None, :]
    pick = lambda tab: jnp.sum(jnp.where(oh_e[:, :, None], tab[None, :, :], 0), axis=1)
    ecum_i, o_i, lst_i = pick(ecum), pick(O), pick(lstart)
    tot_i = jnp.sum(jnp.where(oh_e, tot[None, :], 0), axis=1)
    bst_i = jnp.sum(jnp.where(oh_e, bstart[None, :], 0), axis=1)
    g = (bi - bst_i)[:, None] * BLK_OCT + jnp.arange(BLK_OCT, dtype=I32)[None, :]
    tau = jnp.minimum(jnp.sum((g[:, :, None] >= (ecum_i + o_i)[:, None, :]).astype(I32), axis=2), n_t - 1)
    off = tr[None, :] * SL_OCT + lst_i - ecum_i
    src = jnp.sum(jnp.where(tau[:, :, None] == tr[None, None, :], off[:, None, :], 0), axis=2) + g
    src = jnp.where((g < tot_i[:, None]) & (bi[:, None] < n_valid), src, 0)
    s = jnp.arange(SL_OCT, dtype=I32)
    lend_t = (lstart + O).T
    e_s = jnp.minimum(jnp.sum((s[None, :, None] >= lend_t[:, None, :]).astype(I32), axis=2), E - 1)
    offc = (bstart[:, None] * BLK_OCT + ecum - lstart).T
    csrc = jnp.sum(jnp.where(e_s[:, :, None] == er[None, None, :], offc[:, None, :], 0), axis=2) + s[None, :]
    csrc = jnp.where(s[None, :] < jnp.sum(O, axis=0)[:, None], csrc, 0)
    return be, n_valid.astype(I32), src, csrc


def _gather_octets(src_ref, n_oct, src_hbm, dst, sem):
    for q in range(n_oct):
        row = pl.multiple_of(src_ref[0, q] * OCT, OCT)
        pltpu.make_async_copy(src_hbm.at[pl.ds(row, OCT), :], dst.at[pl.ds(OCT * q, OCT), :], sem).start()


def _expert_kernel(be_ref, nv_ref, src_cur_ref, src_nxt_ref, xs_hbm, wup_ref, bup_ref, wdn_ref, bdn_ref,
                   perm_ref, y_ref, xbuf, wup_bf, wdn_bf, act_ref, sem):
    i = pl.program_id(0)
    nv = nv_ref[0]
    slot = lax.rem(i, 2)

    @pl.when(i == 0)
    def _():
        _gather_octets(src_cur_ref, BLK_OCT, xs_hbm, xbuf.at[0], sem.at[0])

    @pl.when(i + 1 < nv)
    def _():
        _gather_octets(src_nxt_ref, BLK_OCT, xs_hbm, xbuf.at[1 - slot], sem.at[1 - slot])

    @pl.when(i < nv)
    def _():
        pltpu.make_async_copy(xs_hbm.at[pl.ds(0, MOE_ROWS), :], xbuf.at[slot], sem.at[slot]).wait()

        changed = jnp.logical_or(i == 0, be_ref[i] != be_ref[jnp.maximum(i - 1, 0)])

        @pl.when(changed)
        def _():
            for cb in range(8):
                blk = wup_ref[:, 256 * cb:256 * (cb + 1)].astype(BF16)
                wup_bf[:, 256 * cb:256 * (cb + 1)] = _dot(blk, perm_ref[...]).astype(BF16)
            wdn_bf[...] = wdn_ref[...].astype(BF16)

        x = xbuf[slot].astype(BF16)
        for cb in range(8):
            hp = _dot(x, wup_bf[:, 256 * cb:256 * (cb + 1)]) + bup_ref[:, 256 * cb:256 * (cb + 1)]
            glu = jnp.minimum(hp[:, 0:128], SWIGLU_LIMIT)
            lin = jnp.clip(hp[:, 128:256], -SWIGLU_LIMIT, SWIGLU_LIMIT)
            act = glu * jax.nn.sigmoid(SWIGLU_ALPHA * glu) * (lin + 1.0)
            act_ref[:, 128 * cb:128 * (cb + 1)] = act.astype(BF16)
        y_ref[...] = _dot(act_ref[...], wdn_bf[...]) + bdn_ref[...]

    @pl.when(i >= nv)
    def _():
        y_ref[...] = jnp.zeros_like(y_ref)


def _experts(block_e, n_valid, src3, xs, w_up, b_up_p, w_down, b_down, perm):
    n_blocks = src3.shape[0]
    D = xs.shape[1]
    DE2 = w_up.shape[2]
    tab = src3.shape[2]
    grid_spec = pltpu.PrefetchScalarGridSpec(
        num_scalar_prefetch=2,
        grid=(n_blocks,),
        in_specs=[
            pl.BlockSpec((None, 1, tab), lambda i, be, nv: (i, 0, 0), memory_space=pltpu.SMEM),
            pl.BlockSpec((None, 1, tab), lambda i, be, nv: (jnp.minimum(i + 1, n_blocks - 1), 0, 0),
                         memory_space=pltpu.SMEM),
            pl.BlockSpec(memory_space=pl.ANY),
            pl.BlockSpec((None, D, DE2), lambda i, be, nv: (be[i], 0, 0)),
            pl.BlockSpec((None, 1, DE2), lambda i, be, nv: (be[i], 0, 0)),
            pl.BlockSpec((None, DE2 // 2, D), lambda i, be, nv: (be[i], 0, 0)),
            pl.BlockSpec((None, 1, D), lambda i, be, nv: (be[i], 0, 0)),
            pl.BlockSpec((256, 256), lambda i, be, nv: (0, 0)),
        ],
        out_specs=pl.BlockSpec((MOE_ROWS, D), lambda i, be, nv: (i, 0)),
        scratch_shapes=[
            pltpu.VMEM((2, MOE_ROWS, D), F32),
            pltpu.VMEM((D, DE2), BF16),
            pltpu.VMEM((DE2 // 2, D), BF16),
            pltpu.VMEM((MOE_ROWS, DE2 // 2), BF16),
            pltpu.SemaphoreType.DMA((2,)),
        ],
    )
    return pl.pallas_call(
        _expert_kernel,
        grid_spec=grid_spec,
        out_shape=jax.ShapeDtypeStruct((n_blocks * MOE_ROWS, D), F32),
        compiler_params=_params(("arbitrary",)),
        name="experts",
    )(block_e, n_valid, src3, src3, xs, w_up, b_up_p, w_down, b_down, perm)


def _combine_kernel(src_cur_ref, src_nxt_ref, h_ref, slot_ref, w_ref, ys_hbm, o_ref, ybuf, wmat, sem):
    i = pl.program_id(0)
    n = pl.num_programs(0)
    slot = lax.rem(i, 2)

    @pl.when(i == 0)
    def _():
        _gather_octets(src_cur_ref, SL_OCT, ys_hbm, ybuf.at[0], sem.at[0])

    @pl.when(i + 1 < n)
    def _():
        _gather_octets(src_nxt_ref, SL_OCT, ys_hbm, ybuf.at[1 - slot], sem.at[1 - slot])

    tl = h_ref.shape[0]
    for c in range(SLOTS // V7X_MXU_DIM):
        sio = lax.broadcasted_iota(I32, (tl, V7X_MXU_DIM), 1) + V7X_MXU_DIM * c
        wm = jnp.where(sio == slot_ref[:, 0:1], w_ref[:, 0:1], 0.0)
        for k in range(1, TOP_K):
            wm = jnp.where(sio == slot_ref[:, k:k + 1], w_ref[:, k:k + 1], wm)
        wmat[:, V7X_MXU_DIM * c:V7X_MXU_DIM * (c + 1)] = wm.astype(BF16)

    pltpu.make_async_copy(ys_hbm.at[pl.ds(0, SLOTS), :], ybuf.at[slot], sem.at[slot]).wait()
    o_ref[...] = h_ref[...] + _dot(wmat[...], ybuf[slot].astype(BF16))


def _combine(csrc3, h2, slot_tok, w_tok, ys):
    n_tok, D = h2.shape
    n_steps = n_tok // TL_PROJ
    tab = csrc3.shape[2]
    return pl.pallas_call(
        _combine_kernel,
        grid=(n_steps,),
        in_specs=[
            pl.BlockSpec((None, 1, tab), lambda i: (i, 0, 0), memory_space=pltpu.SMEM),
            pl.BlockSpec((None, 1, tab), lambda i: (jnp.minimum(i + 1, n_steps - 1), 0, 0),
                         memory_space=pltpu.SMEM),
            pl.BlockSpec((TL_PROJ, D), lambda i: (i, 0)),
            pl.BlockSpec((TL_PROJ, TOP_K), lambda i: (i, 0)),
            pl.BlockSpec((TL_PROJ, TOP_K), lambda i: (i, 0)),
            pl.BlockSpec(memory_space=pl.ANY),
        ],
        out_specs=pl.BlockSpec((TL_PROJ, D), lambda i: (i, 0)),
        out_shape=jax.ShapeDtypeStruct((n_tok, D), F32),
        scratch_shapes=[pltpu.VMEM((2, SLOTS, D), F32), pltpu.VMEM((TL_PROJ, SLOTS), BF16),
                        pltpu.SemaphoreType.DMA((2,))],
        compiler_params=_params(("arbitrary",)),
        name="combine",
    )(csrc3, csrc3, h2, slot_tok, w_tok, ys)


def _s5_matrices(a_re, a_im, log_dt, b_re, b_im, c_re, c_im):
    G, P = a_re.shape
    dt = jnp.exp(log_dt.astype(F32))[:, None]
    lr = a_re.astype(F32)
    li = a_im.astype(F32)
    mag = jnp.exp(lr * dt)
    abar_r = mag * jnp.cos(li * dt)
    abar_i = mag * jnp.sin(li * dt)
    den = lr * lr + li * li
    nr = abar_r - 1.0
    ni = abar_i
    coef_r = (nr * lr + ni * li) / den
    coef_i = (ni * lr - nr * li) / den
    br_ = b_re.astype(F32)
    bi_ = b_im.astype(F32)
    bbar_r = coef_r[..., None] * br_ - coef_i[..., None] * bi_
    bbar_i = coef_r[..., None] * bi_ + coef_i[..., None] * br_
    eye = jnp.eye(G // 2, dtype=F32)

    def in_mat(bb):
        bb = bb.reshape(2, G // 2, P, SSM_GROUP)
        return jnp.einsum('xgph,gk->xghkp', bb, eye).reshape(2, (G // 2) * SSM_GROUP, (G // 2) * P)

    def out_mat(cc):
        cc = cc.reshape(2, G // 2, SSM_GROUP, P)
        return jnp.einsum('xghp,gk->xgpkh', cc, eye).reshape(2, (G // 2) * P, (G // 2) * SSM_GROUP)

    bmat = jnp.concatenate([in_mat(bbar_r), in_mat(bbar_i)], axis=-1).astype(BF16)
    cmat = jnp.concatenate([out_mat(c_re.astype(F32)), -out_mat(c_im.astype(F32))], axis=1).astype(BF16)
    return bmat, abar_r.reshape(1, G * P), abar_i.reshape(1, G * P), cmat


def _pad_lanes(t):
    n, w = t.shape
    wp = ((w + V7X_LANES - 1) // V7X_LANES) * V7X_LANES
    return jnp.pad(t, ((0, 0), (0, wp - w))).reshape(n, 1, wp)


def kernel(x, norm1_g, w_in, b_gate, ssm_a_re, ssm_a_im, ssm_log_dt, ssm_b_re, ssm_b_im, ssm_c_re, ssm_c_im, ssm_d, ssm_w_glu, ssm_b_glu, q_norm_g, k_norm_g, lambda_q1, lambda_k1, lambda_q2, lambda_k2, subln_g, w_proj_ssm, w_proj_att, w_out, norm2_g, w_router, b_router, w_up, b_up, w_down, b_down):
    B, L, D = x.shape
    assert B == V7X_SUBLANES and D == 1024 and L % TL_PROJ == 0 and w_in.shape[0] == 1
    n_tok = B * L
    n_t = n_tok // TL_PROJ
    assert n_t <= V7X_LANES
    l = 0

    scale = ATT_SUB_DIM ** -0.5
    qg = (jnp.tile(q_norm_g[l].astype(F32), 2 * ATT_HEADS) * (scale * math.log2(math.e))).reshape(1, 512)
    kg = jnp.tile(k_norm_g[l].astype(F32), 2 * ATT_HEADS).reshape(1, 512)
    blk = jnp.arange(256) // ATT_SUB_DIM
    ones_bd = (blk[:, None] == blk[None, :]).astype(BF16)
    u_tm, q, k, vt, gates = _inproj(x, norm1_g[l].reshape(1, D), w_in[l].astype(BF16),
                                    b_gate[l].reshape(1, 2 * D), qg, kg, ones_bd)

    bmat, ar, ai, cmat = _s5_matrices(ssm_a_re[l], ssm_a_im[l], ssm_log_dt[l], ssm_b_re[l], ssm_b_im[l],
                                      ssm_c_re[l], ssm_c_im[l])
    y_ssm = _s5(u_tm.reshape(L * B, 512), bmat, ar, ai, cmat, ssm_d[l].reshape(1, 512).astype(F32),
                ssm_w_glu[l].astype(BF16), ssm_b_glu[l].reshape(1, 512).astype(F32))

    lam4 = jnp.stack([lambda_q1[l], lambda_k1[l], lambda_q2[l], lambda_k2[l]]).astype(F32)
    sg = jnp.tile(subln_g[l].astype(F32), ATT_HEADS).reshape(1, 512)
    y_att = _attn(q, k, vt, lam4, sg)

    tpos = jnp.arange(TL_PROJ)
    sut = (tpos[:, None] < tpos[None, :]).astype(BF16)
    epos = jnp.arange(N_EXPERTS)
    lt = (epos[None, :] < epos[:, None]).astype(BF16)
    h, xs, slot, top_w, oct_tab = _mix(
        x, y_ssm.reshape(L, B * 512), y_att, gates,
        w_proj_ssm[l].astype(BF16), w_proj_att[l].astype(BF16), w_out[l].astype(BF16),
        norm2_g[l].reshape(1, D), w_router[l].T.astype(BF16), b_router[l].reshape(N_EXPERTS, 1).astype(F32),
        sut, lt)

    n_blocks = (n_tok * TOP_K + n_t * N_EXPERTS * (OCT - 1) + MOE_ROWS - 1) // MOE_ROWS + N_EXPERTS
    blk_e, n_valid, src, csrc = _moe_tables(oct_tab, n_t, n_blocks)

    de2 = w_up.shape[-1]
    b_up_p = b_up[l].reshape(N_EXPERTS, de2 // 256, 128, 2).transpose(0, 1, 3, 2).reshape(N_EXPERTS, 1, de2)
    pr = jnp.arange(256)
    col_src = jnp.where(pr < 128, 2 * pr, 2 * (pr - 128) + 1)
    perm = (jnp.arange(256)[:, None] == col_src[None, :]).astype(BF16)
    ys = _experts(blk_e, n_valid.reshape(1), _pad_lanes(src), xs,
                  w_up[l], b_up_p, w_down[l], b_down[l].reshape(N_EXPERTS, 1, D), perm)

    out = _combine(_pad_lanes(csrc), h.reshape(n_tok, D), slot.T, top_w.T, ys)
    return out.reshape(B, L, D)
```

```python
import math

import jax
import jax.numpy as jnp
from jax import lax
from jax.experimental import pallas as pl
from jax.experimental.pallas import tpu as pltpu

F32 = jnp.float32
BF16 = jnp.bfloat16
I32 = jnp.int32

RMS_EPS = 1e-6
CHUNK = 64
SSM_GROUP = 16
SSM_STATE = 64
ATT_HEADS = 4
ATT_SUB_DIM = 64
ATT_V_DIM = 128
N_EXPERTS = 32
TOP_K = 4
SWIGLU_ALPHA = 1.702
SWIGLU_LIMIT = 7.0
LAMBDA_INIT = 0.8 - 0.6 * math.exp(-0.3 * 0)

V7X_SUBLANES = 8
V7X_LANES = 128
V7X_MXU_DIM = 256

TL_PROJ = 512
S5_STEPS = 64
TQ = 256
TK = 256
MOE_ROWS = 512
OCT = V7X_SUBLANES
BLK_OCT = MOE_ROWS // OCT
SLOTS = ((TL_PROJ * TOP_K + N_EXPERTS * (OCT - 1) + V7X_MXU_DIM - 1) // V7X_MXU_DIM) * V7X_MXU_DIM
SL_OCT = SLOTS // OCT
NEG = -1e30
VMEM_LIMIT = 56 * 1024 * 1024


def _dot(a, b):
    return jnp.dot(a, b, preferred_element_type=F32)


def _dot_nt(a, b):
    return lax.dot_general(a, b, (((1,), (1,)), ((), ())), preferred_element_type=F32)


def _params(sem, vmem=VMEM_LIMIT):
    return pltpu.CompilerParams(dimension_semantics=sem, vmem_limit_bytes=vmem)


def _inproj_kernel(x_ref, g1_ref, w_ref, bg_ref, qg_ref, kg_ref, ones_ref,
                   u_ref, q_ref, k_ref, vt_ref, gate_ref):
    x = x_ref[...]
    ms = jnp.mean(x * x, axis=-1, keepdims=True)
    xn = (x * lax.rsqrt(ms + RMS_EPS) * g1_ref[...]).astype(BF16)

    u_ref[...] = _dot(xn, w_ref[:, 0:512]).astype(BF16)

    def head_norm(z, g):
        sq = (z * z).astype(BF16)
        ss = jnp.concatenate([_dot(sq[:, 0:256], ones_ref[...]),
                              _dot(sq[:, 256:512], ones_ref[...])], axis=1)
        return z * lax.rsqrt(ss * (1.0 / ATT_SUB_DIM) + RMS_EPS) * g

    q_ref[...] = head_norm(_dot(xn, w_ref[:, 512:1024]), qg_ref[...]).astype(BF16)
    k_ref[...] = head_norm(_dot(xn, w_ref[:, 1024:1536]), kg_ref[...]).astype(BF16)
    vt_ref[...] = _dot(xn, w_ref[:, 1536:2048]).T.astype(BF16)
    for c in range(4):
        lo = 2048 + 512 * c
        z = _dot(xn, w_ref[:, lo:lo + 512]) + bg_ref[:, 512 * c:512 * (c + 1)]
        gate_ref[:, 512 * c:512 * (c + 1)] = jax.nn.sigmoid(z).astype(BF16)


def _inproj(x, g1, w_in_bf, b_gate, qg, kg, ones_bd):
    B, L, D = x.shape
    nl = L // TL_PROJ
    full = lambda shape: pl.BlockSpec(shape, lambda b, l: (0,) * len(shape))
    return pl.pallas_call(
        _inproj_kernel,
        grid=(B, nl),
        in_specs=[
            pl.BlockSpec((None, TL_PROJ, D), lambda b, l: (b, l, 0)),
            full((1, D)), full(w_in_bf.shape), full((1, 2 * D)),
            full((1, 512)), full((1, 512)), full((256, 256)),
        ],
        out_specs=[
            pl.BlockSpec((TL_PROJ, 512), lambda b, l: (l, b)),
            pl.BlockSpec((None, TL_PROJ, 512), lambda b, l: (b, l, 0)),
            pl.BlockSpec((None, TL_PROJ, 512), lambda b, l: (b, l, 0)),
            pl.BlockSpec((None, 512, TL_PROJ), lambda b, l: (b, 0, l)),
            pl.BlockSpec((None, TL_PROJ, 2 * D), lambda b, l: (b, l, 0)),
        ],
        out_shape=[
            jax.ShapeDtypeStruct((L, B * 512), BF16),
            jax.ShapeDtypeStruct((B, L, 512), BF16),
            jax.ShapeDtypeStruct((B, L, 512), BF16),
            jax.ShapeDtypeStruct((B, 512, L), BF16),
            jax.ShapeDtypeStruct((B, L, 2 * D), BF16),
        ],
        compiler_params=_params(("arbitrary", "arbitrary")),
        name="inproj",
    )(x, g1, w_in_bf, b_gate, qg, kg, ones_bd)


def _s5_kernel(u_ref, bm_ref, ar_ref, ai_ref, cm_ref, d_ref, wg_ref, bgl_ref,
               y_ref, hs_ref, h_ref):
    i = pl.program_id(0)

    @pl.when(i == 0)
    def _():
        h_ref[...] = jnp.zeros_like(h_ref)

    u = u_ref[...]
    for hh in range(2):
        hs_ref[:, 2048 * hh:2048 * (hh + 1)] = _dot(u[:, 256 * hh:256 * (hh + 1)], bm_ref[hh])

    for hh in range(2):
        re0 = 2048 * hh
        im0 = re0 + 1024
        ar = jnp.broadcast_to(ar_ref[:, 1024 * hh:1024 * (hh + 1)], (V7X_SUBLANES, 1024))
        ai = jnp.broadcast_to(ai_ref[:, 1024 * hh:1024 * (hh + 1)], (V7X_SUBLANES, 1024))

        def step(t, carry):
            hr, hi = carry
            r0 = pl.multiple_of(t * V7X_SUBLANES, V7X_SUBLANES)
            xr = hs_ref[pl.ds(r0, V7X_SUBLANES), re0:re0 + 1024]
            xi = hs_ref[pl.ds(r0, V7X_SUBLANES), im0:im0 + 1024]
            nr = ar * hr - ai * hi + xr
            ni = ar * hi + ai * hr + xi
            hs_ref[pl.ds(r0, V7X_SUBLANES), re0:re0 + 1024] = nr
            hs_ref[pl.ds(r0, V7X_SUBLANES), im0:im0 + 1024] = ni
            return nr, ni

        hr, hi = lax.fori_loop(0, S5_STEPS, step,
                               (h_ref[:, re0:re0 + 1024], h_ref[:, im0:im0 + 1024]), unroll=2)
        h_ref[:, re0:re0 + 1024] = hr
        h_ref[:, im0:im0 + 1024] = hi

    y = jnp.concatenate(
        [_dot(hs_ref[:, 2048 * hh:2048 * (hh + 1)].astype(BF16), cm_ref[hh]) for hh in range(2)], axis=1)
    y = y + d_ref[...] * u.astype(F32)
    y = 0.5 * y * (1.0 + jnp.tanh(math.sqrt(2.0 / math.pi) * (y + 0.044715 * (y * y * y))))
    z = _dot(y.astype(BF16), wg_ref[...]) + bgl_ref[...]
    y_ref[...] = (y * jax.nn.sigmoid(z)).astype(BF16)


def _s5(u2, bmat, ar, ai, cmat, d_skip, w_glu_bf, b_glu):
    rows = u2.shape[0]
    R = S5_STEPS * V7X_SUBLANES
    full = lambda shape: pl.BlockSpec(shape, lambda i: (0,) * len(shape))
    return pl.pallas_call(
        _s5_kernel,
        grid=(rows // R,),
        in_specs=[
            pl.BlockSpec((R, 512), lambda i: (i, 0)),
            full(bmat.shape), full(ar.shape), full(ai.shape), full(cmat.shape),
            full(d_skip.shape), full(w_glu_bf.shape), full(b_glu.shape),
        ],
        out_specs=pl.BlockSpec((R, 512), lambda i: (i, 0)),
        out_shape=jax.ShapeDtypeStruct((rows, 512), BF16),
        scratch_shapes=[pltpu.VMEM((R, 4096), F32), pltpu.VMEM((V7X_SUBLANES, 4096), F32)],
        compiler_params=_params(("arbitrary",)),
        name="s5",
    )(u2, bmat, ar, ai, cmat, d_skip, w_glu_bf, b_glu)


def _attn_kernel(q_ref, k_ref, vt_ref, lam_ref, sg_ref, o_ref, qm_ref, acc_ref, ml_ref):
    i = pl.program_id(1)
    n_chain = 2 * ATT_HEADS
    lamv = lam_ref[...]
    lam = (jnp.exp(jnp.sum(lamv[0:1] * lamv[1:2], axis=-1, keepdims=True))
           - jnp.exp(jnp.sum(lamv[2:3] * lamv[3:4], axis=-1, keepdims=True)) + LAMBDA_INIT)
    lane = lax.broadcasted_iota(I32, (TQ, 128), 1)
    key_chunk = lax.broadcasted_iota(I32, (TK, TQ), 0) // CHUNK
    qry_chunk = lax.broadcasted_iota(I32, (TK, TQ), 1) // CHUNK
    diag_ok = key_chunk <= qry_chunk

    for hd in range(ATT_HEADS):
        qh = q_ref[:, 128 * hd:128 * (hd + 1)]
        zero = jnp.zeros_like(qh)
        qm_ref[2 * hd] = jnp.where(lane < ATT_SUB_DIM, qh, zero)
        qm_ref[2 * hd + 1] = jnp.where(lane >= ATT_SUB_DIM, qh, zero)
    acc_ref[...] = jnp.zeros_like(acc_ref)
    row = lax.broadcasted_iota(I32, (2 * n_chain, TQ), 0)
    ml_ref[...] = jnp.where(row % 2 == 0, NEG, 0.0)

    def tile(j, masked):
        off = pl.multiple_of(j * TK, TK)
        sts = []
        for hd in range(ATT_HEADS):
            ks = k_ref[pl.ds(off, TK), 128 * hd:128 * (hd + 1)]
            for s in range(2):
                sts.append(_dot_nt(ks, qm_ref[2 * hd + s]))
        ps, alphas = [], []
        for c in range(n_chain):
            st = jnp.where(diag_ok, sts[c], NEG) if masked else sts[c]
            m = ml_ref[2 * c:2 * c + 1, :]
            l = ml_ref[2 * c + 1:2 * c + 2, :]
            mn = jnp.maximum(m, jnp.max(st, axis=0, keepdims=True))
            alpha = jnp.exp2(m - mn)
            p = jnp.exp2(st - mn)
            ml_ref[2 * c:2 * c + 1, :] = mn
            ml_ref[2 * c + 1:2 * c + 2, :] = alpha * l + jnp.sum(p, axis=0, keepdims=True)
            ps.append(p.astype(BF16))
            alphas.append(alpha)
        for c in range(n_chain):
            hd = c // 2
            vts = vt_ref[128 * hd:128 * (hd + 1), pl.ds(off, TK)]
            acc_ref[c] = acc_ref[c] * alphas[c] + _dot(vts, ps[c])

    def body(j, carry):
        tile(j, False)
        return carry

    lax.fori_loop(0, i, body, 0)
    tile(i, True)

    for hd in range(ATT_HEADS):
        c0 = 128 * hd
        l1 = ml_ref[4 * hd + 1:4 * hd + 2, :]
        l2 = ml_ref[4 * hd + 3:4 * hd + 4, :]
        o = acc_ref[2 * hd] / l1 - lam * (acc_ref[2 * hd + 1] / l2)
        ms = jnp.mean(o * o, axis=0, keepdims=True)
        on = (o * lax.rsqrt(ms + RMS_EPS)).T
        o_ref[:, c0:c0 + 128] = (on * sg_ref[:, c0:c0 + 128] * (1.0 - LAMBDA_INIT)).astype(BF16)


def _attn(q, k, vt, lam4, sg):
    B, L, W = q.shape
    n_chain = 2 * ATT_HEADS
    return pl.pallas_call(
        _attn_kernel,
        grid=(B, L // TQ),
        in_specs=[
            pl.BlockSpec((None, TQ, W), lambda b, i: (b, i, 0)),
            pl.BlockSpec((None, L, W), lambda b, i: (b, 0, 0)),
            pl.BlockSpec((None, W, L), lambda b, i: (b, 0, 0)),
            pl.BlockSpec((4, ATT_SUB_DIM), lambda b, i: (0, 0)),
            pl.BlockSpec((1, W), lambda b, i: (0, 0)),
        ],
        out_specs=pl.BlockSpec((None, TQ, W), lambda b, i: (b, i, 0)),
        out_shape=jax.ShapeDtypeStruct((B, L, W), BF16),
        scratch_shapes=[pltpu.VMEM((n_chain, TQ, 128), BF16),
                        pltpu.VMEM((n_chain, ATT_V_DIM, TQ), F32),
                        pltpu.VMEM((2 * n_chain, TQ), F32)],
        compiler_params=_params(("arbitrary", "arbitrary")),
        name="diff_attn",
    )(q, k, vt, lam4, sg)


def _mix_kernel(x_ref, ys_ref, ya_ref, gate_ref, wps_ref, wpa_ref, wo_ref, g2_ref, wrt_ref, br_ref,
                sut_ref, lt_ref, h_ref, xs_ref, slot_ref, w_ref, oct_ref):
    tile = pl.program_id(0) * pl.num_programs(1) + pl.program_id(1)

    D = x_ref.shape[-1]
    tl = x_ref.shape[0]
    gs = gate_ref[:, 0:D].astype(F32)
    ga = gate_ref[:, D:2 * D].astype(F32)
    mixed = gs * _dot(ys_ref[...], wps_ref[...]) + ga * _dot(ya_ref[...], wpa_ref[...])
    h = x_ref[...] + _dot(mixed.astype(BF16), wo_ref[...])
    h_ref[...] = h
    ms = jnp.mean(h * h, axis=-1, keepdims=True)
    hb = (h * lax.rsqrt(ms + RMS_EPS) * g2_ref[...]).astype(BF16)

    lg = _dot_nt(wrt_ref[...], hb) + br_ref[...]
    eio = lax.broadcasted_iota(I32, (N_EXPERTS, tl), 0)
    vals, sels = [], []
    for k in range(TOP_K):
        m = jnp.max(lg, axis=0, keepdims=True)
        idx = jnp.min(jnp.where(lg == m, eio, N_EXPERTS), axis=0, keepdims=True)
        sel = eio == idx
        vals.append(m)
        sels.append(sel)
        lg = jnp.where(sel, -jnp.inf, lg)
    ex = [jnp.exp(v - vals[0]) for v in vals]
    den = ex[0] + ex[1] + ex[2] + ex[3]
    for k in range(TOP_K):
        w_ref[k:k + 1, :] = ex[k] / den

    mtot = jnp.zeros((N_EXPERTS, tl), F32)
    for k in range(TOP_K):
        mtot = mtot + jnp.where(sels[k], 1.0, 0.0)
    pre = _dot(mtot.astype(BF16), sut_ref[...])
    cnt = jnp.sum(mtot, axis=1, keepdims=True)
    seg_oct = jnp.floor((cnt + (OCT - 1.0)) * (1.0 / OCT))
    seg_b = jnp.broadcast_to(seg_oct, (N_EXPERTS, V7X_LANES))
    start_oct = _dot(lt_ref[...], seg_b.astype(BF16))
    base = pre + start_oct[:, 0:1] * float(OCT)
    slots = []
    for k in range(TOP_K):
        s_k = jnp.sum(jnp.where(sels[k], base, 0.0), axis=0, keepdims=True).astype(I32)
        slot_ref[k:k + 1, :] = s_k
        slots.append(s_k)

    for c in range(SLOTS // V7X_MXU_DIM):
        sio = lax.broadcasted_iota(I32, (V7X_MXU_DIM, tl), 0) + V7X_MXU_DIM * c
        p = jnp.where(sio == slots[0], 1.0, 0.0)
        for k in range(1, TOP_K):
            p = jnp.where(sio == slots[k], 1.0, p)
        xs_ref[V7X_MXU_DIM * c:V7X_MXU_DIM * (c + 1), :] = _dot(p.astype(BF16), hb)

    lane = lax.broadcasted_iota(I32, (N_EXPERTS, V7X_LANES), 1)

    @pl.when(tile == 0)
    def _():
        oct_ref[...] = jnp.zeros_like(oct_ref)

    oct_ref[...] = jnp.where(lane == tile, seg_b, oct_ref[...])


def _mix(x, ys_tm, ya, gates, wps, wpa, wo, g2, wrt, br, sut, lt):
    B, L, D = x.shape
    nl = L // TL_PROJ
    n_tok = B * L
    full = lambda shape: pl.BlockSpec(shape, lambda b, l: (0,) * len(shape))
    tok_spec = pl.BlockSpec((TOP_K, TL_PROJ), lambda b, l: (0, b * nl + l))
    return pl.pallas_call(
        _mix_kernel,
        grid=(B, nl),
        in_specs=[
            pl.BlockSpec((None, TL_PROJ, D), lambda b, l: (b, l, 0)),
            pl.BlockSpec((TL_PROJ, 512), lambda b, l: (l, b)),
            pl.BlockSpec((None, TL_PROJ, 512), lambda b, l: (b, l, 0)),
            pl.BlockSpec((None, TL_PROJ, 2 * D), lambda b, l: (b, l, 0)),
            full(wps.shape), full(wpa.shape), full(wo.shape), full((1, D)),
            full(wrt.shape), full(br.shape), full(sut.shape), full(lt.shape),
        ],
        out_specs=[
            pl.BlockSpec((None, TL_PROJ, D), lambda b, l: (b, l, 0)),
            pl.BlockSpec((SLOTS, D), lambda b, l: (b * nl + l, 0)),
            tok_spec, tok_spec,
            full((N_EXPERTS, V7X_LANES)),
        ],
        out_shape=[
            jax.ShapeDtypeStruct((B, L, D), F32),
            jax.ShapeDtypeStruct((B * nl * SLOTS, D), F32),
            jax.ShapeDtypeStruct((TOP_K, n_tok), I32),
            jax.ShapeDtypeStruct((TOP_K, n_tok), F32),
            jax.ShapeDtypeStruct((N_EXPERTS, V7X_LANES), F32),
        ],
        compiler_params=_params(("arbitrary", "arbitrary")),
        name="mix_router",
    )(x, ys_tm, ya, gates, wps, wpa, wo, g2, wrt, br, sut, lt)


def _moe_tables(oct_tab, n_t, n_blocks):
    E = N_EXPERTS
    O = oct_tab[:, :n_t].astype(I32)
    lstart = jnp.cumsum(O, axis=0) - O
    ecum = jnp.cumsum(O, axis=1) - O
    tot = jnp.sum(O, axis=1)
    nb = (tot + BLK_OCT - 1) // BLK_OCT
    bend = jnp.cumsum(nb)
    bstart = bend - nb
    n_valid = bend[-1]
    bi = jnp.arange(n_blocks, dtype=I32)
    er = jnp.arange(E, dtype=I32)
    tr = jnp.arange(n_t, dtype=I32)
    be = jnp.minimum(jnp.sum((bi[:, None] >= bend[None, :]).astype(I32), axis=1), E - 1)
    last_e = jnp.sum(jnp.where(bi == jnp.maximum(n_valid - 1, 0), be, 0))
    be = jnp.where(bi < n_valid, be, last_e)
    oh_e = be[:, None] == er[None, :]
    pick = lambda tab: jnp.sum(jnp.where(oh_e[:, :, None], tab[None, :, :], 0), axis=1)
    ecum_i, o_i, lst_i = pick(ecum), pick(O), pick(lstart)
    tot_i = jnp.sum(jnp.where(oh_e, tot[None, :], 0), axis=1)
    bst_i = jnp.sum(jnp.where(oh_e, bstart[None, :], 0), axis=1)
    g = (bi - bst_i)[:, None] * BLK_OCT + jnp.arange(BLK_OCT, dtype=I32)[None, :]
    tau = jnp.minimum(jnp.sum((g[:, :, None] >= (ecum_i + o_i)[:, None, :]).astype(I32), axis=2), n_t - 1)
    off = tr[None, :] * SL_OCT + lst_i - ecum_i
    src = jnp.sum(jnp.where(tau[:, :, None] == tr[None, None, :], off[:, None, :], 0), axis=2) + g
    src = jnp.where((g < tot_i[:, None]) & (bi[:, None] < n_valid), src, 0)
    s = jnp.arange(SL_OCT, dtype=I32)
    lend_t = (lstart + O).T
    e_s = jnp.minimum(jnp.sum((s[None, :, None] >= lend_t[:, None, :]).astype(I32), axis=2), E - 1)
    offc = (bstart[:, None] * BLK_OCT + ecum - lstart).T
    csrc = jnp.sum(jnp.where(e_s[:, :, None] == er[None, None, :], offc[:, None, :], 0), axis=2) + s[None, :]
    csrc = jnp.where(s[None, :] < jnp.sum(O, axis=0)[:, None], csrc, 0)
    prev_e = jnp.concatenate([jnp.full((1,), -1, I32), be[:-1]])
    first = ((be != prev_e) & (bi < n_valid)).astype(I32)
    par = (jnp.cumsum(first) - 1) % 2
    later = (er[None, :] > er[:, None]) & (nb[None, :] > 0)
    nxt_of_e = jnp.min(jnp.where(later, er[None, :], E), axis=1)
    nxt_of_e = jnp.where(nxt_of_e < E, nxt_of_e, -1)
    nxt = jnp.sum(jnp.where(oh_e, nxt_of_e[None, :], 0), axis=1)
    meta = (be, first, par.astype(I32), nxt.astype(I32), n_valid.astype(I32).reshape(1))
    return meta, src, csrc


def _gather_octets(src_ref, n_oct, src_hbm, dst, sem):
    for q in range(n_oct):
        row = pl.multiple_of(src_ref[0, q] * OCT, OCT)
        pltpu.make_async_copy(src_hbm.at[pl.ds(row, OCT), :], dst.at[pl.ds(OCT * q, OCT), :], sem).start()


def _expert_kernel(be_ref, first_ref, par_ref, nxt_ref, nv_ref, src_cur_ref, src_nxt_ref, xs_hbm,
                   wup_hbm, bup_ref, wdn_hbm, bdn_ref, perm_ref, y_ref,
                   xbuf, wup_st, wdn_st, wup_bf, wdn_bf, act_ref, sem, wsem):
    i = pl.program_id(0)
    nv = nv_ref[0]
    slot = lax.rem(i, 2)

    def weight_copies(e, s):
        return (pltpu.make_async_copy(wup_hbm.at[e], wup_st.at[s], wsem.at[0, s]),
                pltpu.make_async_copy(wdn_hbm.at[e], wdn_st.at[s], wsem.at[1, s]))

    @pl.when(i == 0)
    def _():
        _gather_octets(src_cur_ref, BLK_OCT, xs_hbm, xbuf.at[0], sem.at[0])
        for cp in weight_copies(be_ref[0], par_ref[0]):
            cp.start()

    @pl.when(i + 1 < nv)
    def _():
        _gather_octets(src_nxt_ref, BLK_OCT, xs_hbm, xbuf.at[1 - slot], sem.at[1 - slot])

    @pl.when(i < nv)
    def _():
        @pl.when(first_ref[i] == 1)
        def _():
            s = par_ref[i]
            for cp in weight_copies(be_ref[i], s):
                cp.wait()

            @pl.when(nxt_ref[i] >= 0)
            def _():
                for cp in weight_copies(nxt_ref[i], 1 - s):
                    cp.start()

            for cb in range(8):
                blk = wup_st[s, :, 256 * cb:256 * (cb + 1)].astype(BF16)
                wup_bf[:, 256 * cb:256 * (cb + 1)] = _dot(blk, perm_ref[...]).astype(BF16)
            wdn_bf[...] = wdn_st[s].astype(BF16)

        pltpu.make_async_copy(xs_hbm.at[pl.ds(0, MOE_ROWS), :], xbuf.at[slot], sem.at[slot]).wait()
        x = xbuf[slot].astype(BF16)
        for cb in range(8):
            hp = _dot(x, wup_bf[:, 256 * cb:256 * (cb + 1)]) + bup_ref[:, 256 * cb:256 * (cb + 1)]
            glu = jnp.minimum(hp[:, 0:128], SWIGLU_LIMIT)
            lin = jnp.clip(hp[:, 128:256], -SWIGLU_LIMIT, SWIGLU_LIMIT)
            act = glu * jax.nn.sigmoid(SWIGLU_ALPHA * glu) * (lin + 1.0)
            act_ref[:, 128 * cb:128 * (cb + 1)] = act.astype(BF16)
        y_ref[...] = _dot(act_ref[...], wdn_bf[...]) + bdn_ref[...]

    @pl.when(i >= nv)
    def _():
        y_ref[...] = jnp.zeros_like(y_ref)


def _experts(meta, src3, xs, w_up, b_up_p, w_down, b_down, perm):
    n_blocks = src3.shape[0]
    DE2 = w_up.shape[2]
    D = w_up.shape[1]
    tab = src3.shape[2]
    be_map = lambda i, be, *_: (be[i], 0, 0)
    grid_spec = pltpu.PrefetchScalarGridSpec(
        num_scalar_prefetch=len(meta),
        grid=(n_blocks,),
        in_specs=[
            pl.BlockSpec((None, 1, tab), lambda i, *_: (i, 0, 0), memory_space=pltpu.SMEM),
            pl.BlockSpec((None, 1, tab), lambda i, *_: (jnp.minimum(i + 1, n_blocks - 1), 0, 0),
                         memory_space=pltpu.SMEM),
            pl.BlockSpec(memory_space=pl.ANY),
            pl.BlockSpec(memory_space=pl.ANY),
            pl.BlockSpec((None, 1, DE2), be_map),
            pl.BlockSpec(memory_space=pl.ANY),
            pl.BlockSpec((None, 1, D), be_map),
            pl.BlockSpec((256, 256), lambda i, *_: (0, 0)),
        ],
        out_specs=pl.BlockSpec((MOE_ROWS, D), lambda i, *_: (i, 0)),
        scratch_shapes=[
            pltpu.VMEM((2, MOE_ROWS, D), F32),
            pltpu.VMEM((2, D, DE2), F32),
            pltpu.VMEM((2, DE2 // 2, D), F32),
            pltpu.VMEM((D, DE2), BF16),
            pltpu.VMEM((DE2 // 2, D), BF16),
            pltpu.VMEM((MOE_ROWS, DE2 // 2), BF16),
            pltpu.SemaphoreType.DMA((2,)),
            pltpu.SemaphoreType.DMA((2, 2)),
        ],
    )
    return pl.pallas_call(
        _expert_kernel,
        grid_spec=grid_spec,
        out_shape=jax.ShapeDtypeStruct((n_blocks * MOE_ROWS, D), F32),
        compiler_params=_params(("arbitrary",)),
        name="experts",
    )(*meta, src3, src3, xs, w_up, b_up_p, w_down, b_down, perm)


def _combine_kernel(src_cur_ref, src_nxt_ref, h_ref, slot_ref, w_ref, ys_hbm, o_ref,
                    ybuf, wmat, sl_b, w_b, sem):
    i = pl.program_id(0)
    n = pl.num_programs(0)
    slot = lax.rem(i, 2)

    @pl.when(i == 0)
    def _():
        _gather_octets(src_cur_ref, SL_OCT, ys_hbm, ybuf.at[0], sem.at[0])

    @pl.when(i + 1 < n)
    def _():
        _gather_octets(src_nxt_ref, SL_OCT, ys_hbm, ybuf.at[1 - slot], sem.at[1 - slot])

    tl = h_ref.shape[0]
    for k in range(TOP_K):
        sl_b[k] = jnp.broadcast_to(slot_ref[:, k:k + 1], (tl, V7X_LANES))
        w_b[k] = jnp.broadcast_to(w_ref[:, k:k + 1], (tl, V7X_LANES))
    lane = lax.broadcasted_iota(I32, (tl, V7X_LANES), 1)
    for c in range(SLOTS // V7X_LANES):
        sio = lane + V7X_LANES * c
        wm = jnp.where(sio == sl_b[0], w_b[0], 0.0)
        for k in range(1, TOP_K):
            wm = jnp.where(sio == sl_b[k], w_b[k], wm)
        wmat[:, V7X_LANES * c:V7X_LANES * (c + 1)] = wm.astype(BF16)

    pltpu.make_async_copy(ys_hbm.at[pl.ds(0, SLOTS), :], ybuf.at[slot], sem.at[slot]).wait()
    o_ref[...] = h_ref[...] + _dot(wmat[...], ybuf[slot].astype(BF16))


def _combine(csrc3, h2, slot_tok, w_tok, ys):
    n_tok, D = h2.shape
    n_steps = n_tok // TL_PROJ
    tab = csrc3.shape[2]
    return pl.pallas_call(
        _combine_kernel,
        grid=(n_steps,),
        in_specs=[
            pl.BlockSpec((None, 1, tab), lambda i: (i, 0, 0), memory_space=pltpu.SMEM),
            pl.BlockSpec((None, 1, tab), lambda i: (jnp.minimum(i + 1, n_steps - 1), 0, 0),
                         memory_space=pltpu.SMEM),
            pl.BlockSpec((TL_PROJ, D), lambda i: (i, 0)),
            pl.BlockSpec((TL_PROJ, TOP_K), lambda i: (i, 0)),
            pl.BlockSpec((TL_PROJ, TOP_K), lambda i: (i, 0)),
            pl.BlockSpec(memory_space=pl.ANY),
        ],
        out_specs=pl.BlockSpec((TL_PROJ, D), lambda i: (i, 0)),
        out_shape=jax.ShapeDtypeStruct((n_tok, D), F32),
        scratch_shapes=[pltpu.VMEM((2, SLOTS, D), F32), pltpu.VMEM((TL_PROJ, SLOTS), BF16),
                        pltpu.VMEM((TOP_K, TL_PROJ, V7X_LANES), I32),
                        pltpu.VMEM((TOP_K, TL_PROJ, V7X_LANES), F32),
                        pltpu.SemaphoreType.DMA((2,))],
        compiler_params=_params(("arbitrary",)),
        name="combine",
    )(csrc3, csrc3, h2, slot_tok, w_tok, ys)


def _s5_matrices(a_re, a_im, log_dt, b_re, b_im, c_re, c_im):
    G, P = a_re.shape
    dt = jnp.exp(log_dt.astype(F32))[:, None]
    lr = a_re.astype(F32)
    li = a_im.astype(F32)
    mag = jnp.exp(lr * dt)
    abar_r = mag * jnp.cos(li * dt)
    abar_i = mag * jnp.sin(li * dt)
    den = lr * lr + li * li
    nr = abar_r - 1.0
    ni = abar_i
    coef_r = (nr * lr + ni * li) / den
    coef_i = (ni * lr - nr * li) / den
    br_ = b_re.astype(F32)
    bi_ = b_im.astype(F32)
    bbar_r = coef_r[..., None] * br_ - coef_i[..., None] * bi_
    bbar_i = coef_r[..., None] * bi_ + coef_i[..., None] * br_
    eye = jnp.eye(G // 2, dtype=F32)

    def in_mat(bb):
        bb = bb.reshape(2, G // 2, P, SSM_GROUP)
        return jnp.einsum('xgph,gk->xghkp', bb, eye).reshape(2, (G // 2) * SSM_GROUP, (G // 2) * P)

    def out_mat(cc):
        cc = cc.reshape(2, G // 2, SSM_GROUP, P)
        return jnp.einsum('xghp,gk->xgpkh', cc, eye).reshape(2, (G // 2) * P, (G // 2) * SSM_GROUP)

    bmat = jnp.concatenate([in_mat(bbar_r), in_mat(bbar_i)], axis=-1).astype(BF16)
    cmat = jnp.concatenate([out_mat(c_re.astype(F32)), -out_mat(c_im.astype(F32))], axis=1).astype(BF16)
    return bmat, abar_r.reshape(1, G * P), abar_i.reshape(1, G * P), cmat


def _pad_lanes(t):
    n, w = t.shape
    wp = ((w + V7X_LANES - 1) // V7X_LANES) * V7X_LANES
    return jnp.pad(t, ((0, 0), (0, wp - w))).reshape(n, 1, wp)


def kernel(x, norm1_g, w_in, b_gate, ssm_a_re, ssm_a_im, ssm_log_dt, ssm_b_re, ssm_b_im, ssm_c_re, ssm_c_im, ssm_d, ssm_w_glu, ssm_b_glu, q_norm_g, k_norm_g, lambda_q1, lambda_k1, lambda_q2, lambda_k2, subln_g, w_proj_ssm, w_proj_att, w_out, norm2_g, w_router, b_router, w_up, b_up, w_down, b_down):
    B, L, D = x.shape
    assert B == V7X_SUBLANES and D == 1024 and L % TL_PROJ == 0 and w_in.shape[0] == 1
    n_tok = B * L
    n_t = n_tok // TL_PROJ
    assert n_t <= V7X_LANES
    l = 0

    scale = ATT_SUB_DIM ** -0.5
    qg = (jnp.tile(q_norm_g[l].astype(F32), 2 * ATT_HEADS) * (scale * math.log2(math.e))).reshape(1, 512)
    kg = jnp.tile(k_norm_g[l].astype(F32), 2 * ATT_HEADS).reshape(1, 512)
    blk = jnp.arange(256) // ATT_SUB_DIM
    ones_bd = (blk[:, None] == blk[None, :]).astype(BF16)
    u_tm, q, k, vt, gates = _inproj(x, norm1_g[l].reshape(1, D), w_in[l].astype(BF16),
                                    b_gate[l].reshape(1, 2 * D), qg, kg, ones_bd)

    bmat, ar, ai, cmat = _s5_matrices(ssm_a_re[l], ssm_a_im[l], ssm_log_dt[l], ssm_b_re[l], ssm_b_im[l],
                                      ssm_c_re[l], ssm_c_im[l])
    y_ssm = _s5(u_tm.reshape(L * B, 512), bmat, ar, ai, cmat, ssm_d[l].reshape(1, 512).astype(F32),
                ssm_w_glu[l].astype(BF16), ssm_b_glu[l].reshape(1, 512).astype(F32))

    lam4 = jnp.stack([lambda_q1[l], lambda_k1[l], lambda_q2[l], lambda_k2[l]]).astype(F32)
    sg = jnp.tile(subln_g[l].astype(F32), ATT_HEADS).reshape(1, 512)
    y_att = _attn(q, k, vt, lam4, sg)

    tpos = jnp.arange(TL_PROJ)
    sut = (tpos[:, None] < tpos[None, :]).astype(BF16)
    epos = jnp.arange(N_EXPERTS)
    lt = (epos[None, :] < epos[:, None]).astype(BF16)
    h, xs, slot, top_w, oct_tab = _mix(
        x, y_ssm.reshape(L, B * 512), y_att, gates,
        w_proj_ssm[l].astype(BF16), w_proj_att[l].astype(BF16), w_out[l].astype(BF16),
        norm2_g[l].reshape(1, D), w_router[l].T.astype(BF16), b_router[l].reshape(N_EXPERTS, 1).astype(F32),
        sut, lt)

    n_blocks = (n_tok * TOP_K + n_t * N_EXPERTS * (OCT - 1) + MOE_ROWS - 1) // MOE_ROWS + N_EXPERTS
    meta, src, csrc = _moe_tables(oct_tab, n_t, n_blocks)

    de2 = w_up.shape[-1]
    b_up_p = b_up[l].reshape(N_EXPERTS, de2 // 256, 128, 2).transpose(0, 1, 3, 2).reshape(N_EXPERTS, 1, de2)
    pr = jnp.arange(256)
    col_src = jnp.where(pr < 128, 2 * pr, 2 * (pr - 128) + 1)
    perm = (jnp.arange(256)[:, None] == col_src[None, :]).astype(BF16)
    ys = _experts(meta, _pad_lanes(src), xs,
                  w_up[l], b_up_p, w_down[l], b_down[l].reshape(N_EXPERTS, 1, D), perm)

    out = _combine(_pad_lanes(csrc), h.reshape(n_tok, D), slot.T, top_w.T, ys)
    return out.reshape(B, L, D)
```

```python
import math

import jax
import jax.numpy as jnp
from jax import lax
from jax.experimental import pallas as pl
from jax.experimental.pallas import tpu as pltpu

F32 = jnp.float32
BF16 = jnp.bfloat16
I32 = jnp.int32

RMS_EPS = 1e-6
CHUNK = 64
SSM_GROUP = 16
SSM_STATE = 64
ATT_HEADS = 4
ATT_SUB_DIM = 64
ATT_V_DIM = 128
N_EXPERTS = 32
TOP_K = 4
SWIGLU_ALPHA = 1.702
SWIGLU_LIMIT = 7.0
LAMBDA_INIT = 0.8 - 0.6 * math.exp(-0.3 * 0)

V7X_SUBLANES = 8
V7X_LANES = 128
V7X_MXU_DIM = 256

TL_PROJ = 512
S5_STEPS = 64
TQ = 256
TK = 256
MOE_ROWS = 512
OCT = V7X_SUBLANES
BLK_OCT = MOE_ROWS // OCT
SLOTS = ((TL_PROJ * TOP_K + N_EXPERTS * (OCT - 1) + V7X_MXU_DIM - 1) // V7X_MXU_DIM) * V7X_MXU_DIM
SL_OCT = SLOTS // OCT
NEG = -1e30
VMEM_LIMIT = 56 * 1024 * 1024


def _dot(a, b):
    return jnp.dot(a, b, preferred_element_type=F32)


def _dot_nt(a, b):
    return lax.dot_general(a, b, (((1,), (1,)), ((), ())), preferred_element_type=F32)


def _params(sem, vmem=VMEM_LIMIT):
    return pltpu.CompilerParams(dimension_semantics=sem, vmem_limit_bytes=vmem)


def _inproj_kernel(x_ref, g1_ref, w_ref, bg_ref, qg_ref, kg_ref, ones_ref,
                   u_ref, q_ref, k_ref, vt_ref, gate_ref):
    x = x_ref[...]
    ms = jnp.mean(x * x, axis=-1, keepdims=True)
    xn = (x * lax.rsqrt(ms + RMS_EPS) * g1_ref[...]).astype(BF16)

    u_ref[...] = _dot(xn, w_ref[:, 0:512])

    def head_norm(z, g):
        sq = (z * z).astype(BF16)
        ss = jnp.concatenate([_dot(sq[:, 0:256], ones_ref[...]),
                              _dot(sq[:, 256:512], ones_ref[...])], axis=1)
        return z * lax.rsqrt(ss * (1.0 / ATT_SUB_DIM) + RMS_EPS) * g

    q_ref[...] = head_norm(_dot(xn, w_ref[:, 512:1024]), qg_ref[...]).astype(BF16)
    k_ref[...] = head_norm(_dot(xn, w_ref[:, 1024:1536]), kg_ref[...]).astype(BF16)
    vt_ref[...] = _dot(xn, w_ref[:, 1536:2048]).T.astype(BF16)
    for c in range(4):
        lo = 2048 + 512 * c
        z = _dot(xn, w_ref[:, lo:lo + 512]) + bg_ref[:, 512 * c:512 * (c + 1)]
        gate_ref[:, 512 * c:512 * (c + 1)] = jax.nn.sigmoid(z).astype(BF16)


def _inproj(x, g1, w_in_bf, b_gate, qg, kg, ones_bd):
    B, L, D = x.shape
    nl = L // TL_PROJ
    full = lambda shape: pl.BlockSpec(shape, lambda b, l: (0,) * len(shape))
    return pl.pallas_call(
        _inproj_kernel,
        grid=(B, nl),
        in_specs=[
            pl.BlockSpec((None, TL_PROJ, D), lambda b, l: (b, l, 0)),
            full((1, D)), full(w_in_bf.shape), full((1, 2 * D)),
            full((1, 512)), full((1, 512)), full((256, 256)),
        ],
        out_specs=[
            pl.BlockSpec((None, TL_PROJ, 512), lambda b, l: (b, l, 0)),
            pl.BlockSpec((None, TL_PROJ, 512), lambda b, l: (b, l, 0)),
            pl.BlockSpec((None, TL_PROJ, 512), lambda b, l: (b, l, 0)),
            pl.BlockSpec((None, 512, TL_PROJ), lambda b, l: (b, 0, l)),
            pl.BlockSpec((None, TL_PROJ, 2 * D), lambda b, l: (b, l, 0)),
        ],
        out_shape=[
            jax.ShapeDtypeStruct((B, L, 512), F32),
            jax.ShapeDtypeStruct((B, L, 512), BF16),
            jax.ShapeDtypeStruct((B, L, 512), BF16),
            jax.ShapeDtypeStruct((B, 512, L), BF16),
            jax.ShapeDtypeStruct((B, L, 2 * D), BF16),
        ],
        compiler_params=_params(("arbitrary", "arbitrary")),
        name="inproj",
    )(x, g1, w_in_bf, b_gate, qg, kg, ones_bd)


def _s5_kernel(u_hbm, bm_ref, ar_ref, ai_ref, cm_ref, d_ref, wg_ref, bgl_ref,
               y_hbm, ubuf, ybuf, hs_ref, h_ref, usem, ysem):
    i = pl.program_id(0)
    n = pl.num_programs(0)
    slot = lax.rem(i, 2)
    nb = ubuf.shape[2]

    def u_copies(step, s):
        t0 = pl.multiple_of(step * S5_STEPS, S5_STEPS)
        return [pltpu.make_async_copy(u_hbm.at[b, pl.ds(t0, S5_STEPS), :], ubuf.at[s, :, b, :], usem.at[s])
                for b in range(nb)]

    def y_copies(step, s):
        t0 = pl.multiple_of(step * S5_STEPS, S5_STEPS)
        return [pltpu.make_async_copy(ybuf.at[s, :, b, :], y_hbm.at[b, pl.ds(t0, S5_STEPS), :], ysem.at[s])
                for b in range(nb)]

    @pl.when(i == 0)
    def _():
        h_ref[...] = jnp.zeros_like(h_ref)
        for cp in u_copies(0, 0):
            cp.start()

    @pl.when(i + 1 < n)
    def _():
        for cp in u_copies(i + 1, 1 - slot):
            cp.start()

    for cp in u_copies(i, slot):
        cp.wait()
    uf = ubuf[slot].reshape(S5_STEPS * nb, ubuf.shape[3])
    u = uf.astype(BF16)
    for hh in range(2):
        hs_ref[:, 2048 * hh:2048 * (hh + 1)] = _dot(u[:, 256 * hh:256 * (hh + 1)], bm_ref[hh])

    for hh in range(2):
        re0 = 2048 * hh
        im0 = re0 + 1024
        ar = jnp.broadcast_to(ar_ref[:, 1024 * hh:1024 * (hh + 1)], (V7X_SUBLANES, 1024))
        ai = jnp.broadcast_to(ai_ref[:, 1024 * hh:1024 * (hh + 1)], (V7X_SUBLANES, 1024))

        def step(t, carry):
            hr, hi = carry
            r0 = pl.multiple_of(t * V7X_SUBLANES, V7X_SUBLANES)
            xr = hs_ref[pl.ds(r0, V7X_SUBLANES), re0:re0 + 1024]
            xi = hs_ref[pl.ds(r0, V7X_SUBLANES), im0:im0 + 1024]
            nr = ar * hr - ai * hi + xr
            ni = ar * hi + ai * hr + xi
            hs_ref[pl.ds(r0, V7X_SUBLANES), re0:re0 + 1024] = nr
            hs_ref[pl.ds(r0, V7X_SUBLANES), im0:im0 + 1024] = ni
            return nr, ni

        hr, hi = lax.fori_loop(0, S5_STEPS, step,
                               (h_ref[:, re0:re0 + 1024], h_ref[:, im0:im0 + 1024]), unroll=2)
        h_ref[:, re0:re0 + 1024] = hr
        h_ref[:, im0:im0 + 1024] = hi

    y = jnp.concatenate(
        [_dot(hs_ref[:, 2048 * hh:2048 * (hh + 1)].astype(BF16), cm_ref[hh]) for hh in range(2)], axis=1)
    y = y + d_ref[...] * uf
    y = 0.5 * y * (1.0 + jnp.tanh(math.sqrt(2.0 / math.pi) * (y + 0.044715 * (y * y * y))))
    z = _dot(y.astype(BF16), wg_ref[...]) + bgl_ref[...]

    @pl.when(i >= 2)
    def _():
        for cp in y_copies(i - 2, slot):
            cp.wait()

    ybuf[slot] = (y * jax.nn.sigmoid(z)).reshape(S5_STEPS, nb, ybuf.shape[3])
    for cp in y_copies(i, slot):
        cp.start()

    @pl.when(i == n - 1)
    def _():
        for cp in y_copies(i, slot):
            cp.wait()

    @pl.when(jnp.logical_and(i == n - 1, i >= 1))
    def _():
        for cp in y_copies(i - 1, 1 - slot):
            cp.wait()


def _s5(u, bmat, ar, ai, cmat, d_skip, w_glu_bf, b_glu):
    B, L, C = u.shape
    R = S5_STEPS * B
    full = lambda shape: pl.BlockSpec(shape, lambda i: (0,) * len(shape))
    return pl.pallas_call(
        _s5_kernel,
        grid=(L // S5_STEPS,),
        in_specs=[
            pl.BlockSpec(memory_space=pl.ANY),
            full(bmat.shape), full(ar.shape), full(ai.shape), full(cmat.shape),
            full(d_skip.shape), full(w_glu_bf.shape), full(b_glu.shape),
        ],
        out_specs=pl.BlockSpec(memory_space=pl.ANY),
        out_shape=jax.ShapeDtypeStruct((B, L, C), F32),
        scratch_shapes=[pltpu.VMEM((2, S5_STEPS, B, C), F32), pltpu.VMEM((2, S5_STEPS, B, C), F32),
                        pltpu.VMEM((R, 4096), F32), pltpu.VMEM((B, 4096), F32),
                        pltpu.SemaphoreType.DMA((2,)), pltpu.SemaphoreType.DMA((2,))],
        compiler_params=_params(("arbitrary",)),
        name="s5",
    )(u, bmat, ar, ai, cmat, d_skip, w_glu_bf, b_glu)


def _attn_kernel(q_ref, k_ref, vt_ref, lam_ref, sg_ref, o_ref, qm_ref, acc_ref, ml_ref, s_ref):
    i = pl.program_id(1)
    n_chain = 2 * ATT_HEADS
    lamv = lam_ref[...]
    lam = (jnp.exp(jnp.sum(lamv[0:1] * lamv[1:2], axis=-1, keepdims=True))
           - jnp.exp(jnp.sum(lamv[2:3] * lamv[3:4], axis=-1, keepdims=True)) + LAMBDA_INIT)
    lane = lax.broadcasted_iota(I32, (TQ, 128), 1)
    key_chunk = lax.broadcasted_iota(I32, (TK, TQ), 0) // CHUNK
    qry_chunk = lax.broadcasted_iota(I32, (TK, TQ), 1) // CHUNK
    diag_ok = key_chunk <= qry_chunk

    for hd in range(ATT_HEADS):
        qh = q_ref[:, 128 * hd:128 * (hd + 1)]
        zero = jnp.zeros_like(qh)
        qm_ref[2 * hd] = jnp.where(lane < ATT_SUB_DIM, qh, zero)
        qm_ref[2 * hd + 1] = jnp.where(lane >= ATT_SUB_DIM, qh, zero)
    acc_ref[...] = jnp.zeros_like(acc_ref)
    row = lax.broadcasted_iota(I32, (2 * n_chain, TQ), 0)
    ml_ref[...] = jnp.where(row % 2 == 0, NEG, 0.0)

    def scores(j, par):
        off = pl.multiple_of(j * TK, TK)
        for hd in range(ATT_HEADS):
            ks = k_ref[pl.ds(off, TK), 128 * hd:128 * (hd + 1)]
            for s in range(2):
                s_ref[par, 2 * hd + s] = _dot_nt(ks, qm_ref[2 * hd + s])

    def consume(j, par, masked):
        off = pl.multiple_of(j * TK, TK)
        ps, alphas = [], []
        for c in range(n_chain):
            st = s_ref[par, c]
            if masked:
                st = jnp.where(diag_ok, st, NEG)
            m = ml_ref[2 * c:2 * c + 1, :]
            l = ml_ref[2 * c + 1:2 * c + 2, :]
            mn = jnp.maximum(m, jnp.max(st, axis=0, keepdims=True))
            alpha = jnp.exp2(m - mn)
            p = jnp.exp2(st - mn)
            ml_ref[2 * c:2 * c + 1, :] = mn
            ml_ref[2 * c + 1:2 * c + 2, :] = alpha * l + jnp.sum(p, axis=0, keepdims=True)
            ps.append(p.astype(BF16))
            alphas.append(alpha)
        for c in range(n_chain):
            hd = c // 2
            vts = vt_ref[128 * hd:128 * (hd + 1), pl.ds(off, TK)]
            acc_ref[c] = acc_ref[c] * alphas[c] + _dot(vts, ps[c])

    scores(0, 0)

    def body(jj, carry):
        j = 2 * jj
        scores(j + 1, 1)
        consume(j, 0, False)
        scores(j + 2, 0)
        consume(j + 1, 1, False)
        return carry

    lax.fori_loop(0, i // 2, body, 0)

    @pl.when(i % 2 == 1)
    def _():
        scores(i, 1)
        consume(i - 1, 0, False)
        consume(i, 1, True)

    @pl.when(i % 2 == 0)
    def _():
        consume(i, 0, True)

    for hd in range(ATT_HEADS):
        c0 = 128 * hd
        l1 = ml_ref[4 * hd + 1:4 * hd + 2, :]
        l2 = ml_ref[4 * hd + 3:4 * hd + 4, :]
        o = acc_ref[2 * hd] / l1 - lam * (acc_ref[2 * hd + 1] / l2)
        ms = jnp.mean(o * o, axis=0, keepdims=True)
        on = (o * lax.rsqrt(ms + RMS_EPS)).T
        o_ref[:, c0:c0 + 128] = (on * sg_ref[:, c0:c0 + 128] * (1.0 - LAMBDA_INIT)).astype(BF16)


def _attn(q, k, vt, lam4, sg):
    B, L, W = q.shape
    n_chain = 2 * ATT_HEADS
    return pl.pallas_call(
        _attn_kernel,
        grid=(B, L // TQ),
        in_specs=[
            pl.BlockSpec((None, TQ, W), lambda b, i: (b, i, 0)),
            pl.BlockSpec((None, L, W), lambda b, i: (b, 0, 0)),
            pl.BlockSpec((None, W, L), lambda b, i: (b, 0, 0)),
            pl.BlockSpec((4, ATT_SUB_DIM), lambda b, i: (0, 0)),
            pl.BlockSpec((1, W), lambda b, i: (0, 0)),
        ],
        out_specs=pl.BlockSpec((None, TQ, W), lambda b, i: (b, i, 0)),
        out_shape=jax.ShapeDtypeStruct((B, L, W), BF16),
        scratch_shapes=[pltpu.VMEM((n_chain, TQ, 128), BF16),
                        pltpu.VMEM((n_chain, ATT_V_DIM, TQ), F32),
                        pltpu.VMEM((2 * n_chain, TQ), F32),
                        pltpu.VMEM((2, n_chain, TK, TQ), F32)],
        compiler_params=_params(("arbitrary", "arbitrary")),
        name="diff_attn",
    )(q, k, vt, lam4, sg)


def _mix_kernel(x_ref, ys_ref, ya_ref, gate_ref, wps_ref, wpa_ref, wo_ref, g2_ref, wrt_ref, br_ref,
                sut_ref, lt_ref, h_ref, xs_ref, slot_ref, w_ref, oct_ref):
    tile = pl.program_id(0) * pl.num_programs(1) + pl.program_id(1)

    D = x_ref.shape[-1]
    tl = x_ref.shape[0]
    gs = gate_ref[:, 0:D].astype(F32)
    ga = gate_ref[:, D:2 * D].astype(F32)
    mixed = gs * _dot(ys_ref[...].astype(BF16), wps_ref[...]) + ga * _dot(ya_ref[...], wpa_ref[...])
    h = x_ref[...] + _dot(mixed.astype(BF16), wo_ref[...])
    h_ref[...] = h
    ms = jnp.mean(h * h, axis=-1, keepdims=True)
    hb = (h * lax.rsqrt(ms + RMS_EPS) * g2_ref[...]).astype(BF16)

    lg = _dot_nt(wrt_ref[...], hb) + br_ref[...]
    eio = lax.broadcasted_iota(I32, (N_EXPERTS, tl), 0)
    vals, sels = [], []
    for k in range(TOP_K):
        m = jnp.max(lg, axis=0, keepdims=True)
        idx = jnp.min(jnp.where(lg == m, eio, N_EXPERTS), axis=0, keepdims=True)
        sel = eio == idx
        vals.append(m)
        sels.append(sel)
        lg = jnp.where(sel, -jnp.inf, lg)
    ex = [jnp.exp(v - vals[0]) for v in vals]
    den = ex[0] + ex[1] + ex[2] + ex[3]
    for k in range(TOP_K):
        w_ref[k:k + 1, :] = ex[k] / den

    mtot = jnp.zeros((N_EXPERTS, tl), F32)
    for k in range(TOP_K):
        mtot = mtot + jnp.where(sels[k], 1.0, 0.0)
    pre = _dot(mtot.astype(BF16), sut_ref[...])
    cnt = jnp.sum(mtot, axis=1, keepdims=True)
    seg_oct = jnp.floor((cnt + (OCT - 1.0)) * (1.0 / OCT))
    seg_b = jnp.broadcast_to(seg_oct, (N_EXPERTS, V7X_LANES))
    start_oct = _dot(lt_ref[...], seg_b.astype(BF16))
    base = pre + start_oct[:, 0:1] * float(OCT)
    slots = []
    for k in range(TOP_K):
        s_k = jnp.sum(jnp.where(sels[k], base, 0.0), axis=0, keepdims=True).astype(I32)
        slot_ref[k:k + 1, :] = s_k
        slots.append(s_k)

    for c in range(SLOTS // V7X_MXU_DIM):
        sio = lax.broadcasted_iota(I32, (V7X_MXU_DIM, tl), 0) + V7X_MXU_DIM * c
        p = jnp.where(sio == slots[0], 1.0, 0.0)
        for k in range(1, TOP_K):
            p = jnp.where(sio == slots[k], 1.0, p)
        xs_ref[V7X_MXU_DIM * c:V7X_MXU_DIM * (c + 1), :] = _dot(p.astype(BF16), hb)

    lane = lax.broadcasted_iota(I32, (N_EXPERTS, V7X_LANES), 1)

    @pl.when(tile == 0)
    def _():
        oct_ref[...] = jnp.zeros_like(oct_ref)

    oct_ref[...] = jnp.where(lane == tile, seg_b, oct_ref[...])


def _mix(x, ys_tm, ya, gates, wps, wpa, wo, g2, wrt, br, sut, lt):
    B, L, D = x.shape
    nl = L // TL_PROJ
    n_tok = B * L
    full = lambda shape: pl.BlockSpec(shape, lambda b, l: (0,) * len(shape))
    tok_spec = pl.BlockSpec((TOP_K, TL_PROJ), lambda b, l: (0, b * nl + l))
    return pl.pallas_call(
        _mix_kernel,
        grid=(B, nl),
        in_specs=[
            pl.BlockSpec((None, TL_PROJ, D), lambda b, l: (b, l, 0)),
            pl.BlockSpec((None, TL_PROJ, 512), lambda b, l: (b, l, 0)),
            pl.BlockSpec((None, TL_PROJ, 512), lambda b, l: (b, l, 0)),
            pl.BlockSpec((None, TL_PROJ, 2 * D), lambda b, l: (b, l, 0)),
            full(wps.shape), full(wpa.shape), full(wo.shape), full((1, D)),
            full(wrt.shape), full(br.shape), full(sut.shape), full(lt.shape),
        ],
        out_specs=[
            pl.BlockSpec((None, TL_PROJ, D), lambda b, l: (b, l, 0)),
            pl.BlockSpec((SLOTS, D), lambda b, l: (b * nl + l, 0)),
            tok_spec, tok_spec,
            full((N_EXPERTS, V7X_LANES)),
        ],
        out_shape=[
            jax.ShapeDtypeStruct((B, L, D), F32),
            jax.ShapeDtypeStruct((B * nl * SLOTS, D), F32),
            jax.ShapeDtypeStruct((TOP_K, n_tok), I32),
            jax.ShapeDtypeStruct((TOP_K, n_tok), F32),
            jax.ShapeDtypeStruct((N_EXPERTS, V7X_LANES), F32),
        ],
        compiler_params=_params(("arbitrary", "arbitrary")),
        name="mix_router",
    )(x, ys_tm, ya, gates, wps, wpa, wo, g2, wrt, br, sut, lt)


def _moe_tables(oct_tab, n_t, n_blocks):
    E = N_EXPERTS
    O = oct_tab[:, :n_t].astype(I32)
    lstart = jnp.cumsum(O, axis=0) - O
    ecum = jnp.cumsum(O, axis=1) - O
    tot = jnp.sum(O, axis=1)
    nb = (tot + BLK_OCT - 1) // BLK_OCT
    bend = jnp.cumsum(nb)
    bstart = bend - nb
    n_valid = bend[-1]
    bi = jnp.arange(n_blocks, dtype=I32)
    er = jnp.arange(E, dtype=I32)
    tr = jnp.arange(n_t, dtype=I32)
    be = jnp.minimum(jnp.sum((bi[:, None] >= bend[None, :]).astype(I32), axis=1), E - 1)
    last_e = jnp.sum(jnp.where(bi == jnp.maximum(n_valid - 1, 0), be, 0))
    be = jnp.where(bi < n_valid, be, last_e)
    oh_e = be[:, None] == er[None, :]
    pick = lambda tab: jnp.sum(jnp.where(oh_e[:, :, None], tab[None, :, :], 0), axis=1)
    ecum_i, o_i, lst_i = pick(ecum), pick(O), pick(lstart)
    tot_i = jnp.sum(jnp.where(oh_e, tot[None, :], 0), axis=1)
    bst_i = jnp.sum(jnp.where(oh_e, bstart[None, :], 0), axis=1)
    g = (bi - bst_i)[:, None] * BLK_OCT + jnp.arange(BLK_OCT, dtype=I32)[None, :]
    tau = jnp.minimum(jnp.sum((g[:, :, None] >= (ecum_i + o_i)[:, None, :]).astype(I32), axis=2), n_t - 1)
    off = tr[None, :] * SL_OCT + lst_i - ecum_i
    src = jnp.sum(jnp.where(tau[:, :, None] == tr[None, None, :], off[:, None, :], 0), axis=2) + g
    src = jnp.where((g < tot_i[:, None]) & (bi[:, None] < n_valid), src, 0)
    s = jnp.arange(SL_OCT, dtype=I32)
    lend_t = (lstart + O).T
    e_s = jnp.minimum(jnp.sum((s[None, :, None] >= lend_t[:, None, :]).astype(I32), axis=2), E - 1)
    offc = (bstart[:, None] * BLK_OCT + ecum - lstart).T
    csrc = jnp.sum(jnp.where(e_s[:, :, None] == er[None, None, :], offc[:, None, :], 0), axis=2) + s[None, :]
    csrc = jnp.where(s[None, :] < jnp.sum(O, axis=0)[:, None], csrc, 0)
    prev_e = jnp.concatenate([jnp.full((1,), -1, I32), be[:-1]])
    first = ((be != prev_e) & (bi < n_valid)).astype(I32)
    par = (jnp.cumsum(first) - 1) % 2
    later = (er[None, :] > er[:, None]) & (nb[None, :] > 0)
    nxt_of_e = jnp.min(jnp.where(later, er[None, :], E), axis=1)
    nxt_of_e = jnp.where(nxt_of_e < E, nxt_of_e, -1)
    nxt = jnp.sum(jnp.where(oh_e, nxt_of_e[None, :], 0), axis=1)
    meta = (be, first, par.astype(I32), nxt.astype(I32), n_valid.astype(I32).reshape(1))
    return meta, src, csrc


def _gather_octets(src_ref, n_oct, src_hbm, dst, sem):
    for q in range(n_oct):
        row = pl.multiple_of(src_ref[0, q] * OCT, OCT)
        pltpu.make_async_copy(src_hbm.at[pl.ds(row, OCT), :], dst.at[pl.ds(OCT * q, OCT), :], sem).start()


def _expert_kernel(be_ref, first_ref, par_ref, nxt_ref, nv_ref, src_cur_ref, src_nxt_ref, xs_hbm,
                   wup_hbm, bup_ref, wdn_hbm, bdn_ref, perm_ref, y_ref,
                   xbuf, wup_st, wdn_st, wup_bf, wdn_bf, act_ref, sem, wsem):
    i = pl.program_id(0)
    nv = nv_ref[0]
    slot = lax.rem(i, 2)

    def weight_copies(e, s):
        return (pltpu.make_async_copy(wup_hbm.at[e], wup_st.at[s], wsem.at[0, s]),
                pltpu.make_async_copy(wdn_hbm.at[e], wdn_st.at[s], wsem.at[1, s]))

    @pl.when(i == 0)
    def _():
        _gather_octets(src_cur_ref, BLK_OCT, xs_hbm, xbuf.at[0], sem.at[0])
        for cp in weight_copies(be_ref[0], par_ref[0]):
            cp.start()

    @pl.when(i + 1 < nv)
    def _():
        _gather_octets(src_nxt_ref, BLK_OCT, xs_hbm, xbuf.at[1 - slot], sem.at[1 - slot])

    @pl.when(i < nv)
    def _():
        @pl.when(first_ref[i] == 1)
        def _():
            s = par_ref[i]
            for cp in weight_copies(be_ref[i], s):
                cp.wait()

            @pl.when(nxt_ref[i] >= 0)
            def _():
                for cp in weight_copies(nxt_ref[i], 1 - s):
                    cp.start()

            for cb in range(8):
                blk = wup_st[s, :, 256 * cb:256 * (cb + 1)].astype(BF16)
                wup_bf[:, 256 * cb:256 * (cb + 1)] = _dot(blk, perm_ref[...]).astype(BF16)
            wdn_bf[...] = wdn_st[s].astype(BF16)

        pltpu.make_async_copy(xs_hbm.at[pl.ds(0, MOE_ROWS), :], xbuf.at[slot], sem.at[slot]).wait()
        x = xbuf[slot].astype(BF16)
        for cb in range(8):
            hp = _dot(x, wup_bf[:, 256 * cb:256 * (cb + 1)]) + bup_ref[:, 256 * cb:256 * (cb + 1)]
            glu = jnp.minimum(hp[:, 0:128], SWIGLU_LIMIT)
            lin = jnp.clip(hp[:, 128:256], -SWIGLU_LIMIT, SWIGLU_LIMIT)
            act = glu * jax.nn.sigmoid(SWIGLU_ALPHA * glu) * (lin + 1.0)
            act_ref[:, 128 * cb:128 * (cb + 1)] = act.astype(BF16)
        y_ref[...] = _dot(act_ref[...], wdn_bf[...]) + bdn_ref[...]

    @pl.when(i >= nv)
    def _():
        y_ref[...] = jnp.zeros_like(y_ref)


def _experts(meta, src3, xs, w_up, b_up_p, w_down, b_down, perm):
    n_blocks = src3.shape[0]
    DE2 = w_up.shape[2]
    D = w_up.shape[1]
    tab = src3.shape[2]
    be_map = lambda i, be, *_: (be[i], 0, 0)
    grid_spec = pltpu.PrefetchScalarGridSpec(
        num_scalar_prefetch=len(meta),
        grid=(n_blocks,),
        in_specs=[
            pl.BlockSpec((None, 1, tab), lambda i, *_: (i, 0, 0), memory_space=pltpu.SMEM),
            pl.BlockSpec((None, 1, tab), lambda i, *_: (jnp.minimum(i + 1, n_blocks - 1), 0, 0),
                         memory_space=pltpu.SMEM),
            pl.BlockSpec(memory_space=pl.ANY),
            pl.BlockSpec(memory_space=pl.ANY),
            pl.BlockSpec((None, 1, DE2), be_map),
            pl.BlockSpec(memory_space=pl.ANY),
            pl.BlockSpec((None, 1, D), be_map),
            pl.BlockSpec((256, 256), lambda i, *_: (0, 0)),
        ],
        out_specs=pl.BlockSpec((MOE_ROWS, D), lambda i, *_: (i, 0)),
        scratch_shapes=[
            pltpu.VMEM((2, MOE_ROWS, D), F32),
            pltpu.VMEM((2, D, DE2), F32),
            pltpu.VMEM((2, DE2 // 2, D), F32),
            pltpu.VMEM((D, DE2), BF16),
            pltpu.VMEM((DE2 // 2, D), BF16),
            pltpu.VMEM((MOE_ROWS, DE2 // 2), BF16),
            pltpu.SemaphoreType.DMA((2,)),
            pltpu.SemaphoreType.DMA((2, 2)),
        ],
    )
    return pl.pallas_call(
        _expert_kernel,
        grid_spec=grid_spec,
        out_shape=jax.ShapeDtypeStruct((n_blocks * MOE_ROWS, D), F32),
        compiler_params=_params(("arbitrary",)),
        name="experts",
    )(*meta, src3, src3, xs, w_up, b_up_p, w_down, b_down, perm)


def _combine_kernel(src_cur_ref, src_nxt_ref, h_ref, slot_ref, w_ref, ys_hbm, o_ref,
                    ybuf, wmat, sl_b, w_b, sem):
    i = pl.program_id(0)
    n = pl.num_programs(0)
    slot = lax.rem(i, 2)

    @pl.when(i == 0)
    def _():
        _gather_octets(src_cur_ref, SL_OCT, ys_hbm, ybuf.at[0], sem.at[0])

    @pl.when(i + 1 < n)
    def _():
        _gather_octets(src_nxt_ref, SL_OCT, ys_hbm, ybuf.at[1 - slot], sem.at[1 - slot])

    tl = h_ref.shape[0]
    for k in range(TOP_K):
        sl_b[k] = jnp.broadcast_to(slot_ref[:, k:k + 1], (tl, V7X_LANES))
        w_b[k] = jnp.broadcast_to(w_ref[:, k:k + 1], (tl, V7X_LANES))
    lane = lax.broadcasted_iota(I32, (tl, V7X_LANES), 1)
    for c in range(SLOTS // V7X_LANES):
        sio = lane + V7X_LANES * c
        wm = jnp.where(sio == sl_b[0], w_b[0], 0.0)
        for k in range(1, TOP_K):
            wm = jnp.where(sio == sl_b[k], w_b[k], wm)
        wmat[:, V7X_LANES * c:V7X_LANES * (c + 1)] = wm.astype(BF16)

    pltpu.make_async_copy(ys_hbm.at[pl.ds(0, SLOTS), :], ybuf.at[slot], sem.at[slot]).wait()
    o_ref[...] = h_ref[...] + _dot(wmat[...], ybuf[slot].astype(BF16))


def _combine(csrc3, h2, slot_tok, w_tok, ys):
    n_tok, D = h2.shape
    n_steps = n_tok // TL_PROJ
    tab = csrc3.shape[2]
    return pl.pallas_call(
        _combine_kernel,
        grid=(n_steps,),
        in_specs=[
            pl.BlockSpec((None, 1, tab), lambda i: (i, 0, 0), memory_space=pltpu.SMEM),
            pl.BlockSpec((None, 1, tab), lambda i: (jnp.minimum(i + 1, n_steps - 1), 0, 0),
                         memory_space=pltpu.SMEM),
            pl.BlockSpec((TL_PROJ, D), lambda i: (i, 0)),
            pl.BlockSpec((TL_PROJ, TOP_K), lambda i: (i, 0)),
            pl.BlockSpec((TL_PROJ, TOP_K), lambda i: (i, 0)),
            pl.BlockSpec(memory_space=pl.ANY),
        ],
        out_specs=pl.BlockSpec((TL_PROJ, D), lambda i: (i, 0)),
        out_shape=jax.ShapeDtypeStruct((n_tok, D), F32),
        scratch_shapes=[pltpu.VMEM((2, SLOTS, D), F32), pltpu.VMEM((TL_PROJ, SLOTS), BF16),
                        pltpu.VMEM((TOP_K, TL_PROJ, V7X_LANES), I32),
                        pltpu.VMEM((TOP_K, TL_PROJ, V7X_LANES), F32),
                        pltpu.SemaphoreType.DMA((2,))],
        compiler_params=_params(("arbitrary",)),
        name="combine",
    )(csrc3, csrc3, h2, slot_tok, w_tok, ys)


def _s5_matrices(a_re, a_im, log_dt, b_re, b_im, c_re, c_im):
    G, P = a_re.shape
    dt = jnp.exp(log_dt.astype(F32))[:, None]
    lr = a_re.astype(F32)
    li = a_im.astype(F32)
    mag = jnp.exp(lr * dt)
    abar_r = mag * jnp.cos(li * dt)
    abar_i = mag * jnp.sin(li * dt)
    den = lr * lr + li * li
    nr = abar_r - 1.0
    ni = abar_i
    coef_r = (nr * lr + ni * li) / den
    coef_i = (ni * lr - nr * li) / den
    br_ = b_re.astype(F32)
    bi_ = b_im.astype(F32)
    bbar_r = coef_r[..., None] * br_ - coef_i[..., None] * bi_
    bbar_i = coef_r[..., None] * bi_ + coef_i[..., None] * br_
    eye = jnp.eye(G // 2, dtype=F32)

    def in_mat(bb):
        bb = bb.reshape(2, G // 2, P, SSM_GROUP)
        return jnp.einsum('xgph,gk->xghkp', bb, eye).reshape(2, (G // 2) * SSM_GROUP, (G // 2) * P)

    def out_mat(cc):
        cc = cc.reshape(2, G // 2, SSM_GROUP, P)
        return jnp.einsum('xghp,gk->xgpkh', cc, eye).reshape(2, (G // 2) * P, (G // 2) * SSM_GROUP)

    bmat = jnp.concatenate([in_mat(bbar_r), in_mat(bbar_i)], axis=-1).astype(BF16)
    cmat = jnp.concatenate([out_mat(c_re.astype(F32)), -out_mat(c_im.astype(F32))], axis=1).astype(BF16)
    return bmat, abar_r.reshape(1, G * P), abar_i.reshape(1, G * P), cmat


def _pad_lanes(t):
    n, w = t.shape
    wp = ((w + V7X_LANES - 1) // V7X_LANES) * V7X_LANES
    return jnp.pad(t, ((0, 0), (0, wp - w))).reshape(n, 1, wp)


def kernel(x, norm1_g, w_in, b_gate, ssm_a_re, ssm_a_im, ssm_log_dt, ssm_b_re, ssm_b_im, ssm_c_re, ssm_c_im, ssm_d, ssm_w_glu, ssm_b_glu, q_norm_g, k_norm_g, lambda_q1, lambda_k1, lambda_q2, lambda_k2, subln_g, w_proj_ssm, w_proj_att, w_out, norm2_g, w_router, b_router, w_up, b_up, w_down, b_down):
    B, L, D = x.shape
    assert B == V7X_SUBLANES and D == 1024 and L % TL_PROJ == 0 and w_in.shape[0] == 1
    n_tok = B * L
    n_t = n_tok // TL_PROJ
    assert n_t <= V7X_LANES
    l = 0

    scale = ATT_SUB_DIM ** -0.5
    qg = (jnp.tile(q_norm_g[l].astype(F32), 2 * ATT_HEADS) * (scale * math.log2(math.e))).reshape(1, 512)
    kg = jnp.tile(k_norm_g[l].astype(F32), 2 * ATT_HEADS).reshape(1, 512)
    blk = jnp.arange(256) // ATT_SUB_DIM
    ones_bd = (blk[:, None] == blk[None, :]).astype(BF16)
    u_tm, q, k, vt, gates = _inproj(x, norm1_g[l].reshape(1, D), w_in[l].astype(BF16),
                                    b_gate[l].reshape(1, 2 * D), qg, kg, ones_bd)

    bmat, ar, ai, cmat = _s5_matrices(ssm_a_re[l], ssm_a_im[l], ssm_log_dt[l], ssm_b_re[l], ssm_b_im[l],
                                      ssm_c_re[l], ssm_c_im[l])
    y_ssm = _s5(u_tm, bmat, ar, ai, cmat, ssm_d[l].reshape(1, 512).astype(F32),
                ssm_w_glu[l].astype(BF16), ssm_b_glu[l].reshape(1, 512).astype(F32))

    lam4 = jnp.stack([lambda_q1[l], lambda_k1[l], lambda_q2[l], lambda_k2[l]]).astype(F32)
    sg = jnp.tile(subln_g[l].astype(F32), ATT_HEADS).reshape(1, 512)
    y_att = _attn(q, k, vt, lam4, sg)

    tpos = jnp.arange(TL_PROJ)
    sut = (tpos[:, None] < tpos[None, :]).astype(BF16)
    epos = jnp.arange(N_EXPERTS)
    lt = (epos[None, :] < epos[:, None]).astype(BF16)
    h, xs, slot, top_w, oct_tab = _mix(
        x, y_ssm, y_att, gates,
        w_proj_ssm[l].astype(BF16), w_proj_att[l].astype(BF16), w_out[l].astype(BF16),
        norm2_g[l].reshape(1, D), w_router[l].T.astype(BF16), b_router[l].reshape(N_EXPERTS, 1).astype(F32),
        sut, lt)

    n_blocks = (n_tok * TOP_K + n_t * N_EXPERTS * (OCT - 1) + MOE_ROWS - 1) // MOE_ROWS + N_EXPERTS
    meta, src, csrc = _moe_tables(oct_tab, n_t, n_blocks)

    de2 = w_up.shape[-1]
    b_up_p = b_up[l].reshape(N_EXPERTS, de2 // 256, 128, 2).transpose(0, 1, 3, 2).reshape(N_EXPERTS, 1, de2)
    pr = jnp.arange(256)
    col_src = jnp.where(pr < 128, 2 * pr, 2 * (pr - 128) + 1)
    perm = (jnp.arange(256)[:, None] == col_src[None, :]).astype(BF16)
    ys = _experts(meta, _pad_lanes(src), xs,
                  w_up[l], b_up_p, w_down[l], b_down[l].reshape(N_EXPERTS, 1, D), perm)

    out = _combine(_pad_lanes(csrc), h.reshape(n_tok, D), slot.T, top_w.T, ys)
    return out.reshape(B, L, D)
```

```python
import math

import jax
import jax.numpy as jnp
from jax import lax
from jax.experimental import pallas as pl
from jax.experimental.pallas import tpu as pltpu

F32 = jnp.float32
BF16 = jnp.bfloat16
I32 = jnp.int32

RMS_EPS = 1e-6
CHUNK = 64
SSM_GROUP = 16
SSM_STATE = 64
ATT_HEADS = 4
ATT_SUB_DIM = 64
ATT_V_DIM = 128
N_EXPERTS = 32
TOP_K = 4
SWIGLU_ALPHA = 1.702
SWIGLU_LIMIT = 7.0
LAMBDA_INIT = 0.8 - 0.6 * math.exp(-0.3 * 0)

V7X_SUBLANES = 8
V7X_LANES = 128
V7X_MXU_DIM = 256

TL_PROJ = 512
S5_STEPS = 64
TQ = 256
TK = 256
MOE_ROWS = 512
OCT = V7X_SUBLANES
BLK_OCT = MOE_ROWS // OCT
SLOTS = ((TL_PROJ * TOP_K + N_EXPERTS * (OCT - 1) + V7X_MXU_DIM - 1) // V7X_MXU_DIM) * V7X_MXU_DIM
SL_OCT = SLOTS // OCT
NEG = -1e30
VMEM_LIMIT = 56 * 1024 * 1024


def _dot(a, b):
    return jnp.dot(a, b, preferred_element_type=F32)


def _dot_nt(a, b):
    return lax.dot_general(a, b, (((1,), (1,)), ((), ())), preferred_element_type=F32)


def _params(sem, vmem=VMEM_LIMIT):
    return pltpu.CompilerParams(dimension_semantics=sem, vmem_limit_bytes=vmem)


def _inproj_kernel(x_ref, g1_ref, w_ref, bg_ref, qg_ref, kg_ref, ones_ref,
                   u_ref, q_ref, k_ref, vt_ref, gate_ref):
    x = x_ref[...]
    ms = jnp.mean(x * x, axis=-1, keepdims=True)
    xn = (x * lax.rsqrt(ms + RMS_EPS) * g1_ref[...]).astype(BF16)

    u_ref[...] = _dot(xn, w_ref[:, 0:512])

    def head_norm(z, g):
        sq = (z * z).astype(BF16)
        ss = jnp.concatenate([_dot(sq[:, 0:256], ones_ref[...]),
                              _dot(sq[:, 256:512], ones_ref[...])], axis=1)
        return z * lax.rsqrt(ss * (1.0 / ATT_SUB_DIM) + RMS_EPS) * g

    q_ref[...] = head_norm(_dot(xn, w_ref[:, 512:1024]), qg_ref[...]).astype(BF16)
    k_ref[...] = head_norm(_dot(xn, w_ref[:, 1024:1536]), kg_ref[...]).astype(BF16)
    vt_ref[...] = _dot(xn, w_ref[:, 1536:2048]).T.astype(BF16)
    for c in range(4):
        lo = 2048 + 512 * c
        z = _dot(xn, w_ref[:, lo:lo + 512]) + bg_ref[:, 512 * c:512 * (c + 1)]
        gate_ref[:, 512 * c:512 * (c + 1)] = jax.nn.sigmoid(z).astype(BF16)


def _inproj(x, g1, w_in_bf, b_gate, qg, kg, ones_bd):
    B, L, D = x.shape
    nl = L // TL_PROJ
    full = lambda shape: pl.BlockSpec(shape, lambda b, l: (0,) * len(shape))
    return pl.pallas_call(
        _inproj_kernel,
        grid=(B, nl),
        in_specs=[
            pl.BlockSpec((None, TL_PROJ, D), lambda b, l: (b, l, 0)),
            full((1, D)), full(w_in_bf.shape), full((1, 2 * D)),
            full((1, 512)), full((1, 512)), full((256, 256)),
        ],
        out_specs=[
            pl.BlockSpec((None, TL_PROJ, 512), lambda b, l: (b, l, 0)),
            pl.BlockSpec((None, TL_PROJ, 512), lambda b, l: (b, l, 0)),
            pl.BlockSpec((None, TL_PROJ, 512), lambda b, l: (b, l, 0)),
            pl.BlockSpec((None, 512, TL_PROJ), lambda b, l: (b, 0, l)),
            pl.BlockSpec((None, TL_PROJ, 2 * D), lambda b, l: (b, l, 0)),
        ],
        out_shape=[
            jax.ShapeDtypeStruct((B, L, 512), F32),
            jax.ShapeDtypeStruct((B, L, 512), BF16),
            jax.ShapeDtypeStruct((B, L, 512), BF16),
            jax.ShapeDtypeStruct((B, 512, L), BF16),
            jax.ShapeDtypeStruct((B, L, 2 * D), BF16),
        ],
        compiler_params=_params(("arbitrary", "arbitrary")),
        name="inproj",
    )(x, g1, w_in_bf, b_gate, qg, kg, ones_bd)


def _s5_kernel(u_hbm, bm_ref, ar_ref, ai_ref, cm_ref, d_ref, wg_ref, bgl_ref,
               y_hbm, ubuf, ybuf, hs_ref, h_ref, usem, ysem):
    i = pl.program_id(0)
    n = pl.num_programs(0)
    slot = lax.rem(i, 2)
    nb = ubuf.shape[2]

    def u_copies(step, s):
        t0 = pl.multiple_of(step * S5_STEPS, S5_STEPS)
        return [pltpu.make_async_copy(u_hbm.at[b, pl.ds(t0, S5_STEPS), :], ubuf.at[s, :, b, :], usem.at[s])
                for b in range(nb)]

    def y_copies(step, s):
        t0 = pl.multiple_of(step * S5_STEPS, S5_STEPS)
        return [pltpu.make_async_copy(ybuf.at[s, :, b, :], y_hbm.at[b, pl.ds(t0, S5_STEPS), :], ysem.at[s])
                for b in range(nb)]

    @pl.when(i == 0)
    def _():
        h_ref[...] = jnp.zeros_like(h_ref)
        for cp in u_copies(0, 0):
            cp.start()

    @pl.when(i + 1 < n)
    def _():
        for cp in u_copies(i + 1, 1 - slot):
            cp.start()

    for cp in u_copies(i, slot):
        cp.wait()
    uf = ubuf[slot].reshape(S5_STEPS * nb, ubuf.shape[3])
    u = uf.astype(BF16)
    for hh in range(2):
        hs_ref[:, 2048 * hh:2048 * (hh + 1)] = _dot(u[:, 256 * hh:256 * (hh + 1)], bm_ref[hh])

    ys = []
    for hh in range(2):
        re0 = 2048 * hh
        im0 = re0 + 1024
        ar = jnp.broadcast_to(ar_ref[:, 1024 * hh:1024 * (hh + 1)], (V7X_SUBLANES, 1024))
        ai = jnp.broadcast_to(ai_ref[:, 1024 * hh:1024 * (hh + 1)], (V7X_SUBLANES, 1024))
        hr = h_ref[:, re0:re0 + 1024]
        hi = h_ref[:, im0:im0 + 1024]
        for t in range(S5_STEPS):
            r0 = t * V7X_SUBLANES
            xr = hs_ref[r0:r0 + V7X_SUBLANES, re0:re0 + 1024]
            xi = hs_ref[r0:r0 + V7X_SUBLANES, im0:im0 + 1024]
            hr, hi = ar * hr - ai * hi + xr, ar * hi + ai * hr + xi
            hs_ref[r0:r0 + V7X_SUBLANES, re0:re0 + 1024] = hr
            hs_ref[r0:r0 + V7X_SUBLANES, im0:im0 + 1024] = hi
        h_ref[:, re0:re0 + 1024] = hr
        h_ref[:, im0:im0 + 1024] = hi
        ys.append(_dot(hs_ref[:, re0:re0 + 2048].astype(BF16), cm_ref[hh]))

    y = jnp.concatenate(ys, axis=1)
    y = y + d_ref[...] * uf
    y = 0.5 * y * (1.0 + jnp.tanh(math.sqrt(2.0 / math.pi) * (y + 0.044715 * (y * y * y))))
    z = _dot(y.astype(BF16), wg_ref[...]) + bgl_ref[...]

    @pl.when(i >= 2)
    def _():
        for cp in y_copies(i - 2, slot):
            cp.wait()

    ybuf[slot] = (y * jax.nn.sigmoid(z)).reshape(S5_STEPS, nb, ybuf.shape[3])
    for cp in y_copies(i, slot):
        cp.start()

    @pl.when(i == n - 1)
    def _():
        for cp in y_copies(i, slot):
            cp.wait()

    @pl.when(jnp.logical_and(i == n - 1, i >= 1))
    def _():
        for cp in y_copies(i - 1, 1 - slot):
            cp.wait()


def _s5(u, bmat, ar, ai, cmat, d_skip, w_glu_bf, b_glu):
    B, L, C = u.shape
    R = S5_STEPS * B
    full = lambda shape: pl.BlockSpec(shape, lambda i: (0,) * len(shape))
    return pl.pallas_call(
        _s5_kernel,
        grid=(L // S5_STEPS,),
        in_specs=[
            pl.BlockSpec(memory_space=pl.ANY),
            full(bmat.shape), full(ar.shape), full(ai.shape), full(cmat.shape),
            full(d_skip.shape), full(w_glu_bf.shape), full(b_glu.shape),
        ],
        out_specs=pl.BlockSpec(memory_space=pl.ANY),
        out_shape=jax.ShapeDtypeStruct((B, L, C), F32),
        scratch_shapes=[pltpu.VMEM((2, S5_STEPS, B, C), F32), pltpu.VMEM((2, S5_STEPS, B, C), F32),
                        pltpu.VMEM((R, 4096), F32), pltpu.VMEM((B, 4096), F32),
                        pltpu.SemaphoreType.DMA((2,)), pltpu.SemaphoreType.DMA((2,))],
        compiler_params=_params(("arbitrary",)),
        name="s5",
    )(u, bmat, ar, ai, cmat, d_skip, w_glu_bf, b_glu)


def _attn_kernel(q_ref, k_ref, vt_ref, lam_ref, sg_ref, o_ref, qm_ref, acc_ref, ml_ref, s_ref):
    i = pl.program_id(1)
    n_chain = 2 * ATT_HEADS
    lamv = lam_ref[...]
    lam = (jnp.exp(jnp.sum(lamv[0:1] * lamv[1:2], axis=-1, keepdims=True))
           - jnp.exp(jnp.sum(lamv[2:3] * lamv[3:4], axis=-1, keepdims=True)) + LAMBDA_INIT)
    lane = lax.broadcasted_iota(I32, (TQ, 128), 1)
    key_chunk = lax.broadcasted_iota(I32, (TK, TQ), 0) // CHUNK
    qry_chunk = lax.broadcasted_iota(I32, (TK, TQ), 1) // CHUNK
    diag_ok = key_chunk <= qry_chunk

    for hd in range(ATT_HEADS):
        qh = q_ref[:, 128 * hd:128 * (hd + 1)]
        zero = jnp.zeros_like(qh)
        qm_ref[2 * hd] = jnp.where(lane < ATT_SUB_DIM, qh, zero)
        qm_ref[2 * hd + 1] = jnp.where(lane >= ATT_SUB_DIM, qh, zero)
    acc_ref[...] = jnp.zeros_like(acc_ref)
    row = lax.broadcasted_iota(I32, (2 * n_chain, TQ), 0)
    ml_ref[...] = jnp.where(row % 2 == 0, NEG, 0.0)

    def scores(j, par):
        off = pl.multiple_of(j * TK, TK)
        for hd in range(ATT_HEADS):
            ks = k_ref[pl.ds(off, TK), 128 * hd:128 * (hd + 1)]
            for s in range(2):
                s_ref[par, 2 * hd + s] = _dot_nt(ks, qm_ref[2 * hd + s])

    def consume(j, par, masked):
        off = pl.multiple_of(j * TK, TK)
        ps, alphas = [], []
        for c in range(n_chain):
            st = s_ref[par, c]
            if masked:
                st = jnp.where(diag_ok, st, NEG)
            m = ml_ref[2 * c:2 * c + 1, :]
            l = ml_ref[2 * c + 1:2 * c + 2, :]
            mn = jnp.maximum(m, jnp.max(st, axis=0, keepdims=True))
            alpha = jnp.exp2(m - mn)
            p = jnp.exp2(st - mn)
            ml_ref[2 * c:2 * c + 1, :] = mn
            ml_ref[2 * c + 1:2 * c + 2, :] = alpha * l + jnp.sum(p, axis=0, keepdims=True)
            ps.append(p.astype(BF16))
            alphas.append(alpha)
        for c in range(n_chain):
            hd = c // 2
            vts = vt_ref[128 * hd:128 * (hd + 1), pl.ds(off, TK)]
            acc_ref[c] = acc_ref[c] * alphas[c] + _dot(vts, ps[c])

    scores(0, 0)

    def body(jj, carry):
        j = 2 * jj
        scores(j + 1, 1)
        consume(j, 0, False)
        scores(j + 2, 0)
        consume(j + 1, 1, False)
        return carry

    lax.fori_loop(0, i // 2, body, 0)

    @pl.when(i % 2 == 1)
    def _():
        scores(i, 1)
        consume(i - 1, 0, False)
        consume(i, 1, True)

    @pl.when(i % 2 == 0)
    def _():
        consume(i, 0, True)

    for hd in range(ATT_HEADS):
        c0 = 128 * hd
        l1 = ml_ref[4 * hd + 1:4 * hd + 2, :]
        l2 = ml_ref[4 * hd + 3:4 * hd + 4, :]
        o = acc_ref[2 * hd] / l1 - lam * (acc_ref[2 * hd + 1] / l2)
        ms = jnp.mean(o * o, axis=0, keepdims=True)
        on = (o * lax.rsqrt(ms + RMS_EPS)).T
        o_ref[:, c0:c0 + 128] = (on * sg_ref[:, c0:c0 + 128] * (1.0 - LAMBDA_INIT)).astype(BF16)


def _attn(q, k, vt, lam4, sg):
    B, L, W = q.shape
    n_chain = 2 * ATT_HEADS
    return pl.pallas_call(
        _attn_kernel,
        grid=(B, L // TQ),
        in_specs=[
            pl.BlockSpec((None, TQ, W), lambda b, i: (b, i, 0)),
            pl.BlockSpec((None, L, W), lambda b, i: (b, 0, 0)),
            pl.BlockSpec((None, W, L), lambda b, i: (b, 0, 0)),
            pl.BlockSpec((4, ATT_SUB_DIM), lambda b, i: (0, 0)),
            pl.BlockSpec((1, W), lambda b, i: (0, 0)),
        ],
        out_specs=pl.BlockSpec((None, TQ, W), lambda b, i: (b, i, 0)),
        out_shape=jax.ShapeDtypeStruct((B, L, W), BF16),
        scratch_shapes=[pltpu.VMEM((n_chain, TQ, 128), BF16),
                        pltpu.VMEM((n_chain, ATT_V_DIM, TQ), F32),
                        pltpu.VMEM((2 * n_chain, TQ), F32),
                        pltpu.VMEM((2, n_chain, TK, TQ), F32)],
        compiler_params=_params(("arbitrary", "arbitrary")),
        name="diff_attn",
    )(q, k, vt, lam4, sg)


def _mix_kernel(x_ref, ys_ref, ya_ref, gate_ref, wps_ref, wpa_ref, wo_ref, g2_ref, wrt_ref, br_ref,
                sut_ref, lt_ref, h_ref, xs_ref, slot_ref, w_ref, oct_ref):
    tile = pl.program_id(0) * pl.num_programs(1) + pl.program_id(1)

    D = x_ref.shape[-1]
    tl = x_ref.shape[0]
    gs = gate_ref[:, 0:D].astype(F32)
    ga = gate_ref[:, D:2 * D].astype(F32)
    mixed = gs * _dot(ys_ref[...].astype(BF16), wps_ref[...]) + ga * _dot(ya_ref[...], wpa_ref[...])
    h = x_ref[...] + _dot(mixed.astype(BF16), wo_ref[...])
    h_ref[...] = h
    ms = jnp.mean(h * h, axis=-1, keepdims=True)
    hb = (h * lax.rsqrt(ms + RMS_EPS) * g2_ref[...]).astype(BF16)

    lg = _dot_nt(wrt_ref[...], hb) + br_ref[...]
    eio = lax.broadcasted_iota(I32, (N_EXPERTS, tl), 0)
    vals, sels = [], []
    for k in range(TOP_K):
        m = jnp.max(lg, axis=0, keepdims=True)
        idx = jnp.min(jnp.where(lg == m, eio, N_EXPERTS), axis=0, keepdims=True)
        sel = eio == idx
        vals.append(m)
        sels.append(sel)
        lg = jnp.where(sel, -jnp.inf, lg)
    ex = [jnp.exp(v - vals[0]) for v in vals]
    den = ex[0] + ex[1] + ex[2] + ex[3]
    for k in range(TOP_K):
        w_ref[k:k + 1, :] = ex[k] / den

    mtot = jnp.zeros((N_EXPERTS, tl), F32)
    for k in range(TOP_K):
        mtot = mtot + jnp.where(sels[k], 1.0, 0.0)
    pre = _dot(mtot.astype(BF16), sut_ref[...])
    cnt = jnp.sum(mtot, axis=1, keepdims=True)
    seg_oct = jnp.floor((cnt + (OCT - 1.0)) * (1.0 / OCT))
    seg_b = jnp.broadcast_to(seg_oct, (N_EXPERTS, V7X_LANES))
    start_oct = _dot(lt_ref[...], seg_b.astype(BF16))
    base = pre + start_oct[:, 0:1] * float(OCT)
    slots = []
    for k in range(TOP_K):
        s_k = jnp.sum(jnp.where(sels[k], base, 0.0), axis=0, keepdims=True).astype(I32)
        slot_ref[k:k + 1, :] = s_k
        slots.append(s_k)

    for c in range(SLOTS // V7X_MXU_DIM):
        sio = lax.broadcasted_iota(I32, (V7X_MXU_DIM, tl), 0) + V7X_MXU_DIM * c
        p = jnp.where(sio == slots[0], 1.0, 0.0)
        for k in range(1, TOP_K):
            p = jnp.where(sio == slots[k], 1.0, p)
        xs_ref[V7X_MXU_DIM * c:V7X_MXU_DIM * (c + 1), :] = _dot(p.astype(BF16), hb)

    lane = lax.broadcasted_iota(I32, (N_EXPERTS, V7X_LANES), 1)

    @pl.when(tile == 0)
    def _():
        oct_ref[...] = jnp.zeros_like(oct_ref)

    oct_ref[...] = jnp.where(lane == tile, seg_b, oct_ref[...])


def _mix(x, ys_tm, ya, gates, wps, wpa, wo, g2, wrt, br, sut, lt):
    B, L, D = x.shape
    nl = L // TL_PROJ
    n_tok = B * L
    full = lambda shape: pl.BlockSpec(shape, lambda b, l: (0,) * len(shape))
    tok_spec = pl.BlockSpec((TOP_K, TL_PROJ), lambda b, l: (0, b * nl + l))
    return pl.pallas_call(
        _mix_kernel,
        grid=(B, nl),
        in_specs=[
            pl.BlockSpec((None, TL_PROJ, D), lambda b, l: (b, l, 0)),
            pl.BlockSpec((None, TL_PROJ, 512), lambda b, l: (b, l, 0)),
            pl.BlockSpec((None, TL_PROJ, 512), lambda b, l: (b, l, 0)),
            pl.BlockSpec((None, TL_PROJ, 2 * D), lambda b, l: (b, l, 0)),
            full(wps.shape), full(wpa.shape), full(wo.shape), full((1, D)),
            full(wrt.shape), full(br.shape), full(sut.shape), full(lt.shape),
        ],
        out_specs=[
            pl.BlockSpec((None, TL_PROJ, D), lambda b, l: (b, l, 0)),
            pl.BlockSpec((SLOTS, D), lambda b, l: (b * nl + l, 0)),
            tok_spec, tok_spec,
            full((N_EXPERTS, V7X_LANES)),
        ],
        out_shape=[
            jax.ShapeDtypeStruct((B, L, D), F32),
            jax.ShapeDtypeStruct((B * nl * SLOTS, D), F32),
            jax.ShapeDtypeStruct((TOP_K, n_tok), I32),
            jax.ShapeDtypeStruct((TOP_K, n_tok), F32),
            jax.ShapeDtypeStruct((N_EXPERTS, V7X_LANES), F32),
        ],
        compiler_params=_params(("arbitrary", "arbitrary")),
        name="mix_router",
    )(x, ys_tm, ya, gates, wps, wpa, wo, g2, wrt, br, sut, lt)


def _moe_tables(oct_tab, n_t, n_blocks):
    E = N_EXPERTS
    O = oct_tab[:, :n_t].astype(I32)
    lstart = jnp.cumsum(O, axis=0) - O
    ecum = jnp.cumsum(O, axis=1) - O
    tot = jnp.sum(O, axis=1)
    nb = (tot + BLK_OCT - 1) // BLK_OCT
    bend = jnp.cumsum(nb)
    bstart = bend - nb
    n_valid = bend[-1]
    bi = jnp.arange(n_blocks, dtype=I32)
    er = jnp.arange(E, dtype=I32)
    tr = jnp.arange(n_t, dtype=I32)
    be = jnp.minimum(jnp.sum((bi[:, None] >= bend[None, :]).astype(I32), axis=1), E - 1)
    last_e = jnp.sum(jnp.where(bi == jnp.maximum(n_valid - 1, 0), be, 0))
    be = jnp.where(bi < n_valid, be, last_e)
    oh_e = be[:, None] == er[None, :]
    pick = lambda tab: jnp.sum(jnp.where(oh_e[:, :, None], tab[None, :, :], 0), axis=1)
    ecum_i, o_i, lst_i = pick(ecum), pick(O), pick(lstart)
    tot_i = jnp.sum(jnp.where(oh_e, tot[None, :], 0), axis=1)
    bst_i = jnp.sum(jnp.where(oh_e, bstart[None, :], 0), axis=1)
    g = (bi - bst_i)[:, None] * BLK_OCT + jnp.arange(BLK_OCT, dtype=I32)[None, :]
    tau = jnp.minimum(jnp.sum((g[:, :, None] >= (ecum_i + o_i)[:, None, :]).astype(I32), axis=2), n_t - 1)
    off = tr[None, :] * SL_OCT + lst_i - ecum_i
    src = jnp.sum(jnp.where(tau[:, :, None] == tr[None, None, :], off[:, None, :], 0), axis=2) + g
    src = jnp.where((g < tot_i[:, None]) & (bi[:, None] < n_valid), src, 0)
    s = jnp.arange(SL_OCT, dtype=I32)
    lend_t = (lstart + O).T
    e_s = jnp.minimum(jnp.sum((s[None, :, None] >= lend_t[:, None, :]).astype(I32), axis=2), E - 1)
    offc = (bstart[:, None] * BLK_OCT + ecum - lstart).T
    csrc = jnp.sum(jnp.where(e_s[:, :, None] == er[None, None, :], offc[:, None, :], 0), axis=2) + s[None, :]
    csrc = jnp.where(s[None, :] < jnp.sum(O, axis=0)[:, None], csrc, 0)
    prev_e = jnp.concatenate([jnp.full((1,), -1, I32), be[:-1]])
    first = ((be != prev_e) & (bi < n_valid)).astype(I32)
    par = (jnp.cumsum(first) - 1) % 2
    later = (er[None, :] > er[:, None]) & (nb[None, :] > 0)
    nxt_of_e = jnp.min(jnp.where(later, er[None, :], E), axis=1)
    nxt_of_e = jnp.where(nxt_of_e < E, nxt_of_e, -1)
    nxt = jnp.sum(jnp.where(oh_e, nxt_of_e[None, :], 0), axis=1)
    meta = (be, first, par.astype(I32), nxt.astype(I32), n_valid.astype(I32).reshape(1))
    return meta, src, csrc


def _gather_octets(src_ref, n_oct, src_hbm, dst, sem, first=0):
    for q in range(first, first + n_oct):
        row = pl.multiple_of(src_ref[0, q] * OCT, OCT)
        pltpu.make_async_copy(src_hbm.at[pl.ds(row, OCT), :], dst.at[pl.ds(OCT * q, OCT), :], sem).start()


def _expert_kernel(be_ref, first_ref, par_ref, nxt_ref, nv_ref, src_cur_ref, src_nxt_ref, xs_hbm,
                   wup_hbm, bup_ref, wdn_hbm, bdn_ref, perm_ref, y_ref,
                   xbuf, wup_st, wdn_st, wup_bf, wdn_bf, act_ref, sem, wsem):
    i = pl.program_id(0)
    nv = nv_ref[0]
    slot = lax.rem(i, 2)

    def weight_copies(e, s):
        return (pltpu.make_async_copy(wup_hbm.at[e], wup_st.at[s], wsem.at[0, s]),
                pltpu.make_async_copy(wdn_hbm.at[e], wdn_st.at[s], wsem.at[1, s]))

    @pl.when(i == 0)
    def _():
        _gather_octets(src_cur_ref, BLK_OCT, xs_hbm, xbuf.at[0], sem.at[0])
        for cp in weight_copies(be_ref[0], par_ref[0]):
            cp.start()

    def block_rows_wait(s):
        pltpu.make_async_copy(xs_hbm.at[pl.ds(0, MOE_ROWS), :], xbuf.at[s], sem.at[s]).wait()

    @pl.when(i < nv)
    def _():
        @pl.when(first_ref[i] == 1)
        def _():
            s = par_ref[i]
            for cp in weight_copies(be_ref[i], s):
                cp.wait()

            @pl.when(nxt_ref[i] >= 0)
            def _():
                for cp in weight_copies(nxt_ref[i], 1 - s):
                    cp.start()

            for cb in range(8):
                blk = wup_st[s, :, 256 * cb:256 * (cb + 1)].astype(BF16)
                wup_bf[:, 256 * cb:256 * (cb + 1)] = _dot(blk, perm_ref[...]).astype(BF16)
            wdn_bf[...] = wdn_st[s].astype(BF16)

        block_rows_wait(slot)
        x = xbuf[slot].astype(BF16)
        for cb in range(8):
            hp = _dot(x, wup_bf[:, 256 * cb:256 * (cb + 1)]) + bup_ref[:, 256 * cb:256 * (cb + 1)]
            glu = jnp.minimum(hp[:, 0:128], SWIGLU_LIMIT)
            lin = jnp.clip(hp[:, 128:256], -SWIGLU_LIMIT, SWIGLU_LIMIT)
            act = glu * jax.nn.sigmoid(SWIGLU_ALPHA * glu) * (lin + 1.0)
            act_ref[:, 128 * cb:128 * (cb + 1)] = act.astype(BF16)
            _gather_octets(src_nxt_ref, BLK_OCT // 8, xs_hbm, xbuf.at[1 - slot], sem.at[1 - slot],
                           first=cb * (BLK_OCT // 8))
        y_ref[...] = _dot(act_ref[...], wdn_bf[...]) + bdn_ref[...]

        @pl.when(i == pl.num_programs(0) - 1)
        def _():
            block_rows_wait(1 - slot)

    @pl.when(i >= nv)
    def _():
        y_ref[...] = jnp.zeros_like(y_ref)

        @pl.when(i == nv)
        def _():
            block_rows_wait(slot)


def _experts(meta, src3, xs, w_up, b_up_p, w_down, b_down, perm):
    n_blocks = src3.shape[0]
    DE2 = w_up.shape[2]
    D = w_up.shape[1]
    tab = src3.shape[2]
    be_map = lambda i, be, *_: (be[i], 0, 0)
    grid_spec = pltpu.PrefetchScalarGridSpec(
        num_scalar_prefetch=len(meta),
        grid=(n_blocks,),
        in_specs=[
            pl.BlockSpec((None, 1, tab), lambda i, *_: (i, 0, 0), memory_space=pltpu.SMEM),
            pl.BlockSpec((None, 1, tab), lambda i, *_: (jnp.minimum(i + 1, n_blocks - 1), 0, 0),
                         memory_space=pltpu.SMEM),
            pl.BlockSpec(memory_space=pl.ANY),
            pl.BlockSpec(memory_space=pl.ANY),
            pl.BlockSpec((None, 1, DE2), be_map),
            pl.BlockSpec(memory_space=pl.ANY),
            pl.BlockSpec((None, 1, D), be_map),
            pl.BlockSpec((256, 256), lambda i, *_: (0, 0)),
        ],
        out_specs=pl.BlockSpec((MOE_ROWS, D), lambda i, *_: (i, 0)),
        scratch_shapes=[
            pltpu.VMEM((2, MOE_ROWS, D), F32),
            pltpu.VMEM((2, D, DE2), F32),
            pltpu.VMEM((2, DE2 // 2, D), F32),
            pltpu.VMEM((D, DE2), BF16),
            pltpu.VMEM((DE2 // 2, D), BF16),
            pltpu.VMEM((MOE_ROWS, DE2 // 2), BF16),
            pltpu.SemaphoreType.DMA((2,)),
            pltpu.SemaphoreType.DMA((2, 2)),
        ],
    )
    return pl.pallas_call(
        _expert_kernel,
        grid_spec=grid_spec,
        out_shape=jax.ShapeDtypeStruct((n_blocks * MOE_ROWS, D), F32),
        compiler_params=_params(("arbitrary",)),
        name="experts",
    )(*meta, src3, src3, xs, w_up, b_up_p, w_down, b_down, perm)


def _combine_kernel(src_cur_ref, src_nxt_ref, h_ref, slot_ref, w_ref, ys_hbm, o_ref,
                    ybuf, wmat, sl_b, w_b, sem):
    i = pl.program_id(0)
    n = pl.num_programs(0)
    slot = lax.rem(i, 2)

    @pl.when(i == 0)
    def _():
        _gather_octets(src_cur_ref, SL_OCT, ys_hbm, ybuf.at[0], sem.at[0])

    tl = h_ref.shape[0]
    for k in range(TOP_K):
        sl_b[k] = jnp.broadcast_to(slot_ref[:, k:k + 1], (tl, V7X_LANES))
        w_b[k] = jnp.broadcast_to(w_ref[:, k:k + 1], (tl, V7X_LANES))
    lane = lax.broadcasted_iota(I32, (tl, V7X_LANES), 1)
    n_chunk = SLOTS // V7X_LANES
    per_chunk = SL_OCT // n_chunk
    for c in range(n_chunk):
        sio = lane + V7X_LANES * c
        wm = jnp.where(sio == sl_b[0], w_b[0], 0.0)
        for k in range(1, TOP_K):
            wm = jnp.where(sio == sl_b[k], w_b[k], wm)
        wmat[:, V7X_LANES * c:V7X_LANES * (c + 1)] = wm.astype(BF16)
        _gather_octets(src_nxt_ref, per_chunk, ys_hbm, ybuf.at[1 - slot], sem.at[1 - slot],
                       first=c * per_chunk)

    def tile_rows_wait(s):
        pltpu.make_async_copy(ys_hbm.at[pl.ds(0, SLOTS), :], ybuf.at[s], sem.at[s]).wait()

    tile_rows_wait(slot)
    o_ref[...] = h_ref[...] + _dot(wmat[...], ybuf[slot].astype(BF16))

    @pl.when(i == n - 1)
    def _():
        tile_rows_wait(1 - slot)


def _combine(csrc3, h2, slot_tok, w_tok, ys):
    n_tok, D = h2.shape
    n_steps = n_tok // TL_PROJ
    tab = csrc3.shape[2]
    return pl.pallas_call(
        _combine_kernel,
        grid=(n_steps,),
        in_specs=[
            pl.BlockSpec((None, 1, tab), lambda i: (i, 0, 0), memory_space=pltpu.SMEM),
            pl.BlockSpec((None, 1, tab), lambda i: (jnp.minimum(i + 1, n_steps - 1), 0, 0),
                         memory_space=pltpu.SMEM),
            pl.BlockSpec((TL_PROJ, D), lambda i: (i, 0)),
            pl.BlockSpec((TL_PROJ, TOP_K), lambda i: (i, 0)),
            pl.BlockSpec((TL_PROJ, TOP_K), lambda i: (i, 0)),
            pl.BlockSpec(memory_space=pl.ANY),
        ],
        out_specs=pl.BlockSpec((TL_PROJ, D), lambda i: (i, 0)),
        out_shape=jax.ShapeDtypeStruct((n_tok, D), F32),
        scratch_shapes=[pltpu.VMEM((2, SLOTS, D), F32), pltpu.VMEM((TL_PROJ, SLOTS), BF16),
                        pltpu.VMEM((TOP_K, TL_PROJ, V7X_LANES), I32),
                        pltpu.VMEM((TOP_K, TL_PROJ, V7X_LANES), F32),
                        pltpu.SemaphoreType.DMA((2,))],
        compiler_params=_params(("arbitrary",)),
        name="combine",
    )(csrc3, csrc3, h2, slot_tok, w_tok, ys)


def _s5_matrices(a_re, a_im, log_dt, b_re, b_im, c_re, c_im):
    G, P = a_re.shape
    dt = jnp.exp(log_dt.astype(F32))[:, None]
    lr = a_re.astype(F32)
    li = a_im.astype(F32)
    mag = jnp.exp(lr * dt)
    abar_r = mag * jnp.cos(li * dt)
    abar_i = mag * jnp.sin(li * dt)
    den = lr * lr + li * li
    nr = abar_r - 1.0
    ni = abar_i
    coef_r = (nr * lr + ni * li) / den
    coef_i = (ni * lr - nr * li) / den
    br_ = b_re.astype(F32)
    bi_ = b_im.astype(F32)
    bbar_r = coef_r[..., None] * br_ - coef_i[..., None] * bi_
    bbar_i = coef_r[..., None] * bi_ + coef_i[..., None] * br_
    eye = jnp.eye(G // 2, dtype=F32)

    def in_mat(bb):
        bb = bb.reshape(2, G // 2, P, SSM_GROUP)
        return jnp.einsum('xgph,gk->xghkp', bb, eye).reshape(2, (G // 2) * SSM_GROUP, (G // 2) * P)

    def out_mat(cc):
        cc = cc.reshape(2, G // 2, SSM_GROUP, P)
        return jnp.einsum('xghp,gk->xgpkh', cc, eye).reshape(2, (G // 2) * P, (G // 2) * SSM_GROUP)

    bmat = jnp.concatenate([in_mat(bbar_r), in_mat(bbar_i)], axis=-1).astype(BF16)
    cmat = jnp.concatenate([out_mat(c_re.astype(F32)), -out_mat(c_im.astype(F32))], axis=1).astype(BF16)
    return bmat, abar_r.reshape(1, G * P), abar_i.reshape(1, G * P), cmat


def _pad_lanes(t):
    n, w = t.shape
    wp = ((w + V7X_LANES - 1) // V7X_LANES) * V7X_LANES
    return jnp.pad(t, ((0, 0), (0, wp - w))).reshape(n, 1, wp)


def kernel(x, norm1_g, w_in, b_gate, ssm_a_re, ssm_a_im, ssm_log_dt, ssm_b_re, ssm_b_im, ssm_c_re, ssm_c_im, ssm_d, ssm_w_glu, ssm_b_glu, q_norm_g, k_norm_g, lambda_q1, lambda_k1, lambda_q2, lambda_k2, subln_g, w_proj_ssm, w_proj_att, w_out, norm2_g, w_router, b_router, w_up, b_up, w_down, b_down):
    B, L, D = x.shape
    assert B == V7X_SUBLANES and D == 1024 and L % TL_PROJ == 0 and w_in.shape[0] == 1
    n_tok = B * L
    n_t = n_tok // TL_PROJ
    assert n_t <= V7X_LANES
    l = 0

    scale = ATT_SUB_DIM ** -0.5
    qg = (jnp.tile(q_norm_g[l].astype(F32), 2 * ATT_HEADS) * (scale * math.log2(math.e))).reshape(1, 512)
    kg = jnp.tile(k_norm_g[l].astype(F32), 2 * ATT_HEADS).reshape(1, 512)
    blk = jnp.arange(256) // ATT_SUB_DIM
    ones_bd = (blk[:, None] == blk[None, :]).astype(BF16)
    u_tm, q, k, vt, gates = _inproj(x, norm1_g[l].reshape(1, D), w_in[l].astype(BF16),
                                    b_gate[l].reshape(1, 2 * D), qg, kg, ones_bd)

    bmat, ar, ai, cmat = _s5_matrices(ssm_a_re[l], ssm_a_im[l], ssm_log_dt[l], ssm_b_re[l], ssm_b_im[l],
                                      ssm_c_re[l], ssm_c_im[l])
    y_ssm = _s5(u_tm, bmat, ar, ai, cmat, ssm_d[l].reshape(1, 512).astype(F32),
                ssm_w_glu[l].astype(BF16), ssm_b_glu[l].reshape(1, 512).astype(F32))

    lam4 = jnp.stack([lambda_q1[l], lambda_k1[l], lambda_q2[l], lambda_k2[l]]).astype(F32)
    sg = jnp.tile(subln_g[l].astype(F32), ATT_HEADS).reshape(1, 512)
    y_att = _attn(q, k, vt, lam4, sg)

    tpos = jnp.arange(TL_PROJ)
    sut = (tpos[:, None] < tpos[None, :]).astype(BF16)
    epos = jnp.arange(N_EXPERTS)
    lt = (epos[None, :] < epos[:, None]).astype(BF16)
    h, xs, slot, top_w, oct_tab = _mix(
        x, y_ssm, y_att, gates,
        w_proj_ssm[l].astype(BF16), w_proj_att[l].astype(BF16), w_out[l].astype(BF16),
        norm2_g[l].reshape(1, D), w_router[l].T.astype(BF16), b_router[l].reshape(N_EXPERTS, 1).astype(F32),
        sut, lt)

    n_blocks = (n_tok * TOP_K + n_t * N_EXPERTS * (OCT - 1) + MOE_ROWS - 1) // MOE_ROWS + N_EXPERTS
    meta, src, csrc = _moe_tables(oct_tab, n_t, n_blocks)

    de2 = w_up.shape[-1]
    b_up_p = b_up[l].reshape(N_EXPERTS, de2 // 256, 128, 2).transpose(0, 1, 3, 2).reshape(N_EXPERTS, 1, de2)
    pr = jnp.arange(256)
    col_src = jnp.where(pr < 128, 2 * pr, 2 * (pr - 128) + 1)
    perm = (jnp.arange(256)[:, None] == col_src[None, :]).astype(BF16)
    ys = _experts(meta, _pad_lanes(src), xs,
                  w_up[l], b_up_p, w_down[l], b_down[l].reshape(N_EXPERTS, 1, D), perm)

    out = _combine(_pad_lanes(csrc), h.reshape(n_tok, D), slot.T, top_w.T, ys)
    return out.reshape(B, L, D)
```

```python
import math

import jax
import jax.numpy as jnp
from jax import lax
from jax.experimental import pallas as pl
from jax.experimental.pallas import tpu as pltpu

F32 = jnp.float32
BF16 = jnp.bfloat16
I32 = jnp.int32

RMS_EPS = 1e-6
CHUNK = 64
SSM_GROUP = 16
SSM_STATE = 64
ATT_HEADS = 4
ATT_SUB_DIM = 64
ATT_V_DIM = 128
N_EXPERTS = 32
TOP_K = 4
SWIGLU_ALPHA = 1.702
SWIGLU_LIMIT = 7.0
LAMBDA_INIT = 0.8 - 0.6 * math.exp(-0.3 * 0)

V7X_SUBLANES = 8
V7X_LANES = 128
V7X_MXU_DIM = 256

TL_PROJ = 512
S5_STEPS = 64
TQ = 256
TK = 256
MOE_ROWS = 512
OCT = V7X_SUBLANES
BLK_OCT = MOE_ROWS // OCT
SLOTS = ((TL_PROJ * TOP_K + N_EXPERTS * (OCT - 1) + V7X_MXU_DIM - 1) // V7X_MXU_DIM) * V7X_MXU_DIM
SL_OCT = SLOTS // OCT
NEG = -1e30
VMEM_LIMIT = 56 * 1024 * 1024


def _dot(a, b):
    return jnp.dot(a, b, preferred_element_type=F32)


def _dot_nt(a, b):
    return lax.dot_general(a, b, (((1,), (1,)), ((), ())), preferred_element_type=F32)


def _params(sem, vmem=VMEM_LIMIT):
    return pltpu.CompilerParams(dimension_semantics=sem, vmem_limit_bytes=vmem)


def _inproj_kernel(x_ref, g1_ref, w_ref, bg_ref, qg_ref, kg_ref, ones_ref,
                   u_ref, q_ref, k_ref, vt_ref, gate_ref):
    x = x_ref[...]
    ms = jnp.mean(x * x, axis=-1, keepdims=True)
    xn = (x * lax.rsqrt(ms + RMS_EPS) * g1_ref[...]).astype(BF16)

    u_ref[...] = _dot(xn, w_ref[:, 0:512])

    def head_norm(z, g):
        sq = (z * z).astype(BF16)
        ss = jnp.concatenate([_dot(sq[:, 0:256], ones_ref[...]),
                              _dot(sq[:, 256:512], ones_ref[...])], axis=1)
        return z * lax.rsqrt(ss * (1.0 / ATT_SUB_DIM) + RMS_EPS) * g

    q_ref[...] = head_norm(_dot(xn, w_ref[:, 512:1024]), qg_ref[...]).astype(BF16)
    k_ref[...] = head_norm(_dot(xn, w_ref[:, 1024:1536]), kg_ref[...]).astype(BF16)
    vt_ref[...] = _dot(xn, w_ref[:, 1536:2048]).T.astype(BF16)
    for c in range(4):
        lo = 2048 + 512 * c
        z = _dot(xn, w_ref[:, lo:lo + 512]) + bg_ref[:, 512 * c:512 * (c + 1)]
        gate_ref[:, 512 * c:512 * (c + 1)] = jax.nn.sigmoid(z).astype(BF16)


def _inproj(x, g1, w_in_bf, b_gate, qg, kg, ones_bd):
    B, L, D = x.shape
    nl = L // TL_PROJ
    full = lambda shape: pl.BlockSpec(shape, lambda b, l: (0,) * len(shape))
    return pl.pallas_call(
        _inproj_kernel,
        grid=(B, nl),
        in_specs=[
            pl.BlockSpec((None, TL_PROJ, D), lambda b, l: (b, l, 0)),
            full((1, D)), full(w_in_bf.shape), full((1, 2 * D)),
            full((1, 512)), full((1, 512)), full((256, 256)),
        ],
        out_specs=[
            pl.BlockSpec((None, TL_PROJ, 512), lambda b, l: (b, l, 0)),
            pl.BlockSpec((None, TL_PROJ, 512), lambda b, l: (b, l, 0)),
            pl.BlockSpec((None, TL_PROJ, 512), lambda b, l: (b, l, 0)),
            pl.BlockSpec((None, 512, TL_PROJ), lambda b, l: (b, 0, l)),
            pl.BlockSpec((None, TL_PROJ, 2 * D), lambda b, l: (b, l, 0)),
        ],
        out_shape=[
            jax.ShapeDtypeStruct((B, L, 512), F32),
            jax.ShapeDtypeStruct((B, L, 512), BF16),
            jax.ShapeDtypeStruct((B, L, 512), BF16),
            jax.ShapeDtypeStruct((B, 512, L), BF16),
            jax.ShapeDtypeStruct((B, L, 2 * D), BF16),
        ],
        compiler_params=_params(("arbitrary", "arbitrary")),
        name="inproj",
    )(x, g1, w_in_bf, b_gate, qg, kg, ones_bd)


def _s5_kernel(u_hbm, bm_ref, ar_ref, ai_ref, cm_ref, d_ref, wg_ref, bgl_ref,
               y_hbm, ubuf, ybuf, hs_ref, h_ref, usem, ysem):
    i = pl.program_id(0)
    n = pl.num_programs(0)
    slot = lax.rem(i, 2)
    nb = ubuf.shape[2]

    def u_copies(step, s):
        t0 = pl.multiple_of(step * S5_STEPS, S5_STEPS)
        return [pltpu.make_async_copy(u_hbm.at[b, pl.ds(t0, S5_STEPS), :], ubuf.at[s, :, b, :], usem.at[s])
                for b in range(nb)]

    def y_copies(step, s):
        t0 = pl.multiple_of(step * S5_STEPS, S5_STEPS)
        return [pltpu.make_async_copy(ybuf.at[s, :, b, :], y_hbm.at[b, pl.ds(t0, S5_STEPS), :], ysem.at[s])
                for b in range(nb)]

    @pl.when(i == 0)
    def _():
        h_ref[...] = jnp.zeros_like(h_ref)
        for cp in u_copies(0, 0):
            cp.start()

    @pl.when(i + 1 < n)
    def _():
        for cp in u_copies(i + 1, 1 - slot):
            cp.start()

    for cp in u_copies(i, slot):
        cp.wait()
    uf = ubuf[slot].reshape(S5_STEPS * nb, ubuf.shape[3])
    u = uf.astype(BF16)
    for hh in range(2):
        hs_ref[:, 2048 * hh:2048 * (hh + 1)] = _dot(u[:, 256 * hh:256 * (hh + 1)], bm_ref[hh])

    ys = []
    for hh in range(2):
        re0 = 2048 * hh
        im0 = re0 + 1024
        ar = jnp.broadcast_to(ar_ref[:, 1024 * hh:1024 * (hh + 1)], (V7X_SUBLANES, 1024))
        ai = jnp.broadcast_to(ai_ref[:, 1024 * hh:1024 * (hh + 1)], (V7X_SUBLANES, 1024))
        hr = h_ref[:, re0:re0 + 1024]
        hi = h_ref[:, im0:im0 + 1024]
        for t in range(S5_STEPS):
            r0 = t * V7X_SUBLANES
            xr = hs_ref[r0:r0 + V7X_SUBLANES, re0:re0 + 1024]
            xi = hs_ref[r0:r0 + V7X_SUBLANES, im0:im0 + 1024]
            hr, hi = ar * hr - ai * hi + xr, ar * hi + ai * hr + xi
            hs_ref[r0:r0 + V7X_SUBLANES, re0:re0 + 1024] = hr
            hs_ref[r0:r0 + V7X_SUBLANES, im0:im0 + 1024] = hi
        h_ref[:, re0:re0 + 1024] = hr
        h_ref[:, im0:im0 + 1024] = hi
        ys.append(_dot(hs_ref[:, re0:re0 + 2048].astype(BF16), cm_ref[hh]))

    y = jnp.concatenate(ys, axis=1)
    y = y + d_ref[...] * uf
    y = 0.5 * y * (1.0 + jnp.tanh(math.sqrt(2.0 / math.pi) * (y + 0.044715 * (y * y * y))))
    z = _dot(y.astype(BF16), wg_ref[...]) + bgl_ref[...]

    @pl.when(i >= 2)
    def _():
        for cp in y_copies(i - 2, slot):
            cp.wait()

    ybuf[slot] = (y * jax.nn.sigmoid(z)).reshape(S5_STEPS, nb, ybuf.shape[3])
    for cp in y_copies(i, slot):
        cp.start()

    @pl.when(i == n - 1)
    def _():
        for cp in y_copies(i, slot):
            cp.wait()

    @pl.when(jnp.logical_and(i == n - 1, i >= 1))
    def _():
        for cp in y_copies(i - 1, 1 - slot):
            cp.wait()


def _s5(u, bmat, ar, ai, cmat, d_skip, w_glu_bf, b_glu):
    B, L, C = u.shape
    R = S5_STEPS * B
    full = lambda shape: pl.BlockSpec(shape, lambda i: (0,) * len(shape))
    return pl.pallas_call(
        _s5_kernel,
        grid=(L // S5_STEPS,),
        in_specs=[
            pl.BlockSpec(memory_space=pl.ANY),
            full(bmat.shape), full(ar.shape), full(ai.shape), full(cmat.shape),
            full(d_skip.shape), full(w_glu_bf.shape), full(b_glu.shape),
        ],
        out_specs=pl.BlockSpec(memory_space=pl.ANY),
        out_shape=jax.ShapeDtypeStruct((B, L, C), F32),
        scratch_shapes=[pltpu.VMEM((2, S5_STEPS, B, C), F32), pltpu.VMEM((2, S5_STEPS, B, C), F32),
                        pltpu.VMEM((R, 4096), F32), pltpu.VMEM((B, 4096), F32),
                        pltpu.SemaphoreType.DMA((2,)), pltpu.SemaphoreType.DMA((2,))],
        compiler_params=_params(("arbitrary",)),
        name="s5",
    )(u, bmat, ar, ai, cmat, d_skip, w_glu_bf, b_glu)


def _attn_kernel(q_ref, k_ref, vt_ref, lam_ref, sg_ref, o_ref, qm_ref, acc_ref, ml_ref, s_ref):
    i = pl.program_id(1)
    n_chain = 2 * ATT_HEADS
    lamv = lam_ref[...]
    lam = (jnp.exp(jnp.sum(lamv[0:1] * lamv[1:2], axis=-1, keepdims=True))
           - jnp.exp(jnp.sum(lamv[2:3] * lamv[3:4], axis=-1, keepdims=True)) + LAMBDA_INIT)
    lane = lax.broadcasted_iota(I32, (TQ, 128), 1)
    key_chunk = lax.broadcasted_iota(I32, (TK, TQ), 0) // CHUNK
    qry_chunk = lax.broadcasted_iota(I32, (TK, TQ), 1) // CHUNK
    diag_ok = key_chunk <= qry_chunk

    for hd in range(ATT_HEADS):
        qh = q_ref[:, 128 * hd:128 * (hd + 1)]
        zero = jnp.zeros_like(qh)
        qm_ref[2 * hd] = jnp.where(lane < ATT_SUB_DIM, qh, zero)
        qm_ref[2 * hd + 1] = jnp.where(lane >= ATT_SUB_DIM, qh, zero)
    acc_ref[...] = jnp.zeros_like(acc_ref)
    row = lax.broadcasted_iota(I32, (2 * n_chain, TQ), 0)
    ml_ref[...] = jnp.where(row % 2 == 0, NEG, 0.0)

    def scores(j, par):
        off = pl.multiple_of(j * TK, TK)
        for hd in range(ATT_HEADS):
            ks = k_ref[pl.ds(off, TK), 128 * hd:128 * (hd + 1)]
            for s in range(2):
                s_ref[par, 2 * hd + s] = _dot_nt(ks, qm_ref[2 * hd + s])

    def consume(j, par, masked):
        off = pl.multiple_of(j * TK, TK)
        ps, alphas = [], []
        for c in range(n_chain):
            st = s_ref[par, c]
            if masked:
                st = jnp.where(diag_ok, st, NEG)
            m = ml_ref[2 * c:2 * c + 1, :]
            l = ml_ref[2 * c + 1:2 * c + 2, :]
            mn = jnp.maximum(m, jnp.max(st, axis=0, keepdims=True))
            alpha = jnp.exp2(m - mn)
            p = jnp.exp2(st - mn)
            ml_ref[2 * c:2 * c + 1, :] = mn
            ml_ref[2 * c + 1:2 * c + 2, :] = alpha * l + jnp.sum(p, axis=0, keepdims=True)
            ps.append(p.astype(BF16))
            alphas.append(alpha)
        for c in range(n_chain):
            hd = c // 2
            vts = vt_ref[128 * hd:128 * (hd + 1), pl.ds(off, TK)]
            acc_ref[c] = acc_ref[c] * alphas[c] + _dot(vts, ps[c])

    scores(0, 0)

    def body(jj, carry):
        j = 2 * jj
        scores(j + 1, 1)
        consume(j, 0, False)
        scores(j + 2, 0)
        consume(j + 1, 1, False)
        return carry

    lax.fori_loop(0, i // 2, body, 0)

    @pl.when(i % 2 == 1)
    def _():
        scores(i, 1)
        consume(i - 1, 0, False)
        consume(i, 1, True)

    @pl.when(i % 2 == 0)
    def _():
        consume(i, 0, True)

    for hd in range(ATT_HEADS):
        c0 = 128 * hd
        l1 = ml_ref[4 * hd + 1:4 * hd + 2, :]
        l2 = ml_ref[4 * hd + 3:4 * hd + 4, :]
        o = acc_ref[2 * hd] / l1 - lam * (acc_ref[2 * hd + 1] / l2)
        ms = jnp.mean(o * o, axis=0, keepdims=True)
        on = (o * lax.rsqrt(ms + RMS_EPS)).T
        o_ref[:, c0:c0 + 128] = (on * sg_ref[:, c0:c0 + 128] * (1.0 - LAMBDA_INIT)).astype(BF16)


def _attn(q, k, vt, lam4, sg):
    B, L, W = q.shape
    n_chain = 2 * ATT_HEADS
    return pl.pallas_call(
        _attn_kernel,
        grid=(B, L // TQ),
        in_specs=[
            pl.BlockSpec((None, TQ, W), lambda b, i: (b, i, 0)),
            pl.BlockSpec((None, L, W), lambda b, i: (b, 0, 0)),
            pl.BlockSpec((None, W, L), lambda b, i: (b, 0, 0)),
            pl.BlockSpec((4, ATT_SUB_DIM), lambda b, i: (0, 0)),
            pl.BlockSpec((1, W), lambda b, i: (0, 0)),
        ],
        out_specs=pl.BlockSpec((None, TQ, W), lambda b, i: (b, i, 0)),
        out_shape=jax.ShapeDtypeStruct((B, L, W), BF16),
        scratch_shapes=[pltpu.VMEM((n_chain, TQ, 128), BF16),
                        pltpu.VMEM((n_chain, ATT_V_DIM, TQ), F32),
                        pltpu.VMEM((2 * n_chain, TQ), F32),
                        pltpu.VMEM((2, n_chain, TK, TQ), F32)],
        compiler_params=_params(("arbitrary", "arbitrary")),
        name="diff_attn",
    )(q, k, vt, lam4, sg)


def _mix_kernel(x_ref, ys_ref, ya_ref, gate_ref, wps_ref, wpa_ref, wo_ref, g2_ref, wrt_ref, br_ref,
                sut_ref, lt_ref, h_ref, xs_ref, slot_ref, w_ref, oct_ref):
    tile = pl.program_id(0) * pl.num_programs(1) + pl.program_id(1)

    D = x_ref.shape[-1]
    tl = x_ref.shape[0]
    gs = gate_ref[:, 0:D].astype(F32)
    ga = gate_ref[:, D:2 * D].astype(F32)
    mixed = gs * _dot(ys_ref[...].astype(BF16), wps_ref[...]) + ga * _dot(ya_ref[...], wpa_ref[...])
    h = x_ref[...] + _dot(mixed.astype(BF16), wo_ref[...])
    h_ref[...] = h
    ms = jnp.mean(h * h, axis=-1, keepdims=True)
    hb = (h * lax.rsqrt(ms + RMS_EPS) * g2_ref[...]).astype(BF16)

    lg = _dot_nt(wrt_ref[...], hb) + br_ref[...]
    eio = lax.broadcasted_iota(I32, (N_EXPERTS, tl), 0)
    vals, sels = [], []
    for k in range(TOP_K):
        m = jnp.max(lg, axis=0, keepdims=True)
        idx = jnp.min(jnp.where(lg == m, eio, N_EXPERTS), axis=0, keepdims=True)
        sel = eio == idx
        vals.append(m)
        sels.append(sel)
        lg = jnp.where(sel, -jnp.inf, lg)
    ex = [jnp.exp(v - vals[0]) for v in vals]
    den = ex[0] + ex[1] + ex[2] + ex[3]
    for k in range(TOP_K):
        w_ref[k:k + 1, :] = ex[k] / den

    mtot = jnp.zeros((N_EXPERTS, tl), F32)
    for k in range(TOP_K):
        mtot = mtot + jnp.where(sels[k], 1.0, 0.0)
    pre = _dot(mtot.astype(BF16), sut_ref[...])
    cnt = jnp.sum(mtot, axis=1, keepdims=True)
    seg_oct = jnp.floor((cnt + (OCT - 1.0)) * (1.0 / OCT))
    seg_b = jnp.broadcast_to(seg_oct, (N_EXPERTS, V7X_LANES))
    start_oct = _dot(lt_ref[...], seg_b.astype(BF16))
    base = pre + start_oct[:, 0:1] * float(OCT)
    slots = []
    for k in range(TOP_K):
        s_k = jnp.sum(jnp.where(sels[k], base, 0.0), axis=0, keepdims=True).astype(I32)
        slot_ref[k:k + 1, :] = s_k
        slots.append(s_k)

    for c in range(SLOTS // V7X_MXU_DIM):
        sio = lax.broadcasted_iota(I32, (V7X_MXU_DIM, tl), 0) + V7X_MXU_DIM * c
        p = jnp.where(sio == slots[0], 1.0, 0.0)
        for k in range(1, TOP_K):
            p = jnp.where(sio == slots[k], 1.0, p)
        xs_ref[V7X_MXU_DIM * c:V7X_MXU_DIM * (c + 1), :] = _dot(p.astype(BF16), hb)

    lane = lax.broadcasted_iota(I32, (N_EXPERTS, V7X_LANES), 1)

    @pl.when(tile == 0)
    def _():
        oct_ref[...] = jnp.zeros_like(oct_ref)

    oct_ref[...] = jnp.where(lane == tile, seg_b, oct_ref[...])


def _mix(x, ys_tm, ya, gates, wps, wpa, wo, g2, wrt, br, sut, lt):
    B, L, D = x.shape
    nl = L // TL_PROJ
    n_tok = B * L
    full = lambda shape: pl.BlockSpec(shape, lambda b, l: (0,) * len(shape))
    tok_spec = pl.BlockSpec((TOP_K, TL_PROJ), lambda b, l: (0, b * nl + l))
    return pl.pallas_call(
        _mix_kernel,
        grid=(B, nl),
        in_specs=[
            pl.BlockSpec((None, TL_PROJ, D), lambda b, l: (b, l, 0)),
            pl.BlockSpec((None, TL_PROJ, 512), lambda b, l: (b, l, 0)),
            pl.BlockSpec((None, TL_PROJ, 512), lambda b, l: (b, l, 0)),
            pl.BlockSpec((None, TL_PROJ, 2 * D), lambda b, l: (b, l, 0)),
            full(wps.shape), full(wpa.shape), full(wo.shape), full((1, D)),
            full(wrt.shape), full(br.shape), full(sut.shape), full(lt.shape),
        ],
        out_specs=[
            pl.BlockSpec((None, TL_PROJ, D), lambda b, l: (b, l, 0)),
            pl.BlockSpec((SLOTS, D), lambda b, l: (b * nl + l, 0)),
            tok_spec, tok_spec,
            full((N_EXPERTS, V7X_LANES)),
        ],
        out_shape=[
            jax.ShapeDtypeStruct((B, L, D), F32),
            jax.ShapeDtypeStruct((B * nl * SLOTS, D), F32),
            jax.ShapeDtypeStruct((TOP_K, n_tok), I32),
            jax.ShapeDtypeStruct((TOP_K, n_tok), F32),
            jax.ShapeDtypeStruct((N_EXPERTS, V7X_LANES), F32),
        ],
        compiler_params=_params(("arbitrary", "arbitrary")),
        name="mix_router",
    )(x, ys_tm, ya, gates, wps, wpa, wo, g2, wrt, br, sut, lt)


def _moe_tables(oct_tab, n_t, n_blocks):
    E = N_EXPERTS
    O = oct_tab[:, :n_t].astype(I32)
    lstart = jnp.cumsum(O, axis=0) - O
    ecum = jnp.cumsum(O, axis=1) - O
    tot = jnp.sum(O, axis=1)
    nb = (tot + BLK_OCT - 1) // BLK_OCT
    bend = jnp.cumsum(nb)
    bstart = bend - nb
    n_valid = bend[-1]
    bi = jnp.arange(n_blocks, dtype=I32)
    er = jnp.arange(E, dtype=I32)
    tr = jnp.arange(n_t, dtype=I32)
    be = jnp.minimum(jnp.sum((bi[:, None] >= bend[None, :]).astype(I32), axis=1), E - 1)
    last_e = jnp.sum(jnp.where(bi == jnp.maximum(n_valid - 1, 0), be, 0))
    be = jnp.where(bi < n_valid, be, last_e)
    oh_e = be[:, None] == er[None, :]
    pick = lambda tab: jnp.sum(jnp.where(oh_e[:, :, None], tab[None, :, :], 0), axis=1)
    ecum_i, o_i, lst_i = pick(ecum), pick(O), pick(lstart)
    tot_i = jnp.sum(jnp.where(oh_e, tot[None, :], 0), axis=1)
    bst_i = jnp.sum(jnp.where(oh_e, bstart[None, :], 0), axis=1)
    g = (bi - bst_i)[:, None] * BLK_OCT + jnp.arange(BLK_OCT, dtype=I32)[None, :]
    tau = jnp.minimum(jnp.sum((g[:, :, None] >= (ecum_i + o_i)[:, None, :]).astype(I32), axis=2), n_t - 1)
    off = tr[None, :] * SL_OCT + lst_i - ecum_i
    src = jnp.sum(jnp.where(tau[:, :, None] == tr[None, None, :], off[:, None, :], 0), axis=2) + g
    src = jnp.where((g < tot_i[:, None]) & (bi[:, None] < n_valid), src, 0)
    s = jnp.arange(SL_OCT, dtype=I32)
    lend_t = (lstart + O).T
    e_s = jnp.minimum(jnp.sum((s[None, :, None] >= lend_t[:, None, :]).astype(I32), axis=2), E - 1)
    offc = (bstart[:, None] * BLK_OCT + ecum - lstart).T
    csrc = jnp.sum(jnp.where(e_s[:, :, None] == er[None, None, :], offc[:, None, :], 0), axis=2) + s[None, :]
    csrc = jnp.where(s[None, :] < jnp.sum(O, axis=0)[:, None], csrc, 0)
    prev_e = jnp.concatenate([jnp.full((1,), -1, I32), be[:-1]])
    first = ((be != prev_e) & (bi < n_valid)).astype(I32)
    par = (jnp.cumsum(first) - 1) % 2
    later = (er[None, :] > er[:, None]) & (nb[None, :] > 0)
    nxt_of_e = jnp.min(jnp.where(later, er[None, :], E), axis=1)
    nxt_of_e = jnp.where(nxt_of_e < E, nxt_of_e, -1)
    nxt = jnp.sum(jnp.where(oh_e, nxt_of_e[None, :], 0), axis=1)
    meta = (be, first, par.astype(I32), nxt.astype(I32), n_valid.astype(I32).reshape(1))
    return meta, src, csrc


def _gather_octets(src_ref, n_oct, src_hbm, dst, sem, first=0):
    for q in range(first, first + n_oct):
        row = pl.multiple_of(src_ref[0, q] * OCT, OCT)
        pltpu.make_async_copy(src_hbm.at[pl.ds(row, OCT), :], dst.at[pl.ds(OCT * q, OCT), :], sem).start()


def _expert_kernel(be_ref, first_ref, par_ref, nxt_ref, nv_ref, src_cur_ref, src_nxt_ref, xs_hbm,
                   wup_hbm, bup_ref, wdn_hbm, bdn_ref, perm_ref, y_ref,
                   xbuf, wup_st, wdn_st, wup_bf, wdn_bf, act_ref, sem, wsem):
    i = pl.program_id(0)
    nv = nv_ref[0]
    slot = lax.rem(i, 2)

    def weight_copies(e, s):
        return (pltpu.make_async_copy(wup_hbm.at[e], wup_st.at[s], wsem.at[0, s]),
                pltpu.make_async_copy(wdn_hbm.at[e], wdn_st.at[s], wsem.at[1, s]))

    @pl.when(i == 0)
    def _():
        _gather_octets(src_cur_ref, BLK_OCT, xs_hbm, xbuf.at[0], sem.at[0])
        for cp in weight_copies(be_ref[0], par_ref[0]):
            cp.start()

    def block_rows_wait(s):
        pltpu.make_async_copy(xs_hbm.at[pl.ds(0, MOE_ROWS), :], xbuf.at[s], sem.at[s]).wait()

    @pl.when(i + 1 < nv)
    def _():
        _gather_octets(src_nxt_ref, BLK_OCT, xs_hbm, xbuf.at[1 - slot], sem.at[1 - slot])

    @pl.when(i < nv)
    def _():
        @pl.when(first_ref[i] == 1)
        def _():
            s = par_ref[i]
            for cp in weight_copies(be_ref[i], s):
                cp.wait()

            @pl.when(nxt_ref[i] >= 0)
            def _():
                for cp in weight_copies(nxt_ref[i], 1 - s):
                    cp.start()

            for cb in range(8):
                blk = wup_st[s, :, 256 * cb:256 * (cb + 1)].astype(BF16)
                wup_bf[:, 256 * cb:256 * (cb + 1)] = _dot(blk, perm_ref[...]).astype(BF16)
            wdn_bf[...] = wdn_st[s].astype(BF16)

        block_rows_wait(slot)
        x = xbuf[slot].astype(BF16)
        for cb in range(8):
            hp = _dot(x, wup_bf[:, 256 * cb:256 * (cb + 1)]) + bup_ref[:, 256 * cb:256 * (cb + 1)]
            glu = jnp.minimum(hp[:, 0:128], SWIGLU_LIMIT)
            lin = jnp.clip(hp[:, 128:256], -SWIGLU_LIMIT, SWIGLU_LIMIT)
            act = glu * jax.nn.sigmoid(SWIGLU_ALPHA * glu) * (lin + 1.0)
            act_ref[:, 128 * cb:128 * (cb + 1)] = act.astype(BF16)
        y_ref[...] = _dot(act_ref[...], wdn_bf[...]) + bdn_ref[...]

    @pl.when(i >= nv)
    def _():
        y_ref[...] = jnp.zeros_like(y_ref)


def _experts(meta, src3, xs, w_up, b_up_p, w_down, b_down, perm):
    n_blocks = src3.shape[0]
    DE2 = w_up.shape[2]
    D = w_up.shape[1]
    tab = src3.shape[2]
    be_map = lambda i, be, *_: (be[i], 0, 0)
    grid_spec = pltpu.PrefetchScalarGridSpec(
        num_scalar_prefetch=len(meta),
        grid=(n_blocks,),
        in_specs=[
            pl.BlockSpec((None, 1, tab), lambda i, *_: (i, 0, 0), memory_space=pltpu.SMEM),
            pl.BlockSpec((None, 1, tab), lambda i, *_: (jnp.minimum(i + 1, n_blocks - 1), 0, 0),
                         memory_space=pltpu.SMEM),
            pl.BlockSpec(memory_space=pl.ANY),
            pl.BlockSpec(memory_space=pl.ANY),
            pl.BlockSpec((None, 1, DE2), be_map),
            pl.BlockSpec(memory_space=pl.ANY),
            pl.BlockSpec((None, 1, D), be_map),
            pl.BlockSpec((256, 256), lambda i, *_: (0, 0)),
        ],
        out_specs=pl.BlockSpec((MOE_ROWS, D), lambda i, *_: (i, 0)),
        scratch_shapes=[
            pltpu.VMEM((2, MOE_ROWS, D), F32),
            pltpu.VMEM((2, D, DE2), F32),
            pltpu.VMEM((2, DE2 // 2, D), F32),
            pltpu.VMEM((D, DE2), BF16),
            pltpu.VMEM((DE2 // 2, D), BF16),
            pltpu.VMEM((MOE_ROWS, DE2 // 2), BF16),
            pltpu.SemaphoreType.DMA((2,)),
            pltpu.SemaphoreType.DMA((2, 2)),
        ],
    )
    return pl.pallas_call(
        _expert_kernel,
        grid_spec=grid_spec,
        out_shape=jax.ShapeDtypeStruct((n_blocks * MOE_ROWS, D), F32),
        compiler_params=_params(("arbitrary",)),
        name="experts",
    )(*meta, src3, src3, xs, w_up, b_up_p, w_down, b_down, perm)


def _combine_kernel(src_cur_ref, src_nxt_ref, h_ref, slot_ref, w_ref, ys_hbm, o_ref,
                    ybuf, sl_b, w_b, sem):
    i = pl.program_id(0)
    n = pl.num_programs(0)
    slot = lax.rem(i, 2)

    @pl.when(i == 0)
    def _():
        _gather_octets(src_cur_ref, SL_OCT, ys_hbm, ybuf.at[0], sem.at[0])

    @pl.when(i + 1 < n)
    def _():
        _gather_octets(src_nxt_ref, SL_OCT, ys_hbm, ybuf.at[1 - slot], sem.at[1 - slot])

    tl = h_ref.shape[0]
    for k in range(TOP_K):
        sl_b[k] = jnp.broadcast_to(slot_ref[:, k:k + 1], (tl, V7X_LANES))
        w_b[k] = jnp.broadcast_to(w_ref[:, k:k + 1], (tl, V7X_LANES))
    lane = lax.broadcasted_iota(I32, (tl, V7X_LANES), 1)
    pltpu.make_async_copy(ys_hbm.at[pl.ds(0, SLOTS), :], ybuf.at[slot], sem.at[slot]).wait()
    acc = h_ref[...]
    for c in range(SLOTS // V7X_MXU_DIM):
        halves = []
        for hc in range(V7X_MXU_DIM // V7X_LANES):
            sio = lane + (V7X_MXU_DIM * c + V7X_LANES * hc)
            wm = jnp.where(sio == sl_b[0], w_b[0], 0.0)
            for k in range(1, TOP_K):
                wm = jnp.where(sio == sl_b[k], w_b[k], wm)
            halves.append(wm.astype(BF16))
        yb = ybuf[slot, V7X_MXU_DIM * c:V7X_MXU_DIM * (c + 1), :].astype(BF16)
        acc = acc + _dot(jnp.concatenate(halves, axis=1), yb)
    o_ref[...] = acc


def _combine(csrc3, h2, slot_tok, w_tok, ys):
    n_tok, D = h2.shape
    n_steps = n_tok // TL_PROJ
    tab = csrc3.shape[2]
    return pl.pallas_call(
        _combine_kernel,
        grid=(n_steps,),
        in_specs=[
            pl.BlockSpec((None, 1, tab), lambda i: (i, 0, 0), memory_space=pltpu.SMEM),
            pl.BlockSpec((None, 1, tab), lambda i: (jnp.minimum(i + 1, n_steps - 1), 0, 0),
                         memory_space=pltpu.SMEM),
            pl.BlockSpec((TL_PROJ, D), lambda i: (i, 0)),
            pl.BlockSpec((TL_PROJ, TOP_K), lambda i: (i, 0)),
            pl.BlockSpec((TL_PROJ, TOP_K), lambda i: (i, 0)),
            pl.BlockSpec(memory_space=pl.ANY),
        ],
        out_specs=pl.BlockSpec((TL_PROJ, D), lambda i: (i, 0)),
        out_shape=jax.ShapeDtypeStruct((n_tok, D), F32),
        scratch_shapes=[pltpu.VMEM((2, SLOTS, D), F32),
                        pltpu.VMEM((TOP_K, TL_PROJ, V7X_LANES), I32),
                        pltpu.VMEM((TOP_K, TL_PROJ, V7X_LANES), F32),
                        pltpu.SemaphoreType.DMA((2,))],
        compiler_params=_params(("arbitrary",)),
        name="combine",
    )(csrc3, csrc3, h2, slot_tok, w_tok, ys)


def _s5_matrices(a_re, a_im, log_dt, b_re, b_im, c_re, c_im):
    G, P = a_re.shape
    dt = jnp.exp(log_dt.astype(F32))[:, None]
    lr = a_re.astype(F32)
    li = a_im.astype(F32)
    mag = jnp.exp(lr * dt)
    abar_r = mag * jnp.cos(li * dt)
    abar_i = mag * jnp.sin(li * dt)
    den = lr * lr + li * li
    nr = abar_r - 1.0
    ni = abar_i
    coef_r = (nr * lr + ni * li) / den
    coef_i = (ni * lr - nr * li) / den
    br_ = b_re.astype(F32)
    bi_ = b_im.astype(F32)
    bbar_r = coef_r[..., None] * br_ - coef_i[..., None] * bi_
    bbar_i = coef_r[..., None] * bi_ + coef_i[..., None] * br_
    eye = jnp.eye(G // 2, dtype=F32)

    def in_mat(bb):
        bb = bb.reshape(2, G // 2, P, SSM_GROUP)
        return jnp.einsum('xgph,gk->xghkp', bb, eye).reshape(2, (G // 2) * SSM_GROUP, (G // 2) * P)

    def out_mat(cc):
        cc = cc.reshape(2, G // 2, SSM_GROUP, P)
        return jnp.einsum('xghp,gk->xgpkh', cc, eye).reshape(2, (G // 2) * P, (G // 2) * SSM_GROUP)

    bmat = jnp.concatenate([in_mat(bbar_r), in_mat(bbar_i)], axis=-1).astype(BF16)
    cmat = jnp.concatenate([out_mat(c_re.astype(F32)), -out_mat(c_im.astype(F32))], axis=1).astype(BF16)
    return bmat, abar_r.reshape(1, G * P), abar_i.reshape(1, G * P), cmat


def _pad_lanes(t):
    n, w = t.shape
    wp = ((w + V7X_LANES - 1) // V7X_LANES) * V7X_LANES
    return jnp.pad(t, ((0, 0), (0, wp - w))).reshape(n, 1, wp)


def kernel(x, norm1_g, w_in, b_gate, ssm_a_re, ssm_a_im, ssm_log_dt, ssm_b_re, ssm_b_im, ssm_c_re, ssm_c_im, ssm_d, ssm_w_glu, ssm_b_glu, q_norm_g, k_norm_g, lambda_q1, lambda_k1, lambda_q2, lambda_k2, subln_g, w_proj_ssm, w_proj_att, w_out, norm2_g, w_router, b_router, w_up, b_up, w_down, b_down):
    B, L, D = x.shape
    assert B == V7X_SUBLANES and D == 1024 and L % TL_PROJ == 0 and w_in.shape[0] == 1
    n_tok = B * L
    n_t = n_tok // TL_PROJ
    assert n_t <= V7X_LANES
    l = 0

    scale = ATT_SUB_DIM ** -0.5
    qg = (jnp.tile(q_norm_g[l].astype(F32), 2 * ATT_HEADS) * (scale * math.log2(math.e))).reshape(1, 512)
    kg = jnp.tile(k_norm_g[l].astype(F32), 2 * ATT_HEADS).reshape(1, 512)
    blk = jnp.arange(256) // ATT_SUB_DIM
    ones_bd = (blk[:, None] == blk[None, :]).astype(BF16)
    u_tm, q, k, vt, gates = _inproj(x, norm1_g[l].reshape(1, D), w_in[l].astype(BF16),
                                    b_gate[l].reshape(1, 2 * D), qg, kg, ones_bd)

    bmat, ar, ai, cmat = _s5_matrices(ssm_a_re[l], ssm_a_im[l], ssm_log_dt[l], ssm_b_re[l], ssm_b_im[l],
                                      ssm_c_re[l], ssm_c_im[l])
    y_ssm = _s5(u_tm, bmat, ar, ai, cmat, ssm_d[l].reshape(1, 512).astype(F32),
                ssm_w_glu[l].astype(BF16), ssm_b_glu[l].reshape(1, 512).astype(F32))

    lam4 = jnp.stack([lambda_q1[l], lambda_k1[l], lambda_q2[l], lambda_k2[l]]).astype(F32)
    sg = jnp.tile(subln_g[l].astype(F32), ATT_HEADS).reshape(1, 512)
    y_att = _attn(q, k, vt, lam4, sg)

    tpos = jnp.arange(TL_PROJ)
    sut = (tpos[:, None] < tpos[None, :]).astype(BF16)
    epos = jnp.arange(N_EXPERTS)
    lt = (epos[None, :] < epos[:, None]).astype(BF16)
    h, xs, slot, top_w, oct_tab = _mix(
        x, y_ssm, y_att, gates,
        w_proj_ssm[l].astype(BF16), w_proj_att[l].astype(BF16), w_out[l].astype(BF16),
        norm2_g[l].reshape(1, D), w_router[l].T.astype(BF16), b_router[l].reshape(N_EXPERTS, 1).astype(F32),
        sut, lt)

    n_blocks = (n_tok * TOP_K + n_t * N_EXPERTS * (OCT - 1) + MOE_ROWS - 1) // MOE_ROWS + N_EXPERTS
    meta, src, csrc = _moe_tables(oct_tab, n_t, n_blocks)

    de2 = w_up.shape[-1]
    b_up_p = b_up[l].reshape(N_EXPERTS, de2 // 256, 128, 2).transpose(0, 1, 3, 2).reshape(N_EXPERTS, 1, de2)
    pr = jnp.arange(256)
    col_src = jnp.where(pr < 128, 2 * pr, 2 * (pr - 128) + 1)
    perm = (jnp.arange(256)[:, None] == col_src[None, :]).astype(BF16)
    ys = _experts(meta, _pad_lanes(src), xs,
                  w_up[l], b_up_p, w_down[l], b_down[l].reshape(N_EXPERTS, 1, D), perm)

    out = _combine(_pad_lanes(csrc), h.reshape(n_tok, D), slot.T, top_w.T, ys)
    return out.reshape(B, L, D)
```

```python
import math

import jax
import jax.numpy as jnp
from jax import lax
from jax.experimental import pallas as pl
from jax.experimental.pallas import tpu as pltpu

F32 = jnp.float32
BF16 = jnp.bfloat16
I32 = jnp.int32

RMS_EPS = 1e-6
CHUNK = 64
SSM_GROUP = 16
SSM_STATE = 64
ATT_HEADS = 4
ATT_SUB_DIM = 64
ATT_V_DIM = 128
N_EXPERTS = 32
TOP_K = 4
SWIGLU_ALPHA = 1.702
SWIGLU_LIMIT = 7.0
LAMBDA_INIT = 0.8 - 0.6 * math.exp(-0.3 * 0)

V7X_SUBLANES = 8
V7X_LANES = 128
V7X_MXU_DIM = 256

TL_IN = 1024
TL_PROJ = 512
S5_STEPS = 64
TQ = 256
TK = 256
MOE_ROWS = 1024
MOE_ROW_VARIANTS = (1024, 512, 256)
OCT = V7X_SUBLANES
BLK_OCT = MOE_ROWS // OCT
SLOTS = ((TL_PROJ * TOP_K + N_EXPERTS * (OCT - 1) + V7X_MXU_DIM - 1) // V7X_MXU_DIM) * V7X_MXU_DIM
SL_OCT = SLOTS // OCT
NEG = -1e30
VMEM_LIMIT = 56 * 1024 * 1024


def _dot(a, b):
    return jnp.dot(a, b, preferred_element_type=F32)


def _dot_nt(a, b):
    return lax.dot_general(a, b, (((1,), (1,)), ((), ())), preferred_element_type=F32)


def _params(sem, vmem=VMEM_LIMIT):
    return pltpu.CompilerParams(dimension_semantics=sem, vmem_limit_bytes=vmem)


def _inproj_kernel(x_ref, g1_ref, w_ref, bg_ref, qg_ref, kg_ref, ones_ref,
                   u_ref, q_ref, k_ref, vt_ref, gate_ref):
    x = x_ref[...]
    ms = jnp.mean(x * x, axis=-1, keepdims=True)
    xn = (x * lax.rsqrt(ms + RMS_EPS) * g1_ref[...]).astype(BF16)

    u_ref[...] = _dot(xn, w_ref[:, 0:512])

    def head_norm(z, g):
        sq = (z * z).astype(BF16)
        ss = jnp.concatenate([_dot(sq[:, 0:256], ones_ref[...]),
                              _dot(sq[:, 256:512], ones_ref[...])], axis=1)
        return z * lax.rsqrt(ss * (1.0 / ATT_SUB_DIM) + RMS_EPS) * g

    q_ref[...] = head_norm(_dot(xn, w_ref[:, 512:1024]), qg_ref[...]).astype(BF16)
    k_ref[...] = head_norm(_dot(xn, w_ref[:, 1024:1536]), kg_ref[...]).astype(BF16)
    vt_ref[...] = _dot(xn, w_ref[:, 1536:2048]).T.astype(BF16)
    for c in range(4):
        lo = 2048 + 512 * c
        z = _dot(xn, w_ref[:, lo:lo + 512]) + bg_ref[:, 512 * c:512 * (c + 1)]
        gate_ref[:, 512 * c:512 * (c + 1)] = jax.nn.sigmoid(z).astype(BF16)


def _inproj(x, g1, w_in_bf, b_gate, qg, kg, ones_bd):
    B, L, D = x.shape
    nl = L // TL_IN
    full = lambda shape: pl.BlockSpec(shape, lambda b, l: (0,) * len(shape))
    return pl.pallas_call(
        _inproj_kernel,
        grid=(B, nl),
        in_specs=[
            pl.BlockSpec((None, TL_IN, D), lambda b, l: (b, l, 0)),
            full((1, D)), full(w_in_bf.shape), full((1, 2 * D)),
            full((1, 512)), full((1, 512)), full((256, 256)),
        ],
        out_specs=[
            pl.BlockSpec((None, TL_IN, 512), lambda b, l: (b, l, 0)),
            pl.BlockSpec((None, TL_IN, 512), lambda b, l: (b, l, 0)),
            pl.BlockSpec((None, TL_IN, 512), lambda b, l: (b, l, 0)),
            pl.BlockSpec((None, 512, TL_IN), lambda b, l: (b, 0, l)),
            pl.BlockSpec((None, TL_IN, 2 * D), lambda b, l: (b, l, 0)),
        ],
        out_shape=[
            jax.ShapeDtypeStruct((B, L, 512), F32),
            jax.ShapeDtypeStruct((B, L, 512), BF16),
            jax.ShapeDtypeStruct((B, L, 512), BF16),
            jax.ShapeDtypeStruct((B, 512, L), BF16),
            jax.ShapeDtypeStruct((B, L, 2 * D), BF16),
        ],
        compiler_params=_params(("arbitrary", "arbitrary")),
        name="inproj",
    )(x, g1, w_in_bf, b_gate, qg, kg, ones_bd)


def _s5_kernel(u_hbm, bm_ref, ar_ref, ai_ref, cm_ref, d_ref, wg_ref, bgl_ref,
               y_hbm, ubuf, ybuf, hs_ref, h_ref, usem, ysem):
    i = pl.program_id(0)
    n = pl.num_programs(0)
    slot = lax.rem(i, 2)
    nb = ubuf.shape[2]

    def u_copies(step, s):
        t0 = pl.multiple_of(step * S5_STEPS, S5_STEPS)
        return [pltpu.make_async_copy(u_hbm.at[b, pl.ds(t0, S5_STEPS), :], ubuf.at[s, :, b, :], usem.at[s])
                for b in range(nb)]

    def y_copies(step, s):
        t0 = pl.multiple_of(step * S5_STEPS, S5_STEPS)
        return [pltpu.make_async_copy(ybuf.at[s, :, b, :], y_hbm.at[b, pl.ds(t0, S5_STEPS), :], ysem.at[s])
                for b in range(nb)]

    @pl.when(i == 0)
    def _():
        h_ref[...] = jnp.zeros_like(h_ref)
        for cp in u_copies(0, 0):
            cp.start()

    @pl.when(i + 1 < n)
    def _():
        for cp in u_copies(i + 1, 1 - slot):
            cp.start()

    for cp in u_copies(i, slot):
        cp.wait()
    uf = ubuf[slot].reshape(S5_STEPS * nb, ubuf.shape[3])
    u = uf.astype(BF16)
    for hh in range(2):
        hs_ref[:, 2048 * hh:2048 * (hh + 1)] = _dot(u[:, 256 * hh:256 * (hh + 1)], bm_ref[hh])

    ys = []
    for hh in range(2):
        re0 = 2048 * hh
        im0 = re0 + 1024
        ar = jnp.broadcast_to(ar_ref[:, 1024 * hh:1024 * (hh + 1)], (V7X_SUBLANES, 1024))
        ai = jnp.broadcast_to(ai_ref[:, 1024 * hh:1024 * (hh + 1)], (V7X_SUBLANES, 1024))
        hr = h_ref[:, re0:re0 + 1024]
        hi = h_ref[:, im0:im0 + 1024]
        for t in range(S5_STEPS):
            r0 = t * V7X_SUBLANES
            xr = hs_ref[r0:r0 + V7X_SUBLANES, re0:re0 + 1024]
            xi = hs_ref[r0:r0 + V7X_SUBLANES, im0:im0 + 1024]
            hr, hi = ar * hr - ai * hi + xr, ar * hi + ai * hr + xi
            hs_ref[r0:r0 + V7X_SUBLANES, re0:re0 + 1024] = hr
            hs_ref[r0:r0 + V7X_SUBLANES, im0:im0 + 1024] = hi
        h_ref[:, re0:re0 + 1024] = hr
        h_ref[:, im0:im0 + 1024] = hi
        ys.append(_dot(hs_ref[:, re0:re0 + 2048].astype(BF16), cm_ref[hh]))

    y = jnp.concatenate(ys, axis=1)
    y = y + d_ref[...] * uf
    y = 0.5 * y * (1.0 + jnp.tanh(math.sqrt(2.0 / math.pi) * (y + 0.044715 * (y * y * y))))
    z = _dot(y.astype(BF16), wg_ref[...]) + bgl_ref[...]

    @pl.when(i >= 2)
    def _():
        for cp in y_copies(i - 2, slot):
            cp.wait()

    ybuf[slot] = (y * jax.nn.sigmoid(z)).reshape(S5_STEPS, nb, ybuf.shape[3])
    for cp in y_copies(i, slot):
        cp.start()

    @pl.when(i == n - 1)
    def _():
        for cp in y_copies(i, slot):
            cp.wait()

    @pl.when(jnp.logical_and(i == n - 1, i >= 1))
    def _():
        for cp in y_copies(i - 1, 1 - slot):
            cp.wait()


def _s5(u, bmat, ar, ai, cmat, d_skip, w_glu_bf, b_glu):
    B, L, C = u.shape
    R = S5_STEPS * B
    full = lambda shape: pl.BlockSpec(shape, lambda i: (0,) * len(shape))
    return pl.pallas_call(
        _s5_kernel,
        grid=(L // S5_STEPS,),
        in_specs=[
            pl.BlockSpec(memory_space=pl.ANY),
            full(bmat.shape), full(ar.shape), full(ai.shape), full(cmat.shape),
            full(d_skip.shape), full(w_glu_bf.shape), full(b_glu.shape),
        ],
        out_specs=pl.BlockSpec(memory_space=pl.ANY),
        out_shape=jax.ShapeDtypeStruct((B, L, C), F32),
        scratch_shapes=[pltpu.VMEM((2, S5_STEPS, B, C), F32), pltpu.VMEM((2, S5_STEPS, B, C), F32),
                        pltpu.VMEM((R, 4096), F32), pltpu.VMEM((B, 4096), F32),
                        pltpu.SemaphoreType.DMA((2,)), pltpu.SemaphoreType.DMA((2,))],
        compiler_params=_params(("arbitrary",)),
        name="s5",
    )(u, bmat, ar, ai, cmat, d_skip, w_glu_bf, b_glu)


def _attn_kernel(q_ref, k_ref, vt_ref, lam_ref, sg_ref, o_ref, qm_ref, acc_ref, ml_ref, s_ref):
    i = pl.program_id(1)
    n_chain = 2 * ATT_HEADS
    lamv = lam_ref[...]
    lam = (jnp.exp(jnp.sum(lamv[0:1] * lamv[1:2], axis=-1, keepdims=True))
           - jnp.exp(jnp.sum(lamv[2:3] * lamv[3:4], axis=-1, keepdims=True)) + LAMBDA_INIT)
    lane = lax.broadcasted_iota(I32, (TQ, 128), 1)
    key_chunk = lax.broadcasted_iota(I32, (TK, TQ), 0) // CHUNK
    qry_chunk = lax.broadcasted_iota(I32, (TK, TQ), 1) // CHUNK
    diag_ok = key_chunk <= qry_chunk

    for hd in range(ATT_HEADS):
        qh = q_ref[:, 128 * hd:128 * (hd + 1)]
        zero = jnp.zeros_like(qh)
        qm_ref[2 * hd] = jnp.where(lane < ATT_SUB_DIM, qh, zero)
        qm_ref[2 * hd + 1] = jnp.where(lane >= ATT_SUB_DIM, qh, zero)
    acc_ref[...] = jnp.zeros_like(acc_ref)
    row = lax.broadcasted_iota(I32, (2 * n_chain, TQ), 0)
    ml_ref[...] = jnp.where(row % 2 == 0, NEG, 0.0)

    def scores(j, par):
        off = pl.multiple_of(j * TK, TK)
        for hd in range(ATT_HEADS):
            ks = k_ref[pl.ds(off, TK), 128 * hd:128 * (hd + 1)]
            for s in range(2):
                s_ref[par, 2 * hd + s] = _dot_nt(ks, qm_ref[2 * hd + s])

    def consume(j, par, masked):
        off = pl.multiple_of(j * TK, TK)
        ps, alphas = [], []
        for c in range(n_chain):
            st = s_ref[par, c]
            if masked:
                st = jnp.where(diag_ok, st, NEG)
            m = ml_ref[2 * c:2 * c + 1, :]
            l = ml_ref[2 * c + 1:2 * c + 2, :]
            mn = jnp.maximum(m, jnp.max(st, axis=0, keepdims=True))
            alpha = jnp.exp2(m - mn)
            p = jnp.exp2(st - mn)
            ml_ref[2 * c:2 * c + 1, :] = mn
            ml_ref[2 * c + 1:2 * c + 2, :] = alpha * l + jnp.sum(p, axis=0, keepdims=True)
            ps.append(p.astype(BF16))
            alphas.append(alpha)
        for c in range(n_chain):
            hd = c // 2
            vts = vt_ref[128 * hd:128 * (hd + 1), pl.ds(off, TK)]
            acc_ref[c] = acc_ref[c] * alphas[c] + _dot(vts, ps[c])

    scores(0, 0)

    def body(jj, carry):
        j = 2 * jj
        scores(j + 1, 1)
        consume(j, 0, False)
        scores(j + 2, 0)
        consume(j + 1, 1, False)
        return carry

    lax.fori_loop(0, i // 2, body, 0)

    @pl.when(i % 2 == 1)
    def _():
        scores(i, 1)
        consume(i - 1, 0, False)
        consume(i, 1, True)

    @pl.when(i % 2 == 0)
    def _():
        consume(i, 0, True)

    for hd in range(ATT_HEADS):
        c0 = 128 * hd
        l1 = ml_ref[4 * hd + 1:4 * hd + 2, :]
        l2 = ml_ref[4 * hd + 3:4 * hd + 4, :]
        o = acc_ref[2 * hd] / l1 - lam * (acc_ref[2 * hd + 1] / l2)
        ms = jnp.mean(o * o, axis=0, keepdims=True)
        on = (o * lax.rsqrt(ms + RMS_EPS)).T
        o_ref[:, c0:c0 + 128] = (on * sg_ref[:, c0:c0 + 128] * (1.0 - LAMBDA_INIT)).astype(BF16)


def _attn(q, k, vt, lam4, sg):
    B, L, W = q.shape
    n_chain = 2 * ATT_HEADS
    return pl.pallas_call(
        _attn_kernel,
        grid=(B, L // TQ),
        in_specs=[
            pl.BlockSpec((None, TQ, W), lambda b, i: (b, i, 0)),
            pl.BlockSpec((None, L, W), lambda b, i: (b, 0, 0)),
            pl.BlockSpec((None, W, L), lambda b, i: (b, 0, 0)),
            pl.BlockSpec((4, ATT_SUB_DIM), lambda b, i: (0, 0)),
            pl.BlockSpec((1, W), lambda b, i: (0, 0)),
        ],
        out_specs=pl.BlockSpec((None, TQ, W), lambda b, i: (b, i, 0)),
        out_shape=jax.ShapeDtypeStruct((B, L, W), BF16),
        scratch_shapes=[pltpu.VMEM((n_chain, TQ, 128), BF16),
                        pltpu.VMEM((n_chain, ATT_V_DIM, TQ), F32),
                        pltpu.VMEM((2 * n_chain, TQ), F32),
                        pltpu.VMEM((2, n_chain, TK, TQ), F32)],
        compiler_params=_params(("arbitrary", "arbitrary")),
        name="diff_attn",
    )(q, k, vt, lam4, sg)


def _mix_kernel(x_ref, ys_ref, ya_ref, gate_ref, wps_ref, wpa_ref, wo_ref, g2_ref, wrt_ref, br_ref,
                sut_ref, lt_ref, h_ref, xs_ref, slot_ref, w_ref, oct_ref):
    tile = pl.program_id(0) * pl.num_programs(1) + pl.program_id(1)

    D = x_ref.shape[-1]
    tl = x_ref.shape[0]
    gs = gate_ref[:, 0:D].astype(F32)
    ga = gate_ref[:, D:2 * D].astype(F32)
    mixed = gs * _dot(ys_ref[...].astype(BF16), wps_ref[...]) + ga * _dot(ya_ref[...], wpa_ref[...])
    h = x_ref[...] + _dot(mixed.astype(BF16), wo_ref[...])
    h_ref[...] = h
    ms = jnp.mean(h * h, axis=-1, keepdims=True)
    hb = (h * lax.rsqrt(ms + RMS_EPS) * g2_ref[...]).astype(BF16)

    lg = _dot_nt(wrt_ref[...], hb) + br_ref[...]
    eio = lax.broadcasted_iota(I32, (N_EXPERTS, tl), 0)
    vals, sels = [], []
    for k in range(TOP_K):
        m = jnp.max(lg, axis=0, keepdims=True)
        idx = jnp.min(jnp.where(lg == m, eio, N_EXPERTS), axis=0, keepdims=True)
        sel = eio == idx
        vals.append(m)
        sels.append(sel)
        lg = jnp.where(sel, -jnp.inf, lg)
    ex = [jnp.exp(v - vals[0]) for v in vals]
    den = ex[0] + ex[1] + ex[2] + ex[3]
    for k in range(TOP_K):
        w_ref[k:k + 1, :] = ex[k] / den

    mtot = jnp.zeros((N_EXPERTS, tl), F32)
    for k in range(TOP_K):
        mtot = mtot + jnp.where(sels[k], 1.0, 0.0)
    pre = _dot(mtot.astype(BF16), sut_ref[...])
    cnt = jnp.sum(mtot, axis=1, keepdims=True)
    seg_oct = jnp.floor((cnt + (OCT - 1.0)) * (1.0 / OCT))
    seg_b = jnp.broadcast_to(seg_oct, (N_EXPERTS, V7X_LANES))
    start_oct = _dot(lt_ref[...], seg_b.astype(BF16))
    base = pre + start_oct[:, 0:1] * float(OCT)
    slots = []
    for k in range(TOP_K):
        s_k = jnp.sum(jnp.where(sels[k], base, 0.0), axis=0, keepdims=True).astype(I32)
        slot_ref[k:k + 1, :] = s_k
        slots.append(s_k)

    for c in range(SLOTS // V7X_MXU_DIM):
        sio = lax.broadcasted_iota(I32, (V7X_MXU_DIM, tl), 0) + V7X_MXU_DIM * c
        p = jnp.where(sio == slots[0], 1.0, 0.0)
        for k in range(1, TOP_K):
            p = jnp.where(sio == slots[k], 1.0, p)
        xs_ref[V7X_MXU_DIM * c:V7X_MXU_DIM * (c + 1), :] = _dot(p.astype(BF16), hb)

    lane = lax.broadcasted_iota(I32, (N_EXPERTS, V7X_LANES), 1)

    @pl.when(tile == 0)
    def _():
        oct_ref[...] = jnp.zeros_like(oct_ref)

    oct_ref[...] = jnp.where(lane == tile, seg_b, oct_ref[...])


def _mix(x, ys_tm, ya, gates, wps, wpa, wo, g2, wrt, br, sut, lt):
    B, L, D = x.shape
    nl = L // TL_PROJ
    n_tok = B * L
    full = lambda shape: pl.BlockSpec(shape, lambda b, l: (0,) * len(shape))
    tok_spec = pl.BlockSpec((TOP_K, TL_PROJ), lambda b, l: (0, b * nl + l))
    return pl.pallas_call(
        _mix_kernel,
        grid=(B, nl),
        in_specs=[
            pl.BlockSpec((None, TL_PROJ, D), lambda b, l: (b, l, 0)),
            pl.BlockSpec((None, TL_PROJ, 512), lambda b, l: (b, l, 0)),
            pl.BlockSpec((None, TL_PROJ, 512), lambda b, l: (b, l, 0)),
            pl.BlockSpec((None, TL_PROJ, 2 * D), lambda b, l: (b, l, 0)),
            full(wps.shape), full(wpa.shape), full(wo.shape), full((1, D)),
            full(wrt.shape), full(br.shape), full(sut.shape), full(lt.shape),
        ],
        out_specs=[
            pl.BlockSpec((None, TL_PROJ, D), lambda b, l: (b, l, 0)),
            pl.BlockSpec((SLOTS, D), lambda b, l: (b * nl + l, 0)),
            tok_spec, tok_spec,
            full((N_EXPERTS, V7X_LANES)),
        ],
        out_shape=[
            jax.ShapeDtypeStruct((B, L, D), F32),
            jax.ShapeDtypeStruct((B * nl * SLOTS, D), F32),
            jax.ShapeDtypeStruct((TOP_K, n_tok), I32),
            jax.ShapeDtypeStruct((TOP_K, n_tok), F32),
            jax.ShapeDtypeStruct((N_EXPERTS, V7X_LANES), F32),
        ],
        compiler_params=_params(("arbitrary", "arbitrary")),
        name="mix_router",
    )(x, ys_tm, ya, gates, wps, wpa, wo, g2, wrt, br, sut, lt)


def _moe_tables(oct_tab, n_t, n_blocks):
    E = N_EXPERTS
    O = oct_tab[:, :n_t].astype(I32)
    lstart = jnp.cumsum(O, axis=0) - O
    ecum = jnp.cumsum(O, axis=1) - O
    tot = jnp.sum(O, axis=1)
    nb = (tot + BLK_OCT - 1) // BLK_OCT
    bend = jnp.cumsum(nb)
    bstart = bend - nb
    n_valid = bend[-1]
    bi = jnp.arange(n_blocks, dtype=I32)
    er = jnp.arange(E, dtype=I32)
    tr = jnp.arange(n_t, dtype=I32)
    be = jnp.minimum(jnp.sum((bi[:, None] >= bend[None, :]).astype(I32), axis=1), E - 1)
    last_e = jnp.sum(jnp.where(bi == jnp.maximum(n_valid - 1, 0), be, 0))
    be = jnp.where(bi < n_valid, be, last_e)
    oh_e = be[:, None] == er[None, :]
    pick = lambda tab: jnp.sum(jnp.where(oh_e[:, :, None], tab[None, :, :], 0), axis=1)
    ecum_i, o_i, lst_i = pick(ecum), pick(O), pick(lstart)
    tot_i = jnp.sum(jnp.where(oh_e, tot[None, :], 0), axis=1)
    bst_i = jnp.sum(jnp.where(oh_e, bstart[None, :], 0), axis=1)
    g = (bi - bst_i)[:, None] * BLK_OCT + jnp.arange(BLK_OCT, dtype=I32)[None, :]
    tau = jnp.minimum(jnp.sum((g[:, :, None] >= (ecum_i + o_i)[:, None, :]).astype(I32), axis=2), n_t - 1)
    off = tr[None, :] * SL_OCT + lst_i - ecum_i
    src = jnp.sum(jnp.where(tau[:, :, None] == tr[None, None, :], off[:, None, :], 0), axis=2) + g
    src = jnp.where((g < tot_i[:, None]) & (bi[:, None] < n_valid), src, 0)
    s = jnp.arange(SL_OCT, dtype=I32)
    lend_t = (lstart + O).T
    e_s = jnp.minimum(jnp.sum((s[None, :, None] >= lend_t[:, None, :]).astype(I32), axis=2), E - 1)
    offc = (bstart[:, None] * BLK_OCT + ecum - lstart).T
    csrc = jnp.sum(jnp.where(e_s[:, :, None] == er[None, None, :], offc[:, None, :], 0), axis=2) + s[None, :]
    csrc = jnp.where(s[None, :] < jnp.sum(O, axis=0)[:, None], csrc, 0)
    prev_e = jnp.concatenate([jnp.full((1,), -1, I32), be[:-1]])
    first = ((be != prev_e) & (bi < n_valid)).astype(I32)
    par = (jnp.cumsum(first) - 1) % 2
    later = (er[None, :] > er[:, None]) & (nb[None, :] > 0)
    nxt_of_e = jnp.min(jnp.where(later, er[None, :], E), axis=1)
    nxt_of_e = jnp.where(nxt_of_e < E, nxt_of_e, -1)
    nxt = jnp.sum(jnp.where(oh_e, nxt_of_e[None, :], 0), axis=1)
    rows_i = jnp.clip(tot_i * OCT - (bi - bst_i) * MOE_ROWS, 0, MOE_ROWS)
    var = jnp.zeros_like(bi)
    for vi, m in enumerate(MOE_ROW_VARIANTS):
        var = jnp.where(rows_i <= m, vi, var)
    meta = (be, first, par.astype(I32), nxt.astype(I32), var.astype(I32), n_valid.astype(I32).reshape(1))
    return meta, src, csrc


def _gather_octets(src_ref, n_oct, src_hbm, dst, sem, first=0):
    for q in range(first, first + n_oct):
        row = pl.multiple_of(src_ref[0, q] * OCT, OCT)
        pltpu.make_async_copy(src_hbm.at[pl.ds(row, OCT), :], dst.at[pl.ds(OCT * q, OCT), :], sem).start()


def _expert_kernel(be_ref, first_ref, par_ref, nxt_ref, var_ref, nv_ref, src_cur_ref, src_nxt_ref, xs_hbm,
                   wup_hbm, bup_ref, wdn_hbm, bdn_ref, perm_ref, y_ref,
                   xbuf, wup_st, wdn_st, wup_bf, wdn_bf, act_ref, sem, wsem):
    i = pl.program_id(0)
    nv = nv_ref[0]
    slot = lax.rem(i, 2)

    def weight_copies(e, s):
        return (pltpu.make_async_copy(wup_hbm.at[e], wup_st.at[s], wsem.at[0, s]),
                pltpu.make_async_copy(wdn_hbm.at[e], wdn_st.at[s], wsem.at[1, s]))

    def gather_rows(src_ref, variant, s):
        for vi, m in enumerate(MOE_ROW_VARIANTS):
            @pl.when(variant == vi)
            def _():
                _gather_octets(src_ref, m // OCT, xs_hbm, xbuf.at[s], sem.at[s])

    @pl.when(i == 0)
    def _():
        gather_rows(src_cur_ref, var_ref[0], 0)
        for cp in weight_copies(be_ref[0], par_ref[0]):
            cp.start()

    @pl.when(i + 1 < nv)
    def _():
        gather_rows(src_nxt_ref, var_ref[jnp.minimum(i + 1, pl.num_programs(0) - 1)], 1 - slot)

    def block(m):
        pltpu.make_async_copy(xs_hbm.at[pl.ds(0, m), :], xbuf.at[slot, pl.ds(0, m), :], sem.at[slot]).wait()
        x = xbuf[slot, 0:m, :].astype(BF16)
        for cb in range(8):
            hp = _dot(x, wup_bf[:, 256 * cb:256 * (cb + 1)]) + bup_ref[:, 256 * cb:256 * (cb + 1)]
            glu = jnp.minimum(hp[:, 0:128], SWIGLU_LIMIT)
            lin = jnp.clip(hp[:, 128:256], -SWIGLU_LIMIT, SWIGLU_LIMIT)
            act = glu * jax.nn.sigmoid(SWIGLU_ALPHA * glu) * (lin + 1.0)
            act_ref[0:m, 128 * cb:128 * (cb + 1)] = act.astype(BF16)
        y_ref[0:m, :] = _dot(act_ref[0:m, :], wdn_bf[...]) + bdn_ref[...]
        if m < MOE_ROWS:
            y_ref[m:MOE_ROWS, :] = jnp.zeros((MOE_ROWS - m, y_ref.shape[1]), F32)

    @pl.when(i < nv)
    def _():
        @pl.when(first_ref[i] == 1)
        def _():
            s = par_ref[i]
            for cp in weight_copies(be_ref[i], s):
                cp.wait()

            @pl.when(nxt_ref[i] >= 0)
            def _():
                for cp in weight_copies(nxt_ref[i], 1 - s):
                    cp.start()

            for cb in range(8):
                blk = wup_st[s, :, 256 * cb:256 * (cb + 1)].astype(BF16)
                wup_bf[:, 256 * cb:256 * (cb + 1)] = _dot(blk, perm_ref[...]).astype(BF16)
            wdn_bf[...] = wdn_st[s].astype(BF16)

        for vi, m in enumerate(MOE_ROW_VARIANTS):
            @pl.when(var_ref[i] == vi)
            def _():
                block(m)

    @pl.when(i >= nv)
    def _():
        y_ref[...] = jnp.zeros_like(y_ref)


def _experts(meta, src3, xs, w_up, b_up_p, w_down, b_down, perm):
    n_blocks = src3.shape[0]
    DE2 = w_up.shape[2]
    D = w_up.shape[1]
    tab = src3.shape[2]
    be_map = lambda i, be, *_: (be[i], 0, 0)
    grid_spec = pltpu.PrefetchScalarGridSpec(
        num_scalar_prefetch=len(meta),
        grid=(n_blocks,),
        in_specs=[
            pl.BlockSpec((None, 1, tab), lambda i, *_: (i, 0, 0), memory_space=pltpu.SMEM),
            pl.BlockSpec((None, 1, tab), lambda i, *_: (jnp.minimum(i + 1, n_blocks - 1), 0, 0),
                         memory_space=pltpu.SMEM),
            pl.BlockSpec(memory_space=pl.ANY),
            pl.BlockSpec(memory_space=pl.ANY),
            pl.BlockSpec((None, 1, DE2), be_map),
            pl.BlockSpec(memory_space=pl.ANY),
            pl.BlockSpec((None, 1, D), be_map),
            pl.BlockSpec((256, 256), lambda i, *_: (0, 0)),
        ],
        out_specs=pl.BlockSpec((MOE_ROWS, D), lambda i, *_: (i, 0)),
        scratch_shapes=[
            pltpu.VMEM((2, MOE_ROWS, D), F32),
            pltpu.VMEM((2, D, DE2), F32),
            pltpu.VMEM((2, DE2 // 2, D), F32),
            pltpu.VMEM((D, DE2), BF16),
            pltpu.VMEM((DE2 // 2, D), BF16),
            pltpu.VMEM((MOE_ROWS, DE2 // 2), BF16),
            pltpu.SemaphoreType.DMA((2,)),
            pltpu.SemaphoreType.DMA((2, 2)),
        ],
    )
    return pl.pallas_call(
        _expert_kernel,
        grid_spec=grid_spec,
        out_shape=jax.ShapeDtypeStruct((n_blocks * MOE_ROWS, D), F32),
        compiler_params=_params(("arbitrary",)),
        name="experts",
    )(*meta, src3, src3, xs, w_up, b_up_p, w_down, b_down, perm)


def _combine_kernel(src_cur_ref, src_nxt_ref, h_ref, slot_ref, w_ref, ys_hbm, o_ref,
                    ybuf, sl_b, w_b, sem):
    i = pl.program_id(0)
    n = pl.num_programs(0)
    slot = lax.rem(i, 2)

    @pl.when(i == 0)
    def _():
        _gather_octets(src_cur_ref, SL_OCT, ys_hbm, ybuf.at[0], sem.at[0])

    @pl.when(i + 1 < n)
    def _():
        _gather_octets(src_nxt_ref, SL_OCT, ys_hbm, ybuf.at[1 - slot], sem.at[1 - slot])

    tl = h_ref.shape[0]
    for k in range(TOP_K):
        sl_b[k] = jnp.broadcast_to(slot_ref[:, k:k + 1], (tl, V7X_LANES))
        w_b[k] = jnp.broadcast_to(w_ref[:, k:k + 1], (tl, V7X_LANES))
    lane = lax.broadcasted_iota(I32, (tl, V7X_LANES), 1)
    pltpu.make_async_copy(ys_hbm.at[pl.ds(0, SLOTS), :], ybuf.at[slot], sem.at[slot]).wait()
    acc = h_ref[...]
    for c in range(SLOTS // V7X_MXU_DIM):
        halves = []
        for hc in range(V7X_MXU_DIM // V7X_LANES):
            sio = lane + (V7X_MXU_DIM * c + V7X_LANES * hc)
            wm = jnp.where(sio == sl_b[0], w_b[0], 0.0)
            for k in range(1, TOP_K):
                wm = jnp.where(sio == sl_b[k], w_b[k], wm)
            halves.append(wm.astype(BF16))
        yb = ybuf[slot, V7X_MXU_DIM * c:V7X_MXU_DIM * (c + 1), :].astype(BF16)
        acc = acc + _dot(jnp.concatenate(halves, axis=1), yb)
    o_ref[...] = acc


def _combine(csrc3, h2, slot_tok, w_tok, ys):
    n_tok, D = h2.shape
    n_steps = n_tok // TL_PROJ
    tab = csrc3.shape[2]
    return pl.pallas_call(
        _combine_kernel,
        grid=(n_steps,),
        in_specs=[
            pl.BlockSpec((None, 1, tab), lambda i: (i, 0, 0), memory_space=pltpu.SMEM),
            pl.BlockSpec((None, 1, tab), lambda i: (jnp.minimum(i + 1, n_steps - 1), 0, 0),
                         memory_space=pltpu.SMEM),
            pl.BlockSpec((TL_PROJ, D), lambda i: (i, 0)),
            pl.BlockSpec((TL_PROJ, TOP_K), lambda i: (i, 0)),
            pl.BlockSpec((TL_PROJ, TOP_K), lambda i: (i, 0)),
            pl.BlockSpec(memory_space=pl.ANY),
        ],
        out_specs=pl.BlockSpec((TL_PROJ, D), lambda i: (i, 0)),
        out_shape=jax.ShapeDtypeStruct((n_tok, D), F32),
        scratch_shapes=[pltpu.VMEM((2, SLOTS, D), F32),
                        pltpu.VMEM((TOP_K, TL_PROJ, V7X_LANES), I32),
                        pltpu.VMEM((TOP_K, TL_PROJ, V7X_LANES), F32),
                        pltpu.SemaphoreType.DMA((2,))],
        compiler_params=_params(("arbitrary",)),
        name="combine",
    )(csrc3, csrc3, h2, slot_tok, w_tok, ys)


def _s5_matrices(a_re, a_im, log_dt, b_re, b_im, c_re, c_im):
    G, P = a_re.shape
    dt = jnp.exp(log_dt.astype(F32))[:, None]
    lr = a_re.astype(F32)
    li = a_im.astype(F32)
    mag = jnp.exp(lr * dt)
    abar_r = mag * jnp.cos(li * dt)
    abar_i = mag * jnp.sin(li * dt)
    den = lr * lr + li * li
    nr = abar_r - 1.0
    ni = abar_i
    coef_r = (nr * lr + ni * li) / den
    coef_i = (ni * lr - nr * li) / den
    br_ = b_re.astype(F32)
    bi_ = b_im.astype(F32)
    bbar_r = coef_r[..., None] * br_ - coef_i[..., None] * bi_
    bbar_i = coef_r[..., None] * bi_ + coef_i[..., None] * br_
    eye = jnp.eye(G // 2, dtype=F32)

    def in_mat(bb):
        bb = bb.reshape(2, G // 2, P, SSM_GROUP)
        return jnp.einsum('xgph,gk->xghkp', bb, eye).reshape(2, (G // 2) * SSM_GROUP, (G // 2) * P)

    def out_mat(cc):
        cc = cc.reshape(2, G // 2, SSM_GROUP, P)
        return jnp.einsum('xghp,gk->xgpkh', cc, eye).reshape(2, (G // 2) * P, (G // 2) * SSM_GROUP)

    bmat = jnp.concatenate([in_mat(bbar_r), in_mat(bbar_i)], axis=-1).astype(BF16)
    cmat = jnp.concatenate([out_mat(c_re.astype(F32)), -out_mat(c_im.astype(F32))], axis=1).astype(BF16)
    return bmat, abar_r.reshape(1, G * P), abar_i.reshape(1, G * P), cmat


def _pad_lanes(t):
    n, w = t.shape
    wp = ((w + V7X_LANES - 1) // V7X_LANES) * V7X_LANES
    return jnp.pad(t, ((0, 0), (0, wp - w))).reshape(n, 1, wp)


def kernel(x, norm1_g, w_in, b_gate, ssm_a_re, ssm_a_im, ssm_log_dt, ssm_b_re, ssm_b_im, ssm_c_re, ssm_c_im, ssm_d, ssm_w_glu, ssm_b_glu, q_norm_g, k_norm_g, lambda_q1, lambda_k1, lambda_q2, lambda_k2, subln_g, w_proj_ssm, w_proj_att, w_out, norm2_g, w_router, b_router, w_up, b_up, w_down, b_down):
    B, L, D = x.shape
    assert B == V7X_SUBLANES and D == 1024 and L % TL_PROJ == 0 and w_in.shape[0] == 1
    n_tok = B * L
    n_t = n_tok // TL_PROJ
    assert n_t <= V7X_LANES
    l = 0

    scale = ATT_SUB_DIM ** -0.5
    qg = (jnp.tile(q_norm_g[l].astype(F32), 2 * ATT_HEADS) * (scale * math.log2(math.e))).reshape(1, 512)
    kg = jnp.tile(k_norm_g[l].astype(F32), 2 * ATT_HEADS).reshape(1, 512)
    blk = jnp.arange(256) // ATT_SUB_DIM
    ones_bd = (blk[:, None] == blk[None, :]).astype(BF16)
    u_tm, q, k, vt, gates = _inproj(x, norm1_g[l].reshape(1, D), w_in[l].astype(BF16),
                                    b_gate[l].reshape(1, 2 * D), qg, kg, ones_bd)

    bmat, ar, ai, cmat = _s5_matrices(ssm_a_re[l], ssm_a_im[l], ssm_log_dt[l], ssm_b_re[l], ssm_b_im[l],
                                      ssm_c_re[l], ssm_c_im[l])
    y_ssm = _s5(u_tm, bmat, ar, ai, cmat, ssm_d[l].reshape(1, 512).astype(F32),
                ssm_w_glu[l].astype(BF16), ssm_b_glu[l].reshape(1, 512).astype(F32))

    lam4 = jnp.stack([lambda_q1[l], lambda_k1[l], lambda_q2[l], lambda_k2[l]]).astype(F32)
    sg = jnp.tile(subln_g[l].astype(F32), ATT_HEADS).reshape(1, 512)
    y_att = _attn(q, k, vt, lam4, sg)

    tpos = jnp.arange(TL_PROJ)
    sut = (tpos[:, None] < tpos[None, :]).astype(BF16)
    epos = jnp.arange(N_EXPERTS)
    lt = (epos[None, :] < epos[:, None]).astype(BF16)
    h, xs, slot, top_w, oct_tab = _mix(
        x, y_ssm, y_att, gates,
        w_proj_ssm[l].astype(BF16), w_proj_att[l].astype(BF16), w_out[l].astype(BF16),
        norm2_g[l].reshape(1, D), w_router[l].T.astype(BF16), b_router[l].reshape(N_EXPERTS, 1).astype(F32),
        sut, lt)

    n_blocks = (n_tok * TOP_K + n_t * N_EXPERTS * (OCT - 1) + MOE_ROWS - 1) // MOE_ROWS + N_EXPERTS
    meta, src, csrc = _moe_tables(oct_tab, n_t, n_blocks)

    de2 = w_up.shape[-1]
    b_up_p = b_up[l].reshape(N_EXPERTS, de2 // 256, 128, 2).transpose(0, 1, 3, 2).reshape(N_EXPERTS, 1, de2)
    pr = jnp.arange(256)
    col_src = jnp.where(pr < 128, 2 * pr, 2 * (pr - 128) + 1)
    perm = (jnp.arange(256)[:, None] == col_src[None, :]).astype(BF16)
    ys = _experts(meta, _pad_lanes(src), xs,
                  w_up[l], b_up_p, w_down[l], b_down[l].reshape(N_EXPERTS, 1, D), perm)

    out = _combine(_pad_lanes(csrc), h.reshape(n_tok, D), slot.T, top_w.T, ys)
    return out.reshape(B, L, D)
```

```python
import math

import jax
import jax.numpy as jnp
from jax import lax
from jax.experimental import pallas as pl
from jax.experimental.pallas import tpu as pltpu

F32 = jnp.float32
BF16 = jnp.bfloat16
I32 = jnp.int32

RMS_EPS = 1e-6
CHUNK = 64
SSM_GROUP = 16
SSM_STATE = 64
ATT_HEADS = 4
ATT_SUB_DIM = 64
ATT_V_DIM = 128
N_EXPERTS = 32
TOP_K = 4
SWIGLU_ALPHA = 1.702
SWIGLU_LIMIT = 7.0
LAMBDA_INIT = 0.8 - 0.6 * math.exp(-0.3 * 0)

V7X_SUBLANES = 8
V7X_LANES = 128
V7X_MXU_DIM = 256

TL_IN = 1024
TL_PROJ = 512
S5_STEPS = 64
TQ = 256
TK = 256
MOE_ROWS = 1024
MOE_ROW_VARIANTS = (1024, 512, 256)
OCT = V7X_SUBLANES
BLK_OCT = MOE_ROWS // OCT
SLOTS = ((TL_PROJ * TOP_K + N_EXPERTS * (OCT - 1) + V7X_MXU_DIM - 1) // V7X_MXU_DIM) * V7X_MXU_DIM
SL_OCT = SLOTS // OCT
NEG = -1e30
VMEM_LIMIT = 56 * 1024 * 1024


def _dot(a, b):
    return jnp.dot(a, b, preferred_element_type=F32)


def _dot_nt(a, b):
    return lax.dot_general(a, b, (((1,), (1,)), ((), ())), preferred_element_type=F32)


def _params(sem, vmem=VMEM_LIMIT):
    return pltpu.CompilerParams(dimension_semantics=sem, vmem_limit_bytes=vmem)


def _inproj_kernel(x_ref, g1_ref, wf_ref, bg_ref, qg_ref, kg_ref, ones_ref,
                   u_ref, q_ref, k_ref, vt_ref, gate_ref, w_ref):
    @pl.when(jnp.logical_and(pl.program_id(0) == 0, pl.program_id(1) == 0))
    def _():
        for c in range(wf_ref.shape[1] // 512):
            w_ref[:, 512 * c:512 * (c + 1)] = wf_ref[:, 512 * c:512 * (c + 1)].astype(BF16)

    x = x_ref[...]
    ms = jnp.mean(x * x, axis=-1, keepdims=True)
    xn = (x * lax.rsqrt(ms + RMS_EPS) * g1_ref[...]).astype(BF16)

    u_ref[...] = _dot(xn, w_ref[:, 0:512])

    def head_norm(z, g):
        sq = (z * z).astype(BF16)
        ss = jnp.concatenate([_dot(sq[:, 0:256], ones_ref[...]),
                              _dot(sq[:, 256:512], ones_ref[...])], axis=1)
        return z * lax.rsqrt(ss * (1.0 / ATT_SUB_DIM) + RMS_EPS) * g

    q_ref[...] = head_norm(_dot(xn, w_ref[:, 512:1024]), qg_ref[...]).astype(BF16)
    k_ref[...] = head_norm(_dot(xn, w_ref[:, 1024:1536]), kg_ref[...]).astype(BF16)
    vt_ref[...] = _dot(xn, w_ref[:, 1536:2048]).T.astype(BF16)
    for c in range(4):
        lo = 2048 + 512 * c
        z = _dot(xn, w_ref[:, lo:lo + 512]) + bg_ref[:, 512 * c:512 * (c + 1)]
        gate_ref[:, 512 * c:512 * (c + 1)] = jax.nn.sigmoid(z).astype(BF16)


def _inproj(x, g1, w_in, b_gate, qg, kg, ones_bd):
    B, L, D = x.shape
    nl = L // TL_IN
    full = lambda shape: pl.BlockSpec(shape, lambda b, l: (0,) * len(shape))
    return pl.pallas_call(
        _inproj_kernel,
        grid=(B, nl),
        in_specs=[
            pl.BlockSpec((None, TL_IN, D), lambda b, l: (b, l, 0)),
            full((1, D)),
            pl.BlockSpec(w_in.shape, lambda b, l: (0, 0), pipeline_mode=pl.Buffered(1)),
            full((1, 2 * D)),
            full((1, 512)), full((1, 512)), full((256, 256)),
        ],
        out_specs=[
            pl.BlockSpec((None, TL_IN, 512), lambda b, l: (b, l, 0)),
            pl.BlockSpec((None, TL_IN, 512), lambda b, l: (b, l, 0)),
            pl.BlockSpec((None, TL_IN, 512), lambda b, l: (b, l, 0)),
            pl.BlockSpec((None, 512, TL_IN), lambda b, l: (b, 0, l)),
            pl.BlockSpec((None, TL_IN, 2 * D), lambda b, l: (b, l, 0)),
        ],
        out_shape=[
            jax.ShapeDtypeStruct((B, L, 512), F32),
            jax.ShapeDtypeStruct((B, L, 512), BF16),
            jax.ShapeDtypeStruct((B, L, 512), BF16),
            jax.ShapeDtypeStruct((B, 512, L), BF16),
            jax.ShapeDtypeStruct((B, L, 2 * D), BF16),
        ],
        scratch_shapes=[pltpu.VMEM(w_in.shape, BF16)],
        compiler_params=_params(("arbitrary", "arbitrary")),
        name="inproj",
    )(x, g1, w_in, b_gate, qg, kg, ones_bd)


def _s5_kernel(u_hbm, bm_ref, ar_ref, ai_ref, cm_ref, d_ref, wg_ref, bgl_ref,
               y_hbm, ubuf, ybuf, hs_ref, h_ref, usem, ysem):
    i = pl.program_id(0)
    n = pl.num_programs(0)
    slot = lax.rem(i, 2)
    nb = ubuf.shape[2]

    def u_copies(step, s):
        t0 = pl.multiple_of(step * S5_STEPS, S5_STEPS)
        return [pltpu.make_async_copy(u_hbm.at[b, pl.ds(t0, S5_STEPS), :], ubuf.at[s, :, b, :], usem.at[s])
                for b in range(nb)]

    def y_copies(step, s):
        t0 = pl.multiple_of(step * S5_STEPS, S5_STEPS)
        return [pltpu.make_async_copy(ybuf.at[s, :, b, :], y_hbm.at[b, pl.ds(t0, S5_STEPS), :], ysem.at[s])
                for b in range(nb)]

    @pl.when(i == 0)
    def _():
        h_ref[...] = jnp.zeros_like(h_ref)
        for cp in u_copies(0, 0):
            cp.start()

    @pl.when(i + 1 < n)
    def _():
        for cp in u_copies(i + 1, 1 - slot):
            cp.start()

    for cp in u_copies(i, slot):
        cp.wait()
    uf = ubuf[slot].reshape(S5_STEPS * nb, ubuf.shape[3])
    u = uf.astype(BF16)
    for hh in range(2):
        hs_ref[:, 2048 * hh:2048 * (hh + 1)] = _dot(u[:, 256 * hh:256 * (hh + 1)], bm_ref[hh])

    ys = []
    for hh in range(2):
        re0 = 2048 * hh
        im0 = re0 + 1024
        ar = jnp.broadcast_to(ar_ref[:, 1024 * hh:1024 * (hh + 1)], (V7X_SUBLANES, 1024))
        ai = jnp.broadcast_to(ai_ref[:, 1024 * hh:1024 * (hh + 1)], (V7X_SUBLANES, 1024))
        hr = h_ref[:, re0:re0 + 1024]
        hi = h_ref[:, im0:im0 + 1024]
        for t in range(S5_STEPS):
            r0 = t * V7X_SUBLANES
            xr = hs_ref[r0:r0 + V7X_SUBLANES, re0:re0 + 1024]
            xi = hs_ref[r0:r0 + V7X_SUBLANES, im0:im0 + 1024]
            hr, hi = ar * hr - ai * hi + xr, ar * hi + ai * hr + xi
            hs_ref[r0:r0 + V7X_SUBLANES, re0:re0 + 1024] = hr
            hs_ref[r0:r0 + V7X_SUBLANES, im0:im0 + 1024] = hi
        h_ref[:, re0:re0 + 1024] = hr
        h_ref[:, im0:im0 + 1024] = hi
        ys.append(_dot(hs_ref[:, re0:re0 + 2048].astype(BF16), cm_ref[hh]))

    y = jnp.concatenate(ys, axis=1)
    y = y + d_ref[...] * uf
    y = 0.5 * y * (1.0 + jnp.tanh(math.sqrt(2.0 / math.pi) * (y + 0.044715 * (y * y * y))))
    z = _dot(y.astype(BF16), wg_ref[...]) + bgl_ref[...]

    @pl.when(i >= 2)
    def _():
        for cp in y_copies(i - 2, slot):
            cp.wait()

    ybuf[slot] = (y * jax.nn.sigmoid(z)).reshape(S5_STEPS, nb, ybuf.shape[3])
    for cp in y_copies(i, slot):
        cp.start()

    @pl.when(i == n - 1)
    def _():
        for cp in y_copies(i, slot):
            cp.wait()

    @pl.when(jnp.logical_and(i == n - 1, i >= 1))
    def _():
        for cp in y_copies(i - 1, 1 - slot):
            cp.wait()


def _s5(u, bmat, ar, ai, cmat, d_skip, w_glu_bf, b_glu):
    B, L, C = u.shape
    R = S5_STEPS * B
    full = lambda shape: pl.BlockSpec(shape, lambda i: (0,) * len(shape))
    return pl.pallas_call(
        _s5_kernel,
        grid=(L // S5_STEPS,),
        in_specs=[
            pl.BlockSpec(memory_space=pl.ANY),
            full(bmat.shape), full(ar.shape), full(ai.shape), full(cmat.shape),
            full(d_skip.shape), full(w_glu_bf.shape), full(b_glu.shape),
        ],
        out_specs=pl.BlockSpec(memory_space=pl.ANY),
        out_shape=jax.ShapeDtypeStruct((B, L, C), F32),
        scratch_shapes=[pltpu.VMEM((2, S5_STEPS, B, C), F32), pltpu.VMEM((2, S5_STEPS, B, C), F32),
                        pltpu.VMEM((R, 4096), F32), pltpu.VMEM((B, 4096), F32),
                        pltpu.SemaphoreType.DMA((2,)), pltpu.SemaphoreType.DMA((2,))],
        compiler_params=_params(("arbitrary",)),
        name="s5",
    )(u, bmat, ar, ai, cmat, d_skip, w_glu_bf, b_glu)


def _attn_kernel(q_ref, k_ref, vt_ref, lam_ref, sg_ref, o_ref, qm_ref, acc_ref, ml_ref, s_ref):
    i = pl.program_id(1)
    n_chain = 2 * ATT_HEADS
    lamv = lam_ref[...]
    lam = (jnp.exp(jnp.sum(lamv[0:1] * lamv[1:2], axis=-1, keepdims=True))
           - jnp.exp(jnp.sum(lamv[2:3] * lamv[3:4], axis=-1, keepdims=True)) + LAMBDA_INIT)
    lane = lax.broadcasted_iota(I32, (TQ, 128), 1)
    key_chunk = lax.broadcasted_iota(I32, (TK, TQ), 0) // CHUNK
    qry_chunk = lax.broadcasted_iota(I32, (TK, TQ), 1) // CHUNK
    diag_ok = key_chunk <= qry_chunk

    for hd in range(ATT_HEADS):
        qh = q_ref[:, 128 * hd:128 * (hd + 1)]
        zero = jnp.zeros_like(qh)
        qm_ref[2 * hd] = jnp.where(lane < ATT_SUB_DIM, qh, zero)
        qm_ref[2 * hd + 1] = jnp.where(lane >= ATT_SUB_DIM, qh, zero)
    acc_ref[...] = jnp.zeros_like(acc_ref)
    row = lax.broadcasted_iota(I32, (2 * n_chain, TQ), 0)
    ml_ref[...] = jnp.where(row % 2 == 0, NEG, 0.0)

    def scores(j, par):
        off = pl.multiple_of(j * TK, TK)
        for hd in range(ATT_HEADS):
            ks = k_ref[pl.ds(off, TK), 128 * hd:128 * (hd + 1)]
            for s in range(2):
                s_ref[par, 2 * hd + s] = _dot_nt(ks, qm_ref[2 * hd + s])

    def consume(j, par, masked):
        off = pl.multiple_of(j * TK, TK)
        ps, alphas = [], []
        for c in range(n_chain):
            st = s_ref[par, c]
            if masked:
                st = jnp.where(diag_ok, st, NEG)
            m = ml_ref[2 * c:2 * c + 1, :]
            l = ml_ref[2 * c + 1:2 * c + 2, :]
            mn = jnp.maximum(m, jnp.max(st, axis=0, keepdims=True))
            alpha = jnp.exp2(m - mn)
            p = jnp.exp2(st - mn)
            ml_ref[2 * c:2 * c + 1, :] = mn
            ml_ref[2 * c + 1:2 * c + 2, :] = alpha * l + jnp.sum(p, axis=0, keepdims=True)
            ps.append(p.astype(BF16))
            alphas.append(alpha)
        for c in range(n_chain):
            hd = c // 2
            vts = vt_ref[128 * hd:128 * (hd + 1), pl.ds(off, TK)]
            acc_ref[c] = acc_ref[c] * alphas[c] + _dot(vts, ps[c])

    scores(0, 0)

    def body(jj, carry):
        j = 2 * jj
        scores(j + 1, 1)
        consume(j, 0, False)
        scores(j + 2, 0)
        consume(j + 1, 1, False)
        return carry

    lax.fori_loop(0, i // 2, body, 0)

    @pl.when(i % 2 == 1)
    def _():
        scores(i, 1)
        consume(i - 1, 0, False)
        consume(i, 1, True)

    @pl.when(i % 2 == 0)
    def _():
        consume(i, 0, True)

    for hd in range(ATT_HEADS):
        c0 = 128 * hd
        l1 = ml_ref[4 * hd + 1:4 * hd + 2, :]
        l2 = ml_ref[4 * hd + 3:4 * hd + 4, :]
        o = acc_ref[2 * hd] / l1 - lam * (acc_ref[2 * hd + 1] / l2)
        ms = jnp.mean(o * o, axis=0, keepdims=True)
        on = (o * lax.rsqrt(ms + RMS_EPS)).T
        o_ref[:, c0:c0 + 128] = (on * sg_ref[:, c0:c0 + 128] * (1.0 - LAMBDA_INIT)).astype(BF16)


def _attn(q, k, vt, lam4, sg):
    B, L, W = q.shape
    n_chain = 2 * ATT_HEADS
    return pl.pallas_call(
        _attn_kernel,
        grid=(B, L // TQ),
        in_specs=[
            pl.BlockSpec((None, TQ, W), lambda b, i: (b, i, 0)),
            pl.BlockSpec((None, L, W), lambda b, i: (b, 0, 0)),
            pl.BlockSpec((None, W, L), lambda b, i: (b, 0, 0)),
            pl.BlockSpec((4, ATT_SUB_DIM), lambda b, i: (0, 0)),
            pl.BlockSpec((1, W), lambda b, i: (0, 0)),
        ],
        out_specs=pl.BlockSpec((None, TQ, W), lambda b, i: (b, i, 0)),
        out_shape=jax.ShapeDtypeStruct((B, L, W), BF16),
        scratch_shapes=[pltpu.VMEM((n_chain, TQ, 128), BF16),
                        pltpu.VMEM((n_chain, ATT_V_DIM, TQ), F32),
                        pltpu.VMEM((2 * n_chain, TQ), F32),
                        pltpu.VMEM((2, n_chain, TK, TQ), F32)],
        compiler_params=_params(("arbitrary", "arbitrary")),
        name="diff_attn",
    )(q, k, vt, lam4, sg)


def _mix_kernel(x_ref, ys_ref, ya_ref, gate_ref, wps_ref, wpa_ref, wo_ref, g2_ref, wrt_ref, br_ref,
                sut_ref, lt_ref, h_ref, xs_ref, tok_ref, oct_ref):
    tile = pl.program_id(0) * pl.num_programs(1) + pl.program_id(1)

    D = x_ref.shape[-1]
    tl = x_ref.shape[0]
    gs = gate_ref[:, 0:D].astype(F32)
    ga = gate_ref[:, D:2 * D].astype(F32)
    mixed = gs * _dot(ys_ref[...].astype(BF16), wps_ref[...]) + ga * _dot(ya_ref[...], wpa_ref[...])
    h = x_ref[...] + _dot(mixed.astype(BF16), wo_ref[...])
    h_ref[...] = h
    ms = jnp.mean(h * h, axis=-1, keepdims=True)
    hb = (h * lax.rsqrt(ms + RMS_EPS) * g2_ref[...]).astype(BF16)

    lg = _dot_nt(wrt_ref[...], hb) + br_ref[...]
    eio = lax.broadcasted_iota(I32, (N_EXPERTS, tl), 0)
    vals, sels = [], []
    for k in range(TOP_K):
        m = jnp.max(lg, axis=0, keepdims=True)
        idx = jnp.min(jnp.where(lg == m, eio, N_EXPERTS), axis=0, keepdims=True)
        sel = eio == idx
        vals.append(m)
        sels.append(sel)
        lg = jnp.where(sel, -jnp.inf, lg)
    ex = [jnp.exp(v - vals[0]) for v in vals]
    den = ex[0] + ex[1] + ex[2] + ex[3]
    wts = [e / den for e in ex]

    mtot = jnp.zeros((N_EXPERTS, tl), F32)
    for k in range(TOP_K):
        mtot = mtot + jnp.where(sels[k], 1.0, 0.0)
    pre = _dot(mtot.astype(BF16), sut_ref[...])
    cnt = jnp.sum(mtot, axis=1, keepdims=True)
    seg_oct = jnp.floor((cnt + (OCT - 1.0)) * (1.0 / OCT))
    seg_b = jnp.broadcast_to(seg_oct, (N_EXPERTS, V7X_LANES))
    start_oct = _dot(lt_ref[...], seg_b.astype(BF16))
    base = pre + start_oct[:, 0:1] * float(OCT)
    slots_f = [jnp.sum(jnp.where(sels[k], base, 0.0), axis=0, keepdims=True) for k in range(TOP_K)]
    slots = [s.astype(I32) for s in slots_f]
    tok_ref[...] = jnp.concatenate(slots_f + wts, axis=0).T

    for c in range(SLOTS // V7X_MXU_DIM):
        sio = lax.broadcasted_iota(I32, (V7X_MXU_DIM, tl), 0) + V7X_MXU_DIM * c
        p = jnp.where(sio == slots[0], 1.0, 0.0)
        for k in range(1, TOP_K):
            p = jnp.where(sio == slots[k], 1.0, p)
        xs_ref[V7X_MXU_DIM * c:V7X_MXU_DIM * (c + 1), :] = _dot(p.astype(BF16), hb)

    lane = lax.broadcasted_iota(I32, (N_EXPERTS, V7X_LANES), 1)

    @pl.when(tile == 0)
    def _():
        oct_ref[...] = jnp.zeros_like(oct_ref)

    oct_ref[...] = jnp.where(lane == tile, seg_b, oct_ref[...])


def _mix(x, ys_tm, ya, gates, wps, wpa, wo, g2, wrt, br, sut, lt):
    B, L, D = x.shape
    nl = L // TL_PROJ
    n_tok = B * L
    full = lambda shape: pl.BlockSpec(shape, lambda b, l: (0,) * len(shape))
    return pl.pallas_call(
        _mix_kernel,
        grid=(B, nl),
        in_specs=[
            pl.BlockSpec((None, TL_PROJ, D), lambda b, l: (b, l, 0)),
            pl.BlockSpec((None, TL_PROJ, 512), lambda b, l: (b, l, 0)),
            pl.BlockSpec((None, TL_PROJ, 512), lambda b, l: (b, l, 0)),
            pl.BlockSpec((None, TL_PROJ, 2 * D), lambda b, l: (b, l, 0)),
            full(wps.shape), full(wpa.shape), full(wo.shape), full((1, D)),
            full(wrt.shape), full(br.shape), full(sut.shape), full(lt.shape),
        ],
        out_specs=[
            pl.BlockSpec((None, TL_PROJ, D), lambda b, l: (b, l, 0)),
            pl.BlockSpec((SLOTS, D), lambda b, l: (b * nl + l, 0)),
            pl.BlockSpec((TL_PROJ, 2 * TOP_K), lambda b, l: (b * nl + l, 0)),
            full((N_EXPERTS, V7X_LANES)),
        ],
        out_shape=[
            jax.ShapeDtypeStruct((B, L, D), F32),
            jax.ShapeDtypeStruct((B * nl * SLOTS, D), F32),
            jax.ShapeDtypeStruct((n_tok, 2 * TOP_K), F32),
            jax.ShapeDtypeStruct((N_EXPERTS, V7X_LANES), F32),
        ],
        compiler_params=_params(("arbitrary", "arbitrary")),
        name="mix_router",
    )(x, ys_tm, ya, gates, wps, wpa, wo, g2, wrt, br, sut, lt)


def _moe_tables(oct_tab, n_t, n_blocks):
    E = N_EXPERTS
    O = oct_tab[:, :n_t].astype(I32)
    lstart = jnp.cumsum(O, axis=0) - O
    ecum = jnp.cumsum(O, axis=1) - O
    tot = jnp.sum(O, axis=1)
    nb = (tot + BLK_OCT - 1) // BLK_OCT
    bend = jnp.cumsum(nb)
    bstart = bend - nb
    n_valid = bend[-1]
    bi = jnp.arange(n_blocks, dtype=I32)
    er = jnp.arange(E, dtype=I32)
    tr = jnp.arange(n_t, dtype=I32)
    be = jnp.minimum(jnp.sum((bi[:, None] >= bend[None, :]).astype(I32), axis=1), E - 1)
    last_e = jnp.sum(jnp.where(bi == jnp.maximum(n_valid - 1, 0), be, 0))
    be = jnp.where(bi < n_valid, be, last_e)
    oh_e = be[:, None] == er[None, :]
    pick = lambda tab: jnp.sum(jnp.where(oh_e[:, :, None], tab[None, :, :], 0), axis=1)
    ecum_i, o_i, lst_i = pick(ecum), pick(O), pick(lstart)
    tot_i = jnp.sum(jnp.where(oh_e, tot[None, :], 0), axis=1)
    bst_i = jnp.sum(jnp.where(oh_e, bstart[None, :], 0), axis=1)
    g = (bi - bst_i)[:, None] * BLK_OCT + jnp.arange(BLK_OCT, dtype=I32)[None, :]
    tau = jnp.minimum(jnp.sum((g[:, :, None] >= (ecum_i + o_i)[:, None, :]).astype(I32), axis=2), n_t - 1)
    off = tr[None, :] * SL_OCT + lst_i - ecum_i
    src = jnp.sum(jnp.where(tau[:, :, None] == tr[None, None, :], off[:, None, :], 0), axis=2) + g
    src = jnp.where((g < tot_i[:, None]) & (bi[:, None] < n_valid), src, 0)
    s = jnp.arange(SL_OCT, dtype=I32)
    lend_t = (lstart + O).T
    e_s = jnp.minimum(jnp.sum((s[None, :, None] >= lend_t[:, None, :]).astype(I32), axis=2), E - 1)
    offc = (bstart[:, None] * BLK_OCT + ecum - lstart).T
    csrc = jnp.sum(jnp.where(e_s[:, :, None] == er[None, None, :], offc[:, None, :], 0), axis=2) + s[None, :]
    csrc = jnp.where(s[None, :] < jnp.sum(O, axis=0)[:, None], csrc, 0)
    prev_e = jnp.concatenate([jnp.full((1,), -1, I32), be[:-1]])
    first = ((be != prev_e) & (bi < n_valid)).astype(I32)
    par = (jnp.cumsum(first) - 1) % 2
    later = (er[None, :] > er[:, None]) & (nb[None, :] > 0)
    nxt_of_e = jnp.min(jnp.where(later, er[None, :], E), axis=1)
    nxt_of_e = jnp.where(nxt_of_e < E, nxt_of_e, -1)
    nxt = jnp.sum(jnp.where(oh_e, nxt_of_e[None, :], 0), axis=1)
    rows_i = jnp.clip(tot_i * OCT - (bi - bst_i) * MOE_ROWS, 0, MOE_ROWS)
    var = jnp.zeros_like(bi)
    for vi, m in enumerate(MOE_ROW_VARIANTS):
        var = jnp.where(rows_i <= m, vi, var)
    meta = (be, first, par.astype(I32), nxt.astype(I32), var.astype(I32), n_valid.astype(I32).reshape(1))
    return meta, src, csrc


def _gather_octets(src_ref, n_oct, src_hbm, dst, sem, first=0):
    for q in range(first, first + n_oct):
        row = pl.multiple_of(src_ref[0, q] * OCT, OCT)
        pltpu.make_async_copy(src_hbm.at[pl.ds(row, OCT), :], dst.at[pl.ds(OCT * q, OCT), :], sem).start()


def _expert_kernel(be_ref, first_ref, par_ref, nxt_ref, var_ref, nv_ref, src_cur_ref, src_nxt_ref, xs_hbm,
                   wup_hbm, bup_ref, wdn_hbm, bdn_ref, perm_ref, y_ref,
                   xbuf, wup_st, wdn_st, wup_bf, wdn_bf, act_ref, sem, wsem):
    i = pl.program_id(0)
    nv = nv_ref[0]
    slot = lax.rem(i, 2)

    def weight_copies(e, s):
        return (pltpu.make_async_copy(wup_hbm.at[e], wup_st.at[s], wsem.at[0, s]),
                pltpu.make_async_copy(wdn_hbm.at[e], wdn_st.at[s], wsem.at[1, s]))

    def gather_rows(src_ref, variant, s):
        for vi, m in enumerate(MOE_ROW_VARIANTS):
            @pl.when(variant == vi)
            def _():
                _gather_octets(src_ref, m // OCT, xs_hbm, xbuf.at[s], sem.at[s])

    @pl.when(i == 0)
    def _():
        gather_rows(src_cur_ref, var_ref[0], 0)
        for cp in weight_copies(be_ref[0], par_ref[0]):
            cp.start()

    @pl.when(i + 1 < nv)
    def _():
        gather_rows(src_nxt_ref, var_ref[jnp.minimum(i + 1, pl.num_programs(0) - 1)], 1 - slot)

    def block(m):
        pltpu.make_async_copy(xs_hbm.at[pl.ds(0, m), :], xbuf.at[slot, pl.ds(0, m), :], sem.at[slot]).wait()
        x = xbuf[slot, 0:m, :].astype(BF16)
        for cb in range(8):
            hp = _dot(x, wup_bf[:, 256 * cb:256 * (cb + 1)]) + bup_ref[:, 256 * cb:256 * (cb + 1)]
            glu = jnp.minimum(hp[:, 0:128], SWIGLU_LIMIT)
            lin = jnp.clip(hp[:, 128:256], -SWIGLU_LIMIT, SWIGLU_LIMIT)
            act = glu * jax.nn.sigmoid(SWIGLU_ALPHA * glu) * (lin + 1.0)
            act_ref[0:m, 128 * cb:128 * (cb + 1)] = act.astype(BF16)
        y_ref[0:m, :] = _dot(act_ref[0:m, :], wdn_bf[...]) + bdn_ref[...]
        if m < MOE_ROWS:
            y_ref[m:MOE_ROWS, :] = jnp.zeros((MOE_ROWS - m, y_ref.shape[1]), F32)

    @pl.when(i < nv)
    def _():
        @pl.when(first_ref[i] == 1)
        def _():
            s = par_ref[i]
            for cp in weight_copies(be_ref[i], s):
                cp.wait()

            @pl.when(nxt_ref[i] >= 0)
            def _():
                for cp in weight_copies(nxt_ref[i], 1 - s):
                    cp.start()

            for cb in range(8):
                blk = wup_st[s, :, 256 * cb:256 * (cb + 1)].astype(BF16)
                wup_bf[:, 256 * cb:256 * (cb + 1)] = _dot(blk, perm_ref[...]).astype(BF16)
            wdn_bf[...] = wdn_st[s].astype(BF16)

        for vi, m in enumerate(MOE_ROW_VARIANTS):
            @pl.when(var_ref[i] == vi)
            def _():
                block(m)

    @pl.when(i >= nv)
    def _():
        y_ref[...] = jnp.zeros_like(y_ref)


def _experts(meta, src3, xs, w_up, b_up_p, w_down, b_down, perm):
    n_blocks = src3.shape[0]
    DE2 = w_up.shape[2]
    D = w_up.shape[1]
    tab = src3.shape[2]
    be_map = lambda i, be, *_: (be[i], 0, 0)
    grid_spec = pltpu.PrefetchScalarGridSpec(
        num_scalar_prefetch=len(meta),
        grid=(n_blocks,),
        in_specs=[
            pl.BlockSpec((None, 1, tab), lambda i, *_: (i, 0, 0), memory_space=pltpu.SMEM),
            pl.BlockSpec((None, 1, tab), lambda i, *_: (jnp.minimum(i + 1, n_blocks - 1), 0, 0),
                         memory_space=pltpu.SMEM),
            pl.BlockSpec(memory_space=pl.ANY),
            pl.BlockSpec(memory_space=pl.ANY),
            pl.BlockSpec((None, 1, DE2), be_map),
            pl.BlockSpec(memory_space=pl.ANY),
            pl.BlockSpec((None, 1, D), be_map),
            pl.BlockSpec((256, 256), lambda i, *_: (0, 0)),
        ],
        out_specs=pl.BlockSpec((MOE_ROWS, D), lambda i, *_: (i, 0)),
        scratch_shapes=[
            pltpu.VMEM((2, MOE_ROWS, D), F32),
            pltpu.VMEM((2, D, DE2), F32),
            pltpu.VMEM((2, DE2 // 2, D), F32),
            pltpu.VMEM((D, DE2), BF16),
            pltpu.VMEM((DE2 // 2, D), BF16),
            pltpu.VMEM((MOE_ROWS, DE2 // 2), BF16),
            pltpu.SemaphoreType.DMA((2,)),
            pltpu.SemaphoreType.DMA((2, 2)),
        ],
    )
    return pl.pallas_call(
        _expert_kernel,
        grid_spec=grid_spec,
        out_shape=jax.ShapeDtypeStruct((n_blocks * MOE_ROWS, D), F32),
        compiler_params=_params(("arbitrary",)),
        name="experts",
    )(*meta, src3, src3, xs, w_up, b_up_p, w_down, b_down, perm)


def _combine_kernel(src_cur_ref, src_nxt_ref, h_ref, tok_ref, ys_hbm, o_ref,
                    ybuf, sl_b, w_b, sem):
    i = pl.program_id(0)
    n = pl.num_programs(0)
    slot = lax.rem(i, 2)

    @pl.when(i == 0)
    def _():
        _gather_octets(src_cur_ref, SL_OCT, ys_hbm, ybuf.at[0], sem.at[0])

    @pl.when(i + 1 < n)
    def _():
        _gather_octets(src_nxt_ref, SL_OCT, ys_hbm, ybuf.at[1 - slot], sem.at[1 - slot])

    tl = h_ref.shape[0]
    for k in range(TOP_K):
        sl_b[k] = jnp.broadcast_to(tok_ref[:, k:k + 1], (tl, V7X_LANES))
        w_b[k] = jnp.broadcast_to(tok_ref[:, TOP_K + k:TOP_K + k + 1], (tl, V7X_LANES))
    lane = lax.broadcasted_iota(I32, (tl, V7X_LANES), 1).astype(F32)
    pltpu.make_async_copy(ys_hbm.at[pl.ds(0, SLOTS), :], ybuf.at[slot], sem.at[slot]).wait()
    acc = h_ref[...]
    for c in range(SLOTS // V7X_MXU_DIM):
        halves = []
        for hc in range(V7X_MXU_DIM // V7X_LANES):
            sio = lane + float(V7X_MXU_DIM * c + V7X_LANES * hc)
            wm = jnp.where(sio == sl_b[0], w_b[0], 0.0)
            for k in range(1, TOP_K):
                wm = jnp.where(sio == sl_b[k], w_b[k], wm)
            halves.append(wm.astype(BF16))
        yb = ybuf[slot, V7X_MXU_DIM * c:V7X_MXU_DIM * (c + 1), :].astype(BF16)
        acc = acc + _dot(jnp.concatenate(halves, axis=1), yb)
    o_ref[...] = acc


def _combine(csrc3, h2, tok_tab, ys):
    n_tok, D = h2.shape
    n_steps = n_tok // TL_PROJ
    tab = csrc3.shape[2]
    return pl.pallas_call(
        _combine_kernel,
        grid=(n_steps,),
        in_specs=[
            pl.BlockSpec((None, 1, tab), lambda i: (i, 0, 0), memory_space=pltpu.SMEM),
            pl.BlockSpec((None, 1, tab), lambda i: (jnp.minimum(i + 1, n_steps - 1), 0, 0),
                         memory_space=pltpu.SMEM),
            pl.BlockSpec((TL_PROJ, D), lambda i: (i, 0)),
            pl.BlockSpec((TL_PROJ, 2 * TOP_K), lambda i: (i, 0)),
            pl.BlockSpec(memory_space=pl.ANY),
        ],
        out_specs=pl.BlockSpec((TL_PROJ, D), lambda i: (i, 0)),
        out_shape=jax.ShapeDtypeStruct((n_tok, D), F32),
        scratch_shapes=[pltpu.VMEM((2, SLOTS, D), F32),
                        pltpu.VMEM((TOP_K, TL_PROJ, V7X_LANES), F32),
                        pltpu.VMEM((TOP_K, TL_PROJ, V7X_LANES), F32),
                        pltpu.SemaphoreType.DMA((2,))],
        compiler_params=_params(("arbitrary",)),
        name="combine",
    )(csrc3, csrc3, h2, tok_tab, ys)


def _s5_matrices(a_re, a_im, log_dt, b_re, b_im, c_re, c_im):
    G, P = a_re.shape
    dt = jnp.exp(log_dt.astype(F32))[:, None]
    lr = a_re.astype(F32)
    li = a_im.astype(F32)
    mag = jnp.exp(lr * dt)
    abar_r = mag * jnp.cos(li * dt)
    abar_i = mag * jnp.sin(li * dt)
    den = lr * lr + li * li
    nr = abar_r - 1.0
    ni = abar_i
    coef_r = (nr * lr + ni * li) / den
    coef_i = (ni * lr - nr * li) / den
    br_ = b_re.astype(F32)
    bi_ = b_im.astype(F32)
    bbar_r = coef_r[..., None] * br_ - coef_i[..., None] * bi_
    bbar_i = coef_r[..., None] * bi_ + coef_i[..., None] * br_
    eye = jnp.eye(G // 2, dtype=F32)

    def block_diag(t):
        _, g, a, b = t.shape
        return (t[:, :, :, None, :] * eye[None, :, None, :, None]).reshape(2, g * a, g * b)

    def in_mat(bb):
        return block_diag(jnp.swapaxes(bb.reshape(2, G // 2, P, SSM_GROUP), 2, 3))

    def out_mat(cc):
        return block_diag(jnp.swapaxes(cc.reshape(2, G // 2, SSM_GROUP, P), 2, 3))

    bmat = jnp.concatenate([in_mat(bbar_r), in_mat(bbar_i)], axis=-1).astype(BF16)
    cmat = jnp.concatenate([out_mat(c_re.astype(F32)), -out_mat(c_im.astype(F32))], axis=1).astype(BF16)
    return bmat, abar_r.reshape(1, G * P), abar_i.reshape(1, G * P), cmat


def _pad_lanes(t):
    n, w = t.shape
    wp = ((w + V7X_LANES - 1) // V7X_LANES) * V7X_LANES
    return jnp.pad(t, ((0, 0), (0, wp - w))).reshape(n, 1, wp)


def kernel(x, norm1_g, w_in, b_gate, ssm_a_re, ssm_a_im, ssm_log_dt, ssm_b_re, ssm_b_im, ssm_c_re, ssm_c_im, ssm_d, ssm_w_glu, ssm_b_glu, q_norm_g, k_norm_g, lambda_q1, lambda_k1, lambda_q2, lambda_k2, subln_g, w_proj_ssm, w_proj_att, w_out, norm2_g, w_router, b_router, w_up, b_up, w_down, b_down):
    B, L, D = x.shape
    assert B == V7X_SUBLANES and D == 1024 and L % TL_PROJ == 0 and w_in.shape[0] == 1
    n_tok = B * L
    n_t = n_tok // TL_PROJ
    assert n_t <= V7X_LANES
    l = 0

    scale = ATT_SUB_DIM ** -0.5
    qg = (jnp.tile(q_norm_g[l].astype(F32), 2 * ATT_HEADS) * (scale * math.log2(math.e))).reshape(1, 512)
    kg = jnp.tile(k_norm_g[l].astype(F32), 2 * ATT_HEADS).reshape(1, 512)
    blk = jnp.arange(256) // ATT_SUB_DIM
    ones_bd = (blk[:, None] == blk[None, :]).astype(BF16)
    u_tm, q, k, vt, gates = _inproj(x, norm1_g[l].reshape(1, D), w_in[l],
                                    b_gate[l].reshape(1, 2 * D), qg, kg, ones_bd)

    bmat, ar, ai, cmat = _s5_matrices(ssm_a_re[l], ssm_a_im[l], ssm_log_dt[l], ssm_b_re[l], ssm_b_im[l],
                                      ssm_c_re[l], ssm_c_im[l])
    y_ssm = _s5(u_tm, bmat, ar, ai, cmat, ssm_d[l].reshape(1, 512).astype(F32),
                ssm_w_glu[l].astype(BF16), ssm_b_glu[l].reshape(1, 512).astype(F32))

    lam4 = jnp.stack([lambda_q1[l], lambda_k1[l], lambda_q2[l], lambda_k2[l]]).astype(F32)
    sg = jnp.tile(subln_g[l].astype(F32), ATT_HEADS).reshape(1, 512)
    y_att = _attn(q, k, vt, lam4, sg)

    tpos = jnp.arange(TL_PROJ)
    sut = (tpos[:, None] < tpos[None, :]).astype(BF16)
    epos = jnp.arange(N_EXPERTS)
    lt = (epos[None, :] < epos[:, None]).astype(BF16)
    h, xs, tok_tab, oct_tab = _mix(
        x, y_ssm, y_att, gates,
        w_proj_ssm[l].astype(BF16), w_proj_att[l].astype(BF16), w_out[l].astype(BF16),
        norm2_g[l].reshape(1, D), w_router[l].T.astype(BF16), b_router[l].reshape(N_EXPERTS, 1).astype(F32),
        sut, lt)

    n_blocks = (n_tok * TOP_K + n_t * N_EXPERTS * (OCT - 1) + MOE_ROWS - 1) // MOE_ROWS + N_EXPERTS
    meta, src, csrc = _moe_tables(oct_tab, n_t, n_blocks)

    de2 = w_up.shape[-1]
    b_up_p = b_up[l].reshape(N_EXPERTS, de2 // 256, 128, 2).transpose(0, 1, 3, 2).reshape(N_EXPERTS, 1, de2)
    pr = jnp.arange(256)
    col_src = jnp.where(pr < 128, 2 * pr, 2 * (pr - 128) + 1)
    perm = (jnp.arange(256)[:, None] == col_src[None, :]).astype(BF16)
    ys = _experts(meta, _pad_lanes(src), xs,
                  w_up[l], b_up_p, w_down[l], b_down[l].reshape(N_EXPERTS, 1, D), perm)

    out = _combine(_pad_lanes(csrc), h.reshape(n_tok, D), tok_tab, ys)
    return out.reshape(B, L, D)
```

```python
import math

import jax
import jax.numpy as jnp
from jax import lax
from jax.experimental import pallas as pl
from jax.experimental.pallas import tpu as pltpu

F32 = jnp.float32
BF16 = jnp.bfloat16
I32 = jnp.int32

RMS_EPS = 1e-6
CHUNK = 64
SSM_GROUP = 16
SSM_STATE = 64
ATT_HEADS = 4
ATT_SUB_DIM = 64
ATT_V_DIM = 128
N_EXPERTS = 32
TOP_K = 4
SWIGLU_ALPHA = 1.702
SWIGLU_LIMIT = 7.0
LAMBDA_INIT = 0.8 - 0.6 * math.exp(-0.3 * 0)

V7X_SUBLANES = 8
V7X_LANES = 128
V7X_MXU_DIM = 256

TL_IN = 1024
TL_PROJ = 512
S5_STEPS = 128
TQ = 256
TK = 256
MOE_ROWS = 1024
MOE_ROW_VARIANTS = (1024, 512, 256)
OCT = V7X_SUBLANES
BLK_OCT = MOE_ROWS // OCT
SLOTS = ((TL_PROJ * TOP_K + N_EXPERTS * (OCT - 1) + V7X_MXU_DIM - 1) // V7X_MXU_DIM) * V7X_MXU_DIM
SL_OCT = SLOTS // OCT
NEG = -1e30
VMEM_LIMIT = 56 * 1024 * 1024


def _dot(a, b):
    return jnp.dot(a, b, preferred_element_type=F32)


def _dot_nt(a, b):
    return lax.dot_general(a, b, (((1,), (1,)), ((), ())), preferred_element_type=F32)


def _params(sem, vmem=VMEM_LIMIT):
    return pltpu.CompilerParams(dimension_semantics=sem, vmem_limit_bytes=vmem)


def _inproj_kernel(x_ref, g1_ref, wf_ref, bg_ref, qg_ref, kg_ref, ones_ref,
                   u_ref, q_ref, k_ref, vt_ref, gate_ref, w_ref):
    @pl.when(jnp.logical_and(pl.program_id(0) == 0, pl.program_id(1) == 0))
    def _():
        for c in range(wf_ref.shape[1] // 512):
            w_ref[:, 512 * c:512 * (c + 1)] = wf_ref[:, 512 * c:512 * (c + 1)].astype(BF16)

    x = x_ref[...]
    ms = jnp.mean(x * x, axis=-1, keepdims=True)
    xn = (x * lax.rsqrt(ms + RMS_EPS) * g1_ref[...]).astype(BF16)

    u_ref[...] = _dot(xn, w_ref[:, 0:512])

    def head_norm(z, g):
        sq = (z * z).astype(BF16)
        ss = jnp.concatenate([_dot(sq[:, 0:256], ones_ref[...]),
                              _dot(sq[:, 256:512], ones_ref[...])], axis=1)
        return z * lax.rsqrt(ss * (1.0 / ATT_SUB_DIM) + RMS_EPS) * g

    q_ref[...] = head_norm(_dot(xn, w_ref[:, 512:1024]), qg_ref[...]).astype(BF16)
    k_ref[...] = head_norm(_dot(xn, w_ref[:, 1024:1536]), kg_ref[...]).astype(BF16)
    vt_ref[...] = _dot(xn, w_ref[:, 1536:2048]).T.astype(BF16)
    for c in range(4):
        lo = 2048 + 512 * c
        z = _dot(xn, w_ref[:, lo:lo + 512]) + bg_ref[:, 512 * c:512 * (c + 1)]
        gate_ref[:, 512 * c:512 * (c + 1)] = jax.nn.sigmoid(z).astype(BF16)


def _inproj(x, g1, w_in, b_gate, qg, kg, ones_bd):
    B, L, D = x.shape
    nl = L // TL_IN
    full = lambda shape: pl.BlockSpec(shape, lambda b, l: (0,) * len(shape))
    return pl.pallas_call(
        _inproj_kernel,
        grid=(B, nl),
        in_specs=[
            pl.BlockSpec((None, TL_IN, D), lambda b, l: (b, l, 0)),
            full((1, D)),
            pl.BlockSpec(w_in.shape, lambda b, l: (0, 0), pipeline_mode=pl.Buffered(1)),
            full((1, 2 * D)),
            full((1, 512)), full((1, 512)), full((256, 256)),
        ],
        out_specs=[
            pl.BlockSpec((None, TL_IN, 512), lambda b, l: (b, l, 0)),
            pl.BlockSpec((None, TL_IN, 512), lambda b, l: (b, l, 0)),
            pl.BlockSpec((None, TL_IN, 512), lambda b, l: (b, l, 0)),
            pl.BlockSpec((None, 512, TL_IN), lambda b, l: (b, 0, l)),
            pl.BlockSpec((None, TL_IN, 2 * D), lambda b, l: (b, l, 0)),
        ],
        out_shape=[
            jax.ShapeDtypeStruct((B, L, 512), F32),
            jax.ShapeDtypeStruct((B, L, 512), BF16),
            jax.ShapeDtypeStruct((B, L, 512), BF16),
            jax.ShapeDtypeStruct((B, 512, L), BF16),
            jax.ShapeDtypeStruct((B, L, 2 * D), BF16),
        ],
        scratch_shapes=[pltpu.VMEM(w_in.shape, BF16)],
        compiler_params=_params(("arbitrary", "arbitrary")),
        name="inproj",
    )(x, g1, w_in, b_gate, qg, kg, ones_bd)


def _s5_kernel(u_hbm, bm_ref, ar_ref, ai_ref, cm_ref, d_ref, wg_ref, bgl_ref,
               y_hbm, ubuf, ybuf, hs_ref, h_ref, usem, ysem):
    i = pl.program_id(0)
    n = pl.num_programs(0)
    slot = lax.rem(i, 2)
    nb = ubuf.shape[2]

    def u_copies(step, s):
        t0 = pl.multiple_of(step * S5_STEPS, S5_STEPS)
        return [pltpu.make_async_copy(u_hbm.at[b, pl.ds(t0, S5_STEPS), :], ubuf.at[s, :, b, :], usem.at[s])
                for b in range(nb)]

    def y_copies(step, s):
        t0 = pl.multiple_of(step * S5_STEPS, S5_STEPS)
        return [pltpu.make_async_copy(ybuf.at[s, :, b, :], y_hbm.at[b, pl.ds(t0, S5_STEPS), :], ysem.at[s])
                for b in range(nb)]

    @pl.when(i == 0)
    def _():
        h_ref[...] = jnp.zeros_like(h_ref)
        for cp in u_copies(0, 0):
            cp.start()

    @pl.when(i + 1 < n)
    def _():
        for cp in u_copies(i + 1, 1 - slot):
            cp.start()

    for cp in u_copies(i, slot):
        cp.wait()
    uf = ubuf[slot].reshape(S5_STEPS * nb, ubuf.shape[3])
    u = uf.astype(BF16)
    for hh in range(2):
        hs_ref[:, 2048 * hh:2048 * (hh + 1)] = _dot(u[:, 256 * hh:256 * (hh + 1)], bm_ref[hh])

    ys = []
    for hh in range(2):
        re0 = 2048 * hh
        im0 = re0 + 1024
        ar = jnp.broadcast_to(ar_ref[:, 1024 * hh:1024 * (hh + 1)], (V7X_SUBLANES, 1024))
        ai = jnp.broadcast_to(ai_ref[:, 1024 * hh:1024 * (hh + 1)], (V7X_SUBLANES, 1024))
        hr = h_ref[:, re0:re0 + 1024]
        hi = h_ref[:, im0:im0 + 1024]
        for t in range(S5_STEPS):
            r0 = t * V7X_SUBLANES
            xr = hs_ref[r0:r0 + V7X_SUBLANES, re0:re0 + 1024]
            xi = hs_ref[r0:r0 + V7X_SUBLANES, im0:im0 + 1024]
            hr, hi = ar * hr - ai * hi + xr, ar * hi + ai * hr + xi
            hs_ref[r0:r0 + V7X_SUBLANES, re0:re0 + 1024] = hr
            hs_ref[r0:r0 + V7X_SUBLANES, im0:im0 + 1024] = hi
        h_ref[:, re0:re0 + 1024] = hr
        h_ref[:, im0:im0 + 1024] = hi
        ys.append(_dot(hs_ref[:, re0:re0 + 2048].astype(BF16), cm_ref[hh]))

    y = jnp.concatenate(ys, axis=1)
    y = y + d_ref[...] * uf
    y = 0.5 * y * (1.0 + jnp.tanh(math.sqrt(2.0 / math.pi) * (y + 0.044715 * (y * y * y))))
    z = _dot(y.astype(BF16), wg_ref[...]) + bgl_ref[...]

    @pl.when(i >= 2)
    def _():
        for cp in y_copies(i - 2, slot):
            cp.wait()

    ybuf[slot] = (y * jax.nn.sigmoid(z)).reshape(S5_STEPS, nb, ybuf.shape[3])
    for cp in y_copies(i, slot):
        cp.start()

    @pl.when(i == n - 1)
    def _():
        for cp in y_copies(i, slot):
            cp.wait()

    @pl.when(jnp.logical_and(i == n - 1, i >= 1))
    def _():
        for cp in y_copies(i - 1, 1 - slot):
            cp.wait()


def _s5(u, bmat, ar, ai, cmat, d_skip, w_glu_bf, b_glu):
    B, L, C = u.shape
    R = S5_STEPS * B
    full = lambda shape: pl.BlockSpec(shape, lambda i: (0,) * len(shape))
    return pl.pallas_call(
        _s5_kernel,
        grid=(L // S5_STEPS,),
        in_specs=[
            pl.BlockSpec(memory_space=pl.ANY),
            full(bmat.shape), full(ar.shape), full(ai.shape), full(cmat.shape),
            full(d_skip.shape), full(w_glu_bf.shape), full(b_glu.shape),
        ],
        out_specs=pl.BlockSpec(memory_space=pl.ANY),
        out_shape=jax.ShapeDtypeStruct((B, L, C), F32),
        scratch_shapes=[pltpu.VMEM((2, S5_STEPS, B, C), F32), pltpu.VMEM((2, S5_STEPS, B, C), F32),
                        pltpu.VMEM((R, 4096), F32), pltpu.VMEM((B, 4096), F32),
                        pltpu.SemaphoreType.DMA((2,)), pltpu.SemaphoreType.DMA((2,))],
        compiler_params=_params(("arbitrary",)),
        name="s5",
    )(u, bmat, ar, ai, cmat, d_skip, w_glu_bf, b_glu)


def _attn_kernel(q_ref, k_ref, vt_ref, lam_ref, sg_ref, o_ref, qm_ref, acc_ref, ml_ref, s_ref):
    i = pl.program_id(1)
    n_chain = 2 * ATT_HEADS
    lamv = lam_ref[...]
    lam = (jnp.exp(jnp.sum(lamv[0:1] * lamv[1:2], axis=-1, keepdims=True))
           - jnp.exp(jnp.sum(lamv[2:3] * lamv[3:4], axis=-1, keepdims=True)) + LAMBDA_INIT)
    lane = lax.broadcasted_iota(I32, (TQ, 128), 1)
    key_chunk = lax.broadcasted_iota(I32, (TK, TQ), 0) // CHUNK
    qry_chunk = lax.broadcasted_iota(I32, (TK, TQ), 1) // CHUNK
    diag_ok = key_chunk <= qry_chunk

    for hd in range(ATT_HEADS):
        qh = q_ref[:, 128 * hd:128 * (hd + 1)]
        zero = jnp.zeros_like(qh)
        qm_ref[2 * hd] = jnp.where(lane < ATT_SUB_DIM, qh, zero)
        qm_ref[2 * hd + 1] = jnp.where(lane >= ATT_SUB_DIM, qh, zero)
    acc_ref[...] = jnp.zeros_like(acc_ref)
    row = lax.broadcasted_iota(I32, (2 * n_chain, TQ), 0)
    ml_ref[...] = jnp.where(row % 2 == 0, NEG, 0.0)

    def scores(j, par):
        off = pl.multiple_of(j * TK, TK)
        for hd in range(ATT_HEADS):
            ks = k_ref[pl.ds(off, TK), 128 * hd:128 * (hd + 1)]
            for s in range(2):
                s_ref[par, 2 * hd + s] = _dot_nt(ks, qm_ref[2 * hd + s])

    def consume(j, par, masked):
        off = pl.multiple_of(j * TK, TK)
        ps, alphas = [], []
        for c in range(n_chain):
            st = s_ref[par, c]
            if masked:
                st = jnp.where(diag_ok, st, NEG)
            m = ml_ref[2 * c:2 * c + 1, :]
            l = ml_ref[2 * c + 1:2 * c + 2, :]
            mn = jnp.maximum(m, jnp.max(st, axis=0, keepdims=True))
            alpha = jnp.exp2(m - mn)
            p = jnp.exp2(st - mn)
            ml_ref[2 * c:2 * c + 1, :] = mn
            ml_ref[2 * c + 1:2 * c + 2, :] = alpha * l + jnp.sum(p, axis=0, keepdims=True)
            ps.append(p.astype(BF16))
            alphas.append(alpha)
        for c in range(n_chain):
            hd = c // 2
            vts = vt_ref[128 * hd:128 * (hd + 1), pl.ds(off, TK)]
            acc_ref[c] = acc_ref[c] * alphas[c] + _dot(vts, ps[c])

    scores(0, 0)

    def body(jj, carry):
        j = 2 * jj
        scores(j + 1, 1)
        consume(j, 0, False)
        scores(j + 2, 0)
        consume(j + 1, 1, False)
        return carry

    lax.fori_loop(0, i // 2, body, 0)

    @pl.when(i % 2 == 1)
    def _():
        scores(i, 1)
        consume(i - 1, 0, False)
        consume(i, 1, True)

    @pl.when(i % 2 == 0)
    def _():
        consume(i, 0, True)

    for hd in range(ATT_HEADS):
        c0 = 128 * hd
        l1 = ml_ref[4 * hd + 1:4 * hd + 2, :]
        l2 = ml_ref[4 * hd + 3:4 * hd + 4, :]
        o = acc_ref[2 * hd] / l1 - lam * (acc_ref[2 * hd + 1] / l2)
        ms = jnp.mean(o * o, axis=0, keepdims=True)
        on = (o * lax.rsqrt(ms + RMS_EPS)).T
        o_ref[:, c0:c0 + 128] = (on * sg_ref[:, c0:c0 + 128] * (1.0 - LAMBDA_INIT)).astype(BF16)


def _attn(q, k, vt, lam4, sg):
    B, L, W = q.shape
    n_chain = 2 * ATT_HEADS
    return pl.pallas_call(
        _attn_kernel,
        grid=(B, L // TQ),
        in_specs=[
            pl.BlockSpec((None, TQ, W), lambda b, i: (b, i, 0)),
            pl.BlockSpec((None, L, W), lambda b, i: (b, 0, 0)),
            pl.BlockSpec((None, W, L), lambda b, i: (b, 0, 0)),
            pl.BlockSpec((4, ATT_SUB_DIM), lambda b, i: (0, 0)),
            pl.BlockSpec((1, W), lambda b, i: (0, 0)),
        ],
        out_specs=pl.BlockSpec((None, TQ, W), lambda b, i: (b, i, 0)),
        out_shape=jax.ShapeDtypeStruct((B, L, W), BF16),
        scratch_shapes=[pltpu.VMEM((n_chain, TQ, 128), BF16),
                        pltpu.VMEM((n_chain, ATT_V_DIM, TQ), F32),
                        pltpu.VMEM((2 * n_chain, TQ), F32),
                        pltpu.VMEM((2, n_chain, TK, TQ), F32)],
        compiler_params=_params(("arbitrary", "arbitrary")),
        name="diff_attn",
    )(q, k, vt, lam4, sg)


def _mix_kernel(x_ref, ys_ref, ya_ref, gate_ref, wps_ref, wpa_ref, wo_ref, g2_ref, wrt_ref, br_ref,
                sut_ref, lt_ref, h_ref, xs_ref, tok_ref, oct_ref):
    tile = pl.program_id(0) * pl.num_programs(1) + pl.program_id(1)

    D = x_ref.shape[-1]
    tl = x_ref.shape[0]
    gs = gate_ref[:, 0:D].astype(F32)
    ga = gate_ref[:, D:2 * D].astype(F32)
    mixed = gs * _dot(ys_ref[...].astype(BF16), wps_ref[...]) + ga * _dot(ya_ref[...], wpa_ref[...])
    h = x_ref[...] + _dot(mixed.astype(BF16), wo_ref[...])
    h_ref[...] = h
    ms = jnp.mean(h * h, axis=-1, keepdims=True)
    hb = (h * lax.rsqrt(ms + RMS_EPS) * g2_ref[...]).astype(BF16)

    lg = _dot_nt(wrt_ref[...], hb) + br_ref[...]
    eio = lax.broadcasted_iota(I32, (N_EXPERTS, tl), 0)
    vals, sels = [], []
    for k in range(TOP_K):
        m = jnp.max(lg, axis=0, keepdims=True)
        idx = jnp.min(jnp.where(lg == m, eio, N_EXPERTS), axis=0, keepdims=True)
        sel = eio == idx
        vals.append(m)
        sels.append(sel)
        lg = jnp.where(sel, -jnp.inf, lg)
    ex = [jnp.exp(v - vals[0]) for v in vals]
    den = ex[0] + ex[1] + ex[2] + ex[3]
    wts = [e / den for e in ex]

    mtot = jnp.zeros((N_EXPERTS, tl), F32)
    for k in range(TOP_K):
        mtot = mtot + jnp.where(sels[k], 1.0, 0.0)
    pre = _dot(mtot.astype(BF16), sut_ref[...])
    cnt = jnp.sum(mtot, axis=1, keepdims=True)
    seg_oct = jnp.floor((cnt + (OCT - 1.0)) * (1.0 / OCT))
    seg_b = jnp.broadcast_to(seg_oct, (N_EXPERTS, V7X_LANES))
    start_oct = _dot(lt_ref[...], seg_b.astype(BF16))
    base = pre + start_oct[:, 0:1] * float(OCT)
    slots_f = [jnp.sum(jnp.where(sels[k], base, 0.0), axis=0, keepdims=True) for k in range(TOP_K)]
    slots = [s.astype(I32) for s in slots_f]
    tok_ref[...] = jnp.concatenate(slots_f + wts, axis=0).T

    for c in range(SLOTS // V7X_MXU_DIM):
        sio = lax.broadcasted_iota(I32, (V7X_MXU_DIM, tl), 0) + V7X_MXU_DIM * c
        p = jnp.where(sio == slots[0], 1.0, 0.0)
        for k in range(1, TOP_K):
            p = jnp.where(sio == slots[k], 1.0, p)
        xs_ref[V7X_MXU_DIM * c:V7X_MXU_DIM * (c + 1), :] = _dot(p.astype(BF16), hb)

    lane = lax.broadcasted_iota(I32, (N_EXPERTS, V7X_LANES), 1)

    @pl.when(tile == 0)
    def _():
        oct_ref[...] = jnp.zeros_like(oct_ref)

    oct_ref[...] = jnp.where(lane == tile, seg_b, oct_ref[...])


def _mix(x, ys_tm, ya, gates, wps, wpa, wo, g2, wrt, br, sut, lt):
    B, L, D = x.shape
    nl = L // TL_PROJ
    n_tok = B * L
    full = lambda shape: pl.BlockSpec(shape, lambda b, l: (0,) * len(shape))
    return pl.pallas_call(
        _mix_kernel,
        grid=(B, nl),
        in_specs=[
            pl.BlockSpec((None, TL_PROJ, D), lambda b, l: (b, l, 0)),
            pl.BlockSpec((None, TL_PROJ, 512), lambda b, l: (b, l, 0)),
            pl.BlockSpec((None, TL_PROJ, 512), lambda b, l: (b, l, 0)),
            pl.BlockSpec((None, TL_PROJ, 2 * D), lambda b, l: (b, l, 0)),
            full(wps.shape), full(wpa.shape), full(wo.shape), full((1, D)),
            full(wrt.shape), full(br.shape), full(sut.shape), full(lt.shape),
        ],
        out_specs=[
            pl.BlockSpec((None, TL_PROJ, D), lambda b, l: (b, l, 0)),
            pl.BlockSpec((SLOTS, D), lambda b, l: (b * nl + l, 0)),
            pl.BlockSpec((TL_PROJ, 2 * TOP_K), lambda b, l: (b * nl + l, 0)),
            full((N_EXPERTS, V7X_LANES)),
        ],
        out_shape=[
            jax.ShapeDtypeStruct((B, L, D), F32),
            jax.ShapeDtypeStruct((B * nl * SLOTS, D), F32),
            jax.ShapeDtypeStruct((n_tok, 2 * TOP_K), F32),
            jax.ShapeDtypeStruct((N_EXPERTS, V7X_LANES), F32),
        ],
        compiler_params=_params(("arbitrary", "arbitrary")),
        name="mix_router",
    )(x, ys_tm, ya, gates, wps, wpa, wo, g2, wrt, br, sut, lt)


def _moe_tables(oct_tab, n_t, n_blocks):
    E = N_EXPERTS
    O = oct_tab[:, :n_t].astype(I32)
    lstart = jnp.cumsum(O, axis=0) - O
    ecum = jnp.cumsum(O, axis=1) - O
    tot = jnp.sum(O, axis=1)
    nb = (tot + BLK_OCT - 1) // BLK_OCT
    bend = jnp.cumsum(nb)
    bstart = bend - nb
    n_valid = bend[-1]
    bi = jnp.arange(n_blocks, dtype=I32)
    er = jnp.arange(E, dtype=I32)
    tr = jnp.arange(n_t, dtype=I32)
    be = jnp.minimum(jnp.sum((bi[:, None] >= bend[None, :]).astype(I32), axis=1), E - 1)
    last_e = jnp.sum(jnp.where(bi == jnp.maximum(n_valid - 1, 0), be, 0))
    be = jnp.where(bi < n_valid, be, last_e)
    oh_e = be[:, None] == er[None, :]
    pick = lambda tab: jnp.sum(jnp.where(oh_e[:, :, None], tab[None, :, :], 0), axis=1)
    ecum_i, o_i, lst_i = pick(ecum), pick(O), pick(lstart)
    tot_i = jnp.sum(jnp.where(oh_e, tot[None, :], 0), axis=1)
    bst_i = jnp.sum(jnp.where(oh_e, bstart[None, :], 0), axis=1)
    g = (bi - bst_i)[:, None] * BLK_OCT + jnp.arange(BLK_OCT, dtype=I32)[None, :]
    tau = jnp.minimum(jnp.sum((g[:, :, None] >= (ecum_i + o_i)[:, None, :]).astype(I32), axis=2), n_t - 1)
    off = tr[None, :] * SL_OCT + lst_i - ecum_i
    src = jnp.sum(jnp.where(tau[:, :, None] == tr[None, None, :], off[:, None, :], 0), axis=2) + g
    src = jnp.where((g < tot_i[:, None]) & (bi[:, None] < n_valid), src, 0)
    s = jnp.arange(SL_OCT, dtype=I32)
    lend_t = (lstart + O).T
    e_s = jnp.minimum(jnp.sum((s[None, :, None] >= lend_t[:, None, :]).astype(I32), axis=2), E - 1)
    offc = (bstart[:, None] * BLK_OCT + ecum - lstart).T
    csrc = jnp.sum(jnp.where(e_s[:, :, None] == er[None, None, :], offc[:, None, :], 0), axis=2) + s[None, :]
    csrc = jnp.where(s[None, :] < jnp.sum(O, axis=0)[:, None], csrc, 0)
    prev_e = jnp.concatenate([jnp.full((1,), -1, I32), be[:-1]])
    first = ((be != prev_e) & (bi < n_valid)).astype(I32)
    par = (jnp.cumsum(first) - 1) % 2
    later = (er[None, :] > er[:, None]) & (nb[None, :] > 0)
    nxt_of_e = jnp.min(jnp.where(later, er[None, :], E), axis=1)
    nxt_of_e = jnp.where(nxt_of_e < E, nxt_of_e, -1)
    nxt = jnp.sum(jnp.where(oh_e, nxt_of_e[None, :], 0), axis=1)
    rows_i = jnp.clip(tot_i * OCT - (bi - bst_i) * MOE_ROWS, 0, MOE_ROWS)
    var = jnp.zeros_like(bi)
    for vi, m in enumerate(MOE_ROW_VARIANTS):
        var = jnp.where(rows_i <= m, vi, var)
    meta = (be, first, par.astype(I32), nxt.astype(I32), var.astype(I32), n_valid.astype(I32).reshape(1))
    return meta, src, csrc


def _gather_octets(src_ref, n_oct, src_hbm, dst, sem, first=0):
    for q in range(first, first + n_oct):
        row = pl.multiple_of(src_ref[0, q] * OCT, OCT)
        pltpu.make_async_copy(src_hbm.at[pl.ds(row, OCT), :], dst.at[pl.ds(OCT * q, OCT), :], sem).start()


def _expert_kernel(be_ref, first_ref, par_ref, nxt_ref, var_ref, nv_ref, src_cur_ref, src_nxt_ref, xs_hbm,
                   wup_hbm, bup_ref, wdn_hbm, bdn_ref, perm_ref, y_ref,
                   xbuf, wup_st, wdn_st, wup_bf, wdn_bf, act_ref, sem, wsem):
    i = pl.program_id(0)
    nv = nv_ref[0]
    slot = lax.rem(i, 2)

    def weight_copies(e, s):
        return (pltpu.make_async_copy(wup_hbm.at[e], wup_st.at[s], wsem.at[0, s]),
                pltpu.make_async_copy(wdn_hbm.at[e], wdn_st.at[s], wsem.at[1, s]))

    def gather_rows(src_ref, variant, s):
        for vi, m in enumerate(MOE_ROW_VARIANTS):
            @pl.when(variant == vi)
            def _():
                _gather_octets(src_ref, m // OCT, xs_hbm, xbuf.at[s], sem.at[s])

    @pl.when(i == 0)
    def _():
        gather_rows(src_cur_ref, var_ref[0], 0)
        for cp in weight_copies(be_ref[0], par_ref[0]):
            cp.start()

    @pl.when(i + 1 < nv)
    def _():
        gather_rows(src_nxt_ref, var_ref[jnp.minimum(i + 1, pl.num_programs(0) - 1)], 1 - slot)

    def block(m):
        pltpu.make_async_copy(xs_hbm.at[pl.ds(0, m), :], xbuf.at[slot, pl.ds(0, m), :], sem.at[slot]).wait()
        x = xbuf[slot, 0:m, :].astype(BF16)
        for cb in range(8):
            hp = _dot(x, wup_bf[:, 256 * cb:256 * (cb + 1)]) + bup_ref[:, 256 * cb:256 * (cb + 1)]
            glu = jnp.minimum(hp[:, 0:128], SWIGLU_LIMIT)
            lin = jnp.clip(hp[:, 128:256], -SWIGLU_LIMIT, SWIGLU_LIMIT)
            act = glu * jax.nn.sigmoid(SWIGLU_ALPHA * glu) * (lin + 1.0)
            act_ref[0:m, 128 * cb:128 * (cb + 1)] = act.astype(BF16)
        y_ref[0:m, :] = _dot(act_ref[0:m, :], wdn_bf[...]) + bdn_ref[...]
        if m < MOE_ROWS:
            y_ref[m:MOE_ROWS, :] = jnp.zeros((MOE_ROWS - m, y_ref.shape[1]), F32)

    @pl.when(i < nv)
    def _():
        @pl.when(first_ref[i] == 1)
        def _():
            s = par_ref[i]
            for cp in weight_copies(be_ref[i], s):
                cp.wait()

            @pl.when(nxt_ref[i] >= 0)
            def _():
                for cp in weight_copies(nxt_ref[i], 1 - s):
                    cp.start()

            for cb in range(8):
                blk = wup_st[s, :, 256 * cb:256 * (cb + 1)].astype(BF16)
                wup_bf[:, 256 * cb:256 * (cb + 1)] = _dot(blk, perm_ref[...]).astype(BF16)
            wdn_bf[...] = wdn_st[s].astype(BF16)

        for vi, m in enumerate(MOE_ROW_VARIANTS):
            @pl.when(var_ref[i] == vi)
            def _():
                block(m)

    @pl.when(i >= nv)
    def _():
        y_ref[...] = jnp.zeros_like(y_ref)


def _experts(meta, src3, xs, w_up, b_up_p, w_down, b_down, perm):
    n_blocks = src3.shape[0]
    DE2 = w_up.shape[2]
    D = w_up.shape[1]
    tab = src3.shape[2]
    be_map = lambda i, be, *_: (be[i], 0, 0)
    grid_spec = pltpu.PrefetchScalarGridSpec(
        num_scalar_prefetch=len(meta),
        grid=(n_blocks,),
        in_specs=[
            pl.BlockSpec((None, 1, tab), lambda i, *_: (i, 0, 0), memory_space=pltpu.SMEM),
            pl.BlockSpec((None, 1, tab), lambda i, *_: (jnp.minimum(i + 1, n_blocks - 1), 0, 0),
                         memory_space=pltpu.SMEM),
            pl.BlockSpec(memory_space=pl.ANY),
            pl.BlockSpec(memory_space=pl.ANY),
            pl.BlockSpec((None, 1, DE2), be_map),
            pl.BlockSpec(memory_space=pl.ANY),
            pl.BlockSpec((None, 1, D), be_map),
            pl.BlockSpec((256, 256), lambda i, *_: (0, 0)),
        ],
        out_specs=pl.BlockSpec((MOE_ROWS, D), lambda i, *_: (i, 0)),
        scratch_shapes=[
            pltpu.VMEM((2, MOE_ROWS, D), F32),
            pltpu.VMEM((2, D, DE2), F32),
            pltpu.VMEM((2, DE2 // 2, D), F32),
            pltpu.VMEM((D, DE2), BF16),
            pltpu.VMEM((DE2 // 2, D), BF16),
            pltpu.VMEM((MOE_ROWS, DE2 // 2), BF16),
            pltpu.SemaphoreType.DMA((2,)),
            pltpu.SemaphoreType.DMA((2, 2)),
        ],
    )
    return pl.pallas_call(
        _expert_kernel,
        grid_spec=grid_spec,
        out_shape=jax.ShapeDtypeStruct((n_blocks * MOE_ROWS, D), F32),
        compiler_params=_params(("arbitrary",)),
        name="experts",
    )(*meta, src3, src3, xs, w_up, b_up_p, w_down, b_down, perm)


def _combine_kernel(src_cur_ref, src_nxt_ref, h_ref, tok_ref, ys_hbm, o_ref,
                    ybuf, sl_b, w_b, sem):
    i = pl.program_id(0)
    n = pl.num_programs(0)
    slot = lax.rem(i, 2)

    @pl.when(i == 0)
    def _():
        _gather_octets(src_cur_ref, SL_OCT, ys_hbm, ybuf.at[0], sem.at[0])

    @pl.when(i + 1 < n)
    def _():
        _gather_octets(src_nxt_ref, SL_OCT, ys_hbm, ybuf.at[1 - slot], sem.at[1 - slot])

    tl = h_ref.shape[0]
    for k in range(TOP_K):
        sl_b[k] = jnp.broadcast_to(tok_ref[:, k:k + 1], (tl, V7X_LANES))
        w_b[k] = jnp.broadcast_to(tok_ref[:, TOP_K + k:TOP_K + k + 1], (tl, V7X_LANES))
    lane = lax.broadcasted_iota(I32, (tl, V7X_LANES), 1).astype(F32)
    pltpu.make_async_copy(ys_hbm.at[pl.ds(0, SLOTS), :], ybuf.at[slot], sem.at[slot]).wait()
    acc = h_ref[...]
    for c in range(SLOTS // V7X_MXU_DIM):
        halves = []
        for hc in range(V7X_MXU_DIM // V7X_LANES):
            sio = lane + float(V7X_MXU_DIM * c + V7X_LANES * hc)
            wm = jnp.where(sio == sl_b[0], w_b[0], 0.0)
            for k in range(1, TOP_K):
                wm = jnp.where(sio == sl_b[k], w_b[k], wm)
            halves.append(wm.astype(BF16))
        yb = ybuf[slot, V7X_MXU_DIM * c:V7X_MXU_DIM * (c + 1), :].astype(BF16)
        acc = acc + _dot(jnp.concatenate(halves, axis=1), yb)
    o_ref[...] = acc


def _combine(csrc3, h2, tok_tab, ys):
    n_tok, D = h2.shape
    n_steps = n_tok // TL_PROJ
    tab = csrc3.shape[2]
    return pl.pallas_call(
        _combine_kernel,
        grid=(n_steps,),
        in_specs=[
            pl.BlockSpec((None, 1, tab), lambda i: (i, 0, 0), memory_space=pltpu.SMEM),
            pl.BlockSpec((None, 1, tab), lambda i: (jnp.minimum(i + 1, n_steps - 1), 0, 0),
                         memory_space=pltpu.SMEM),
            pl.BlockSpec((TL_PROJ, D), lambda i: (i, 0)),
            pl.BlockSpec((TL_PROJ, 2 * TOP_K), lambda i: (i, 0)),
            pl.BlockSpec(memory_space=pl.ANY),
        ],
        out_specs=pl.BlockSpec((TL_PROJ, D), lambda i: (i, 0)),
        out_shape=jax.ShapeDtypeStruct((n_tok, D), F32),
        scratch_shapes=[pltpu.VMEM((2, SLOTS, D), F32),
                        pltpu.VMEM((TOP_K, TL_PROJ, V7X_LANES), F32),
                        pltpu.VMEM((TOP_K, TL_PROJ, V7X_LANES), F32),
                        pltpu.SemaphoreType.DMA((2,))],
        compiler_params=_params(("arbitrary",)),
        name="combine",
    )(csrc3, csrc3, h2, tok_tab, ys)


def _s5_matrices(a_re, a_im, log_dt, b_re, b_im, c_re, c_im):
    G, P = a_re.shape
    dt = jnp.exp(log_dt.astype(F32))[:, None]
    lr = a_re.astype(F32)
    li = a_im.astype(F32)
    mag = jnp.exp(lr * dt)
    abar_r = mag * jnp.cos(li * dt)
    abar_i = mag * jnp.sin(li * dt)
    den = lr * lr + li * li
    nr = abar_r - 1.0
    ni = abar_i
    coef_r = (nr * lr + ni * li) / den
    coef_i = (ni * lr - nr * li) / den
    br_ = b_re.astype(F32)
    bi_ = b_im.astype(F32)
    bbar_r = coef_r[..., None] * br_ - coef_i[..., None] * bi_
    bbar_i = coef_r[..., None] * bi_ + coef_i[..., None] * br_

    def block_diag(t):
        _, g, a, b = t.shape
        rep = (jnp.arange(b)[:, None] == (jnp.arange(g * b) % b)[None, :]).astype(F32)
        tiled = jnp.einsum('xrb,bc->xrc', t.reshape(2, g * a, b), rep, precision=lax.Precision.HIGHEST)
        keep = (jnp.arange(g * a) // a)[:, None] == (jnp.arange(g * b) // b)[None, :]
        return jnp.where(keep[None], tiled, 0.0)

    def in_mat(bb):
        return block_diag(jnp.swapaxes(bb.reshape(2, G // 2, P, SSM_GROUP), 2, 3))

    def out_mat(cc):
        return block_diag(jnp.swapaxes(cc.reshape(2, G // 2, SSM_GROUP, P), 2, 3))

    bmat = jnp.concatenate([in_mat(bbar_r), in_mat(bbar_i)], axis=-1).astype(BF16)
    cmat = jnp.concatenate([out_mat(c_re.astype(F32)), -out_mat(c_im.astype(F32))], axis=1).astype(BF16)
    return bmat, abar_r.reshape(1, G * P), abar_i.reshape(1, G * P), cmat


def _pad_lanes(t):
    n, w = t.shape
    wp = ((w + V7X_LANES - 1) // V7X_LANES) * V7X_LANES
    return jnp.pad(t, ((0, 0), (0, wp - w))).reshape(n, 1, wp)


def kernel(x, norm1_g, w_in, b_gate, ssm_a_re, ssm_a_im, ssm_log_dt, ssm_b_re, ssm_b_im, ssm_c_re, ssm_c_im, ssm_d, ssm_w_glu, ssm_b_glu, q_norm_g, k_norm_g, lambda_q1, lambda_k1, lambda_q2, lambda_k2, subln_g, w_proj_ssm, w_proj_att, w_out, norm2_g, w_router, b_router, w_up, b_up, w_down, b_down):
    B, L, D = x.shape
    assert B == V7X_SUBLANES and D == 1024 and L % TL_PROJ == 0 and w_in.shape[0] == 1
    n_tok = B * L
    n_t = n_tok // TL_PROJ
    assert n_t <= V7X_LANES
    l = 0

    scale = ATT_SUB_DIM ** -0.5
    qg = (jnp.tile(q_norm_g[l].astype(F32), 2 * ATT_HEADS) * (scale * math.log2(math.e))).reshape(1, 512)
    kg = jnp.tile(k_norm_g[l].astype(F32), 2 * ATT_HEADS).reshape(1, 512)
    blk = jnp.arange(256) // ATT_SUB_DIM
    ones_bd = (blk[:, None] == blk[None, :]).astype(BF16)
    u_tm, q, k, vt, gates = _inproj(x, norm1_g[l].reshape(1, D), w_in[l],
                                    b_gate[l].reshape(1, 2 * D), qg, kg, ones_bd)

    bmat, ar, ai, cmat = _s5_matrices(ssm_a_re[l], ssm_a_im[l], ssm_log_dt[l], ssm_b_re[l], ssm_b_im[l],
                                      ssm_c_re[l], ssm_c_im[l])
    y_ssm = _s5(u_tm, bmat, ar, ai, cmat, ssm_d[l].reshape(1, 512).astype(F32),
                ssm_w_glu[l].astype(BF16), ssm_b_glu[l].reshape(1, 512).astype(F32))

    lam4 = jnp.stack([lambda_q1[l], lambda_k1[l], lambda_q2[l], lambda_k2[l]]).astype(F32)
    sg = jnp.tile(subln_g[l].astype(F32), ATT_HEADS).reshape(1, 512)
    y_att = _attn(q, k, vt, lam4, sg)

    tpos = jnp.arange(TL_PROJ)
    sut = (tpos[:, None] < tpos[None, :]).astype(BF16)
    epos = jnp.arange(N_EXPERTS)
    lt = (epos[None, :] < epos[:, None]).astype(BF16)
    h, xs, tok_tab, oct_tab = _mix(
        x, y_ssm, y_att, gates,
        w_proj_ssm[l].astype(BF16), w_proj_att[l].astype(BF16), w_out[l].astype(BF16),
        norm2_g[l].reshape(1, D), w_router[l].T.astype(BF16), b_router[l].reshape(N_EXPERTS, 1).astype(F32),
        sut, lt)

    n_blocks = (n_tok * TOP_K + n_t * N_EXPERTS * (OCT - 1) + MOE_ROWS - 1) // MOE_ROWS + N_EXPERTS
    meta, src, csrc = _moe_tables(oct_tab, n_t, n_blocks)

    de2 = w_up.shape[-1]
    b_up_p = b_up[l].reshape(N_EXPERTS, de2 // 256, 128, 2).transpose(0, 1, 3, 2).reshape(N_EXPERTS, 1, de2)
    pr = jnp.arange(256)
    col_src = jnp.where(pr < 128, 2 * pr, 2 * (pr - 128) + 1)
    perm = (jnp.arange(256)[:, None] == col_src[None, :]).astype(BF16)
    ys = _experts(meta, _pad_lanes(src), xs,
                  w_up[l], b_up_p, w_down[l], b_down[l].reshape(N_EXPERTS, 1, D), perm)

    out = _combine(_pad_lanes(csrc), h.reshape(n_tok, D), tok_tab, ys)
    return out.reshape(B, L, D)
```

```python
import math

import jax
import jax.numpy as jnp
from jax import lax
from jax.experimental import pallas as pl
from jax.experimental.pallas import tpu as pltpu

F32 = jnp.float32
BF16 = jnp.bfloat16
I32 = jnp.int32

RMS_EPS = 1e-6
CHUNK = 64
SSM_GROUP = 16
SSM_STATE = 64
ATT_HEADS = 4
ATT_SUB_DIM = 64
ATT_V_DIM = 128
N_EXPERTS = 32
TOP_K = 4
SWIGLU_ALPHA = 1.702
SWIGLU_LIMIT = 7.0
LAMBDA_INIT = 0.8 - 0.6 * math.exp(-0.3 * 0)

V7X_SUBLANES = 8
V7X_LANES = 128
V7X_MXU_DIM = 256

TL_IN = 1024
TL_PROJ = 512
TM = 256
SUB = TL_PROJ // TM
S5_STEPS = 128
TQ = 256
TK = 256
MOE_ROWS = 1024
MOE_ROW_VARIANTS = (1024, 512, 256)
OCT = V7X_SUBLANES
BLK_OCT = MOE_ROWS // OCT
SLOTS = ((TM * TOP_K + N_EXPERTS * (OCT - 1) + V7X_MXU_DIM - 1) // V7X_MXU_DIM) * V7X_MXU_DIM
SL_OCT = SLOTS // OCT
NEG = -1e30
VMEM_LIMIT = 56 * 1024 * 1024


def _dot(a, b):
    return jnp.dot(a, b, preferred_element_type=F32)


def _dot_nt(a, b):
    return lax.dot_general(a, b, (((1,), (1,)), ((), ())), preferred_element_type=F32)


def _params(sem, vmem=VMEM_LIMIT):
    return pltpu.CompilerParams(dimension_semantics=sem, vmem_limit_bytes=vmem)


def _inproj_kernel(x_ref, g1_ref, wf_ref, bg_ref, qg_ref, kg_ref, ones_ref,
                   u_ref, q_ref, k_ref, vt_ref, gate_ref, w_ref):
    @pl.when(jnp.logical_and(pl.program_id(0) == 0, pl.program_id(1) == 0))
    def _():
        for c in range(wf_ref.shape[1] // 512):
            w_ref[:, 512 * c:512 * (c + 1)] = wf_ref[:, 512 * c:512 * (c + 1)].astype(BF16)

    x = x_ref[...]
    ms = jnp.mean(x * x, axis=-1, keepdims=True)
    xn = (x * lax.rsqrt(ms + RMS_EPS) * g1_ref[...]).astype(BF16)

    u_ref[...] = _dot(xn, w_ref[:, 0:512])

    def head_norm(z, g):
        sq = (z * z).astype(BF16)
        ss = jnp.concatenate([_dot(sq[:, 0:256], ones_ref[...]),
                              _dot(sq[:, 256:512], ones_ref[...])], axis=1)
        return z * lax.rsqrt(ss * (1.0 / ATT_SUB_DIM) + RMS_EPS) * g

    q_ref[...] = head_norm(_dot(xn, w_ref[:, 512:1024]), qg_ref[...]).astype(BF16)
    k_ref[...] = head_norm(_dot(xn, w_ref[:, 1024:1536]), kg_ref[...]).astype(BF16)
    vt_ref[...] = _dot(xn, w_ref[:, 1536:2048]).T.astype(BF16)
    for c in range(4):
        lo = 2048 + 512 * c
        z = _dot(xn, w_ref[:, lo:lo + 512]) + bg_ref[:, 512 * c:512 * (c + 1)]
        gate_ref[:, 512 * c:512 * (c + 1)] = jax.nn.sigmoid(z).astype(BF16)


def _inproj(x, g1, w_in, b_gate, qg, kg, ones_bd):
    B, L, D = x.shape
    nl = L // TL_IN
    full = lambda shape: pl.BlockSpec(shape, lambda b, l: (0,) * len(shape))
    return pl.pallas_call(
        _inproj_kernel,
        grid=(B, nl),
        in_specs=[
            pl.BlockSpec((None, TL_IN, D), lambda b, l: (b, l, 0)),
            full((1, D)),
            pl.BlockSpec(w_in.shape, lambda b, l: (0, 0), pipeline_mode=pl.Buffered(1)),
            full((1, 2 * D)),
            full((1, 512)), full((1, 512)), full((256, 256)),
        ],
        out_specs=[
            pl.BlockSpec((None, TL_IN, 512), lambda b, l: (b, l, 0)),
            pl.BlockSpec((None, TL_IN, 512), lambda b, l: (b, l, 0)),
            pl.BlockSpec((None, TL_IN, 512), lambda b, l: (b, l, 0)),
            pl.BlockSpec((None, 512, TL_IN), lambda b, l: (b, 0, l)),
            pl.BlockSpec((None, TL_IN, 2 * D), lambda b, l: (b, l, 0)),
        ],
        out_shape=[
            jax.ShapeDtypeStruct((B, L, 512), F32),
            jax.ShapeDtypeStruct((B, L, 512), BF16),
            jax.ShapeDtypeStruct((B, L, 512), BF16),
            jax.ShapeDtypeStruct((B, 512, L), BF16),
            jax.ShapeDtypeStruct((B, L, 2 * D), BF16),
        ],
        scratch_shapes=[pltpu.VMEM(w_in.shape, BF16)],
        compiler_params=_params(("arbitrary", "arbitrary")),
        name="inproj",
    )(x, g1, w_in, b_gate, qg, kg, ones_bd)


def _s5_kernel(u_hbm, bm_ref, ar_ref, ai_ref, cm_ref, d_ref, wg_ref, bgl_ref,
               y_hbm, ubuf, ybuf, hs_ref, h_ref, usem, ysem):
    i = pl.program_id(0)
    n = pl.num_programs(0)
    slot = lax.rem(i, 2)
    nb = ubuf.shape[2]

    def u_copies(step, s):
        t0 = pl.multiple_of(step * S5_STEPS, S5_STEPS)
        return [pltpu.make_async_copy(u_hbm.at[b, pl.ds(t0, S5_STEPS), :], ubuf.at[s, :, b, :], usem.at[s])
                for b in range(nb)]

    def y_copies(step, s):
        t0 = pl.multiple_of(step * S5_STEPS, S5_STEPS)
        return [pltpu.make_async_copy(ybuf.at[s, :, b, :], y_hbm.at[b, pl.ds(t0, S5_STEPS), :], ysem.at[s])
                for b in range(nb)]

    @pl.when(i == 0)
    def _():
        h_ref[...] = jnp.zeros_like(h_ref)
        for cp in u_copies(0, 0):
            cp.start()

    @pl.when(i + 1 < n)
    def _():
        for cp in u_copies(i + 1, 1 - slot):
            cp.start()

    for cp in u_copies(i, slot):
        cp.wait()
    uf = ubuf[slot].reshape(S5_STEPS * nb, ubuf.shape[3])
    u = uf.astype(BF16)
    for hh in range(2):
        hs_ref[:, 2048 * hh:2048 * (hh + 1)] = _dot(u[:, 256 * hh:256 * (hh + 1)], bm_ref[hh])

    ys = []
    for hh in range(2):
        re0 = 2048 * hh
        im0 = re0 + 1024
        ar = jnp.broadcast_to(ar_ref[:, 1024 * hh:1024 * (hh + 1)], (V7X_SUBLANES, 1024))
        ai = jnp.broadcast_to(ai_ref[:, 1024 * hh:1024 * (hh + 1)], (V7X_SUBLANES, 1024))
        hr = h_ref[:, re0:re0 + 1024]
        hi = h_ref[:, im0:im0 + 1024]
        for t in range(S5_STEPS):
            r0 = t * V7X_SUBLANES
            xr = hs_ref[r0:r0 + V7X_SUBLANES, re0:re0 + 1024]
            xi = hs_ref[r0:r0 + V7X_SUBLANES, im0:im0 + 1024]
            hr, hi = ar * hr - ai * hi + xr, ar * hi + ai * hr + xi
            hs_ref[r0:r0 + V7X_SUBLANES, re0:re0 + 1024] = hr
            hs_ref[r0:r0 + V7X_SUBLANES, im0:im0 + 1024] = hi
        h_ref[:, re0:re0 + 1024] = hr
        h_ref[:, im0:im0 + 1024] = hi
        ys.append(_dot(hs_ref[:, re0:re0 + 2048].astype(BF16), cm_ref[hh]))

    y = jnp.concatenate(ys, axis=1)
    y = y + d_ref[...] * uf
    y = 0.5 * y * (1.0 + jnp.tanh(math.sqrt(2.0 / math.pi) * (y + 0.044715 * (y * y * y))))
    z = _dot(y.astype(BF16), wg_ref[...]) + bgl_ref[...]

    @pl.when(i >= 2)
    def _():
        for cp in y_copies(i - 2, slot):
            cp.wait()

    ybuf[slot] = (y * jax.nn.sigmoid(z)).reshape(S5_STEPS, nb, ybuf.shape[3])
    for cp in y_copies(i, slot):
        cp.start()

    @pl.when(i == n - 1)
    def _():
        for cp in y_copies(i, slot):
            cp.wait()

    @pl.when(jnp.logical_and(i == n - 1, i >= 1))
    def _():
        for cp in y_copies(i - 1, 1 - slot):
            cp.wait()


def _s5(u, bmat, ar, ai, cmat, d_skip, w_glu_bf, b_glu):
    B, L, C = u.shape
    R = S5_STEPS * B
    full = lambda shape: pl.BlockSpec(shape, lambda i: (0,) * len(shape))
    return pl.pallas_call(
        _s5_kernel,
        grid=(L // S5_STEPS,),
        in_specs=[
            pl.BlockSpec(memory_space=pl.ANY),
            full(bmat.shape), full(ar.shape), full(ai.shape), full(cmat.shape),
            full(d_skip.shape), full(w_glu_bf.shape), full(b_glu.shape),
        ],
        out_specs=pl.BlockSpec(memory_space=pl.ANY),
        out_shape=jax.ShapeDtypeStruct((B, L, C), F32),
        scratch_shapes=[pltpu.VMEM((2, S5_STEPS, B, C), F32), pltpu.VMEM((2, S5_STEPS, B, C), F32),
                        pltpu.VMEM((R, 4096), F32), pltpu.VMEM((B, 4096), F32),
                        pltpu.SemaphoreType.DMA((2,)), pltpu.SemaphoreType.DMA((2,))],
        compiler_params=_params(("arbitrary",)),
        name="s5",
    )(u, bmat, ar, ai, cmat, d_skip, w_glu_bf, b_glu)


def _attn_kernel(q_ref, k_ref, vt_ref, lam_ref, sg_ref, o_ref, qm_ref, acc_ref, ml_ref, s_ref):
    i = pl.program_id(1)
    n_chain = 2 * ATT_HEADS
    lamv = lam_ref[...]
    lam = (jnp.exp(jnp.sum(lamv[0:1] * lamv[1:2], axis=-1, keepdims=True))
           - jnp.exp(jnp.sum(lamv[2:3] * lamv[3:4], axis=-1, keepdims=True)) + LAMBDA_INIT)
    lane = lax.broadcasted_iota(I32, (TQ, 128), 1)
    key_chunk = lax.broadcasted_iota(I32, (TK, TQ), 0) // CHUNK
    qry_chunk = lax.broadcasted_iota(I32, (TK, TQ), 1) // CHUNK
    diag_ok = key_chunk <= qry_chunk

    for hd in range(ATT_HEADS):
        qh = q_ref[:, 128 * hd:128 * (hd + 1)]
        zero = jnp.zeros_like(qh)
        qm_ref[2 * hd] = jnp.where(lane < ATT_SUB_DIM, qh, zero)
        qm_ref[2 * hd + 1] = jnp.where(lane >= ATT_SUB_DIM, qh, zero)
    acc_ref[...] = jnp.zeros_like(acc_ref)
    row = lax.broadcasted_iota(I32, (2 * n_chain, TQ), 0)
    ml_ref[...] = jnp.where(row % 2 == 0, NEG, 0.0)

    def scores(j, par):
        off = pl.multiple_of(j * TK, TK)
        for hd in range(ATT_HEADS):
            ks = k_ref[pl.ds(off, TK), 128 * hd:128 * (hd + 1)]
            for s in range(2):
                s_ref[par, 2 * hd + s] = _dot_nt(ks, qm_ref[2 * hd + s])

    def consume(j, par, masked):
        off = pl.multiple_of(j * TK, TK)
        ps, alphas = [], []
        for c in range(n_chain):
            st = s_ref[par, c]
            if masked:
                st = jnp.where(diag_ok, st, NEG)
            m = ml_ref[2 * c:2 * c + 1, :]
            l = ml_ref[2 * c + 1:2 * c + 2, :]
            mn = jnp.maximum(m, jnp.max(st, axis=0, keepdims=True))
            alpha = jnp.exp2(m - mn)
            p = jnp.exp2(st - mn)
            ml_ref[2 * c:2 * c + 1, :] = mn
            ml_ref[2 * c + 1:2 * c + 2, :] = alpha * l + jnp.sum(p, axis=0, keepdims=True)
            ps.append(p.astype(BF16))
            alphas.append(alpha)
        for c in range(n_chain):
            hd = c // 2
            vts = vt_ref[128 * hd:128 * (hd + 1), pl.ds(off, TK)]
            acc_ref[c] = acc_ref[c] * alphas[c] + _dot(vts, ps[c])

    scores(0, 0)

    def body(jj, carry):
        j = 2 * jj
        scores(j + 1, 1)
        consume(j, 0, False)
        scores(j + 2, 0)
        consume(j + 1, 1, False)
        return carry

    lax.fori_loop(0, i // 2, body, 0)

    @pl.when(i % 2 == 1)
    def _():
        scores(i, 1)
        consume(i - 1, 0, False)
        consume(i, 1, True)

    @pl.when(i % 2 == 0)
    def _():
        consume(i, 0, True)

    for hd in range(ATT_HEADS):
        c0 = 128 * hd
        l1 = ml_ref[4 * hd + 1:4 * hd + 2, :]
        l2 = ml_ref[4 * hd + 3:4 * hd + 4, :]
        o = acc_ref[2 * hd] / l1 - lam * (acc_ref[2 * hd + 1] / l2)
        ms = jnp.mean(o * o, axis=0, keepdims=True)
        on = (o * lax.rsqrt(ms + RMS_EPS)).T
        o_ref[:, c0:c0 + 128] = (on * sg_ref[:, c0:c0 + 128] * (1.0 - LAMBDA_INIT)).astype(BF16)


def _attn(q, k, vt, lam4, sg):
    B, L, W = q.shape
    n_chain = 2 * ATT_HEADS
    return pl.pallas_call(
        _attn_kernel,
        grid=(B, L // TQ),
        in_specs=[
            pl.BlockSpec((None, TQ, W), lambda b, i: (b, i, 0)),
            pl.BlockSpec((None, L, W), lambda b, i: (b, 0, 0)),
            pl.BlockSpec((None, W, L), lambda b, i: (b, 0, 0)),
            pl.BlockSpec((4, ATT_SUB_DIM), lambda b, i: (0, 0)),
            pl.BlockSpec((1, W), lambda b, i: (0, 0)),
        ],
        out_specs=pl.BlockSpec((None, TQ, W), lambda b, i: (b, i, 0)),
        out_shape=jax.ShapeDtypeStruct((B, L, W), BF16),
        scratch_shapes=[pltpu.VMEM((n_chain, TQ, 128), BF16),
                        pltpu.VMEM((n_chain, ATT_V_DIM, TQ), F32),
                        pltpu.VMEM((2 * n_chain, TQ), F32),
                        pltpu.VMEM((2, n_chain, TK, TQ), F32)],
        compiler_params=_params(("arbitrary", "arbitrary")),
        name="diff_attn",
    )(q, k, vt, lam4, sg)


def _mix_kernel(x_ref, ys_ref, ya_ref, gate_ref, wps_ref, wpa_ref, wo_ref, g2_ref, wrt_ref, br_ref,
                sut_ref, lt_ref, h_ref, xs_ref, tok_ref, oct_ref):
    tile = pl.program_id(0) * pl.num_programs(1) + pl.program_id(1)

    D = x_ref.shape[-1]
    tl = x_ref.shape[0]
    gs = gate_ref[:, 0:D].astype(F32)
    ga = gate_ref[:, D:2 * D].astype(F32)
    mixed = gs * _dot(ys_ref[...].astype(BF16), wps_ref[...]) + ga * _dot(ya_ref[...], wpa_ref[...])
    h = x_ref[...] + _dot(mixed.astype(BF16), wo_ref[...])
    h_ref[...] = h
    ms = jnp.mean(h * h, axis=-1, keepdims=True)
    hb = (h * lax.rsqrt(ms + RMS_EPS) * g2_ref[...]).astype(BF16)

    lg = _dot_nt(wrt_ref[...], hb) + br_ref[...]
    eio = lax.broadcasted_iota(I32, (N_EXPERTS, tl), 0)
    vals, idxs = [], []
    for k in range(TOP_K):
        m = jnp.max(lg, axis=0, keepdims=True)
        idx = jnp.min(jnp.where(lg == m, eio, N_EXPERTS), axis=0, keepdims=True)
        vals.append(m)
        idxs.append(idx)
        lg = jnp.where(eio == idx, -jnp.inf, lg)
    ex = [jnp.exp(v - vals[0]) for v in vals]
    den = ex[0] + ex[1] + ex[2] + ex[3]
    wts = [e / den for e in ex]

    lane = lax.broadcasted_iota(I32, (N_EXPERTS, V7X_LANES), 1)

    @pl.when(tile == 0)
    def _():
        oct_ref[...] = jnp.zeros_like(oct_ref)

    slot_rows = [[] for _ in range(TOP_K)]
    for sb in range(SUB):
        t0 = TM * sb
        eio_t = lax.broadcasted_iota(I32, (N_EXPERTS, TM), 0)
        sel_t = [eio_t == idxs[k][:, t0:t0 + TM] for k in range(TOP_K)]
        mtot = jnp.zeros((N_EXPERTS, TM), F32)
        for k in range(TOP_K):
            mtot = mtot + jnp.where(sel_t[k], 1.0, 0.0)
        pre = _dot(mtot.astype(BF16), sut_ref[...])
        cnt = jnp.sum(mtot, axis=1, keepdims=True)
        seg_oct = jnp.floor((cnt + (OCT - 1.0)) * (1.0 / OCT))
        seg_b = jnp.broadcast_to(seg_oct, (N_EXPERTS, V7X_LANES))
        start_oct = _dot(lt_ref[...], seg_b.astype(BF16))
        base = pre + start_oct[:, 0:1] * float(OCT)
        slots_f = [jnp.sum(jnp.where(sel_t[k], base, 0.0), axis=0, keepdims=True) for k in range(TOP_K)]
        slots = [s.astype(I32) for s in slots_f]
        for k in range(TOP_K):
            slot_rows[k].append(slots_f[k])

        hb_t = hb[t0:t0 + TM, :]
        for c in range(SLOTS // V7X_MXU_DIM):
            sio = lax.broadcasted_iota(I32, (V7X_MXU_DIM, TM), 0) + V7X_MXU_DIM * c
            p = jnp.where(sio == slots[0], 1.0, 0.0)
            for k in range(1, TOP_K):
                p = jnp.where(sio == slots[k], 1.0, p)
            r0 = SLOTS * sb + V7X_MXU_DIM * c
            xs_ref[r0:r0 + V7X_MXU_DIM, :] = _dot(p.astype(BF16), hb_t)

        oct_ref[...] = jnp.where(lane == SUB * tile + sb, seg_b, oct_ref[...])

    slot_full = [jnp.concatenate(slot_rows[k], axis=1) for k in range(TOP_K)]
    tok_ref[...] = jnp.concatenate(slot_full + wts, axis=0).T


def _mix(x, ys_tm, ya, gates, wps, wpa, wo, g2, wrt, br, sut, lt):
    B, L, D = x.shape
    nl = L // TL_PROJ
    n_tok = B * L
    full = lambda shape: pl.BlockSpec(shape, lambda b, l: (0,) * len(shape))
    return pl.pallas_call(
        _mix_kernel,
        grid=(B, nl),
        in_specs=[
            pl.BlockSpec((None, TL_PROJ, D), lambda b, l: (b, l, 0)),
            pl.BlockSpec((None, TL_PROJ, 512), lambda b, l: (b, l, 0)),
            pl.BlockSpec((None, TL_PROJ, 512), lambda b, l: (b, l, 0)),
            pl.BlockSpec((None, TL_PROJ, 2 * D), lambda b, l: (b, l, 0)),
            full(wps.shape), full(wpa.shape), full(wo.shape), full((1, D)),
            full(wrt.shape), full(br.shape), full(sut.shape), full(lt.shape),
        ],
        out_specs=[
            pl.BlockSpec((None, TL_PROJ, D), lambda b, l: (b, l, 0)),
            pl.BlockSpec((SUB * SLOTS, D), lambda b, l: (b * nl + l, 0)),
            pl.BlockSpec((TL_PROJ, 2 * TOP_K), lambda b, l: (b * nl + l, 0)),
            full((N_EXPERTS, V7X_LANES)),
        ],
        out_shape=[
            jax.ShapeDtypeStruct((B, L, D), F32),
            jax.ShapeDtypeStruct((B * nl * SUB * SLOTS, D), F32),
            jax.ShapeDtypeStruct((n_tok, 2 * TOP_K), F32),
            jax.ShapeDtypeStruct((N_EXPERTS, V7X_LANES), F32),
        ],
        compiler_params=_params(("arbitrary", "arbitrary")),
        name="mix_router",
    )(x, ys_tm, ya, gates, wps, wpa, wo, g2, wrt, br, sut, lt)


def _moe_tables(oct_tab, n_t, n_blocks):
    E = N_EXPERTS
    O = oct_tab[:, :n_t].astype(I32)
    lstart = jnp.cumsum(O, axis=0) - O
    ecum = jnp.cumsum(O, axis=1) - O
    tot = jnp.sum(O, axis=1)
    nb = (tot + BLK_OCT - 1) // BLK_OCT
    bend = jnp.cumsum(nb)
    bstart = bend - nb
    n_valid = bend[-1]
    bi = jnp.arange(n_blocks, dtype=I32)
    er = jnp.arange(E, dtype=I32)
    tr = jnp.arange(n_t, dtype=I32)
    be = jnp.minimum(jnp.sum((bi[:, None] >= bend[None, :]).astype(I32), axis=1), E - 1)
    last_e = jnp.sum(jnp.where(bi == jnp.maximum(n_valid - 1, 0), be, 0))
    be = jnp.where(bi < n_valid, be, last_e)
    oh_e = be[:, None] == er[None, :]
    pick = lambda tab: jnp.sum(jnp.where(oh_e[:, :, None], tab[None, :, :], 0), axis=1)
    ecum_i, o_i, lst_i = pick(ecum), pick(O), pick(lstart)
    tot_i = jnp.sum(jnp.where(oh_e, tot[None, :], 0), axis=1)
    bst_i = jnp.sum(jnp.where(oh_e, bstart[None, :], 0), axis=1)
    g = (bi - bst_i)[:, None] * BLK_OCT + jnp.arange(BLK_OCT, dtype=I32)[None, :]
    tau = jnp.minimum(jnp.sum((g[:, :, None] >= (ecum_i + o_i)[:, None, :]).astype(I32), axis=2), n_t - 1)
    off = tr[None, :] * SL_OCT + lst_i - ecum_i
    src = jnp.sum(jnp.where(tau[:, :, None] == tr[None, None, :], off[:, None, :], 0), axis=2) + g
    src = jnp.where((g < tot_i[:, None]) & (bi[:, None] < n_valid), src, 0)
    s = jnp.arange(SL_OCT, dtype=I32)
    lend_t = (lstart + O).T
    e_s = jnp.minimum(jnp.sum((s[None, :, None] >= lend_t[:, None, :]).astype(I32), axis=2), E - 1)
    offc = (bstart[:, None] * BLK_OCT + ecum - lstart).T
    csrc = jnp.sum(jnp.where(e_s[:, :, None] == er[None, None, :], offc[:, None, :], 0), axis=2) + s[None, :]
    csrc = jnp.where(s[None, :] < jnp.sum(O, axis=0)[:, None], csrc, 0)
    prev_e = jnp.concatenate([jnp.full((1,), -1, I32), be[:-1]])
    first = ((be != prev_e) & (bi < n_valid)).astype(I32)
    par = (jnp.cumsum(first) - 1) % 2
    later = (er[None, :] > er[:, None]) & (nb[None, :] > 0)
    nxt_of_e = jnp.min(jnp.where(later, er[None, :], E), axis=1)
    nxt_of_e = jnp.where(nxt_of_e < E, nxt_of_e, -1)
    nxt = jnp.sum(jnp.where(oh_e, nxt_of_e[None, :], 0), axis=1)
    rows_i = jnp.clip(tot_i * OCT - (bi - bst_i) * MOE_ROWS, 0, MOE_ROWS)
    var = jnp.zeros_like(bi)
    for vi, m in enumerate(MOE_ROW_VARIANTS):
        var = jnp.where(rows_i <= m, vi, var)
    meta = (be, first, par.astype(I32), nxt.astype(I32), var.astype(I32), n_valid.astype(I32).reshape(1))
    return meta, src, csrc


def _gather_octets(src_ref, n_oct, src_hbm, dst, sem, first=0):
    for q in range(first, first + n_oct):
        row = pl.multiple_of(src_ref[0, q] * OCT, OCT)
        pltpu.make_async_copy(src_hbm.at[pl.ds(row, OCT), :], dst.at[pl.ds(OCT * q, OCT), :], sem).start()


def _expert_kernel(be_ref, first_ref, par_ref, nxt_ref, var_ref, nv_ref, src_cur_ref, src_nxt_ref, xs_hbm,
                   wup_hbm, bup_ref, wdn_hbm, bdn_ref, perm_ref, y_ref,
                   xbuf, wup_st, wdn_st, wup_bf, wdn_bf, act_ref, sem, wsem):
    i = pl.program_id(0)
    nv = nv_ref[0]
    slot = lax.rem(i, 2)

    def weight_copies(e, s):
        return (pltpu.make_async_copy(wup_hbm.at[e], wup_st.at[s], wsem.at[0, s]),
                pltpu.make_async_copy(wdn_hbm.at[e], wdn_st.at[s], wsem.at[1, s]))

    def gather_rows(src_ref, variant, s):
        for vi, m in enumerate(MOE_ROW_VARIANTS):
            @pl.when(variant == vi)
            def _():
                _gather_octets(src_ref, m // OCT, xs_hbm, xbuf.at[s], sem.at[s])

    @pl.when(i == 0)
    def _():
        gather_rows(src_cur_ref, var_ref[0], 0)
        for cp in weight_copies(be_ref[0], par_ref[0]):
            cp.start()

    @pl.when(i + 1 < nv)
    def _():
        gather_rows(src_nxt_ref, var_ref[jnp.minimum(i + 1, pl.num_programs(0) - 1)], 1 - slot)

    def block(m):
        pltpu.make_async_copy(xs_hbm.at[pl.ds(0, m), :], xbuf.at[slot, pl.ds(0, m), :], sem.at[slot]).wait()
        x = xbuf[slot, 0:m, :].astype(BF16)
        for cb in range(8):
            hp = _dot(x, wup_bf[:, 256 * cb:256 * (cb + 1)]) + bup_ref[:, 256 * cb:256 * (cb + 1)]
            glu = jnp.minimum(hp[:, 0:128], SWIGLU_LIMIT)
            lin = jnp.clip(hp[:, 128:256], -SWIGLU_LIMIT, SWIGLU_LIMIT)
            act = glu * jax.nn.sigmoid(SWIGLU_ALPHA * glu) * (lin + 1.0)
            act_ref[0:m, 128 * cb:128 * (cb + 1)] = act.astype(BF16)
        y_ref[0:m, :] = _dot(act_ref[0:m, :], wdn_bf[...]) + bdn_ref[...]
        if m < MOE_ROWS:
            y_ref[m:MOE_ROWS, :] = jnp.zeros((MOE_ROWS - m, y_ref.shape[1]), F32)

    @pl.when(i < nv)
    def _():
        @pl.when(first_ref[i] == 1)
        def _():
            s = par_ref[i]
            for cp in weight_copies(be_ref[i], s):
                cp.wait()

            @pl.when(nxt_ref[i] >= 0)
            def _():
                for cp in weight_copies(nxt_ref[i], 1 - s):
                    cp.start()

            for cb in range(8):
                blk = wup_st[s, :, 256 * cb:256 * (cb + 1)].astype(BF16)
                wup_bf[:, 256 * cb:256 * (cb + 1)] = _dot(blk, perm_ref[...]).astype(BF16)
            wdn_bf[...] = wdn_st[s].astype(BF16)

        for vi, m in enumerate(MOE_ROW_VARIANTS):
            @pl.when(var_ref[i] == vi)
            def _():
                block(m)

    @pl.when(i >= nv)
    def _():
        y_ref[...] = jnp.zeros_like(y_ref)


def _experts(meta, src3, xs, w_up, b_up_p, w_down, b_down, perm):
    n_blocks = src3.shape[0]
    DE2 = w_up.shape[2]
    D = w_up.shape[1]
    tab = src3.shape[2]
    be_map = lambda i, be, *_: (be[i], 0, 0)
    grid_spec = pltpu.PrefetchScalarGridSpec(
        num_scalar_prefetch=len(meta),
        grid=(n_blocks,),
        in_specs=[
            pl.BlockSpec((None, 1, tab), lambda i, *_: (i, 0, 0), memory_space=pltpu.SMEM),
            pl.BlockSpec((None, 1, tab), lambda i, *_: (jnp.minimum(i + 1, n_blocks - 1), 0, 0),
                         memory_space=pltpu.SMEM),
            pl.BlockSpec(memory_space=pl.ANY),
            pl.BlockSpec(memory_space=pl.ANY),
            pl.BlockSpec((None, 1, DE2), be_map),
            pl.BlockSpec(memory_space=pl.ANY),
            pl.BlockSpec((None, 1, D), be_map),
            pl.BlockSpec((256, 256), lambda i, *_: (0, 0)),
        ],
        out_specs=pl.BlockSpec((MOE_ROWS, D), lambda i, *_: (i, 0)),
        scratch_shapes=[
            pltpu.VMEM((2, MOE_ROWS, D), F32),
            pltpu.VMEM((2, D, DE2), F32),
            pltpu.VMEM((2, DE2 // 2, D), F32),
            pltpu.VMEM((D, DE2), BF16),
            pltpu.VMEM((DE2 // 2, D), BF16),
            pltpu.VMEM((MOE_ROWS, DE2 // 2), BF16),
            pltpu.SemaphoreType.DMA((2,)),
            pltpu.SemaphoreType.DMA((2, 2)),
        ],
    )
    return pl.pallas_call(
        _expert_kernel,
        grid_spec=grid_spec,
        out_shape=jax.ShapeDtypeStruct((n_blocks * MOE_ROWS, D), F32),
        compiler_params=_params(("arbitrary",)),
        name="experts",
    )(*meta, src3, src3, xs, w_up, b_up_p, w_down, b_down, perm)


def _combine_kernel(src_cur_ref, src_nxt_ref, h_ref, tok_ref, ys_hbm, o_ref,
                    ybuf, sl_b, w_b, sem):
    i = pl.program_id(0)
    n = pl.num_programs(0)
    slot = lax.rem(i, 2)

    @pl.when(i == 0)
    def _():
        _gather_octets(src_cur_ref, SUB * SL_OCT, ys_hbm, ybuf.at[0], sem.at[0])

    @pl.when(i + 1 < n)
    def _():
        _gather_octets(src_nxt_ref, SUB * SL_OCT, ys_hbm, ybuf.at[1 - slot], sem.at[1 - slot])

    tl = h_ref.shape[0]
    for k in range(TOP_K):
        sl_b[k] = jnp.broadcast_to(tok_ref[:, k:k + 1], (tl, V7X_LANES))
        w_b[k] = jnp.broadcast_to(tok_ref[:, TOP_K + k:TOP_K + k + 1], (tl, V7X_LANES))
    lane = lax.broadcasted_iota(I32, (tl, V7X_LANES), 1).astype(F32)
    pltpu.make_async_copy(ys_hbm.at[pl.ds(0, SUB * SLOTS), :], ybuf.at[slot], sem.at[slot]).wait()
    lane_t = lax.broadcasted_iota(I32, (TM, V7X_LANES), 1).astype(F32)
    for sb in range(SUB):
        t0 = TM * sb
        acc = h_ref[t0:t0 + TM, :]
        for c in range(SLOTS // V7X_MXU_DIM):
            halves = []
            for hc in range(V7X_MXU_DIM // V7X_LANES):
                sio = lane_t + float(V7X_MXU_DIM * c + V7X_LANES * hc)
                wm = jnp.where(sio == sl_b[0, t0:t0 + TM], w_b[0, t0:t0 + TM], 0.0)
                for k in range(1, TOP_K):
                    wm = jnp.where(sio == sl_b[k, t0:t0 + TM], w_b[k, t0:t0 + TM], wm)
                halves.append(wm.astype(BF16))
            r0 = SLOTS * sb + V7X_MXU_DIM * c
            acc = acc + _dot(jnp.concatenate(halves, axis=1), ybuf[slot, r0:r0 + V7X_MXU_DIM, :].astype(BF16))
        o_ref[t0:t0 + TM, :] = acc


def _combine(csrc3, h2, tok_tab, ys):
    n_tok, D = h2.shape
    n_steps = n_tok // TL_PROJ
    tab = csrc3.shape[2]
    return pl.pallas_call(
        _combine_kernel,
        grid=(n_steps,),
        in_specs=[
            pl.BlockSpec((None, 1, tab), lambda i: (i, 0, 0), memory_space=pltpu.SMEM),
            pl.BlockSpec((None, 1, tab), lambda i: (jnp.minimum(i + 1, n_steps - 1), 0, 0),
                         memory_space=pltpu.SMEM),
            pl.BlockSpec((TL_PROJ, D), lambda i: (i, 0)),
            pl.BlockSpec((TL_PROJ, 2 * TOP_K), lambda i: (i, 0)),
            pl.BlockSpec(memory_space=pl.ANY),
        ],
        out_specs=pl.BlockSpec((TL_PROJ, D), lambda i: (i, 0)),
        out_shape=jax.ShapeDtypeStruct((n_tok, D), F32),
        scratch_shapes=[pltpu.VMEM((2, SUB * SLOTS, D), F32),
                        pltpu.VMEM((TOP_K, TL_PROJ, V7X_LANES), F32),
                        pltpu.VMEM((TOP_K, TL_PROJ, V7X_LANES), F32),
                        pltpu.SemaphoreType.DMA((2,))],
        compiler_params=_params(("arbitrary",)),
        name="combine",
    )(csrc3, csrc3, h2, tok_tab, ys)


def _s5_matrices(a_re, a_im, log_dt, b_re, b_im, c_re, c_im):
    G, P = a_re.shape
    dt = jnp.exp(log_dt.astype(F32))[:, None]
    lr = a_re.astype(F32)
    li = a_im.astype(F32)
    mag = jnp.exp(lr * dt)
    abar_r = mag * jnp.cos(li * dt)
    abar_i = mag * jnp.sin(li * dt)
    den = lr * lr + li * li
    nr = abar_r - 1.0
    ni = abar_i
    coef_r = (nr * lr + ni * li) / den
    coef_i = (ni * lr - nr * li) / den
    br_ = b_re.astype(F32)
    bi_ = b_im.astype(F32)
    bbar_r = coef_r[..., None] * br_ - coef_i[..., None] * bi_
    bbar_i = coef_r[..., None] * bi_ + coef_i[..., None] * br_

    def block_diag(t):
        _, g, a, b = t.shape
        rep = (jnp.arange(b)[:, None] == (jnp.arange(g * b) % b)[None, :]).astype(F32)
        tiled = jnp.einsum('xrb,bc->xrc', t.reshape(2, g * a, b), rep, precision=lax.Precision.HIGHEST)
        keep = (jnp.arange(g * a) // a)[:, None] == (jnp.arange(g * b) // b)[None, :]
        return jnp.where(keep[None], tiled, 0.0)

    def in_mat(bb):
        return block_diag(jnp.swapaxes(bb.reshape(2, G // 2, P, SSM_GROUP), 2, 3))

    def out_mat(cc):
        return block_diag(jnp.swapaxes(cc.reshape(2, G // 2, SSM_GROUP, P), 2, 3))

    bmat = jnp.concatenate([in_mat(bbar_r), in_mat(bbar_i)], axis=-1).astype(BF16)
    cmat = jnp.concatenate([out_mat(c_re.astype(F32)), -out_mat(c_im.astype(F32))], axis=1).astype(BF16)
    return bmat, abar_r.reshape(1, G * P), abar_i.reshape(1, G * P), cmat


def _pad_lanes(t):
    n, w = t.shape
    wp = ((w + V7X_LANES - 1) // V7X_LANES) * V7X_LANES
    return jnp.pad(t, ((0, 0), (0, wp - w))).reshape(n, 1, wp)


def kernel(x, norm1_g, w_in, b_gate, ssm_a_re, ssm_a_im, ssm_log_dt, ssm_b_re, ssm_b_im, ssm_c_re, ssm_c_im, ssm_d, ssm_w_glu, ssm_b_glu, q_norm_g, k_norm_g, lambda_q1, lambda_k1, lambda_q2, lambda_k2, subln_g, w_proj_ssm, w_proj_att, w_out, norm2_g, w_router, b_router, w_up, b_up, w_down, b_down):
    B, L, D = x.shape
    assert B == V7X_SUBLANES and D == 1024 and L % TL_PROJ == 0 and w_in.shape[0] == 1
    n_tok = B * L
    n_t = n_tok // TM
    assert n_t <= V7X_LANES
    l = 0

    scale = ATT_SUB_DIM ** -0.5
    qg = (jnp.tile(q_norm_g[l].astype(F32), 2 * ATT_HEADS) * (scale * math.log2(math.e))).reshape(1, 512)
    kg = jnp.tile(k_norm_g[l].astype(F32), 2 * ATT_HEADS).reshape(1, 512)
    blk = jnp.arange(256) // ATT_SUB_DIM
    ones_bd = (blk[:, None] == blk[None, :]).astype(BF16)
    u_tm, q, k, vt, gates = _inproj(x, norm1_g[l].reshape(1, D), w_in[l],
                                    b_gate[l].reshape(1, 2 * D), qg, kg, ones_bd)

    bmat, ar, ai, cmat = _s5_matrices(ssm_a_re[l], ssm_a_im[l], ssm_log_dt[l], ssm_b_re[l], ssm_b_im[l],
                                      ssm_c_re[l], ssm_c_im[l])
    y_ssm = _s5(u_tm, bmat, ar, ai, cmat, ssm_d[l].reshape(1, 512).astype(F32),
                ssm_w_glu[l].astype(BF16), ssm_b_glu[l].reshape(1, 512).astype(F32))

    lam4 = jnp.stack([lambda_q1[l], lambda_k1[l], lambda_q2[l], lambda_k2[l]]).astype(F32)
    sg = jnp.tile(subln_g[l].astype(F32), ATT_HEADS).reshape(1, 512)
    y_att = _attn(q, k, vt, lam4, sg)

    tpos = jnp.arange(TM)
    sut = (tpos[:, None] < tpos[None, :]).astype(BF16)
    epos = jnp.arange(N_EXPERTS)
    lt = (epos[None, :] < epos[:, None]).astype(BF16)
    h, xs, tok_tab, oct_tab = _mix(
        x, y_ssm, y_att, gates,
        w_proj_ssm[l].astype(BF16), w_proj_att[l].astype(BF16), w_out[l].astype(BF16),
        norm2_g[l].reshape(1, D), w_router[l].T.astype(BF16), b_router[l].reshape(N_EXPERTS, 1).astype(F32),
        sut, lt)

    n_blocks = (n_tok * TOP_K + n_t * N_EXPERTS * (OCT - 1) + MOE_ROWS - 1) // MOE_ROWS + N_EXPERTS
    meta, src, csrc = _moe_tables(oct_tab, n_t, n_blocks)

    de2 = w_up.shape[-1]
    b_up_p = b_up[l].reshape(N_EXPERTS, de2 // 256, 128, 2).transpose(0, 1, 3, 2).reshape(N_EXPERTS, 1, de2)
    pr = jnp.arange(256)
    col_src = jnp.where(pr < 128, 2 * pr, 2 * (pr - 128) + 1)
    perm = (jnp.arange(256)[:, None] == col_src[None, :]).astype(BF16)
    ys = _experts(meta, _pad_lanes(src), xs,
                  w_up[l], b_up_p, w_down[l], b_down[l].reshape(N_EXPERTS, 1, D), perm)

    out = _combine(_pad_lanes(csrc.reshape(n_t // SUB, SUB * SL_OCT)), h.reshape(n_tok, D), tok_tab, ys)
    return out.reshape(B, L, D)
```

```python
import math

import jax
import jax.numpy as jnp
from jax import lax
from jax.experimental import pallas as pl
from jax.experimental.pallas import tpu as pltpu

F32 = jnp.float32
BF16 = jnp.bfloat16
I32 = jnp.int32

RMS_EPS = 1e-6
CHUNK = 64
SSM_GROUP = 16
SSM_STATE = 64
ATT_HEADS = 4
ATT_SUB_DIM = 64
ATT_V_DIM = 128
N_EXPERTS = 32
TOP_K = 4
SWIGLU_ALPHA = 1.702
SWIGLU_LIMIT = 7.0
LAMBDA_INIT = 0.8 - 0.6 * math.exp(-0.3 * 0)

V7X_SUBLANES = 8
V7X_LANES = 128
V7X_MXU_DIM = 256

TL_IN = 1024
TL_PROJ = 512
TM = 512
SUB = TL_PROJ // TM
S5_STEPS = 128
S5_PARTS = 4
TQ = 256
TK = 256
MOE_ROWS = 1024
MOE_ROW_VARIANTS = (1024, 768, 512, 384, 256, 128)
OCT = V7X_SUBLANES
BLK_OCT = MOE_ROWS // OCT
SLOTS = ((TM * TOP_K + N_EXPERTS * (OCT - 1) + V7X_MXU_DIM - 1) // V7X_MXU_DIM) * V7X_MXU_DIM
SL_OCT = SLOTS // OCT
NEG = -1e30
VMEM_LIMIT = 56 * 1024 * 1024


def _dot(a, b):
    return jnp.dot(a, b, preferred_element_type=F32)


def _dot_nt(a, b):
    return lax.dot_general(a, b, (((1,), (1,)), ((), ())), preferred_element_type=F32)


def _params(sem, vmem=VMEM_LIMIT):
    return pltpu.CompilerParams(dimension_semantics=sem, vmem_limit_bytes=vmem)


def _inproj_kernel(x_ref, g1_ref, wf_ref, bg_ref, qg_ref, kg_ref, ones_ref,
                   u_ref, q_ref, k_ref, vt_ref, gate_ref, w_ref):
    @pl.when(jnp.logical_and(pl.program_id(0) == 0, pl.program_id(1) == 0))
    def _():
        for c in range(wf_ref.shape[1] // 512):
            w_ref[:, 512 * c:512 * (c + 1)] = wf_ref[:, 512 * c:512 * (c + 1)].astype(BF16)

    x = x_ref[...]
    ms = jnp.mean(x * x, axis=-1, keepdims=True)
    xn = (x * lax.rsqrt(ms + RMS_EPS) * g1_ref[...]).astype(BF16)

    u_ref[...] = _dot(xn, w_ref[:, 0:512])

    def head_norm(z, g):
        sq = (z * z).astype(BF16)
        ss = jnp.concatenate([_dot(sq[:, 0:256], ones_ref[...]),
                              _dot(sq[:, 256:512], ones_ref[...])], axis=1)
        return z * lax.rsqrt(ss * (1.0 / ATT_SUB_DIM) + RMS_EPS) * g

    q_ref[...] = head_norm(_dot(xn, w_ref[:, 512:1024]), qg_ref[...]).astype(BF16)
    k_ref[...] = head_norm(_dot(xn, w_ref[:, 1024:1536]), kg_ref[...]).astype(BF16)
    vt_ref[...] = _dot(xn, w_ref[:, 1536:2048]).T.astype(BF16)
    for c in range(4):
        lo = 2048 + 512 * c
        z = _dot(xn, w_ref[:, lo:lo + 512]) + bg_ref[:, 512 * c:512 * (c + 1)]
        gate_ref[:, 512 * c:512 * (c + 1)] = jax.nn.sigmoid(z).astype(BF16)


def _inproj(x, g1, w_in, b_gate, qg, kg, ones_bd):
    B, L, D = x.shape
    nl = L // TL_IN
    full = lambda shape: pl.BlockSpec(shape, lambda b, l: (0,) * len(shape))
    return pl.pallas_call(
        _inproj_kernel,
        grid=(B, nl),
        in_specs=[
            pl.BlockSpec((None, TL_IN, D), lambda b, l: (b, l, 0)),
            full((1, D)),
            pl.BlockSpec(w_in.shape, lambda b, l: (0, 0), pipeline_mode=pl.Buffered(1)),
            full((1, 2 * D)),
            full((1, 512)), full((1, 512)), full((256, 256)),
        ],
        out_specs=[
            pl.BlockSpec((None, TL_IN, 512), lambda b, l: (b, l, 0)),
            pl.BlockSpec((None, TL_IN, 512), lambda b, l: (b, l, 0)),
            pl.BlockSpec((None, TL_IN, 512), lambda b, l: (b, l, 0)),
            pl.BlockSpec((None, 512, TL_IN), lambda b, l: (b, 0, l)),
            pl.BlockSpec((None, TL_IN, 2 * D), lambda b, l: (b, l, 0)),
        ],
        out_shape=[
            jax.ShapeDtypeStruct((B, L, 512), F32),
            jax.ShapeDtypeStruct((B, L, 512), BF16),
            jax.ShapeDtypeStruct((B, L, 512), BF16),
            jax.ShapeDtypeStruct((B, 512, L), BF16),
            jax.ShapeDtypeStruct((B, L, 2 * D), BF16),
        ],
        scratch_shapes=[pltpu.VMEM(w_in.shape, BF16)],
        compiler_params=_params(("arbitrary", "arbitrary")),
        name="inproj",
    )(x, g1, w_in, b_gate, qg, kg, ones_bd)


def _s5_kernel(u_hbm, bm_ref, ar_ref, ai_ref, cm_ref, d_ref, wg_ref, bgl_ref,
               y_hbm, ubuf, ybuf, hs_ref, h_ref, usem, ysem):
    i = pl.program_id(0)
    n = pl.num_programs(0)
    slot = lax.rem(i, 2)
    nb = ubuf.shape[2]

    def u_copies(step, s):
        t0 = pl.multiple_of(step * S5_STEPS, S5_STEPS)
        return [pltpu.make_async_copy(u_hbm.at[b, pl.ds(t0, S5_STEPS), :], ubuf.at[s, :, b, :], usem.at[s])
                for b in range(nb)]

    def y_copies(step, s):
        t0 = pl.multiple_of(step * S5_STEPS, S5_STEPS)
        return [pltpu.make_async_copy(ybuf.at[s, :, b, :], y_hbm.at[b, pl.ds(t0, S5_STEPS), :], ysem.at[s])
                for b in range(nb)]

    @pl.when(i == 0)
    def _():
        h_ref[...] = jnp.zeros_like(h_ref)
        for cp in u_copies(0, 0):
            cp.start()

    @pl.when(i + 1 < n)
    def _():
        for cp in u_copies(i + 1, 1 - slot):
            cp.start()

    for cp in u_copies(i, slot):
        cp.wait()
    uf = ubuf[slot].reshape(S5_STEPS * nb, ubuf.shape[3])
    u = uf.astype(BF16)
    n_part = bm_ref.shape[0]
    cw = u.shape[1] // n_part
    sw = bm_ref.shape[2] // 2
    for pt in range(n_part):
        hs_ref[:, 2 * sw * pt:2 * sw * (pt + 1)] = _dot(u[:, cw * pt:cw * (pt + 1)], bm_ref[pt])

    ys = []
    for pt in range(n_part):
        re0 = 2 * sw * pt
        im0 = re0 + sw
        ar = jnp.broadcast_to(ar_ref[:, sw * pt:sw * (pt + 1)], (V7X_SUBLANES, sw))
        ai = jnp.broadcast_to(ai_ref[:, sw * pt:sw * (pt + 1)], (V7X_SUBLANES, sw))
        hr = h_ref[:, re0:re0 + sw]
        hi = h_ref[:, im0:im0 + sw]
        for t in range(S5_STEPS):
            r0 = t * V7X_SUBLANES
            xr = hs_ref[r0:r0 + V7X_SUBLANES, re0:re0 + sw]
            xi = hs_ref[r0:r0 + V7X_SUBLANES, im0:im0 + sw]
            hr, hi = ar * hr - ai * hi + xr, ar * hi + ai * hr + xi
            hs_ref[r0:r0 + V7X_SUBLANES, re0:re0 + sw] = hr
            hs_ref[r0:r0 + V7X_SUBLANES, im0:im0 + sw] = hi
        h_ref[:, re0:re0 + sw] = hr
        h_ref[:, im0:im0 + sw] = hi
        ys.append(_dot(hs_ref[:, re0:re0 + 2 * sw].astype(BF16), cm_ref[pt]))

    y = jnp.concatenate(ys, axis=1)
    y = y + d_ref[...] * uf
    y = 0.5 * y * (1.0 + jnp.tanh(math.sqrt(2.0 / math.pi) * (y + 0.044715 * (y * y * y))))
    z = _dot(y.astype(BF16), wg_ref[...]) + bgl_ref[...]

    @pl.when(i >= 2)
    def _():
        for cp in y_copies(i - 2, slot):
            cp.wait()

    ybuf[slot] = (y * jax.nn.sigmoid(z)).reshape(S5_STEPS, nb, ybuf.shape[3])
    for cp in y_copies(i, slot):
        cp.start()

    @pl.when(i == n - 1)
    def _():
        for cp in y_copies(i, slot):
            cp.wait()

    @pl.when(jnp.logical_and(i == n - 1, i >= 1))
    def _():
        for cp in y_copies(i - 1, 1 - slot):
            cp.wait()


def _s5(u, bmat, ar, ai, cmat, d_skip, w_glu_bf, b_glu):
    B, L, C = u.shape
    R = S5_STEPS * B
    full = lambda shape: pl.BlockSpec(shape, lambda i: (0,) * len(shape))
    return pl.pallas_call(
        _s5_kernel,
        grid=(L // S5_STEPS,),
        in_specs=[
            pl.BlockSpec(memory_space=pl.ANY),
            full(bmat.shape), full(ar.shape), full(ai.shape), full(cmat.shape),
            full(d_skip.shape), full(w_glu_bf.shape), full(b_glu.shape),
        ],
        out_specs=pl.BlockSpec(memory_space=pl.ANY),
        out_shape=jax.ShapeDtypeStruct((B, L, C), F32),
        scratch_shapes=[pltpu.VMEM((2, S5_STEPS, B, C), F32), pltpu.VMEM((2, S5_STEPS, B, C), F32),
                        pltpu.VMEM((R, 4096), F32), pltpu.VMEM((B, 4096), F32),
                        pltpu.SemaphoreType.DMA((2,)), pltpu.SemaphoreType.DMA((2,))],
        compiler_params=_params(("arbitrary",)),
        name="s5",
    )(u, bmat, ar, ai, cmat, d_skip, w_glu_bf, b_glu)


def _attn_kernel(q_ref, k_ref, vt_ref, lam_ref, sg_ref, o_ref, qm_ref, acc_ref, ml_ref, s_ref):
    i = pl.program_id(1)
    n_chain = 2 * ATT_HEADS
    lamv = lam_ref[...]
    lam = (jnp.exp(jnp.sum(lamv[0:1] * lamv[1:2], axis=-1, keepdims=True))
           - jnp.exp(jnp.sum(lamv[2:3] * lamv[3:4], axis=-1, keepdims=True)) + LAMBDA_INIT)
    lane = lax.broadcasted_iota(I32, (TQ, 128), 1)
    key_chunk = lax.broadcasted_iota(I32, (TK, TQ), 0) // CHUNK
    qry_chunk = lax.broadcasted_iota(I32, (TK, TQ), 1) // CHUNK
    diag_ok = key_chunk <= qry_chunk

    for hd in range(ATT_HEADS):
        qh = q_ref[:, 128 * hd:128 * (hd + 1)]
        zero = jnp.zeros_like(qh)
        qm_ref[2 * hd] = jnp.where(lane < ATT_SUB_DIM, qh, zero)
        qm_ref[2 * hd + 1] = jnp.where(lane >= ATT_SUB_DIM, qh, zero)
    acc_ref[...] = jnp.zeros_like(acc_ref)
    row = lax.broadcasted_iota(I32, (2 * n_chain, TQ), 0)
    ml_ref[...] = jnp.where(row % 2 == 0, NEG, 0.0)

    def scores(j, par):
        off = pl.multiple_of(j * TK, TK)
        for hd in range(ATT_HEADS):
            ks = k_ref[pl.ds(off, TK), 128 * hd:128 * (hd + 1)]
            for s in range(2):
                s_ref[par, 2 * hd + s] = _dot_nt(ks, qm_ref[2 * hd + s])

    def consume(j, par, masked):
        off = pl.multiple_of(j * TK, TK)
        ps, alphas = [], []
        for c in range(n_chain):
            st = s_ref[par, c]
            if masked:
                st = jnp.where(diag_ok, st, NEG)
            m = ml_ref[2 * c:2 * c + 1, :]
            l = ml_ref[2 * c + 1:2 * c + 2, :]
            mn = jnp.maximum(m, jnp.max(st, axis=0, keepdims=True))
            alpha = jnp.exp2(m - mn)
            p = jnp.exp2(st - mn)
            ml_ref[2 * c:2 * c + 1, :] = mn
            ml_ref[2 * c + 1:2 * c + 2, :] = alpha * l + jnp.sum(p, axis=0, keepdims=True)
            ps.append(p.astype(BF16))
            alphas.append(alpha)
        for c in range(n_chain):
            hd = c // 2
            vts = vt_ref[128 * hd:128 * (hd + 1), pl.ds(off, TK)]
            acc_ref[c] = acc_ref[c] * alphas[c] + _dot(vts, ps[c])

    scores(0, 0)

    def body(jj, carry):
        j = 2 * jj
        scores(j + 1, 1)
        consume(j, 0, False)
        scores(j + 2, 0)
        consume(j + 1, 1, False)
        return carry

    lax.fori_loop(0, i // 2, body, 0)

    @pl.when(i % 2 == 1)
    def _():
        scores(i, 1)
        consume(i - 1, 0, False)
        consume(i, 1, True)

    @pl.when(i % 2 == 0)
    def _():
        consume(i, 0, True)

    for hd in range(ATT_HEADS):
        c0 = 128 * hd
        l1 = ml_ref[4 * hd + 1:4 * hd + 2, :]
        l2 = ml_ref[4 * hd + 3:4 * hd + 4, :]
        o = acc_ref[2 * hd] / l1 - lam * (acc_ref[2 * hd + 1] / l2)
        ms = jnp.mean(o * o, axis=0, keepdims=True)
        on = (o * lax.rsqrt(ms + RMS_EPS)).T
        o_ref[:, c0:c0 + 128] = (on * sg_ref[:, c0:c0 + 128] * (1.0 - LAMBDA_INIT)).astype(BF16)


def _attn(q, k, vt, lam4, sg):
    B, L, W = q.shape
    n_chain = 2 * ATT_HEADS
    return pl.pallas_call(
        _attn_kernel,
        grid=(B, L // TQ),
        in_specs=[
            pl.BlockSpec((None, TQ, W), lambda b, i: (b, i, 0)),
            pl.BlockSpec((None, L, W), lambda b, i: (b, 0, 0)),
            pl.BlockSpec((None, W, L), lambda b, i: (b, 0, 0)),
            pl.BlockSpec((4, ATT_SUB_DIM), lambda b, i: (0, 0)),
            pl.BlockSpec((1, W), lambda b, i: (0, 0)),
        ],
        out_specs=pl.BlockSpec((None, TQ, W), lambda b, i: (b, i, 0)),
        out_shape=jax.ShapeDtypeStruct((B, L, W), BF16),
        scratch_shapes=[pltpu.VMEM((n_chain, TQ, 128), BF16),
                        pltpu.VMEM((n_chain, ATT_V_DIM, TQ), F32),
                        pltpu.VMEM((2 * n_chain, TQ), F32),
                        pltpu.VMEM((2, n_chain, TK, TQ), F32)],
        compiler_params=_params(("arbitrary", "arbitrary")),
        name="diff_attn",
    )(q, k, vt, lam4, sg)


def _mix_kernel(x_ref, ys_ref, ya_ref, gate_ref, wps_ref, wpa_ref, wo_ref, g2_ref, wrt_ref, br_ref,
                sut_ref, lt_ref, h_ref, xs_ref, tok_ref, oct_ref):
    tile = pl.program_id(0) * pl.num_programs(1) + pl.program_id(1)

    D = x_ref.shape[-1]
    tl = x_ref.shape[0]
    gs = gate_ref[:, 0:D].astype(F32)
    ga = gate_ref[:, D:2 * D].astype(F32)
    mixed = gs * _dot(ys_ref[...].astype(BF16), wps_ref[...]) + ga * _dot(ya_ref[...], wpa_ref[...])
    h = x_ref[...] + _dot(mixed.astype(BF16), wo_ref[...])
    h_ref[...] = h
    ms = jnp.mean(h * h, axis=-1, keepdims=True)
    hb = (h * lax.rsqrt(ms + RMS_EPS) * g2_ref[...]).astype(BF16)

    lg = _dot_nt(wrt_ref[...], hb) + br_ref[...]
    eio = lax.broadcasted_iota(I32, (N_EXPERTS, tl), 0)
    vals, idxs = [], []
    for k in range(TOP_K):
        m = jnp.max(lg, axis=0, keepdims=True)
        idx = jnp.min(jnp.where(lg == m, eio, N_EXPERTS), axis=0, keepdims=True)
        vals.append(m)
        idxs.append(idx)
        lg = jnp.where(eio == idx, -jnp.inf, lg)
    ex = [jnp.exp(v - vals[0]) for v in vals]
    den = ex[0] + ex[1] + ex[2] + ex[3]
    wts = [e / den for e in ex]

    lane = lax.broadcasted_iota(I32, (N_EXPERTS, V7X_LANES), 1)

    @pl.when(tile == 0)
    def _():
        oct_ref[...] = jnp.zeros_like(oct_ref)

    slot_rows = [[] for _ in range(TOP_K)]
    for sb in range(SUB):
        t0 = TM * sb
        eio_t = lax.broadcasted_iota(I32, (N_EXPERTS, TM), 0)
        sel_t = [eio_t == idxs[k][:, t0:t0 + TM] for k in range(TOP_K)]
        mtot = jnp.zeros((N_EXPERTS, TM), F32)
        for k in range(TOP_K):
            mtot = mtot + jnp.where(sel_t[k], 1.0, 0.0)
        pre = _dot(mtot.astype(BF16), sut_ref[...])
        cnt = jnp.sum(mtot, axis=1, keepdims=True)
        seg_oct = jnp.floor((cnt + (OCT - 1.0)) * (1.0 / OCT))
        seg_b = jnp.broadcast_to(seg_oct, (N_EXPERTS, V7X_LANES))
        start_oct = _dot(lt_ref[...], seg_b.astype(BF16))
        base = pre + start_oct[:, 0:1] * float(OCT)
        slots_f = [jnp.sum(jnp.where(sel_t[k], base, 0.0), axis=0, keepdims=True) for k in range(TOP_K)]
        slots = [s.astype(I32) for s in slots_f]
        for k in range(TOP_K):
            slot_rows[k].append(slots_f[k])

        hb_t = hb[t0:t0 + TM, :]
        for c in range(SLOTS // V7X_MXU_DIM):
            sio = lax.broadcasted_iota(I32, (V7X_MXU_DIM, TM), 0) + V7X_MXU_DIM * c
            p = jnp.where(sio == slots[0], 1.0, 0.0)
            for k in range(1, TOP_K):
                p = jnp.where(sio == slots[k], 1.0, p)
            r0 = SLOTS * sb + V7X_MXU_DIM * c
            xs_ref[r0:r0 + V7X_MXU_DIM, :] = _dot(p.astype(BF16), hb_t)

        oct_ref[...] = jnp.where(lane == SUB * tile + sb, seg_b, oct_ref[...])

    slot_full = [jnp.concatenate(slot_rows[k], axis=1) for k in range(TOP_K)]
    tok_ref[...] = jnp.concatenate(slot_full + wts, axis=0).T


def _mix(x, ys_tm, ya, gates, wps, wpa, wo, g2, wrt, br, sut, lt):
    B, L, D = x.shape
    nl = L // TL_PROJ
    n_tok = B * L
    full = lambda shape: pl.BlockSpec(shape, lambda b, l: (0,) * len(shape))
    return pl.pallas_call(
        _mix_kernel,
        grid=(B, nl),
        in_specs=[
            pl.BlockSpec((None, TL_PROJ, D), lambda b, l: (b, l, 0)),
            pl.BlockSpec((None, TL_PROJ, 512), lambda b, l: (b, l, 0)),
            pl.BlockSpec((None, TL_PROJ, 512), lambda b, l: (b, l, 0)),
            pl.BlockSpec((None, TL_PROJ, 2 * D), lambda b, l: (b, l, 0)),
            full(wps.shape), full(wpa.shape), full(wo.shape), full((1, D)),
            full(wrt.shape), full(br.shape), full(sut.shape), full(lt.shape),
        ],
        out_specs=[
            pl.BlockSpec((None, TL_PROJ, D), lambda b, l: (b, l, 0)),
            pl.BlockSpec((SUB * SLOTS, D), lambda b, l: (b * nl + l, 0)),
            pl.BlockSpec((TL_PROJ, 2 * TOP_K), lambda b, l: (b * nl + l, 0)),
            full((N_EXPERTS, V7X_LANES)),
        ],
        out_shape=[
            jax.ShapeDtypeStruct((B, L, D), F32),
            jax.ShapeDtypeStruct((B * nl * SUB * SLOTS, D), F32),
            jax.ShapeDtypeStruct((n_tok, 2 * TOP_K), F32),
            jax.ShapeDtypeStruct((N_EXPERTS, V7X_LANES), F32),
        ],
        compiler_params=_params(("arbitrary", "arbitrary")),
        name="mix_router",
    )(x, ys_tm, ya, gates, wps, wpa, wo, g2, wrt, br, sut, lt)


def _moe_tables(oct_tab, n_t, n_blocks):
    E = N_EXPERTS
    O = oct_tab[:, :n_t].astype(I32)
    lstart = jnp.cumsum(O, axis=0) - O
    ecum = jnp.cumsum(O, axis=1) - O
    tot = jnp.sum(O, axis=1)
    nb = (tot + BLK_OCT - 1) // BLK_OCT
    bend = jnp.cumsum(nb)
    bstart = bend - nb
    n_valid = bend[-1]
    bi = jnp.arange(n_blocks, dtype=I32)
    er = jnp.arange(E, dtype=I32)
    tr = jnp.arange(n_t, dtype=I32)
    be = jnp.minimum(jnp.sum((bi[:, None] >= bend[None, :]).astype(I32), axis=1), E - 1)
    last_e = jnp.sum(jnp.where(bi == jnp.maximum(n_valid - 1, 0), be, 0))
    be = jnp.where(bi < n_valid, be, last_e)
    oh_e = be[:, None] == er[None, :]
    pick = lambda tab: jnp.sum(jnp.where(oh_e[:, :, None], tab[None, :, :], 0), axis=1)
    ecum_i, o_i, lst_i = pick(ecum), pick(O), pick(lstart)
    tot_i = jnp.sum(jnp.where(oh_e, tot[None, :], 0), axis=1)
    bst_i = jnp.sum(jnp.where(oh_e, bstart[None, :], 0), axis=1)
    g = (bi - bst_i)[:, None] * BLK_OCT + jnp.arange(BLK_OCT, dtype=I32)[None, :]
    tau = jnp.minimum(jnp.sum((g[:, :, None] >= (ecum_i + o_i)[:, None, :]).astype(I32), axis=2), n_t - 1)
    off = tr[None, :] * SL_OCT + lst_i - ecum_i
    src = jnp.sum(jnp.where(tau[:, :, None] == tr[None, None, :], off[:, None, :], 0), axis=2) + g
    src = jnp.where((g < tot_i[:, None]) & (bi[:, None] < n_valid), src, 0)
    s = jnp.arange(SL_OCT, dtype=I32)
    lend_t = (lstart + O).T
    e_s = jnp.minimum(jnp.sum((s[None, :, None] >= lend_t[:, None, :]).astype(I32), axis=2), E - 1)
    offc = (bstart[:, None] * BLK_OCT + ecum - lstart).T
    csrc = jnp.sum(jnp.where(e_s[:, :, None] == er[None, None, :], offc[:, None, :], 0), axis=2) + s[None, :]
    csrc = jnp.where(s[None, :] < jnp.sum(O, axis=0)[:, None], csrc, 0)
    prev_e = jnp.concatenate([jnp.full((1,), -1, I32), be[:-1]])
    first = ((be != prev_e) & (bi < n_valid)).astype(I32)
    par = (jnp.cumsum(first) - 1) % 2
    later = (er[None, :] > er[:, None]) & (nb[None, :] > 0)
    nxt_of_e = jnp.min(jnp.where(later, er[None, :], E), axis=1)
    nxt_of_e = jnp.where(nxt_of_e < E, nxt_of_e, -1)
    nxt = jnp.sum(jnp.where(oh_e, nxt_of_e[None, :], 0), axis=1)
    rows_i = jnp.clip(tot_i * OCT - (bi - bst_i) * MOE_ROWS, 0, MOE_ROWS)
    var = jnp.zeros_like(bi)
    for vi, m in enumerate(MOE_ROW_VARIANTS):
        var = jnp.where(rows_i <= m, vi, var)
    meta = (be, first, par.astype(I32), nxt.astype(I32), var.astype(I32), n_valid.astype(I32).reshape(1))
    return meta, src, csrc


def _gather_octets(src_ref, n_oct, src_hbm, dst, sem, first=0):
    for q in range(first, first + n_oct):
        row = pl.multiple_of(src_ref[0, q] * OCT, OCT)
        pltpu.make_async_copy(src_hbm.at[pl.ds(row, OCT), :], dst.at[pl.ds(OCT * q, OCT), :], sem).start()


def _expert_kernel(be_ref, first_ref, par_ref, nxt_ref, var_ref, nv_ref, src_cur_ref, src_nxt_ref, xs_hbm,
                   wup_hbm, bup_ref, wdn_hbm, bdn_ref, perm_ref, y_ref,
                   xbuf, wup_st, wdn_st, wup_bf, wdn_bf, act_ref, sem, wsem):
    i = pl.program_id(0)
    nv = nv_ref[0]
    slot = lax.rem(i, 2)

    def weight_copies(e, s):
        return (pltpu.make_async_copy(wup_hbm.at[e], wup_st.at[s], wsem.at[0, s]),
                pltpu.make_async_copy(wdn_hbm.at[e], wdn_st.at[s], wsem.at[1, s]))

    def gather_rows(src_ref, variant, s):
        for vi, m in enumerate(MOE_ROW_VARIANTS):
            @pl.when(variant == vi)
            def _():
                _gather_octets(src_ref, m // OCT, xs_hbm, xbuf.at[s], sem.at[s])

    @pl.when(i == 0)
    def _():
        gather_rows(src_cur_ref, var_ref[0], 0)
        for cp in weight_copies(be_ref[0], par_ref[0]):
            cp.start()

    @pl.when(i + 1 < nv)
    def _():
        gather_rows(src_nxt_ref, var_ref[jnp.minimum(i + 1, pl.num_programs(0) - 1)], 1 - slot)

    def block(m):
        pltpu.make_async_copy(xs_hbm.at[pl.ds(0, m), :], xbuf.at[slot, pl.ds(0, m), :], sem.at[slot]).wait()
        x = xbuf[slot, 0:m, :].astype(BF16)
        for cb in range(8):
            hp = _dot(x, wup_bf[:, 256 * cb:256 * (cb + 1)]) + bup_ref[:, 256 * cb:256 * (cb + 1)]
            glu = jnp.minimum(hp[:, 0:128], SWIGLU_LIMIT)
            lin = jnp.clip(hp[:, 128:256], -SWIGLU_LIMIT, SWIGLU_LIMIT)
            act = glu * jax.nn.sigmoid(SWIGLU_ALPHA * glu) * (lin + 1.0)
            act_ref[0:m, 128 * cb:128 * (cb + 1)] = act.astype(BF16)
        y_ref[0:m, :] = _dot(act_ref[0:m, :], wdn_bf[...]) + bdn_ref[...]
        if m < MOE_ROWS:
            y_ref[m:MOE_ROWS, :] = jnp.zeros((MOE_ROWS - m, y_ref.shape[1]), F32)

    @pl.when(i < nv)
    def _():
        @pl.when(first_ref[i] == 1)
        def _():
            s = par_ref[i]
            for cp in weight_copies(be_ref[i], s):
                cp.wait()

            @pl.when(nxt_ref[i] >= 0)
            def _():
                for cp in weight_copies(nxt_ref[i], 1 - s):
                    cp.start()

            for cb in range(8):
                blk = wup_st[s, :, 256 * cb:256 * (cb + 1)].astype(BF16)
                wup_bf[:, 256 * cb:256 * (cb + 1)] = _dot(blk, perm_ref[...]).astype(BF16)
            wdn_bf[...] = wdn_st[s].astype(BF16)

        for vi, m in enumerate(MOE_ROW_VARIANTS):
            @pl.when(var_ref[i] == vi)
            def _():
                block(m)

    @pl.when(i >= nv)
    def _():
        y_ref[...] = jnp.zeros_like(y_ref)


def _experts(meta, src3, xs, w_up, b_up_p, w_down, b_down, perm):
    n_blocks = src3.shape[0]
    DE2 = w_up.shape[2]
    D = w_up.shape[1]
    tab = src3.shape[2]
    be_map = lambda i, be, *_: (be[i], 0, 0)
    grid_spec = pltpu.PrefetchScalarGridSpec(
        num_scalar_prefetch=len(meta),
        grid=(n_blocks,),
        in_specs=[
            pl.BlockSpec((None, 1, tab), lambda i, *_: (i, 0, 0), memory_space=pltpu.SMEM),
            pl.BlockSpec((None, 1, tab), lambda i, *_: (jnp.minimum(i + 1, n_blocks - 1), 0, 0),
                         memory_space=pltpu.SMEM),
            pl.BlockSpec(memory_space=pl.ANY),
            pl.BlockSpec(memory_space=pl.ANY),
            pl.BlockSpec((None, 1, DE2), be_map),
            pl.BlockSpec(memory_space=pl.ANY),
            pl.BlockSpec((None, 1, D), be_map),
            pl.BlockSpec((256, 256), lambda i, *_: (0, 0)),
        ],
        out_specs=pl.BlockSpec((MOE_ROWS, D), lambda i, *_: (i, 0)),
        scratch_shapes=[
            pltpu.VMEM((2, MOE_ROWS, D), F32),
            pltpu.VMEM((2, D, DE2), F32),
            pltpu.VMEM((2, DE2 // 2, D), F32),
            pltpu.VMEM((D, DE2), BF16),
            pltpu.VMEM((DE2 // 2, D), BF16),
            pltpu.VMEM((MOE_ROWS, DE2 // 2), BF16),
            pltpu.SemaphoreType.DMA((2,)),
            pltpu.SemaphoreType.DMA((2, 2)),
        ],
    )
    return pl.pallas_call(
        _expert_kernel,
        grid_spec=grid_spec,
        out_shape=jax.ShapeDtypeStruct((n_blocks * MOE_ROWS, D), F32),
        compiler_params=_params(("arbitrary",)),
        name="experts",
    )(*meta, src3, src3, xs, w_up, b_up_p, w_down, b_down, perm)


def _combine_kernel(src_cur_ref, src_nxt_ref, h_ref, tok_ref, ys_hbm, o_ref,
                    ybuf, sl_b, w_b, sem):
    i = pl.program_id(0)
    n = pl.num_programs(0)
    slot = lax.rem(i, 2)

    @pl.when(i == 0)
    def _():
        _gather_octets(src_cur_ref, SUB * SL_OCT, ys_hbm, ybuf.at[0], sem.at[0])

    @pl.when(i + 1 < n)
    def _():
        _gather_octets(src_nxt_ref, SUB * SL_OCT, ys_hbm, ybuf.at[1 - slot], sem.at[1 - slot])

    tl = h_ref.shape[0]
    for k in range(TOP_K):
        sl_b[k] = jnp.broadcast_to(tok_ref[:, k:k + 1], (tl, V7X_LANES))
        w_b[k] = jnp.broadcast_to(tok_ref[:, TOP_K + k:TOP_K + k + 1], (tl, V7X_LANES))
    lane = lax.broadcasted_iota(I32, (tl, V7X_LANES), 1).astype(F32)
    pltpu.make_async_copy(ys_hbm.at[pl.ds(0, SUB * SLOTS), :], ybuf.at[slot], sem.at[slot]).wait()
    lane_t = lax.broadcasted_iota(I32, (TM, V7X_LANES), 1).astype(F32)
    for sb in range(SUB):
        t0 = TM * sb
        acc = h_ref[t0:t0 + TM, :]
        for c in range(SLOTS // V7X_MXU_DIM):
            halves = []
            for hc in range(V7X_MXU_DIM // V7X_LANES):
                sio = lane_t + float(V7X_MXU_DIM * c + V7X_LANES * hc)
                wm = jnp.where(sio == sl_b[0, t0:t0 + TM], w_b[0, t0:t0 + TM], 0.0)
                for k in range(1, TOP_K):
                    wm = jnp.where(sio == sl_b[k, t0:t0 + TM], w_b[k, t0:t0 + TM], wm)
                halves.append(wm.astype(BF16))
            r0 = SLOTS * sb + V7X_MXU_DIM * c
            acc = acc + _dot(jnp.concatenate(halves, axis=1), ybuf[slot, r0:r0 + V7X_MXU_DIM, :].astype(BF16))
        o_ref[t0:t0 + TM, :] = acc


def _combine(csrc3, h2, tok_tab, ys):
    n_tok, D = h2.shape
    n_steps = n_tok // TL_PROJ
    tab = csrc3.shape[2]
    return pl.pallas_call(
        _combine_kernel,
        grid=(n_steps,),
        in_specs=[
            pl.BlockSpec((None, 1, tab), lambda i: (i, 0, 0), memory_space=pltpu.SMEM),
            pl.BlockSpec((None, 1, tab), lambda i: (jnp.minimum(i + 1, n_steps - 1), 0, 0),
                         memory_space=pltpu.SMEM),
            pl.BlockSpec((TL_PROJ, D), lambda i: (i, 0)),
            pl.BlockSpec((TL_PROJ, 2 * TOP_K), lambda i: (i, 0)),
            pl.BlockSpec(memory_space=pl.ANY),
        ],
        out_specs=pl.BlockSpec((TL_PROJ, D), lambda i: (i, 0)),
        out_shape=jax.ShapeDtypeStruct((n_tok, D), F32),
        scratch_shapes=[pltpu.VMEM((2, SUB * SLOTS, D), F32),
                        pltpu.VMEM((TOP_K, TL_PROJ, V7X_LANES), F32),
                        pltpu.VMEM((TOP_K, TL_PROJ, V7X_LANES), F32),
                        pltpu.SemaphoreType.DMA((2,))],
        compiler_params=_params(("arbitrary",)),
        name="combine",
    )(csrc3, csrc3, h2, tok_tab, ys)


def _s5_matrices(a_re, a_im, log_dt, b_re, b_im, c_re, c_im):
    G, P = a_re.shape
    dt = jnp.exp(log_dt.astype(F32))[:, None]
    lr = a_re.astype(F32)
    li = a_im.astype(F32)
    mag = jnp.exp(lr * dt)
    abar_r = mag * jnp.cos(li * dt)
    abar_i = mag * jnp.sin(li * dt)
    den = lr * lr + li * li
    nr = abar_r - 1.0
    ni = abar_i
    coef_r = (nr * lr + ni * li) / den
    coef_i = (ni * lr - nr * li) / den
    br_ = b_re.astype(F32)
    bi_ = b_im.astype(F32)
    bbar_r = coef_r[..., None] * br_ - coef_i[..., None] * bi_
    bbar_i = coef_r[..., None] * bi_ + coef_i[..., None] * br_

    gp = G // S5_PARTS

    def block_diag(t):
        n, g, a, b = t.shape
        rep = (jnp.arange(b)[:, None] == (jnp.arange(g * b) % b)[None, :]).astype(F32)
        tiled = jnp.einsum('xrb,bc->xrc', t.reshape(n, g * a, b), rep, precision=lax.Precision.HIGHEST)
        keep = (jnp.arange(g * a) // a)[:, None] == (jnp.arange(g * b) // b)[None, :]
        return jnp.where(keep[None], tiled, 0.0)

    def in_mat(bb):
        return block_diag(jnp.swapaxes(bb.reshape(S5_PARTS, gp, P, SSM_GROUP), 2, 3))

    def out_mat(cc):
        return block_diag(jnp.swapaxes(cc.reshape(S5_PARTS, gp, SSM_GROUP, P), 2, 3))

    bmat = jnp.concatenate([in_mat(bbar_r), in_mat(bbar_i)], axis=-1).astype(BF16)
    cmat = jnp.concatenate([out_mat(c_re.astype(F32)), -out_mat(c_im.astype(F32))], axis=1).astype(BF16)
    return bmat, abar_r.reshape(1, G * P), abar_i.reshape(1, G * P), cmat


def _pad_lanes(t):
    n, w = t.shape
    wp = ((w + V7X_LANES - 1) // V7X_LANES) * V7X_LANES
    return jnp.pad(t, ((0, 0), (0, wp - w))).reshape(n, 1, wp)


def kernel(x, norm1_g, w_in, b_gate, ssm_a_re, ssm_a_im, ssm_log_dt, ssm_b_re, ssm_b_im, ssm_c_re, ssm_c_im, ssm_d, ssm_w_glu, ssm_b_glu, q_norm_g, k_norm_g, lambda_q1, lambda_k1, lambda_q2, lambda_k2, subln_g, w_proj_ssm, w_proj_att, w_out, norm2_g, w_router, b_router, w_up, b_up, w_down, b_down):
    B, L, D = x.shape
    assert B == V7X_SUBLANES and D == 1024 and L % TL_PROJ == 0 and w_in.shape[0] == 1
    n_tok = B * L
    n_t = n_tok // TM
    assert n_t <= V7X_LANES
    l = 0

    scale = ATT_SUB_DIM ** -0.5
    qg = (jnp.tile(q_norm_g[l].astype(F32), 2 * ATT_HEADS) * (scale * math.log2(math.e))).reshape(1, 512)
    kg = jnp.tile(k_norm_g[l].astype(F32), 2 * ATT_HEADS).reshape(1, 512)
    blk = jnp.arange(256) // ATT_SUB_DIM
    ones_bd = (blk[:, None] == blk[None, :]).astype(BF16)
    u_tm, q, k, vt, gates = _inproj(x, norm1_g[l].reshape(1, D), w_in[l],
                                    b_gate[l].reshape(1, 2 * D), qg, kg, ones_bd)

    bmat, ar, ai, cmat = _s5_matrices(ssm_a_re[l], ssm_a_im[l], ssm_log_dt[l], ssm_b_re[l], ssm_b_im[l],
                                      ssm_c_re[l], ssm_c_im[l])
    y_ssm = _s5(u_tm, bmat, ar, ai, cmat, ssm_d[l].reshape(1, 512).astype(F32),
                ssm_w_glu[l].astype(BF16), ssm_b_glu[l].reshape(1, 512).astype(F32))

    lam4 = jnp.stack([lambda_q1[l], lambda_k1[l], lambda_q2[l], lambda_k2[l]]).astype(F32)
    sg = jnp.tile(subln_g[l].astype(F32), ATT_HEADS).reshape(1, 512)
    y_att = _attn(q, k, vt, lam4, sg)

    tpos = jnp.arange(TM)
    sut = (tpos[:, None] < tpos[None, :]).astype(BF16)
    epos = jnp.arange(N_EXPERTS)
    lt = (epos[None, :] < epos[:, None]).astype(BF16)
    h, xs, tok_tab, oct_tab = _mix(
        x, y_ssm, y_att, gates,
        w_proj_ssm[l].astype(BF16), w_proj_att[l].astype(BF16), w_out[l].astype(BF16),
        norm2_g[l].reshape(1, D), w_router[l].T.astype(BF16), b_router[l].reshape(N_EXPERTS, 1).astype(F32),
        sut, lt)

    n_blocks = (n_tok * TOP_K + n_t * N_EXPERTS * (OCT - 1) + MOE_ROWS - 1) // MOE_ROWS + N_EXPERTS
    meta, src, csrc = _moe_tables(oct_tab, n_t, n_blocks)

    de2 = w_up.shape[-1]
    b_up_p = b_up[l].reshape(N_EXPERTS, de2 // 256, 128, 2).transpose(0, 1, 3, 2).reshape(N_EXPERTS, 1, de2)
    pr = jnp.arange(256)
    col_src = jnp.where(pr < 128, 2 * pr, 2 * (pr - 128) + 1)
    perm = (jnp.arange(256)[:, None] == col_src[None, :]).astype(BF16)
    ys = _experts(meta, _pad_lanes(src), xs,
                  w_up[l], b_up_p, w_down[l], b_down[l].reshape(N_EXPERTS, 1, D), perm)

    out = _combine(_pad_lanes(csrc.reshape(n_t // SUB, SUB * SL_OCT)), h.reshape(n_tok, D), tok_tab, ys)
    return out.reshape(B, L, D)
```

```python
import math

import jax
import jax.numpy as jnp
from jax import lax
from jax.experimental import pallas as pl
from jax.experimental.pallas import tpu as pltpu

F32 = jnp.float32
BF16 = jnp.bfloat16
I32 = jnp.int32

RMS_EPS = 1e-6
CHUNK = 64
SSM_GROUP = 16
SSM_STATE = 64
ATT_HEADS = 4
ATT_SUB_DIM = 64
ATT_V_DIM = 128
N_EXPERTS = 32
TOP_K = 4
SWIGLU_ALPHA = 1.702
SWIGLU_LIMIT = 7.0
LAMBDA_INIT = 0.8 - 0.6 * math.exp(-0.3 * 0)

V7X_SUBLANES = 8
V7X_LANES = 128
V7X_MXU_DIM = 256

TL_IN = 1024
TL_PROJ = 512
TM = 512
SUB = TL_PROJ // TM
S5_STEPS = 128
S5_PARTS = 4
TQ = 256
TK = 256
MOE_ROWS = 1024
MOE_ROW_VARIANTS = (1024, 768, 512, 384, 256, 128)
OCT = V7X_SUBLANES
BLK_OCT = MOE_ROWS // OCT
SLOTS = ((TM * TOP_K + N_EXPERTS * (OCT - 1) + V7X_MXU_DIM - 1) // V7X_MXU_DIM) * V7X_MXU_DIM
SL_OCT = SLOTS // OCT
NEG = -1e30
VMEM_LIMIT = 56 * 1024 * 1024


def _dot(a, b):
    return jnp.dot(a, b, preferred_element_type=F32)


def _dot_nt(a, b):
    return lax.dot_general(a, b, (((1,), (1,)), ((), ())), preferred_element_type=F32)


def _params(sem, vmem=VMEM_LIMIT):
    return pltpu.CompilerParams(dimension_semantics=sem, vmem_limit_bytes=vmem)


def _inproj_kernel(x_ref, g1_ref, wf_ref, bg_ref, qg_ref, kg_ref, ones_ref,
                   u_ref, q_ref, k_ref, vt_ref, gate_ref, w_ref):
    @pl.when(jnp.logical_and(pl.program_id(0) == 0, pl.program_id(1) == 0))
    def _():
        for c in range(wf_ref.shape[1] // 512):
            w_ref[:, 512 * c:512 * (c + 1)] = wf_ref[:, 512 * c:512 * (c + 1)].astype(BF16)

    x = x_ref[...]
    ms = jnp.mean(x * x, axis=-1, keepdims=True)
    xn = (x * lax.rsqrt(ms + RMS_EPS) * g1_ref[...]).astype(BF16)

    u_ref[...] = _dot(xn, w_ref[:, 0:512])

    def head_norm(z, g):
        sq = (z * z).astype(BF16)
        ss = jnp.concatenate([_dot(sq[:, 0:256], ones_ref[...]),
                              _dot(sq[:, 256:512], ones_ref[...])], axis=1)
        return z * lax.rsqrt(ss * (1.0 / ATT_SUB_DIM) + RMS_EPS) * g

    q_ref[...] = head_norm(_dot(xn, w_ref[:, 512:1024]), qg_ref[...]).astype(BF16)
    k_ref[...] = head_norm(_dot(xn, w_ref[:, 1024:1536]), kg_ref[...]).astype(BF16)
    vt_ref[...] = _dot(xn, w_ref[:, 1536:2048]).T.astype(BF16)
    for c in range(4):
        lo = 2048 + 512 * c
        z = _dot(xn, w_ref[:, lo:lo + 512]) + bg_ref[:, 512 * c:512 * (c + 1)]
        gate_ref[:, 512 * c:512 * (c + 1)] = jax.nn.sigmoid(z).astype(BF16)


def _inproj(x, g1, w_in, b_gate, qg, kg, ones_bd):
    B, L, D = x.shape
    nl = L // TL_IN
    full = lambda shape: pl.BlockSpec(shape, lambda b, l: (0,) * len(shape))
    return pl.pallas_call(
        _inproj_kernel,
        grid=(B, nl),
        in_specs=[
            pl.BlockSpec((None, TL_IN, D), lambda b, l: (b, l, 0)),
            full((1, D)),
            pl.BlockSpec(w_in.shape, lambda b, l: (0, 0), pipeline_mode=pl.Buffered(1)),
            full((1, 2 * D)),
            full((1, 512)), full((1, 512)), full((256, 256)),
        ],
        out_specs=[
            pl.BlockSpec((None, TL_IN, 512), lambda b, l: (b, l, 0)),
            pl.BlockSpec((None, TL_IN, 512), lambda b, l: (b, l, 0)),
            pl.BlockSpec((None, TL_IN, 512), lambda b, l: (b, l, 0)),
            pl.BlockSpec((None, 512, TL_IN), lambda b, l: (b, 0, l)),
            pl.BlockSpec((None, TL_IN, 2 * D), lambda b, l: (b, l, 0)),
        ],
        out_shape=[
            jax.ShapeDtypeStruct((B, L, 512), F32),
            jax.ShapeDtypeStruct((B, L, 512), BF16),
            jax.ShapeDtypeStruct((B, L, 512), BF16),
            jax.ShapeDtypeStruct((B, 512, L), BF16),
            jax.ShapeDtypeStruct((B, L, 2 * D), BF16),
        ],
        scratch_shapes=[pltpu.VMEM(w_in.shape, BF16)],
        compiler_params=_params(("arbitrary", "arbitrary")),
        name="inproj",
    )(x, g1, w_in, b_gate, qg, kg, ones_bd)


def _s5_kernel(u_hbm, bm_ref, ar_ref, ai_ref, cm_ref, d_ref, wg_ref, bgl_ref,
               y_hbm, ubuf, ybuf, hs_ref, h_ref, usem, ysem):
    i = pl.program_id(0)
    n = pl.num_programs(0)
    slot = lax.rem(i, 2)
    nb = ubuf.shape[2]

    def u_copies(step, s):
        t0 = pl.multiple_of(step * S5_STEPS, S5_STEPS)
        return [pltpu.make_async_copy(u_hbm.at[b, pl.ds(t0, S5_STEPS), :], ubuf.at[s, :, b, :], usem.at[s])
                for b in range(nb)]

    def y_copies(step, s):
        t0 = pl.multiple_of(step * S5_STEPS, S5_STEPS)
        return [pltpu.make_async_copy(ybuf.at[s, :, b, :], y_hbm.at[b, pl.ds(t0, S5_STEPS), :], ysem.at[s])
                for b in range(nb)]

    @pl.when(i == 0)
    def _():
        h_ref[...] = jnp.zeros_like(h_ref)
        for cp in u_copies(0, 0):
            cp.start()

    @pl.when(i + 1 < n)
    def _():
        for cp in u_copies(i + 1, 1 - slot):
            cp.start()

    for cp in u_copies(i, slot):
        cp.wait()
    uf = ubuf[slot].reshape(S5_STEPS * nb, ubuf.shape[3])
    u = uf.astype(BF16)
    n_part = bm_ref.shape[0]
    cw = u.shape[1] // n_part
    sw = bm_ref.shape[2] // 2
    for pt in range(n_part):
        hs_ref[:, 2 * sw * pt:2 * sw * (pt + 1)] = _dot(u[:, cw * pt:cw * (pt + 1)], bm_ref[pt])

    ys = []
    for pt in range(n_part):
        re0 = 2 * sw * pt
        im0 = re0 + sw
        ar = jnp.broadcast_to(ar_ref[:, sw * pt:sw * (pt + 1)], (V7X_SUBLANES, sw))
        ai = jnp.broadcast_to(ai_ref[:, sw * pt:sw * (pt + 1)], (V7X_SUBLANES, sw))
        hr = h_ref[:, re0:re0 + sw]
        hi = h_ref[:, im0:im0 + sw]
        for t in range(S5_STEPS):
            r0 = t * V7X_SUBLANES
            xr = hs_ref[r0:r0 + V7X_SUBLANES, re0:re0 + sw]
            xi = hs_ref[r0:r0 + V7X_SUBLANES, im0:im0 + sw]
            hr, hi = ar * hr - ai * hi + xr, ar * hi + ai * hr + xi
            hs_ref[r0:r0 + V7X_SUBLANES, re0:re0 + sw] = hr
            hs_ref[r0:r0 + V7X_SUBLANES, im0:im0 + sw] = hi
        h_ref[:, re0:re0 + sw] = hr
        h_ref[:, im0:im0 + sw] = hi
        ys.append(_dot(hs_ref[:, re0:re0 + 2 * sw].astype(BF16), cm_ref[pt]))

    y = jnp.concatenate(ys, axis=1)
    y = y + d_ref[...] * uf
    y = 0.5 * y * (1.0 + jnp.tanh(math.sqrt(2.0 / math.pi) * (y + 0.044715 * (y * y * y))))
    z = _dot(y.astype(BF16), wg_ref[...]) + bgl_ref[...]

    @pl.when(i >= 2)
    def _():
        for cp in y_copies(i - 2, slot):
            cp.wait()

    ybuf[slot] = (y * jax.nn.sigmoid(z)).reshape(S5_STEPS, nb, ybuf.shape[3])
    for cp in y_copies(i, slot):
        cp.start()

    @pl.when(i == n - 1)
    def _():
        for cp in y_copies(i, slot):
            cp.wait()

    @pl.when(jnp.logical_and(i == n - 1, i >= 1))
    def _():
        for cp in y_copies(i - 1, 1 - slot):
            cp.wait()


def _s5(u, bmat, ar, ai, cmat, d_skip, w_glu_bf, b_glu):
    B, L, C = u.shape
    R = S5_STEPS * B
    full = lambda shape: pl.BlockSpec(shape, lambda i: (0,) * len(shape))
    return pl.pallas_call(
        _s5_kernel,
        grid=(L // S5_STEPS,),
        in_specs=[
            pl.BlockSpec(memory_space=pl.ANY),
            full(bmat.shape), full(ar.shape), full(ai.shape), full(cmat.shape),
            full(d_skip.shape), full(w_glu_bf.shape), full(b_glu.shape),
        ],
        out_specs=pl.BlockSpec(memory_space=pl.ANY),
        out_shape=jax.ShapeDtypeStruct((B, L, C), F32),
        scratch_shapes=[pltpu.VMEM((2, S5_STEPS, B, C), F32), pltpu.VMEM((2, S5_STEPS, B, C), F32),
                        pltpu.VMEM((R, 4096), F32), pltpu.VMEM((B, 4096), F32),
                        pltpu.SemaphoreType.DMA((2,)), pltpu.SemaphoreType.DMA((2,))],
        compiler_params=_params(("arbitrary",)),
        name="s5",
    )(u, bmat, ar, ai, cmat, d_skip, w_glu_bf, b_glu)


def _attn_kernel(q_ref, k_ref, vt_ref, lam_ref, sg_ref, o_ref, qm_ref, acc_ref, ml_ref, s_ref):
    i = pl.program_id(1)
    n_chain = 2 * ATT_HEADS
    lamv = lam_ref[...]
    lam = (jnp.exp(jnp.sum(lamv[0:1] * lamv[1:2], axis=-1, keepdims=True))
           - jnp.exp(jnp.sum(lamv[2:3] * lamv[3:4], axis=-1, keepdims=True)) + LAMBDA_INIT)
    lane = lax.broadcasted_iota(I32, (TQ, 128), 1)
    key_chunk = lax.broadcasted_iota(I32, (TK, TQ), 0) // CHUNK
    qry_chunk = lax.broadcasted_iota(I32, (TK, TQ), 1) // CHUNK
    diag_ok = key_chunk <= qry_chunk

    for hd in range(ATT_HEADS):
        qh = q_ref[:, 128 * hd:128 * (hd + 1)]
        zero = jnp.zeros_like(qh)
        qm_ref[2 * hd] = jnp.where(lane < ATT_SUB_DIM, qh, zero)
        qm_ref[2 * hd + 1] = jnp.where(lane >= ATT_SUB_DIM, qh, zero)
    acc_ref[...] = jnp.zeros_like(acc_ref)
    row = lax.broadcasted_iota(I32, (2 * n_chain, TQ), 0)
    ml_ref[...] = jnp.where(row % 2 == 0, NEG, 0.0)

    def scores(j, par):
        off = pl.multiple_of(j * TK, TK)
        for hd in range(ATT_HEADS):
            ks = k_ref[pl.ds(off, TK), 128 * hd:128 * (hd + 1)]
            for s in range(2):
                s_ref[par, 2 * hd + s] = _dot_nt(ks, qm_ref[2 * hd + s])

    def consume(j, par, masked):
        off = pl.multiple_of(j * TK, TK)
        ps, alphas = [], []
        for c in range(n_chain):
            st = s_ref[par, c]
            if masked:
                st = jnp.where(diag_ok, st, NEG)
            m = ml_ref[2 * c:2 * c + 1, :]
            l = ml_ref[2 * c + 1:2 * c + 2, :]
            mn = jnp.maximum(m, jnp.max(st, axis=0, keepdims=True))
            alpha = jnp.exp2(m - mn)
            p = jnp.exp2(st - mn)
            ml_ref[2 * c:2 * c + 1, :] = mn
            ml_ref[2 * c + 1:2 * c + 2, :] = alpha * l + jnp.sum(p, axis=0, keepdims=True)
            ps.append(p.astype(BF16))
            alphas.append(alpha)
        for c in range(n_chain):
            hd = c // 2
            vts = vt_ref[128 * hd:128 * (hd + 1), pl.ds(off, TK)]
            acc_ref[c] = acc_ref[c] * alphas[c] + _dot(vts, ps[c])

    scores(0, 0)

    def body(jj, carry):
        j = 2 * jj
        scores(j + 1, 1)
        consume(j, 0, False)
        scores(j + 2, 0)
        consume(j + 1, 1, False)
        return carry

    lax.fori_loop(0, i // 2, body, 0)

    @pl.when(i % 2 == 1)
    def _():
        scores(i, 1)
        consume(i - 1, 0, False)
        consume(i, 1, True)

    @pl.when(i % 2 == 0)
    def _():
        consume(i, 0, True)

    for hd in range(ATT_HEADS):
        c0 = 128 * hd
        l1 = ml_ref[4 * hd + 1:4 * hd + 2, :]
        l2 = ml_ref[4 * hd + 3:4 * hd + 4, :]
        o = acc_ref[2 * hd] / l1 - lam * (acc_ref[2 * hd + 1] / l2)
        ms = jnp.mean(o * o, axis=0, keepdims=True)
        on = (o * lax.rsqrt(ms + RMS_EPS)).T
        o_ref[:, c0:c0 + 128] = (on * sg_ref[:, c0:c0 + 128] * (1.0 - LAMBDA_INIT)).astype(BF16)


def _attn(q, k, vt, lam4, sg):
    B, L, W = q.shape
    n_chain = 2 * ATT_HEADS
    return pl.pallas_call(
        _attn_kernel,
        grid=(B, L // TQ),
        in_specs=[
            pl.BlockSpec((None, TQ, W), lambda b, i: (b, i, 0)),
            pl.BlockSpec((None, L, W), lambda b, i: (b, 0, 0)),
            pl.BlockSpec((None, W, L), lambda b, i: (b, 0, 0)),
            pl.BlockSpec((4, ATT_SUB_DIM), lambda b, i: (0, 0)),
            pl.BlockSpec((1, W), lambda b, i: (0, 0)),
        ],
        out_specs=pl.BlockSpec((None, TQ, W), lambda b, i: (b, i, 0)),
        out_shape=jax.ShapeDtypeStruct((B, L, W), BF16),
        scratch_shapes=[pltpu.VMEM((n_chain, TQ, 128), BF16),
                        pltpu.VMEM((n_chain, ATT_V_DIM, TQ), F32),
                        pltpu.VMEM((2 * n_chain, TQ), F32),
                        pltpu.VMEM((2, n_chain, TK, TQ), F32)],
        compiler_params=_params(("arbitrary", "arbitrary")),
        name="diff_attn",
    )(q, k, vt, lam4, sg)


def _mix_kernel(x_ref, ys_ref, ya_ref, gate_ref, wps_ref, wpa_ref, wo_ref, g2_ref, wrt_ref, br_ref,
                sut_ref, lt_ref, h_ref, xs_ref, tok_ref, oct_ref):
    tile = pl.program_id(0) * pl.num_programs(1) + pl.program_id(1)

    D = x_ref.shape[-1]
    tl = x_ref.shape[0]
    gs = gate_ref[:, 0:D].astype(F32)
    ga = gate_ref[:, D:2 * D].astype(F32)
    mixed = gs * _dot(ys_ref[...].astype(BF16), wps_ref[...]) + ga * _dot(ya_ref[...], wpa_ref[...])
    h = x_ref[...] + _dot(mixed.astype(BF16), wo_ref[...])
    h_ref[...] = h
    ms = jnp.mean(h * h, axis=-1, keepdims=True)
    hb = (h * lax.rsqrt(ms + RMS_EPS) * g2_ref[...]).astype(BF16)

    lg = _dot_nt(wrt_ref[...], hb) + br_ref[...]
    eio = lax.broadcasted_iota(I32, (N_EXPERTS, tl), 0)
    vals, idxs = [], []
    for k in range(TOP_K):
        m = jnp.max(lg, axis=0, keepdims=True)
        idx = jnp.min(jnp.where(lg == m, eio, N_EXPERTS), axis=0, keepdims=True)
        vals.append(m)
        idxs.append(idx)
        lg = jnp.where(eio == idx, -jnp.inf, lg)
    ex = [jnp.exp(v - vals[0]) for v in vals]
    den = ex[0] + ex[1] + ex[2] + ex[3]
    wts = [e / den for e in ex]

    lane = lax.broadcasted_iota(I32, (N_EXPERTS, V7X_LANES), 1)

    @pl.when(tile == 0)
    def _():
        oct_ref[...] = jnp.zeros_like(oct_ref)

    slot_rows = [[] for _ in range(TOP_K)]
    for sb in range(SUB):
        t0 = TM * sb
        eio_t = lax.broadcasted_iota(I32, (N_EXPERTS, TM), 0)
        sel_t = [eio_t == idxs[k][:, t0:t0 + TM] for k in range(TOP_K)]
        mtot = jnp.zeros((N_EXPERTS, TM), F32)
        for k in range(TOP_K):
            mtot = mtot + jnp.where(sel_t[k], 1.0, 0.0)
        pre = _dot(mtot.astype(BF16), sut_ref[...])
        cnt = jnp.sum(mtot, axis=1, keepdims=True)
        seg_oct = jnp.floor((cnt + (OCT - 1.0)) * (1.0 / OCT))
        seg_b = jnp.broadcast_to(seg_oct, (N_EXPERTS, V7X_LANES))
        start_oct = _dot(lt_ref[...], seg_b.astype(BF16))
        base = pre + start_oct[:, 0:1] * float(OCT)
        slots_f = [jnp.sum(jnp.where(sel_t[k], base, 0.0), axis=0, keepdims=True) for k in range(TOP_K)]
        slots = [s.astype(I32) for s in slots_f]
        for k in range(TOP_K):
            slot_rows[k].append(slots_f[k])

        hb_t = hb[t0:t0 + TM, :]
        for c in range(SLOTS // V7X_MXU_DIM):
            sio = lax.broadcasted_iota(I32, (V7X_MXU_DIM, TM), 0) + V7X_MXU_DIM * c
            p = jnp.where(sio == slots[0], 1.0, 0.0)
            for k in range(1, TOP_K):
                p = jnp.where(sio == slots[k], 1.0, p)
            r0 = SLOTS * sb + V7X_MXU_DIM * c
            xs_ref[r0:r0 + V7X_MXU_DIM, :] = _dot(p.astype(BF16), hb_t)

        oct_ref[...] = jnp.where(lane == SUB * tile + sb, seg_b, oct_ref[...])

    slot_full = [jnp.concatenate(slot_rows[k], axis=1) for k in range(TOP_K)]
    tok_ref[...] = jnp.concatenate(slot_full + wts, axis=0).T


def _mix(x, ys_tm, ya, gates, wps, wpa, wo, g2, wrt, br, sut, lt):
    B, L, D = x.shape
    nl = L // TL_PROJ
    n_tok = B * L
    full = lambda shape: pl.BlockSpec(shape, lambda b, l: (0,) * len(shape))
    return pl.pallas_call(
        _mix_kernel,
        grid=(B, nl),
        in_specs=[
            pl.BlockSpec((None, TL_PROJ, D), lambda b, l: (b, l, 0)),
            pl.BlockSpec((None, TL_PROJ, 512), lambda b, l: (b, l, 0)),
            pl.BlockSpec((None, TL_PROJ, 512), lambda b, l: (b, l, 0)),
            pl.BlockSpec((None, TL_PROJ, 2 * D), lambda b, l: (b, l, 0)),
            full(wps.shape), full(wpa.shape), full(wo.shape), full((1, D)),
            full(wrt.shape), full(br.shape), full(sut.shape), full(lt.shape),
        ],
        out_specs=[
            pl.BlockSpec((None, TL_PROJ, D), lambda b, l: (b, l, 0)),
            pl.BlockSpec((SUB * SLOTS, D), lambda b, l: (b * nl + l, 0)),
            pl.BlockSpec((TL_PROJ, 2 * TOP_K), lambda b, l: (b * nl + l, 0)),
            full((N_EXPERTS, V7X_LANES)),
        ],
        out_shape=[
            jax.ShapeDtypeStruct((B, L, D), F32),
            jax.ShapeDtypeStruct((B * nl * SUB * SLOTS, D), F32),
            jax.ShapeDtypeStruct((n_tok, 2 * TOP_K), F32),
            jax.ShapeDtypeStruct((N_EXPERTS, V7X_LANES), F32),
        ],
        compiler_params=_params(("arbitrary", "arbitrary")),
        name="mix_router",
    )(x, ys_tm, ya, gates, wps, wpa, wo, g2, wrt, br, sut, lt)


def _moe_tables(oct_tab, n_t, n_blocks):
    E = N_EXPERTS
    O = oct_tab[:, :n_t].astype(I32)
    lstart = jnp.cumsum(O, axis=0) - O
    ecum = jnp.cumsum(O, axis=1) - O
    tot = jnp.sum(O, axis=1)
    nb = (tot + BLK_OCT - 1) // BLK_OCT
    bend = jnp.cumsum(nb)
    bstart = bend - nb
    n_valid = bend[-1]
    bi = jnp.arange(n_blocks, dtype=I32)
    er = jnp.arange(E, dtype=I32)
    tr = jnp.arange(n_t, dtype=I32)
    be = jnp.minimum(jnp.sum((bi[:, None] >= bend[None, :]).astype(I32), axis=1), E - 1)
    last_e = jnp.sum(jnp.where(bi == jnp.maximum(n_valid - 1, 0), be, 0))
    be = jnp.where(bi < n_valid, be, last_e)
    oh_e = be[:, None] == er[None, :]
    pick = lambda tab: jnp.sum(jnp.where(oh_e[:, :, None], tab[None, :, :], 0), axis=1)
    ecum_i, o_i, lst_i = pick(ecum), pick(O), pick(lstart)
    tot_i = jnp.sum(jnp.where(oh_e, tot[None, :], 0), axis=1)
    bst_i = jnp.sum(jnp.where(oh_e, bstart[None, :], 0), axis=1)
    g = (bi - bst_i)[:, None] * BLK_OCT + jnp.arange(BLK_OCT, dtype=I32)[None, :]
    tau = jnp.minimum(jnp.sum((g[:, :, None] >= (ecum_i + o_i)[:, None, :]).astype(I32), axis=2), n_t - 1)
    off = tr[None, :] * SL_OCT + lst_i - ecum_i
    src = jnp.sum(jnp.where(tau[:, :, None] == tr[None, None, :], off[:, None, :], 0), axis=2) + g
    src = jnp.where((g < tot_i[:, None]) & (bi[:, None] < n_valid), src, 0)
    s = jnp.arange(SL_OCT, dtype=I32)
    lend_t = (lstart + O).T
    e_s = jnp.minimum(jnp.sum((s[None, :, None] >= lend_t[:, None, :]).astype(I32), axis=2), E - 1)
    offc = (bstart[:, None] * BLK_OCT + ecum - lstart).T
    csrc = jnp.sum(jnp.where(e_s[:, :, None] == er[None, None, :], offc[:, None, :], 0), axis=2) + s[None, :]
    csrc = jnp.where(s[None, :] < jnp.sum(O, axis=0)[:, None], csrc, 0)
    prev_e = jnp.concatenate([jnp.full((1,), -1, I32), be[:-1]])
    first = ((be != prev_e) & (bi < n_valid)).astype(I32)
    par = (jnp.cumsum(first) - 1) % 2
    later = (er[None, :] > er[:, None]) & (nb[None, :] > 0)
    nxt_of_e = jnp.min(jnp.where(later, er[None, :], E), axis=1)
    nxt_of_e = jnp.where(nxt_of_e < E, nxt_of_e, -1)
    nxt = jnp.sum(jnp.where(oh_e, nxt_of_e[None, :], 0), axis=1)
    rows_i = jnp.clip(tot_i * OCT - (bi - bst_i) * MOE_ROWS, 0, MOE_ROWS)
    var = jnp.zeros_like(bi)
    for vi, m in enumerate(MOE_ROW_VARIANTS):
        var = jnp.where(rows_i <= m, vi, var)
    meta = (be, first, par.astype(I32), nxt.astype(I32), var.astype(I32), n_valid.astype(I32).reshape(1))
    return meta, src, csrc


def _gather_octets(src_ref, n_oct, src_hbm, dst, sem, first=0, priorities=(0, 1)):
    for q in range(first, first + n_oct):
        row = pl.multiple_of(src_ref[0, q] * OCT, OCT)
        pltpu.make_async_copy(src_hbm.at[pl.ds(row, OCT), :], dst.at[pl.ds(OCT * q, OCT), :], sem).start(
            priority=priorities[q % len(priorities)])


def _expert_kernel(be_ref, first_ref, par_ref, nxt_ref, var_ref, nv_ref, src_cur_ref, src_nxt_ref, xs_hbm,
                   wup_hbm, bup_ref, wdn_hbm, bdn_ref, perm_ref, y_ref,
                   xbuf, wup_st, wdn_st, wup_bf, wdn_bf, act_ref, sem, wsem):
    i = pl.program_id(0)
    nv = nv_ref[0]
    slot = lax.rem(i, 2)

    def weight_copies(e, s):
        return (pltpu.make_async_copy(wup_hbm.at[e], wup_st.at[s], wsem.at[0, s]),
                pltpu.make_async_copy(wdn_hbm.at[e], wdn_st.at[s], wsem.at[1, s]))

    def gather_rows(src_ref, variant, s):
        for vi, m in enumerate(MOE_ROW_VARIANTS):
            @pl.when(variant == vi)
            def _():
                _gather_octets(src_ref, m // OCT, xs_hbm, xbuf.at[s], sem.at[s], priorities=(0,))

    @pl.when(i == 0)
    def _():
        gather_rows(src_cur_ref, var_ref[0], 0)
        for cp in weight_copies(be_ref[0], par_ref[0]):
            cp.start(priority=1)

    @pl.when(i + 1 < nv)
    def _():
        gather_rows(src_nxt_ref, var_ref[jnp.minimum(i + 1, pl.num_programs(0) - 1)], 1 - slot)

    def block(m):
        pltpu.make_async_copy(xs_hbm.at[pl.ds(0, m), :], xbuf.at[slot, pl.ds(0, m), :], sem.at[slot]).wait()
        x = xbuf[slot, 0:m, :].astype(BF16)
        for cb in range(8):
            hp = _dot(x, wup_bf[:, 256 * cb:256 * (cb + 1)]) + bup_ref[:, 256 * cb:256 * (cb + 1)]
            glu = jnp.minimum(hp[:, 0:128], SWIGLU_LIMIT)
            lin = jnp.clip(hp[:, 128:256], -SWIGLU_LIMIT, SWIGLU_LIMIT)
            act = glu * jax.nn.sigmoid(SWIGLU_ALPHA * glu) * (lin + 1.0)
            act_ref[0:m, 128 * cb:128 * (cb + 1)] = act.astype(BF16)
        y_ref[0:m, :] = _dot(act_ref[0:m, :], wdn_bf[...]) + bdn_ref[...]
        if m < MOE_ROWS:
            y_ref[m:MOE_ROWS, :] = jnp.zeros((MOE_ROWS - m, y_ref.shape[1]), F32)

    @pl.when(i < nv)
    def _():
        @pl.when(first_ref[i] == 1)
        def _():
            s = par_ref[i]
            for cp in weight_copies(be_ref[i], s):
                cp.wait()

            @pl.when(nxt_ref[i] >= 0)
            def _():
                for cp in weight_copies(nxt_ref[i], 1 - s):
                    cp.start(priority=1)

            for cb in range(8):
                blk = wup_st[s, :, 256 * cb:256 * (cb + 1)].astype(BF16)
                wup_bf[:, 256 * cb:256 * (cb + 1)] = _dot(blk, perm_ref[...]).astype(BF16)
            wdn_bf[...] = wdn_st[s].astype(BF16)

        for vi, m in enumerate(MOE_ROW_VARIANTS):
            @pl.when(var_ref[i] == vi)
            def _():
                block(m)

    @pl.when(i >= nv)
    def _():
        y_ref[...] = jnp.zeros_like(y_ref)


def _experts(meta, src3, xs, w_up, b_up_p, w_down, b_down, perm):
    n_blocks = src3.shape[0]
    DE2 = w_up.shape[2]
    D = w_up.shape[1]
    tab = src3.shape[2]
    be_map = lambda i, be, *_: (be[i], 0, 0)
    grid_spec = pltpu.PrefetchScalarGridSpec(
        num_scalar_prefetch=len(meta),
        grid=(n_blocks,),
        in_specs=[
            pl.BlockSpec((None, 1, tab), lambda i, *_: (i, 0, 0), memory_space=pltpu.SMEM),
            pl.BlockSpec((None, 1, tab), lambda i, *_: (jnp.minimum(i + 1, n_blocks - 1), 0, 0),
                         memory_space=pltpu.SMEM),
            pl.BlockSpec(memory_space=pl.ANY),
            pl.BlockSpec(memory_space=pl.ANY),
            pl.BlockSpec((None, 1, DE2), be_map),
            pl.BlockSpec(memory_space=pl.ANY),
            pl.BlockSpec((None, 1, D), be_map),
            pl.BlockSpec((256, 256), lambda i, *_: (0, 0)),
        ],
        out_specs=pl.BlockSpec((MOE_ROWS, D), lambda i, *_: (i, 0)),
        scratch_shapes=[
            pltpu.VMEM((2, MOE_ROWS, D), F32),
            pltpu.VMEM((2, D, DE2), F32),
            pltpu.VMEM((2, DE2 // 2, D), F32),
            pltpu.VMEM((D, DE2), BF16),
            pltpu.VMEM((DE2 // 2, D), BF16),
            pltpu.VMEM((MOE_ROWS, DE2 // 2), BF16),
            pltpu.SemaphoreType.DMA((2,)),
            pltpu.SemaphoreType.DMA((2, 2)),
        ],
    )
    return pl.pallas_call(
        _expert_kernel,
        grid_spec=grid_spec,
        out_shape=jax.ShapeDtypeStruct((n_blocks * MOE_ROWS, D), F32),
        compiler_params=_params(("arbitrary",)),
        name="experts",
    )(*meta, src3, src3, xs, w_up, b_up_p, w_down, b_down, perm)


def _combine_kernel(src_cur_ref, src_nxt_ref, h_ref, tok_ref, ys_hbm, o_ref,
                    ybuf, sl_b, w_b, sem):
    i = pl.program_id(0)
    n = pl.num_programs(0)
    slot = lax.rem(i, 2)

    @pl.when(i == 0)
    def _():
        _gather_octets(src_cur_ref, SUB * SL_OCT, ys_hbm, ybuf.at[0], sem.at[0])

    @pl.when(i + 1 < n)
    def _():
        _gather_octets(src_nxt_ref, SUB * SL_OCT, ys_hbm, ybuf.at[1 - slot], sem.at[1 - slot])

    tl = h_ref.shape[0]
    for k in range(TOP_K):
        sl_b[k] = jnp.broadcast_to(tok_ref[:, k:k + 1], (tl, V7X_LANES))
        w_b[k] = jnp.broadcast_to(tok_ref[:, TOP_K + k:TOP_K + k + 1], (tl, V7X_LANES))
    lane = lax.broadcasted_iota(I32, (tl, V7X_LANES), 1).astype(F32)
    pltpu.make_async_copy(ys_hbm.at[pl.ds(0, SUB * SLOTS), :], ybuf.at[slot], sem.at[slot]).wait()
    lane_t = lax.broadcasted_iota(I32, (TM, V7X_LANES), 1).astype(F32)
    for sb in range(SUB):
        t0 = TM * sb
        acc = h_ref[t0:t0 + TM, :]
        for c in range(SLOTS // V7X_MXU_DIM):
            halves = []
            for hc in range(V7X_MXU_DIM // V7X_LANES):
                sio = lane_t + float(V7X_MXU_DIM * c + V7X_LANES * hc)
                wm = jnp.where(sio == sl_b[0, t0:t0 + TM], w_b[0, t0:t0 + TM], 0.0)
                for k in range(1, TOP_K):
                    wm = jnp.where(sio == sl_b[k, t0:t0 + TM], w_b[k, t0:t0 + TM], wm)
                halves.append(wm.astype(BF16))
            r0 = SLOTS * sb + V7X_MXU_DIM * c
            acc = acc + _dot(jnp.concatenate(halves, axis=1), ybuf[slot, r0:r0 + V7X_MXU_DIM, :].astype(BF16))
        o_ref[t0:t0 + TM, :] = acc


def _combine(csrc3, h2, tok_tab, ys):
    n_tok, D = h2.shape
    n_steps = n_tok // TL_PROJ
    tab = csrc3.shape[2]
    return pl.pallas_call(
        _combine_kernel,
        grid=(n_steps,),
        in_specs=[
            pl.BlockSpec((None, 1, tab), lambda i: (i, 0, 0), memory_space=pltpu.SMEM),
            pl.BlockSpec((None, 1, tab), lambda i: (jnp.minimum(i + 1, n_steps - 1), 0, 0),
                         memory_space=pltpu.SMEM),
            pl.BlockSpec((TL_PROJ, D), lambda i: (i, 0)),
            pl.BlockSpec((TL_PROJ, 2 * TOP_K), lambda i: (i, 0)),
            pl.BlockSpec(memory_space=pl.ANY),
        ],
        out_specs=pl.BlockSpec((TL_PROJ, D), lambda i: (i, 0)),
        out_shape=jax.ShapeDtypeStruct((n_tok, D), F32),
        scratch_shapes=[pltpu.VMEM((2, SUB * SLOTS, D), F32),
                        pltpu.VMEM((TOP_K, TL_PROJ, V7X_LANES), F32),
                        pltpu.VMEM((TOP_K, TL_PROJ, V7X_LANES), F32),
                        pltpu.SemaphoreType.DMA((2,))],
        compiler_params=_params(("arbitrary",)),
        name="combine",
    )(csrc3, csrc3, h2, tok_tab, ys)


def _s5_matrices(a_re, a_im, log_dt, b_re, b_im, c_re, c_im):
    G, P = a_re.shape
    dt = jnp.exp(log_dt.astype(F32))[:, None]
    lr = a_re.astype(F32)
    li = a_im.astype(F32)
    mag = jnp.exp(lr * dt)
    abar_r = mag * jnp.cos(li * dt)
    abar_i = mag * jnp.sin(li * dt)
    den = lr * lr + li * li
    nr = abar_r - 1.0
    ni = abar_i
    coef_r = (nr * lr + ni * li) / den
    coef_i = (ni * lr - nr * li) / den
    br_ = b_re.astype(F32)
    bi_ = b_im.astype(F32)
    bbar_r = coef_r[..., None] * br_ - coef_i[..., None] * bi_
    bbar_i = coef_r[..., None] * bi_ + coef_i[..., None] * br_

    gp = G // S5_PARTS

    def block_diag(t):
        n, g, a, b = t.shape
        rep = (jnp.arange(b)[:, None] == (jnp.arange(g * b) % b)[None, :]).astype(F32)
        tiled = jnp.einsum('xrb,bc->xrc', t.reshape(n, g * a, b), rep, precision=lax.Precision.HIGHEST)
        keep = (jnp.arange(g * a) // a)[:, None] == (jnp.arange(g * b) // b)[None, :]
        return jnp.where(keep[None], tiled, 0.0)

    def in_mat(bb):
        return block_diag(jnp.swapaxes(bb.reshape(S5_PARTS, gp, P, SSM_GROUP), 2, 3))

    def out_mat(cc):
        return block_diag(jnp.swapaxes(cc.reshape(S5_PARTS, gp, SSM_GROUP, P), 2, 3))

    bmat = jnp.concatenate([in_mat(bbar_r), in_mat(bbar_i)], axis=-1).astype(BF16)
    cmat = jnp.concatenate([out_mat(c_re.astype(F32)), -out_mat(c_im.astype(F32))], axis=1).astype(BF16)
    return bmat, abar_r.reshape(1, G * P), abar_i.reshape(1, G * P), cmat


def _pad_lanes(t):
    n, w = t.shape
    wp = ((w + V7X_LANES - 1) // V7X_LANES) * V7X_LANES
    return jnp.pad(t, ((0, 0), (0, wp - w))).reshape(n, 1, wp)


def kernel(x, norm1_g, w_in, b_gate, ssm_a_re, ssm_a_im, ssm_log_dt, ssm_b_re, ssm_b_im, ssm_c_re, ssm_c_im, ssm_d, ssm_w_glu, ssm_b_glu, q_norm_g, k_norm_g, lambda_q1, lambda_k1, lambda_q2, lambda_k2, subln_g, w_proj_ssm, w_proj_att, w_out, norm2_g, w_router, b_router, w_up, b_up, w_down, b_down):
    B, L, D = x.shape
    assert B == V7X_SUBLANES and D == 1024 and L % TL_PROJ == 0 and w_in.shape[0] == 1
    n_tok = B * L
    n_t = n_tok // TM
    assert n_t <= V7X_LANES
    l = 0

    scale = ATT_SUB_DIM ** -0.5
    qg = (jnp.tile(q_norm_g[l].astype(F32), 2 * ATT_HEADS) * (scale * math.log2(math.e))).reshape(1, 512)
    kg = jnp.tile(k_norm_g[l].astype(F32), 2 * ATT_HEADS).reshape(1, 512)
    blk = jnp.arange(256) // ATT_SUB_DIM
    ones_bd = (blk[:, None] == blk[None, :]).astype(BF16)
    u_tm, q, k, vt, gates = _inproj(x, norm1_g[l].reshape(1, D), w_in[l],
                                    b_gate[l].reshape(1, 2 * D), qg, kg, ones_bd)

    bmat, ar, ai, cmat = _s5_matrices(ssm_a_re[l], ssm_a_im[l], ssm_log_dt[l], ssm_b_re[l], ssm_b_im[l],
                                      ssm_c_re[l], ssm_c_im[l])
    y_ssm = _s5(u_tm, bmat, ar, ai, cmat, ssm_d[l].reshape(1, 512).astype(F32),
                ssm_w_glu[l].astype(BF16), ssm_b_glu[l].reshape(1, 512).astype(F32))

    lam4 = jnp.stack([lambda_q1[l], lambda_k1[l], lambda_q2[l], lambda_k2[l]]).astype(F32)
    sg = jnp.tile(subln_g[l].astype(F32), ATT_HEADS).reshape(1, 512)
    y_att = _attn(q, k, vt, lam4, sg)

    tpos = jnp.arange(TM)
    sut = (tpos[:, None] < tpos[None, :]).astype(BF16)
    epos = jnp.arange(N_EXPERTS)
    lt = (epos[None, :] < epos[:, None]).astype(BF16)
    h, xs, tok_tab, oct_tab = _mix(
        x, y_ssm, y_att, gates,
        w_proj_ssm[l].astype(BF16), w_proj_att[l].astype(BF16), w_out[l].astype(BF16),
        norm2_g[l].reshape(1, D), w_router[l].T.astype(BF16), b_router[l].reshape(N_EXPERTS, 1).astype(F32),
        sut, lt)

    n_blocks = (n_tok * TOP_K + n_t * N_EXPERTS * (OCT - 1) + MOE_ROWS - 1) // MOE_ROWS + N_EXPERTS
    meta, src, csrc = _moe_tables(oct_tab, n_t, n_blocks)

    de2 = w_up.shape[-1]
    b_up_p = b_up[l].reshape(N_EXPERTS, de2 // 256, 128, 2).transpose(0, 1, 3, 2).reshape(N_EXPERTS, 1, de2)
    pr = jnp.arange(256)
    col_src = jnp.where(pr < 128, 2 * pr, 2 * (pr - 128) + 1)
    perm = (jnp.arange(256)[:, None] == col_src[None, :]).astype(BF16)
    ys = _experts(meta, _pad_lanes(src), xs,
                  w_up[l], b_up_p, w_down[l], b_down[l].reshape(N_EXPERTS, 1, D), perm)

    out = _combine(_pad_lanes(csrc.reshape(n_t // SUB, SUB * SL_OCT)), h.reshape(n_tok, D), tok_tab, ys)
    return out.reshape(B, L, D)
```

```python
import math

import jax
import jax.numpy as jnp
from jax import lax
from jax.experimental import pallas as pl
from jax.experimental.pallas import tpu as pltpu

F32 = jnp.float32
BF16 = jnp.bfloat16
I32 = jnp.int32

RMS_EPS = 1e-6
CHUNK = 64
SSM_GROUP = 16
SSM_STATE = 64
ATT_HEADS = 4
ATT_SUB_DIM = 64
ATT_V_DIM = 128
N_EXPERTS = 32
TOP_K = 4
SWIGLU_ALPHA = 1.702
SWIGLU_LIMIT = 7.0
LAMBDA_INIT = 0.8 - 0.6 * math.exp(-0.3 * 0)

V7X_SUBLANES = 8
V7X_LANES = 128
V7X_MXU_DIM = 256

TL_IN = 1024
TL_PROJ = 512
TM = 512
SUB = TL_PROJ // TM
S5_STEPS = 128
S5_PARTS = 4
TQ = 256
TK = 256
MOE_ROWS = 1024
MOE_ROW_VARIANTS = (1024, 768, 512, 384, 256, 128)
OCT = V7X_SUBLANES
BLK_OCT = MOE_ROWS // OCT
SLOTS = ((TM * TOP_K + N_EXPERTS * (OCT - 1) + V7X_MXU_DIM - 1) // V7X_MXU_DIM) * V7X_MXU_DIM
SL_OCT = SLOTS // OCT
NEG = -1e30
VMEM_LIMIT = 56 * 1024 * 1024


def _dot(a, b):
    return jnp.dot(a, b, preferred_element_type=F32)


def _dot_nt(a, b):
    return lax.dot_general(a, b, (((1,), (1,)), ((), ())), preferred_element_type=F32)


def _params(sem, vmem=VMEM_LIMIT):
    return pltpu.CompilerParams(dimension_semantics=sem, vmem_limit_bytes=vmem)


def _inproj_kernel(x_ref, g1_ref, wf_ref, bg_ref, qg_ref, kg_ref, ones_ref,
                   u_ref, q_ref, k_ref, vt_ref, gate_ref, w_ref):
    @pl.when(jnp.logical_and(pl.program_id(0) == 0, pl.program_id(1) == 0))
    def _():
        for c in range(wf_ref.shape[1] // 512):
            w_ref[:, 512 * c:512 * (c + 1)] = wf_ref[:, 512 * c:512 * (c + 1)].astype(BF16)

    x = x_ref[...]
    ms = jnp.mean(x * x, axis=-1, keepdims=True)
    xn = (x * lax.rsqrt(ms + RMS_EPS) * g1_ref[...]).astype(BF16)

    u_ref[...] = _dot(xn, w_ref[:, 0:512])

    def head_norm(z, g):
        sq = (z * z).astype(BF16)
        ss = jnp.concatenate([_dot(sq[:, 0:256], ones_ref[...]),
                              _dot(sq[:, 256:512], ones_ref[...])], axis=1)
        return z * lax.rsqrt(ss * (1.0 / ATT_SUB_DIM) + RMS_EPS) * g

    q_ref[...] = head_norm(_dot(xn, w_ref[:, 512:1024]), qg_ref[...]).astype(BF16)
    k_ref[...] = head_norm(_dot(xn, w_ref[:, 1024:1536]), kg_ref[...]).astype(BF16)
    vt_ref[...] = _dot(xn, w_ref[:, 1536:2048]).T.astype(BF16)
    for c in range(4):
        lo = 2048 + 512 * c
        z = _dot(xn, w_ref[:, lo:lo + 512]) + bg_ref[:, 512 * c:512 * (c + 1)]
        gate_ref[:, 512 * c:512 * (c + 1)] = jax.nn.sigmoid(z).astype(BF16)


def _inproj(x, g1, w_in, b_gate, qg, kg, ones_bd):
    B, L, D = x.shape
    nl = L // TL_IN
    full = lambda shape: pl.BlockSpec(shape, lambda b, l: (0,) * len(shape))
    return pl.pallas_call(
        _inproj_kernel,
        grid=(B, nl),
        in_specs=[
            pl.BlockSpec((None, TL_IN, D), lambda b, l: (b, l, 0)),
            full((1, D)),
            pl.BlockSpec(w_in.shape, lambda b, l: (0, 0), pipeline_mode=pl.Buffered(1)),
            full((1, 2 * D)),
            full((1, 512)), full((1, 512)), full((256, 256)),
        ],
        out_specs=[
            pl.BlockSpec((None, TL_IN, 512), lambda b, l: (b, l, 0)),
            pl.BlockSpec((None, TL_IN, 512), lambda b, l: (b, l, 0)),
            pl.BlockSpec((None, TL_IN, 512), lambda b, l: (b, l, 0)),
            pl.BlockSpec((None, 512, TL_IN), lambda b, l: (b, 0, l)),
            pl.BlockSpec((None, TL_IN, 2 * D), lambda b, l: (b, l, 0)),
        ],
        out_shape=[
            jax.ShapeDtypeStruct((B, L, 512), F32),
            jax.ShapeDtypeStruct((B, L, 512), BF16),
            jax.ShapeDtypeStruct((B, L, 512), BF16),
            jax.ShapeDtypeStruct((B, 512, L), BF16),
            jax.ShapeDtypeStruct((B, L, 2 * D), BF16),
        ],
        scratch_shapes=[pltpu.VMEM(w_in.shape, BF16)],
        compiler_params=_params(("arbitrary", "arbitrary")),
        name="inproj",
    )(x, g1, w_in, b_gate, qg, kg, ones_bd)


def _s5_kernel(u_hbm, bm_ref, ar_ref, ai_ref, cm_ref, d_ref, wg_ref, bgl_ref,
               y_hbm, ubuf, ybuf, hs_ref, h_ref, usem, ysem):
    i = pl.program_id(0)
    n = pl.num_programs(0)
    slot = lax.rem(i, 2)
    nb = ubuf.shape[2]

    def u_copies(step, s):
        t0 = pl.multiple_of(step * S5_STEPS, S5_STEPS)
        return [pltpu.make_async_copy(u_hbm.at[b, pl.ds(t0, S5_STEPS), :], ubuf.at[s, :, b, :], usem.at[s])
                for b in range(nb)]

    def y_copies(step, s):
        t0 = pl.multiple_of(step * S5_STEPS, S5_STEPS)
        return [pltpu.make_async_copy(ybuf.at[s, :, b, :], y_hbm.at[b, pl.ds(t0, S5_STEPS), :], ysem.at[s])
                for b in range(nb)]

    @pl.when(i == 0)
    def _():
        h_ref[...] = jnp.zeros_like(h_ref)
        for cp in u_copies(0, 0):
            cp.start()

    @pl.when(i + 1 < n)
    def _():
        for cp in u_copies(i + 1, 1 - slot):
            cp.start()

    for cp in u_copies(i, slot):
        cp.wait()
    uf = ubuf[slot].reshape(S5_STEPS * nb, ubuf.shape[3])
    u = uf.astype(BF16)
    n_part = bm_ref.shape[0]
    cw = u.shape[1] // n_part
    sw = bm_ref.shape[2] // 2
    for pt in range(n_part):
        hs_ref[:, 2 * sw * pt:2 * sw * (pt + 1)] = _dot(u[:, cw * pt:cw * (pt + 1)], bm_ref[pt])

    ys = []
    for pt in range(n_part):
        re0 = 2 * sw * pt
        im0 = re0 + sw
        ar = jnp.broadcast_to(ar_ref[:, sw * pt:sw * (pt + 1)], (V7X_SUBLANES, sw))
        ai = jnp.broadcast_to(ai_ref[:, sw * pt:sw * (pt + 1)], (V7X_SUBLANES, sw))
        hr = h_ref[:, re0:re0 + sw]
        hi = h_ref[:, im0:im0 + sw]
        for t in range(S5_STEPS):
            r0 = t * V7X_SUBLANES
            xr = hs_ref[r0:r0 + V7X_SUBLANES, re0:re0 + sw]
            xi = hs_ref[r0:r0 + V7X_SUBLANES, im0:im0 + sw]
            hr, hi = ar * hr - ai * hi + xr, ar * hi + ai * hr + xi
            hs_ref[r0:r0 + V7X_SUBLANES, re0:re0 + sw] = hr
            hs_ref[r0:r0 + V7X_SUBLANES, im0:im0 + sw] = hi
        h_ref[:, re0:re0 + sw] = hr
        h_ref[:, im0:im0 + sw] = hi
        ys.append(_dot(hs_ref[:, re0:re0 + 2 * sw].astype(BF16), cm_ref[pt]))

    y = jnp.concatenate(ys, axis=1)
    y = y + d_ref[...] * uf
    y = 0.5 * y * (1.0 + jnp.tanh(math.sqrt(2.0 / math.pi) * (y + 0.044715 * (y * y * y))))
    z = _dot(y.astype(BF16), wg_ref[...]) + bgl_ref[...]

    @pl.when(i >= 2)
    def _():
        for cp in y_copies(i - 2, slot):
            cp.wait()

    ybuf[slot] = (y * jax.nn.sigmoid(z)).reshape(S5_STEPS, nb, ybuf.shape[3])
    for cp in y_copies(i, slot):
        cp.start()

    @pl.when(i == n - 1)
    def _():
        for cp in y_copies(i, slot):
            cp.wait()

    @pl.when(jnp.logical_and(i == n - 1, i >= 1))
    def _():
        for cp in y_copies(i - 1, 1 - slot):
            cp.wait()


def _s5(u, bmat, ar, ai, cmat, d_skip, w_glu_bf, b_glu):
    B, L, C = u.shape
    R = S5_STEPS * B
    full = lambda shape: pl.BlockSpec(shape, lambda i: (0,) * len(shape))
    return pl.pallas_call(
        _s5_kernel,
        grid=(L // S5_STEPS,),
        in_specs=[
            pl.BlockSpec(memory_space=pl.ANY),
            full(bmat.shape), full(ar.shape), full(ai.shape), full(cmat.shape),
            full(d_skip.shape), full(w_glu_bf.shape), full(b_glu.shape),
        ],
        out_specs=pl.BlockSpec(memory_space=pl.ANY),
        out_shape=jax.ShapeDtypeStruct((B, L, C), F32),
        scratch_shapes=[pltpu.VMEM((2, S5_STEPS, B, C), F32), pltpu.VMEM((2, S5_STEPS, B, C), F32),
                        pltpu.VMEM((R, 4096), F32), pltpu.VMEM((B, 4096), F32),
                        pltpu.SemaphoreType.DMA((2,)), pltpu.SemaphoreType.DMA((2,))],
        compiler_params=_params(("arbitrary",)),
        name="s5",
    )(u, bmat, ar, ai, cmat, d_skip, w_glu_bf, b_glu)


def _attn_kernel(q_ref, k_ref, vt_ref, lam_ref, sg_ref, o_ref, qm_ref, acc_ref, ml_ref, s_ref):
    i = pl.program_id(1)
    n_chain = 2 * ATT_HEADS
    lamv = lam_ref[...]
    lam = (jnp.exp(jnp.sum(lamv[0:1] * lamv[1:2], axis=-1, keepdims=True))
           - jnp.exp(jnp.sum(lamv[2:3] * lamv[3:4], axis=-1, keepdims=True)) + LAMBDA_INIT)
    lane = lax.broadcasted_iota(I32, (TQ, 128), 1)
    key_chunk = lax.broadcasted_iota(I32, (TK, TQ), 0) // CHUNK
    qry_chunk = lax.broadcasted_iota(I32, (TK, TQ), 1) // CHUNK
    diag_ok = key_chunk <= qry_chunk

    for hd in range(ATT_HEADS):
        qh = q_ref[:, 128 * hd:128 * (hd + 1)]
        zero = jnp.zeros_like(qh)
        qm_ref[2 * hd] = jnp.where(lane < ATT_SUB_DIM, qh, zero)
        qm_ref[2 * hd + 1] = jnp.where(lane >= ATT_SUB_DIM, qh, zero)
    acc_ref[...] = jnp.zeros_like(acc_ref)
    row = lax.broadcasted_iota(I32, (2 * n_chain, TQ), 0)
    ml_ref[...] = jnp.where(row % 2 == 0, NEG, 0.0)

    def scores(j, par):
        off = pl.multiple_of(j * TK, TK)
        for hd in range(ATT_HEADS):
            ks = k_ref[pl.ds(off, TK), 128 * hd:128 * (hd + 1)]
            for s in range(2):
                s_ref[par, 2 * hd + s] = _dot_nt(ks, qm_ref[2 * hd + s])

    def consume(j, par, masked):
        off = pl.multiple_of(j * TK, TK)
        ps, alphas = [], []
        for c in range(n_chain):
            st = s_ref[par, c]
            if masked:
                st = jnp.where(diag_ok, st, NEG)
            m = ml_ref[2 * c:2 * c + 1, :]
            l = ml_ref[2 * c + 1:2 * c + 2, :]
            mn = jnp.maximum(m, jnp.max(st, axis=0, keepdims=True))
            alpha = jnp.exp2(m - mn)
            p = jnp.exp2(st - mn)
            ml_ref[2 * c:2 * c + 1, :] = mn
            ml_ref[2 * c + 1:2 * c + 2, :] = alpha * l + jnp.sum(p, axis=0, keepdims=True)
            ps.append(p.astype(BF16))
            alphas.append(alpha)
        for c in range(n_chain):
            hd = c // 2
            vts = vt_ref[128 * hd:128 * (hd + 1), pl.ds(off, TK)]
            acc_ref[c] = acc_ref[c] * alphas[c] + _dot(vts, ps[c])

    scores(0, 0)

    def body(jj, carry):
        j = 2 * jj
        scores(j + 1, 1)
        consume(j, 0, False)
        scores(j + 2, 0)
        consume(j + 1, 1, False)
        return carry

    lax.fori_loop(0, i // 2, body, 0)

    @pl.when(i % 2 == 1)
    def _():
        scores(i, 1)
        consume(i - 1, 0, False)
        consume(i, 1, True)

    @pl.when(i % 2 == 0)
    def _():
        consume(i, 0, True)

    for hd in range(ATT_HEADS):
        c0 = 128 * hd
        l1 = ml_ref[4 * hd + 1:4 * hd + 2, :]
        l2 = ml_ref[4 * hd + 3:4 * hd + 4, :]
        o = acc_ref[2 * hd] / l1 - lam * (acc_ref[2 * hd + 1] / l2)
        ms = jnp.mean(o * o, axis=0, keepdims=True)
        on = (o * lax.rsqrt(ms + RMS_EPS)).T
        o_ref[:, c0:c0 + 128] = (on * sg_ref[:, c0:c0 + 128] * (1.0 - LAMBDA_INIT)).astype(BF16)


def _attn(q, k, vt, lam4, sg):
    B, L, W = q.shape
    n_chain = 2 * ATT_HEADS
    return pl.pallas_call(
        _attn_kernel,
        grid=(B, L // TQ),
        in_specs=[
            pl.BlockSpec((None, TQ, W), lambda b, i: (b, i, 0)),
            pl.BlockSpec((None, L, W), lambda b, i: (b, 0, 0)),
            pl.BlockSpec((None, W, L), lambda b, i: (b, 0, 0)),
            pl.BlockSpec((4, ATT_SUB_DIM), lambda b, i: (0, 0)),
            pl.BlockSpec((1, W), lambda b, i: (0, 0)),
        ],
        out_specs=pl.BlockSpec((None, TQ, W), lambda b, i: (b, i, 0)),
        out_shape=jax.ShapeDtypeStruct((B, L, W), BF16),
        scratch_shapes=[pltpu.VMEM((n_chain, TQ, 128), BF16),
                        pltpu.VMEM((n_chain, ATT_V_DIM, TQ), F32),
                        pltpu.VMEM((2 * n_chain, TQ), F32),
                        pltpu.VMEM((2, n_chain, TK, TQ), F32)],
        compiler_params=_params(("arbitrary", "arbitrary")),
        name="diff_attn",
    )(q, k, vt, lam4, sg)


def _mix_kernel(x_ref, ys_ref, ya_ref, gate_ref, wps_ref, wpa_ref, wo_ref, g2_ref, wrt_ref, br_ref,
                sut_ref, lt_ref, h_ref, xs_ref, tok_ref, oct_ref):
    tile = pl.program_id(0) * pl.num_programs(1) + pl.program_id(1)

    D = x_ref.shape[-1]
    tl = x_ref.shape[0]
    gs = gate_ref[:, 0:D].astype(F32)
    ga = gate_ref[:, D:2 * D].astype(F32)
    mixed = gs * _dot(ys_ref[...].astype(BF16), wps_ref[...]) + ga * _dot(ya_ref[...], wpa_ref[...])
    h = x_ref[...] + _dot(mixed.astype(BF16), wo_ref[...])
    h_ref[...] = h
    ms = jnp.mean(h * h, axis=-1, keepdims=True)
    hb = (h * lax.rsqrt(ms + RMS_EPS) * g2_ref[...]).astype(BF16)

    lg = _dot_nt(wrt_ref[...], hb) + br_ref[...]
    eio = lax.broadcasted_iota(I32, (N_EXPERTS, tl), 0)
    vals, idxs = [], []
    for k in range(TOP_K):
        m = jnp.max(lg, axis=0, keepdims=True)
        idx = jnp.min(jnp.where(lg == m, eio, N_EXPERTS), axis=0, keepdims=True)
        vals.append(m)
        idxs.append(idx)
        lg = jnp.where(eio == idx, -jnp.inf, lg)
    ex = [jnp.exp(v - vals[0]) for v in vals]
    den = ex[0] + ex[1] + ex[2] + ex[3]
    wts = [e / den for e in ex]

    lane = lax.broadcasted_iota(I32, (N_EXPERTS, V7X_LANES), 1)

    @pl.when(tile == 0)
    def _():
        oct_ref[...] = jnp.zeros_like(oct_ref)

    slot_rows = [[] for _ in range(TOP_K)]
    for sb in range(SUB):
        t0 = TM * sb
        eio_t = lax.broadcasted_iota(I32, (N_EXPERTS, TM), 0)
        sel_t = [eio_t == idxs[k][:, t0:t0 + TM] for k in range(TOP_K)]
        mtot = jnp.zeros((N_EXPERTS, TM), F32)
        for k in range(TOP_K):
            mtot = mtot + jnp.where(sel_t[k], 1.0, 0.0)
        pre = _dot(mtot.astype(BF16), sut_ref[...])
        cnt = jnp.sum(mtot, axis=1, keepdims=True)
        seg_oct = jnp.floor((cnt + (OCT - 1.0)) * (1.0 / OCT))
        seg_b = jnp.broadcast_to(seg_oct, (N_EXPERTS, V7X_LANES))
        start_oct = _dot(lt_ref[...], seg_b.astype(BF16))
        base = pre + start_oct[:, 0:1] * float(OCT)
        slots_f = [jnp.sum(jnp.where(sel_t[k], base, 0.0), axis=0, keepdims=True) for k in range(TOP_K)]
        slots = [s.astype(I32) for s in slots_f]
        for k in range(TOP_K):
            slot_rows[k].append(slots_f[k])

        hb_t = hb[t0:t0 + TM, :]
        for c in range(SLOTS // V7X_MXU_DIM):
            sio = lax.broadcasted_iota(I32, (V7X_MXU_DIM, TM), 0) + V7X_MXU_DIM * c
            p = jnp.where(sio == slots[0], 1.0, 0.0)
            for k in range(1, TOP_K):
                p = jnp.where(sio == slots[k], 1.0, p)
            r0 = SLOTS * sb + V7X_MXU_DIM * c
            xs_ref[r0:r0 + V7X_MXU_DIM, :] = _dot(p.astype(BF16), hb_t)

        oct_ref[...] = jnp.where(lane == SUB * tile + sb, seg_b, oct_ref[...])

    slot_full = [jnp.concatenate(slot_rows[k], axis=1) for k in range(TOP_K)]
    tok_ref[...] = jnp.concatenate(slot_full + wts, axis=0).T


def _mix(x, ys_tm, ya, gates, wps, wpa, wo, g2, wrt, br, sut, lt):
    B, L, D = x.shape
    nl = L // TL_PROJ
    n_tok = B * L
    full = lambda shape: pl.BlockSpec(shape, lambda b, l: (0,) * len(shape))
    return pl.pallas_call(
        _mix_kernel,
        grid=(B, nl),
        in_specs=[
            pl.BlockSpec((None, TL_PROJ, D), lambda b, l: (b, l, 0)),
            pl.BlockSpec((None, TL_PROJ, 512), lambda b, l: (b, l, 0)),
            pl.BlockSpec((None, TL_PROJ, 512), lambda b, l: (b, l, 0)),
            pl.BlockSpec((None, TL_PROJ, 2 * D), lambda b, l: (b, l, 0)),
            full(wps.shape), full(wpa.shape), full(wo.shape), full((1, D)),
            full(wrt.shape), full(br.shape), full(sut.shape), full(lt.shape),
        ],
        out_specs=[
            pl.BlockSpec((None, TL_PROJ, D), lambda b, l: (b, l, 0)),
            pl.BlockSpec((SUB * SLOTS, D), lambda b, l: (b * nl + l, 0)),
            pl.BlockSpec((TL_PROJ, 2 * TOP_K), lambda b, l: (b * nl + l, 0)),
            full((N_EXPERTS, V7X_LANES)),
        ],
        out_shape=[
            jax.ShapeDtypeStruct((B, L, D), F32),
            jax.ShapeDtypeStruct((B * nl * SUB * SLOTS, D), F32),
            jax.ShapeDtypeStruct((n_tok, 2 * TOP_K), F32),
            jax.ShapeDtypeStruct((N_EXPERTS, V7X_LANES), F32),
        ],
        compiler_params=_params(("arbitrary", "arbitrary")),
        name="mix_router",
    )(x, ys_tm, ya, gates, wps, wpa, wo, g2, wrt, br, sut, lt)


def _moe_tables(oct_tab, n_t, n_blocks):
    E = N_EXPERTS
    O = oct_tab[:, :n_t].astype(I32)
    lstart = jnp.cumsum(O, axis=0) - O
    ecum = jnp.cumsum(O, axis=1) - O
    tot = jnp.sum(O, axis=1)
    nb = (tot + BLK_OCT - 1) // BLK_OCT
    bend = jnp.cumsum(nb)
    bstart = bend - nb
    n_valid = bend[-1]
    bi = jnp.arange(n_blocks, dtype=I32)
    er = jnp.arange(E, dtype=I32)
    tr = jnp.arange(n_t, dtype=I32)
    be = jnp.minimum(jnp.sum((bi[:, None] >= bend[None, :]).astype(I32), axis=1), E - 1)
    last_e = jnp.sum(jnp.where(bi == jnp.maximum(n_valid - 1, 0), be, 0))
    be = jnp.where(bi < n_valid, be, last_e)
    oh_e = be[:, None] == er[None, :]
    pick = lambda tab: jnp.sum(jnp.where(oh_e[:, :, None], tab[None, :, :], 0), axis=1)
    ecum_i, o_i, lst_i = pick(ecum), pick(O), pick(lstart)
    tot_i = jnp.sum(jnp.where(oh_e, tot[None, :], 0), axis=1)
    bst_i = jnp.sum(jnp.where(oh_e, bstart[None, :], 0), axis=1)
    g = (bi - bst_i)[:, None] * BLK_OCT + jnp.arange(BLK_OCT, dtype=I32)[None, :]
    tau = jnp.minimum(jnp.sum((g[:, :, None] >= (ecum_i + o_i)[:, None, :]).astype(I32), axis=2), n_t - 1)
    off = tr[None, :] * SL_OCT + lst_i - ecum_i
    src = jnp.sum(jnp.where(tau[:, :, None] == tr[None, None, :], off[:, None, :], 0), axis=2) + g
    src = jnp.where((g < tot_i[:, None]) & (bi[:, None] < n_valid), src, 0)
    s = jnp.arange(SL_OCT, dtype=I32)
    lend_t = (lstart + O).T
    e_s = jnp.minimum(jnp.sum((s[None, :, None] >= lend_t[:, None, :]).astype(I32), axis=2), E - 1)
    offc = (bstart[:, None] * BLK_OCT + ecum - lstart).T
    csrc = jnp.sum(jnp.where(e_s[:, :, None] == er[None, None, :], offc[:, None, :], 0), axis=2) + s[None, :]
    csrc = jnp.where(s[None, :] < jnp.sum(O, axis=0)[:, None], csrc, 0)
    prev_e = jnp.concatenate([jnp.full((1,), -1, I32), be[:-1]])
    first = ((be != prev_e) & (bi < n_valid)).astype(I32)
    par = (jnp.cumsum(first) - 1) % 2
    later = (er[None, :] > er[:, None]) & (nb[None, :] > 0)
    nxt_of_e = jnp.min(jnp.where(later, er[None, :], E), axis=1)
    nxt_of_e = jnp.where(nxt_of_e < E, nxt_of_e, -1)
    nxt = jnp.sum(jnp.where(oh_e, nxt_of_e[None, :], 0), axis=1)
    rows_i = jnp.clip(tot_i * OCT - (bi - bst_i) * MOE_ROWS, 0, MOE_ROWS)
    var = jnp.zeros_like(bi)
    for vi, m in enumerate(MOE_ROW_VARIANTS):
        var = jnp.where(rows_i <= m, vi, var)
    meta = (be, first, par.astype(I32), nxt.astype(I32), var.astype(I32), n_valid.astype(I32).reshape(1))
    return meta, src, csrc


def _gather_octets(src_ref, n_oct, src_hbm, dst, sem, first=0, priorities=(0, 1)):
    for q in range(first, first + n_oct):
        row = pl.multiple_of(src_ref[0, q] * OCT, OCT)
        pltpu.make_async_copy(src_hbm.at[pl.ds(row, OCT), :], dst.at[pl.ds(OCT * q, OCT), :], sem).start(
            priority=priorities[q % len(priorities)])


def _expert_kernel(be_ref, first_ref, par_ref, nxt_ref, var_ref, nv_ref, src_cur_ref, src_nxt_ref, xs_hbm,
                   wup_hbm, bup_ref, wdn_hbm, bdn_ref, perm_ref, y_ref,
                   xbuf, wup_st, wdn_st, wup_bf, wdn_bf, act_ref, sem, wsem):
    i = pl.program_id(0)
    nv = nv_ref[0]
    slot = lax.rem(i, 2)

    def weight_copies(e, s):
        return (pltpu.make_async_copy(wup_hbm.at[e], wup_st.at[s], wsem.at[0, s]),
                pltpu.make_async_copy(wdn_hbm.at[e], wdn_st.at[s], wsem.at[1, s]))

    def gather_rows(src_ref, variant, s):
        for vi, m in enumerate(MOE_ROW_VARIANTS):
            @pl.when(variant == vi)
            def _():
                _gather_octets(src_ref, m // OCT, xs_hbm, xbuf.at[s], sem.at[s], priorities=(0,))

    @pl.when(i == 0)
    def _():
        gather_rows(src_cur_ref, var_ref[0], 0)
        for cp in weight_copies(be_ref[0], par_ref[0]):
            cp.start()

    @pl.when(i + 1 < nv)
    def _():
        gather_rows(src_nxt_ref, var_ref[jnp.minimum(i + 1, pl.num_programs(0) - 1)], 1 - slot)

    def block(m):
        pltpu.make_async_copy(xs_hbm.at[pl.ds(0, m), :], xbuf.at[slot, pl.ds(0, m), :], sem.at[slot]).wait()
        x = xbuf[slot, 0:m, :].astype(BF16)
        for cb in range(8):
            hp = _dot(x, wup_bf[:, 256 * cb:256 * (cb + 1)]) + bup_ref[:, 256 * cb:256 * (cb + 1)]
            glu = jnp.minimum(hp[:, 0:128], SWIGLU_LIMIT)
            lin = jnp.clip(hp[:, 128:256], -SWIGLU_LIMIT, SWIGLU_LIMIT)
            act = glu * jax.nn.sigmoid(SWIGLU_ALPHA * glu) * (lin + 1.0)
            act_ref[0:m, 128 * cb:128 * (cb + 1)] = act.astype(BF16)
        y_ref[0:m, :] = _dot(act_ref[0:m, :], wdn_bf[...]) + bdn_ref[...]
        if m < MOE_ROWS:
            y_ref[m:MOE_ROWS, :] = jnp.zeros((MOE_ROWS - m, y_ref.shape[1]), F32)

    @pl.when(i < nv)
    def _():
        @pl.when(first_ref[i] == 1)
        def _():
            s = par_ref[i]
            for cp in weight_copies(be_ref[i], s):
                cp.wait()

            @pl.when(nxt_ref[i] >= 0)
            def _():
                for cp in weight_copies(nxt_ref[i], 1 - s):
                    cp.start()

            for cb in range(8):
                blk = wup_st[s, :, 256 * cb:256 * (cb + 1)].astype(BF16)
                wup_bf[:, 256 * cb:256 * (cb + 1)] = _dot(blk, perm_ref[...]).astype(BF16)
            wdn_bf[...] = wdn_st[s].astype(BF16)

        for vi, m in enumerate(MOE_ROW_VARIANTS):
            @pl.when(var_ref[i] == vi)
            def _():
                block(m)

    @pl.when(i >= nv)
    def _():
        y_ref[...] = jnp.zeros_like(y_ref)


def _experts(meta, src3, xs, w_up, b_up_p, w_down, b_down, perm):
    n_blocks = src3.shape[0]
    DE2 = w_up.shape[2]
    D = w_up.shape[1]
    tab = src3.shape[2]
    be_map = lambda i, be, *_: (be[i], 0, 0)
    grid_spec = pltpu.PrefetchScalarGridSpec(
        num_scalar_prefetch=len(meta),
        grid=(n_blocks,),
        in_specs=[
            pl.BlockSpec((None, 1, tab), lambda i, *_: (i, 0, 0), memory_space=pltpu.SMEM),
            pl.BlockSpec((None, 1, tab), lambda i, *_: (jnp.minimum(i + 1, n_blocks - 1), 0, 0),
                         memory_space=pltpu.SMEM),
            pl.BlockSpec(memory_space=pl.ANY),
            pl.BlockSpec(memory_space=pl.ANY),
            pl.BlockSpec((None, 1, DE2), be_map),
            pl.BlockSpec(memory_space=pl.ANY),
            pl.BlockSpec((None, 1, D), be_map),
            pl.BlockSpec((256, 256), lambda i, *_: (0, 0)),
        ],
        out_specs=pl.BlockSpec((MOE_ROWS, D), lambda i, *_: (i, 0)),
        scratch_shapes=[
            pltpu.VMEM((2, MOE_ROWS, D), F32),
            pltpu.VMEM((2, D, DE2), F32),
            pltpu.VMEM((2, DE2 // 2, D), F32),
            pltpu.VMEM((D, DE2), BF16),
            pltpu.VMEM((DE2 // 2, D), BF16),
            pltpu.VMEM((MOE_ROWS, DE2 // 2), BF16),
            pltpu.SemaphoreType.DMA((2,)),
            pltpu.SemaphoreType.DMA((2, 2)),
        ],
    )
    return pl.pallas_call(
        _expert_kernel,
        grid_spec=grid_spec,
        out_shape=jax.ShapeDtypeStruct((n_blocks * MOE_ROWS, D), F32),
        compiler_params=_params(("arbitrary",)),
        name="experts",
    )(*meta, src3, src3, xs, w_up, b_up_p, w_down, b_down, perm)


def _combine_kernel(src_cur_ref, src_nxt_ref, h_ref, tok_ref, ys_hbm, o_ref,
                    ybuf, sl_b, w_b, sem):
    i = pl.program_id(0)
    n = pl.num_programs(0)
    slot = lax.rem(i, 2)

    @pl.when(i == 0)
    def _():
        _gather_octets(src_cur_ref, SUB * SL_OCT, ys_hbm, ybuf.at[0], sem.at[0])

    @pl.when(i + 1 < n)
    def _():
        _gather_octets(src_nxt_ref, SUB * SL_OCT, ys_hbm, ybuf.at[1 - slot], sem.at[1 - slot])

    tl = h_ref.shape[0]
    for k in range(TOP_K):
        sl_b[k] = jnp.broadcast_to(tok_ref[:, k:k + 1], (tl, V7X_LANES))
        w_b[k] = jnp.broadcast_to(tok_ref[:, TOP_K + k:TOP_K + k + 1], (tl, V7X_LANES))
    lane = lax.broadcasted_iota(I32, (tl, V7X_LANES), 1).astype(F32)
    pltpu.make_async_copy(ys_hbm.at[pl.ds(0, SUB * SLOTS), :], ybuf.at[slot], sem.at[slot]).wait()
    lane_t = lax.broadcasted_iota(I32, (TM, V7X_LANES), 1).astype(F32)
    for sb in range(SUB):
        t0 = TM * sb
        acc = h_ref[t0:t0 + TM, :]
        for c in range(SLOTS // V7X_MXU_DIM):
            halves = []
            for hc in range(V7X_MXU_DIM // V7X_LANES):
                sio = lane_t + float(V7X_MXU_DIM * c + V7X_LANES * hc)
                wm = jnp.where(sio == sl_b[0, t0:t0 + TM], w_b[0, t0:t0 + TM], 0.0)
                for k in range(1, TOP_K):
                    wm = jnp.where(sio == sl_b[k, t0:t0 + TM], w_b[k, t0:t0 + TM], wm)
                halves.append(wm.astype(BF16))
            r0 = SLOTS * sb + V7X_MXU_DIM * c
            acc = acc + _dot(jnp.concatenate(halves, axis=1), ybuf[slot, r0:r0 + V7X_MXU_DIM, :].astype(BF16))
        o_ref[t0:t0 + TM, :] = acc


def _combine(csrc3, h2, tok_tab, ys):
    n_tok, D = h2.shape
    n_steps = n_tok // TL_PROJ
    tab = csrc3.shape[2]
    return pl.pallas_call(
        _combine_kernel,
        grid=(n_steps,),
        in_specs=[
            pl.BlockSpec((None, 1, tab), lambda i: (i, 0, 0), memory_space=pltpu.SMEM),
            pl.BlockSpec((None, 1, tab), lambda i: (jnp.minimum(i + 1, n_steps - 1), 0, 0),
                         memory_space=pltpu.SMEM),
            pl.BlockSpec((TL_PROJ, D), lambda i: (i, 0)),
            pl.BlockSpec((TL_PROJ, 2 * TOP_K), lambda i: (i, 0)),
            pl.BlockSpec(memory_space=pl.ANY),
        ],
        out_specs=pl.BlockSpec((TL_PROJ, D), lambda i: (i, 0)),
        out_shape=jax.ShapeDtypeStruct((n_tok, D), F32),
        scratch_shapes=[pltpu.VMEM((2, SUB * SLOTS, D), F32),
                        pltpu.VMEM((TOP_K, TL_PROJ, V7X_LANES), F32),
                        pltpu.VMEM((TOP_K, TL_PROJ, V7X_LANES), F32),
                        pltpu.SemaphoreType.DMA((2,))],
        compiler_params=_params(("arbitrary",)),
        name="combine",
    )(csrc3, csrc3, h2, tok_tab, ys)


def _s5_matrices(a_re, a_im, log_dt, b_re, b_im, c_re, c_im):
    G, P = a_re.shape
    dt = jnp.exp(log_dt.astype(F32))[:, None]
    lr = a_re.astype(F32)
    li = a_im.astype(F32)
    mag = jnp.exp(lr * dt)
    abar_r = mag * jnp.cos(li * dt)
    abar_i = mag * jnp.sin(li * dt)
    den = lr * lr + li * li
    nr = abar_r - 1.0
    ni = abar_i
    coef_r = (nr * lr + ni * li) / den
    coef_i = (ni * lr - nr * li) / den
    br_ = b_re.astype(F32)
    bi_ = b_im.astype(F32)
    bbar_r = coef_r[..., None] * br_ - coef_i[..., None] * bi_
    bbar_i = coef_r[..., None] * bi_ + coef_i[..., None] * br_

    gp = G // S5_PARTS

    def block_diag(t):
        n, g, a, b = t.shape
        rep = (jnp.arange(b)[:, None] == (jnp.arange(g * b) % b)[None, :]).astype(F32)
        tiled = jnp.einsum('xrb,bc->xrc', t.reshape(n, g * a, b), rep, precision=lax.Precision.HIGHEST)
        keep = (jnp.arange(g * a) // a)[:, None] == (jnp.arange(g * b) // b)[None, :]
        return jnp.where(keep[None], tiled, 0.0)

    def in_mat(bb):
        return block_diag(jnp.swapaxes(bb.reshape(S5_PARTS, gp, P, SSM_GROUP), 2, 3))

    def out_mat(cc):
        return block_diag(jnp.swapaxes(cc.reshape(S5_PARTS, gp, SSM_GROUP, P), 2, 3))

    bmat = jnp.concatenate([in_mat(bbar_r), in_mat(bbar_i)], axis=-1).astype(BF16)
    cmat = jnp.concatenate([out_mat(c_re.astype(F32)), -out_mat(c_im.astype(F32))], axis=1).astype(BF16)
    return bmat, abar_r.reshape(1, G * P), abar_i.reshape(1, G * P), cmat


def _pad_lanes(t):
    n, w = t.shape
    wp = ((w + V7X_LANES - 1) // V7X_LANES) * V7X_LANES
    return jnp.pad(t, ((0, 0), (0, wp - w))).reshape(n, 1, wp)


def kernel(x, norm1_g, w_in, b_gate, ssm_a_re, ssm_a_im, ssm_log_dt, ssm_b_re, ssm_b_im, ssm_c_re, ssm_c_im, ssm_d, ssm_w_glu, ssm_b_glu, q_norm_g, k_norm_g, lambda_q1, lambda_k1, lambda_q2, lambda_k2, subln_g, w_proj_ssm, w_proj_att, w_out, norm2_g, w_router, b_router, w_up, b_up, w_down, b_down):
    B, L, D = x.shape
    assert B == V7X_SUBLANES and D == 1024 and L % TL_PROJ == 0 and w_in.shape[0] == 1
    n_tok = B * L
    n_t = n_tok // TM
    assert n_t <= V7X_LANES
    l = 0

    scale = ATT_SUB_DIM ** -0.5
    qg = (jnp.tile(q_norm_g[l].astype(F32), 2 * ATT_HEADS) * (scale * math.log2(math.e))).reshape(1, 512)
    kg = jnp.tile(k_norm_g[l].astype(F32), 2 * ATT_HEADS).reshape(1, 512)
    blk = jnp.arange(256) // ATT_SUB_DIM
    ones_bd = (blk[:, None] == blk[None, :]).astype(BF16)
    u_tm, q, k, vt, gates = _inproj(x, norm1_g[l].reshape(1, D), w_in[l],
                                    b_gate[l].reshape(1, 2 * D), qg, kg, ones_bd)

    bmat, ar, ai, cmat = _s5_matrices(ssm_a_re[l], ssm_a_im[l], ssm_log_dt[l], ssm_b_re[l], ssm_b_im[l],
                                      ssm_c_re[l], ssm_c_im[l])
    y_ssm = _s5(u_tm, bmat, ar, ai, cmat, ssm_d[l].reshape(1, 512).astype(F32),
                ssm_w_glu[l].astype(BF16), ssm_b_glu[l].reshape(1, 512).astype(F32))

    lam4 = jnp.stack([lambda_q1[l], lambda_k1[l], lambda_q2[l], lambda_k2[l]]).astype(F32)
    sg = jnp.tile(subln_g[l].astype(F32), ATT_HEADS).reshape(1, 512)
    y_att = _attn(q, k, vt, lam4, sg)

    tpos = jnp.arange(TM)
    sut = (tpos[:, None] < tpos[None, :]).astype(BF16)
    epos = jnp.arange(N_EXPERTS)
    lt = (epos[None, :] < epos[:, None]).astype(BF16)
    h, xs, tok_tab, oct_tab = _mix(
        x, y_ssm, y_att, gates,
        w_proj_ssm[l].astype(BF16), w_proj_att[l].astype(BF16), w_out[l].astype(BF16),
        norm2_g[l].reshape(1, D), w_router[l].T.astype(BF16), b_router[l].reshape(N_EXPERTS, 1).astype(F32),
        sut, lt)

    n_blocks = (n_tok * TOP_K + n_t * N_EXPERTS * (OCT - 1) + MOE_ROWS - 1) // MOE_ROWS + N_EXPERTS
    meta, src, csrc = _moe_tables(oct_tab, n_t, n_blocks)

    de2 = w_up.shape[-1]
    b_up_p = b_up[l].reshape(N_EXPERTS, de2 // 256, 128, 2).transpose(0, 1, 3, 2).reshape(N_EXPERTS, 1, de2)
    pr = jnp.arange(256)
    col_src = jnp.where(pr < 128, 2 * pr, 2 * (pr - 128) + 1)
    perm = (jnp.arange(256)[:, None] == col_src[None, :]).astype(BF16)
    ys = _experts(meta, _pad_lanes(src), xs,
                  w_up[l], b_up_p, w_down[l], b_down[l].reshape(N_EXPERTS, 1, D), perm)

    out = _combine(_pad_lanes(csrc.reshape(n_t // SUB, SUB * SL_OCT)), h.reshape(n_tok, D), tok_tab, ys)
    return out.reshape(B, L, D)
```

```python
import math

import jax
import jax.numpy as jnp
from jax import lax
from jax.experimental import pallas as pl
from jax.experimental.pallas import tpu as pltpu

F32 = jnp.float32
BF16 = jnp.bfloat16
I32 = jnp.int32

RMS_EPS = 1e-6
CHUNK = 64
SSM_GROUP = 16
SSM_STATE = 64
ATT_HEADS = 4
ATT_SUB_DIM = 64
ATT_V_DIM = 128
N_EXPERTS = 32
TOP_K = 4
SWIGLU_ALPHA = 1.702
SWIGLU_LIMIT = 7.0
LAMBDA_INIT = 0.8 - 0.6 * math.exp(-0.3 * 0)

V7X_SUBLANES = 8
V7X_LANES = 128
V7X_MXU_DIM = 256

TL_IN = 1024
TL_PROJ = 512
TM = 512
SUB = TL_PROJ // TM
S5_STEPS = 128
S5_PARTS = 4
TQ = 256
TK = 256
MOE_ROWS = 1024
MOE_ROW_VARIANTS = (1024, 768, 512, 384, 256, 128)
OCT = V7X_SUBLANES
BLK_OCT = MOE_ROWS // OCT
SLOTS = ((TM * TOP_K + N_EXPERTS * (OCT - 1) + V7X_MXU_DIM - 1) // V7X_MXU_DIM) * V7X_MXU_DIM
SL_OCT = SLOTS // OCT
NEG = -1e30
VMEM_LIMIT = 56 * 1024 * 1024


def _dot(a, b):
    return jnp.dot(a, b, preferred_element_type=F32)


def _dot_nt(a, b):
    return lax.dot_general(a, b, (((1,), (1,)), ((), ())), preferred_element_type=F32)


def _params(sem, vmem=VMEM_LIMIT):
    return pltpu.CompilerParams(dimension_semantics=sem, vmem_limit_bytes=vmem)


def _inproj_kernel(x_ref, g1_ref, wf_ref, bg_ref, qg_ref, kg_ref, ones_ref,
                   u_ref, q_ref, k_ref, vt_ref, gate_ref, w_ref):
    @pl.when(jnp.logical_and(pl.program_id(0) == 0, pl.program_id(1) == 0))
    def _():
        for c in range(wf_ref.shape[1] // 512):
            w_ref[:, 512 * c:512 * (c + 1)] = wf_ref[:, 512 * c:512 * (c + 1)].astype(BF16)

    x = x_ref[...]
    ms = jnp.mean(x * x, axis=-1, keepdims=True)
    xn = (x * lax.rsqrt(ms + RMS_EPS) * g1_ref[...]).astype(BF16)

    u_ref[...] = _dot(xn, w_ref[:, 0:512])

    def head_norm(z, g):
        sq = (z * z).astype(BF16)
        ss = jnp.concatenate([_dot(sq[:, 0:256], ones_ref[...]),
                              _dot(sq[:, 256:512], ones_ref[...])], axis=1)
        return z * lax.rsqrt(ss * (1.0 / ATT_SUB_DIM) + RMS_EPS) * g

    q_ref[...] = head_norm(_dot(xn, w_ref[:, 512:1024]), qg_ref[...]).astype(BF16)
    k_ref[...] = head_norm(_dot(xn, w_ref[:, 1024:1536]), kg_ref[...]).astype(BF16)
    vt_ref[...] = _dot(xn, w_ref[:, 1536:2048]).T.astype(BF16)
    for c in range(4):
        lo = 2048 + 512 * c
        z = _dot(xn, w_ref[:, lo:lo + 512]) + bg_ref[:, 512 * c:512 * (c + 1)]
        gate_ref[:, 512 * c:512 * (c + 1)] = jax.nn.sigmoid(z).astype(BF16)


def _inproj(x, g1, w_in, b_gate, qg, kg, ones_bd):
    B, L, D = x.shape
    nl = L // TL_IN
    full = lambda shape: pl.BlockSpec(shape, lambda b, l: (0,) * len(shape))
    return pl.pallas_call(
        _inproj_kernel,
        grid=(B, nl),
        in_specs=[
            pl.BlockSpec((None, TL_IN, D), lambda b, l: (b, l, 0)),
            full((1, D)),
            pl.BlockSpec(w_in.shape, lambda b, l: (0, 0), pipeline_mode=pl.Buffered(1)),
            full((1, 2 * D)),
            full((1, 512)), full((1, 512)), full((256, 256)),
        ],
        out_specs=[
            pl.BlockSpec((None, TL_IN, 512), lambda b, l: (b, l, 0)),
            pl.BlockSpec((None, TL_IN, 512), lambda b, l: (b, l, 0)),
            pl.BlockSpec((None, TL_IN, 512), lambda b, l: (b, l, 0)),
            pl.BlockSpec((None, 512, TL_IN), lambda b, l: (b, 0, l)),
            pl.BlockSpec((None, TL_IN, 2 * D), lambda b, l: (b, l, 0)),
        ],
        out_shape=[
            jax.ShapeDtypeStruct((B, L, 512), F32),
            jax.ShapeDtypeStruct((B, L, 512), BF16),
            jax.ShapeDtypeStruct((B, L, 512), BF16),
            jax.ShapeDtypeStruct((B, 512, L), BF16),
            jax.ShapeDtypeStruct((B, L, 2 * D), BF16),
        ],
        scratch_shapes=[pltpu.VMEM(w_in.shape, BF16)],
        compiler_params=_params(("arbitrary", "arbitrary")),
        name="inproj",
    )(x, g1, w_in, b_gate, qg, kg, ones_bd)


def _s5_kernel(u_hbm, bm_ref, ar_ref, ai_ref, cm_ref, d_ref, wg_ref, bgl_ref,
               y_hbm, ubuf, ybuf, hs_ref, h_ref, usem, ysem):
    i = pl.program_id(0)
    n = pl.num_programs(0)
    slot = lax.rem(i, 2)
    nb = ubuf.shape[2]

    def u_copies(step, s):
        t0 = pl.multiple_of(step * S5_STEPS, S5_STEPS)
        return [pltpu.make_async_copy(u_hbm.at[b, pl.ds(t0, S5_STEPS), :], ubuf.at[s, :, b, :], usem.at[s])
                for b in range(nb)]

    def y_copies(step, s):
        t0 = pl.multiple_of(step * S5_STEPS, S5_STEPS)
        return [pltpu.make_async_copy(ybuf.at[s, :, b, :], y_hbm.at[b, pl.ds(t0, S5_STEPS), :], ysem.at[s])
                for b in range(nb)]

    @pl.when(i == 0)
    def _():
        h_ref[...] = jnp.zeros_like(h_ref)
        for cp in u_copies(0, 0):
            cp.start()

    @pl.when(i + 1 < n)
    def _():
        for cp in u_copies(i + 1, 1 - slot):
            cp.start()

    for cp in u_copies(i, slot):
        cp.wait()
    uf = ubuf[slot].reshape(S5_STEPS * nb, ubuf.shape[3])
    u = uf.astype(BF16)
    n_part = bm_ref.shape[0]
    cw = u.shape[1] // n_part
    sw = bm_ref.shape[2] // 2
    for pt in range(n_part):
        hs_ref[:, 2 * sw * pt:2 * sw * (pt + 1)] = _dot(u[:, cw * pt:cw * (pt + 1)], bm_ref[pt])

    ys = []
    for pt in range(n_part):
        re0 = 2 * sw * pt
        im0 = re0 + sw
        ar = jnp.broadcast_to(ar_ref[:, sw * pt:sw * (pt + 1)], (V7X_SUBLANES, sw))
        ai = jnp.broadcast_to(ai_ref[:, sw * pt:sw * (pt + 1)], (V7X_SUBLANES, sw))
        hr = h_ref[:, re0:re0 + sw]
        hi = h_ref[:, im0:im0 + sw]
        for t in range(S5_STEPS):
            r0 = t * V7X_SUBLANES
            xr = hs_ref[r0:r0 + V7X_SUBLANES, re0:re0 + sw]
            xi = hs_ref[r0:r0 + V7X_SUBLANES, im0:im0 + sw]
            hr, hi = ar * hr - ai * hi + xr, ar * hi + ai * hr + xi
            hs_ref[r0:r0 + V7X_SUBLANES, re0:re0 + sw] = hr
            hs_ref[r0:r0 + V7X_SUBLANES, im0:im0 + sw] = hi
        h_ref[:, re0:re0 + sw] = hr
        h_ref[:, im0:im0 + sw] = hi
        ys.append(_dot(hs_ref[:, re0:re0 + 2 * sw].astype(BF16), cm_ref[pt]))

    y = jnp.concatenate(ys, axis=1)
    y = y + d_ref[...] * uf
    y = 0.5 * y * (1.0 + jnp.tanh(math.sqrt(2.0 / math.pi) * (y + 0.044715 * (y * y * y))))
    z = _dot(y.astype(BF16), wg_ref[...]) + bgl_ref[...]

    @pl.when(i >= 2)
    def _():
        for cp in y_copies(i - 2, slot):
            cp.wait()

    ybuf[slot] = (y * jax.nn.sigmoid(z)).reshape(S5_STEPS, nb, ybuf.shape[3])
    for cp in y_copies(i, slot):
        cp.start()

    @pl.when(i == n - 1)
    def _():
        for cp in y_copies(i, slot):
            cp.wait()

    @pl.when(jnp.logical_and(i == n - 1, i >= 1))
    def _():
        for cp in y_copies(i - 1, 1 - slot):
            cp.wait()


def _s5(u, bmat, ar, ai, cmat, d_skip, w_glu_bf, b_glu):
    B, L, C = u.shape
    R = S5_STEPS * B
    full = lambda shape: pl.BlockSpec(shape, lambda i: (0,) * len(shape))
    return pl.pallas_call(
        _s5_kernel,
        grid=(L // S5_STEPS,),
        in_specs=[
            pl.BlockSpec(memory_space=pl.ANY),
            full(bmat.shape), full(ar.shape), full(ai.shape), full(cmat.shape),
            full(d_skip.shape), full(w_glu_bf.shape), full(b_glu.shape),
        ],
        out_specs=pl.BlockSpec(memory_space=pl.ANY),
        out_shape=jax.ShapeDtypeStruct((B, L, C), F32),
        scratch_shapes=[pltpu.VMEM((2, S5_STEPS, B, C), F32), pltpu.VMEM((2, S5_STEPS, B, C), F32),
                        pltpu.VMEM((R, 4096), F32), pltpu.VMEM((B, 4096), F32),
                        pltpu.SemaphoreType.DMA((2,)), pltpu.SemaphoreType.DMA((2,))],
        compiler_params=_params(("arbitrary",)),
        name="s5",
    )(u, bmat, ar, ai, cmat, d_skip, w_glu_bf, b_glu)


def _attn_kernel(q_ref, k_ref, vt_ref, lam_ref, sg_ref, o_ref, qm_ref, acc_ref, ml_ref, s_ref):
    i = pl.program_id(1)
    n_chain = 2 * ATT_HEADS
    lamv = lam_ref[...]
    lam = (jnp.exp(jnp.sum(lamv[0:1] * lamv[1:2], axis=-1, keepdims=True))
           - jnp.exp(jnp.sum(lamv[2:3] * lamv[3:4], axis=-1, keepdims=True)) + LAMBDA_INIT)
    lane = lax.broadcasted_iota(I32, (TQ, 128), 1)
    key_chunk = lax.broadcasted_iota(I32, (TK, TQ), 0) // CHUNK
    qry_chunk = lax.broadcasted_iota(I32, (TK, TQ), 1) // CHUNK
    diag_ok = key_chunk <= qry_chunk

    for hd in range(ATT_HEADS):
        qh = q_ref[:, 128 * hd:128 * (hd + 1)]
        zero = jnp.zeros_like(qh)
        qm_ref[2 * hd] = jnp.where(lane < ATT_SUB_DIM, qh, zero)
        qm_ref[2 * hd + 1] = jnp.where(lane >= ATT_SUB_DIM, qh, zero)
    acc_ref[...] = jnp.zeros_like(acc_ref)
    row = lax.broadcasted_iota(I32, (2 * n_chain, TQ), 0)
    ml_ref[...] = jnp.where(row % 2 == 0, NEG, 0.0)

    def scores(j, par):
        off = pl.multiple_of(j * TK, TK)
        for hd in range(ATT_HEADS):
            ks = k_ref[pl.ds(off, TK), 128 * hd:128 * (hd + 1)]
            for s in range(2):
                s_ref[par, 2 * hd + s] = _dot_nt(ks, qm_ref[2 * hd + s])

    def consume(j, par, masked):
        off = pl.multiple_of(j * TK, TK)
        ps, alphas = [], []
        for c in range(n_chain):
            st = s_ref[par, c]
            if masked:
                st = jnp.where(diag_ok, st, NEG)
            m = ml_ref[2 * c:2 * c + 1, :]
            l = ml_ref[2 * c + 1:2 * c + 2, :]
            mn = jnp.maximum(m, jnp.max(st, axis=0, keepdims=True))
            alpha = jnp.exp2(m - mn)
            p = jnp.exp2(st - mn)
            ml_ref[2 * c:2 * c + 1, :] = mn
            ml_ref[2 * c + 1:2 * c + 2, :] = alpha * l + jnp.sum(p, axis=0, keepdims=True)
            ps.append(p.astype(BF16))
            alphas.append(alpha)
        for c in range(n_chain):
            hd = c // 2
            vts = vt_ref[128 * hd:128 * (hd + 1), pl.ds(off, TK)]
            acc_ref[c] = acc_ref[c] * alphas[c] + _dot(vts, ps[c])

    scores(0, 0)

    def body(jj, carry):
        j = 2 * jj
        scores(j + 1, 1)
        consume(j, 0, False)
        scores(j + 2, 0)
        consume(j + 1, 1, False)
        return carry

    lax.fori_loop(0, i // 2, body, 0)

    @pl.when(i % 2 == 1)
    def _():
        scores(i, 1)
        consume(i - 1, 0, False)
        consume(i, 1, True)

    @pl.when(i % 2 == 0)
    def _():
        consume(i, 0, True)

    for hd in range(ATT_HEADS):
        c0 = 128 * hd
        l1 = ml_ref[4 * hd + 1:4 * hd + 2, :]
        l2 = ml_ref[4 * hd + 3:4 * hd + 4, :]
        o = acc_ref[2 * hd] / l1 - lam * (acc_ref[2 * hd + 1] / l2)
        ms = jnp.mean(o * o, axis=0, keepdims=True)
        on = (o * lax.rsqrt(ms + RMS_EPS)).T
        o_ref[:, c0:c0 + 128] = (on * sg_ref[:, c0:c0 + 128] * (1.0 - LAMBDA_INIT)).astype(BF16)


def _attn(q, k, vt, lam4, sg):
    B, L, W = q.shape
    n_chain = 2 * ATT_HEADS
    return pl.pallas_call(
        _attn_kernel,
        grid=(B, L // TQ),
        in_specs=[
            pl.BlockSpec((None, TQ, W), lambda b, i: (b, i, 0)),
            pl.BlockSpec((None, L, W), lambda b, i: (b, 0, 0)),
            pl.BlockSpec((None, W, L), lambda b, i: (b, 0, 0)),
            pl.BlockSpec((4, ATT_SUB_DIM), lambda b, i: (0, 0)),
            pl.BlockSpec((1, W), lambda b, i: (0, 0)),
        ],
        out_specs=pl.BlockSpec((None, TQ, W), lambda b, i: (b, i, 0)),
        out_shape=jax.ShapeDtypeStruct((B, L, W), BF16),
        scratch_shapes=[pltpu.VMEM((n_chain, TQ, 128), BF16),
                        pltpu.VMEM((n_chain, ATT_V_DIM, TQ), F32),
                        pltpu.VMEM((2 * n_chain, TQ), F32),
                        pltpu.VMEM((2, n_chain, TK, TQ), F32)],
        compiler_params=_params(("arbitrary", "arbitrary")),
        name="diff_attn",
    )(q, k, vt, lam4, sg)


def _mix_kernel(x_ref, ys_ref, ya_ref, gate_ref, wps_ref, wpa_ref, wo_ref, g2_ref, wrt_ref, br_ref,
                sut_ref, lt_ref, h_ref, xs_ref, tok_ref, oct_ref):
    tile = pl.program_id(0) * pl.num_programs(1) + pl.program_id(1)

    D = x_ref.shape[-1]
    tl = x_ref.shape[0]
    gs = gate_ref[:, 0:D].astype(F32)
    ga = gate_ref[:, D:2 * D].astype(F32)
    mixed = gs * _dot(ys_ref[...].astype(BF16), wps_ref[...]) + ga * _dot(ya_ref[...], wpa_ref[...])
    h = x_ref[...] + _dot(mixed.astype(BF16), wo_ref[...])
    h_ref[...] = h
    ms = jnp.mean(h * h, axis=-1, keepdims=True)
    hb = (h * lax.rsqrt(ms + RMS_EPS) * g2_ref[...]).astype(BF16)

    lg = _dot_nt(wrt_ref[...], hb) + br_ref[...]
    eio = lax.broadcasted_iota(I32, (N_EXPERTS, tl), 0)
    vals, idxs = [], []
    for k in range(TOP_K):
        m = jnp.max(lg, axis=0, keepdims=True)
        idx = jnp.min(jnp.where(lg == m, eio, N_EXPERTS), axis=0, keepdims=True)
        vals.append(m)
        idxs.append(idx)
        lg = jnp.where(eio == idx, -jnp.inf, lg)
    ex = [jnp.exp(v - vals[0]) for v in vals]
    den = ex[0] + ex[1] + ex[2] + ex[3]
    wts = [e / den for e in ex]

    lane = lax.broadcasted_iota(I32, (N_EXPERTS, V7X_LANES), 1)

    @pl.when(tile == 0)
    def _():
        oct_ref[...] = jnp.zeros_like(oct_ref)

    slot_rows = [[] for _ in range(TOP_K)]
    for sb in range(SUB):
        t0 = TM * sb
        eio_t = lax.broadcasted_iota(I32, (N_EXPERTS, TM), 0)
        sel_t = [eio_t == idxs[k][:, t0:t0 + TM] for k in range(TOP_K)]
        mtot = jnp.zeros((N_EXPERTS, TM), F32)
        for k in range(TOP_K):
            mtot = mtot + jnp.where(sel_t[k], 1.0, 0.0)
        pre = _dot(mtot.astype(BF16), sut_ref[...])
        cnt = jnp.sum(mtot, axis=1, keepdims=True)
        seg_oct = jnp.floor((cnt + (OCT - 1.0)) * (1.0 / OCT))
        seg_b = jnp.broadcast_to(seg_oct, (N_EXPERTS, V7X_LANES))
        start_oct = _dot(lt_ref[...], seg_b.astype(BF16))
        base = pre + start_oct[:, 0:1] * float(OCT)
        slots_f = [jnp.sum(jnp.where(sel_t[k], base, 0.0), axis=0, keepdims=True) for k in range(TOP_K)]
        slots = [s.astype(I32) for s in slots_f]
        for k in range(TOP_K):
            slot_rows[k].append(slots_f[k])

        hb_t = hb[t0:t0 + TM, :]
        for c in range(SLOTS // V7X_MXU_DIM):
            sio = lax.broadcasted_iota(I32, (V7X_MXU_DIM, TM), 0) + V7X_MXU_DIM * c
            p = jnp.where(sio == slots[0], 1.0, 0.0)
            for k in range(1, TOP_K):
                p = jnp.where(sio == slots[k], 1.0, p)
            r0 = SLOTS * sb + V7X_MXU_DIM * c
            xs_ref[r0:r0 + V7X_MXU_DIM, :] = _dot(p.astype(BF16), hb_t)

        oct_ref[...] = jnp.where(lane == SUB * tile + sb, seg_b, oct_ref[...])

    slot_full = [jnp.concatenate(slot_rows[k], axis=1) for k in range(TOP_K)]
    tok_ref[...] = jnp.concatenate(slot_full + wts, axis=0).T


def _mix(x, ys_tm, ya, gates, wps, wpa, wo, g2, wrt, br, sut, lt):
    B, L, D = x.shape
    nl = L // TL_PROJ
    n_tok = B * L
    full = lambda shape: pl.BlockSpec(shape, lambda b, l: (0,) * len(shape))
    return pl.pallas_call(
        _mix_kernel,
        grid=(B, nl),
        in_specs=[
            pl.BlockSpec((None, TL_PROJ, D), lambda b, l: (b, l, 0)),
            pl.BlockSpec((None, TL_PROJ, 512), lambda b, l: (b, l, 0)),
            pl.BlockSpec((None, TL_PROJ, 512), lambda b, l: (b, l, 0)),
            pl.BlockSpec((None, TL_PROJ, 2 * D), lambda b, l: (b, l, 0)),
            full(wps.shape), full(wpa.shape), full(wo.shape), full((1, D)),
            full(wrt.shape), full(br.shape), full(sut.shape), full(lt.shape),
        ],
        out_specs=[
            pl.BlockSpec((None, TL_PROJ, D), lambda b, l: (b, l, 0)),
            pl.BlockSpec((SUB * SLOTS, D), lambda b, l: (b * nl + l, 0)),
            pl.BlockSpec((TL_PROJ, 2 * TOP_K), lambda b, l: (b * nl + l, 0)),
            full((N_EXPERTS, V7X_LANES)),
        ],
        out_shape=[
            jax.ShapeDtypeStruct((B, L, D), F32),
            jax.ShapeDtypeStruct((B * nl * SUB * SLOTS, D), F32),
            jax.ShapeDtypeStruct((n_tok, 2 * TOP_K), F32),
            jax.ShapeDtypeStruct((N_EXPERTS, V7X_LANES), F32),
        ],
        compiler_params=_params(("arbitrary", "arbitrary")),
        name="mix_router",
    )(x, ys_tm, ya, gates, wps, wpa, wo, g2, wrt, br, sut, lt)


def _moe_tables(oct_tab, n_t, n_blocks):
    E = N_EXPERTS
    O = oct_tab[:, :n_t].astype(I32)
    lstart = jnp.cumsum(O, axis=0) - O
    ecum = jnp.cumsum(O, axis=1) - O
    tot = jnp.sum(O, axis=1)
    nb = (tot + BLK_OCT - 1) // BLK_OCT
    bend = jnp.cumsum(nb)
    bstart = bend - nb
    n_valid = bend[-1]
    bi = jnp.arange(n_blocks, dtype=I32)
    er = jnp.arange(E, dtype=I32)
    tr = jnp.arange(n_t, dtype=I32)
    be = jnp.minimum(jnp.sum((bi[:, None] >= bend[None, :]).astype(I32), axis=1), E - 1)
    last_e = jnp.sum(jnp.where(bi == jnp.maximum(n_valid - 1, 0), be, 0))
    be = jnp.where(bi < n_valid, be, last_e)
    oh_e = be[:, None] == er[None, :]
    pick = lambda tab: jnp.sum(jnp.where(oh_e[:, :, None], tab[None, :, :], 0), axis=1)
    ecum_i, o_i, lst_i = pick(ecum), pick(O), pick(lstart)
    tot_i = jnp.sum(jnp.where(oh_e, tot[None, :], 0), axis=1)
    bst_i = jnp.sum(jnp.where(oh_e, bstart[None, :], 0), axis=1)
    g = (bi - bst_i)[:, None] * BLK_OCT + jnp.arange(BLK_OCT, dtype=I32)[None, :]
    tau = jnp.minimum(jnp.sum((g[:, :, None] >= (ecum_i + o_i)[:, None, :]).astype(I32), axis=2), n_t - 1)
    off = tr[None, :] * SL_OCT + lst_i - ecum_i
    src = jnp.sum(jnp.where(tau[:, :, None] == tr[None, None, :], off[:, None, :], 0), axis=2) + g
    src = jnp.where((g < tot_i[:, None]) & (bi[:, None] < n_valid), src, 0)
    s = jnp.arange(SL_OCT, dtype=I32)
    lend_t = (lstart + O).T
    e_s = jnp.minimum(jnp.sum((s[None, :, None] >= lend_t[:, None, :]).astype(I32), axis=2), E - 1)
    offc = (bstart[:, None] * BLK_OCT + ecum - lstart).T
    csrc = jnp.sum(jnp.where(e_s[:, :, None] == er[None, None, :], offc[:, None, :], 0), axis=2) + s[None, :]
    csrc = jnp.where(s[None, :] < jnp.sum(O, axis=0)[:, None], csrc, 0)
    prev_e = jnp.concatenate([jnp.full((1,), -1, I32), be[:-1]])
    first = ((be != prev_e) & (bi < n_valid)).astype(I32)
    par = (jnp.cumsum(first) - 1) % 2
    later = (er[None, :] > er[:, None]) & (nb[None, :] > 0)
    nxt_of_e = jnp.min(jnp.where(later, er[None, :], E), axis=1)
    nxt_of_e = jnp.where(nxt_of_e < E, nxt_of_e, -1)
    nxt = jnp.sum(jnp.where(oh_e, nxt_of_e[None, :], 0), axis=1)
    rows_i = jnp.clip(tot_i * OCT - (bi - bst_i) * MOE_ROWS, 0, MOE_ROWS)
    var = jnp.zeros_like(bi)
    for vi, m in enumerate(MOE_ROW_VARIANTS):
        var = jnp.where(rows_i <= m, vi, var)
    meta = (be, first, par.astype(I32), nxt.astype(I32), var.astype(I32), n_valid.astype(I32).reshape(1))
    return meta, src, csrc


def _gather_octets(src_ref, n_oct, src_hbm, dst, sem, first=0, priorities=(0, 1)):
    for q in range(first, first + n_oct):
        row = pl.multiple_of(src_ref[0, q] * OCT, OCT)
        pltpu.make_async_copy(src_hbm.at[pl.ds(row, OCT), :], dst.at[pl.ds(OCT * q, OCT), :], sem).start(
            priority=priorities[q % len(priorities)])


def _expert_kernel(be_ref, first_ref, par_ref, nxt_ref, var_ref, nv_ref, src_cur_ref, src_nxt_ref, xs_hbm,
                   wup_hbm, bup_ref, wdn_hbm, bdn_ref, perm_ref, y_ref,
                   xbuf, wup_st, wdn_st, wup_bf, wdn_bf, act_ref, sem, wsem):
    i = pl.program_id(0)
    nv = nv_ref[0]
    slot = lax.rem(i, 2)

    def weight_copies(e, s):
        return (pltpu.make_async_copy(wup_hbm.at[e], wup_st.at[s], wsem.at[0, s]),
                pltpu.make_async_copy(wdn_hbm.at[e], wdn_st.at[s], wsem.at[1, s]))

    def gather_rows(src_ref, variant, s):
        for vi, m in enumerate(MOE_ROW_VARIANTS):
            @pl.when(variant == vi)
            def _():
                _gather_octets(src_ref, m // OCT, xs_hbm, xbuf.at[s], sem.at[s])

    @pl.when(i == 0)
    def _():
        gather_rows(src_cur_ref, var_ref[0], 0)
        for cp in weight_copies(be_ref[0], par_ref[0]):
            cp.start()

    @pl.when(i + 1 < nv)
    def _():
        gather_rows(src_nxt_ref, var_ref[jnp.minimum(i + 1, pl.num_programs(0) - 1)], 1 - slot)

    def block(m):
        pltpu.make_async_copy(xs_hbm.at[pl.ds(0, m), :], xbuf.at[slot, pl.ds(0, m), :], sem.at[slot]).wait()
        x = xbuf[slot, 0:m, :].astype(BF16)
        for cb in range(8):
            hp = _dot(x, wup_bf[:, 256 * cb:256 * (cb + 1)]) + bup_ref[:, 256 * cb:256 * (cb + 1)]
            glu = jnp.minimum(hp[:, 0:128], SWIGLU_LIMIT)
            lin = jnp.clip(hp[:, 128:256], -SWIGLU_LIMIT, SWIGLU_LIMIT)
            act = glu * jax.nn.sigmoid(SWIGLU_ALPHA * glu) * (lin + 1.0)
            act_ref[0:m, 128 * cb:128 * (cb + 1)] = act.astype(BF16)
        y_ref[0:m, :] = _dot(act_ref[0:m, :], wdn_bf[...]) + bdn_ref[...]
        if m < MOE_ROWS:
            y_ref[m:MOE_ROWS, :] = jnp.zeros((MOE_ROWS - m, y_ref.shape[1]), F32)

    @pl.when(i < nv)
    def _():
        @pl.when(first_ref[i] == 1)
        def _():
            s = par_ref[i]
            for cp in weight_copies(be_ref[i], s):
                cp.wait()

            @pl.when(nxt_ref[i] >= 0)
            def _():
                for cp in weight_copies(nxt_ref[i], 1 - s):
                    cp.start()

            for cb in range(8):
                blk = wup_st[s, :, 256 * cb:256 * (cb + 1)].astype(BF16)
                wup_bf[:, 256 * cb:256 * (cb + 1)] = _dot(blk, perm_ref[...]).astype(BF16)
            wdn_bf[...] = wdn_st[s].astype(BF16)

        for vi, m in enumerate(MOE_ROW_VARIANTS):
            @pl.when(var_ref[i] == vi)
            def _():
                block(m)

    @pl.when(i >= nv)
    def _():
        y_ref[...] = jnp.zeros_like(y_ref)


def _experts(meta, src3, xs, w_up, b_up_p, w_down, b_down, perm):
    n_blocks = src3.shape[0]
    DE2 = w_up.shape[2]
    D = w_up.shape[1]
    tab = src3.shape[2]
    be_map = lambda i, be, *_: (be[i], 0, 0)
    grid_spec = pltpu.PrefetchScalarGridSpec(
        num_scalar_prefetch=len(meta),
        grid=(n_blocks,),
        in_specs=[
            pl.BlockSpec((None, 1, tab), lambda i, *_: (i, 0, 0), memory_space=pltpu.SMEM),
            pl.BlockSpec((None, 1, tab), lambda i, *_: (jnp.minimum(i + 1, n_blocks - 1), 0, 0),
                         memory_space=pltpu.SMEM),
            pl.BlockSpec(memory_space=pl.ANY),
            pl.BlockSpec(memory_space=pl.ANY),
            pl.BlockSpec((None, 1, DE2), be_map),
            pl.BlockSpec(memory_space=pl.ANY),
            pl.BlockSpec((None, 1, D), be_map),
            pl.BlockSpec((256, 256), lambda i, *_: (0, 0)),
        ],
        out_specs=pl.BlockSpec((MOE_ROWS, D), lambda i, *_: (i, 0)),
        scratch_shapes=[
            pltpu.VMEM((2, MOE_ROWS, D), F32),
            pltpu.VMEM((2, D, DE2), F32),
            pltpu.VMEM((2, DE2 // 2, D), F32),
            pltpu.VMEM((D, DE2), BF16),
            pltpu.VMEM((DE2 // 2, D), BF16),
            pltpu.VMEM((MOE_ROWS, DE2 // 2), BF16),
            pltpu.SemaphoreType.DMA((2,)),
            pltpu.SemaphoreType.DMA((2, 2)),
        ],
    )
    return pl.pallas_call(
        _expert_kernel,
        grid_spec=grid_spec,
        out_shape=jax.ShapeDtypeStruct((n_blocks * MOE_ROWS, D), F32),
        compiler_params=_params(("arbitrary",)),
        name="experts",
    )(*meta, src3, src3, xs, w_up, b_up_p, w_down, b_down, perm)


def _combine_kernel(src_cur_ref, src_nxt_ref, h_ref, tok_ref, ys_hbm, o_ref,
                    ybuf, sl_b, w_b, sem):
    i = pl.program_id(0)
    n = pl.num_programs(0)
    slot = lax.rem(i, 2)

    @pl.when(i == 0)
    def _():
        _gather_octets(src_cur_ref, SUB * SL_OCT, ys_hbm, ybuf.at[0], sem.at[0])

    @pl.when(i + 1 < n)
    def _():
        _gather_octets(src_nxt_ref, SUB * SL_OCT, ys_hbm, ybuf.at[1 - slot], sem.at[1 - slot])

    tl = h_ref.shape[0]
    for k in range(TOP_K):
        sl_b[k] = jnp.broadcast_to(tok_ref[:, k:k + 1], (tl, V7X_LANES))
        w_b[k] = jnp.broadcast_to(tok_ref[:, TOP_K + k:TOP_K + k + 1], (tl, V7X_LANES))
    lane = lax.broadcasted_iota(I32, (tl, V7X_LANES), 1).astype(F32)
    pltpu.make_async_copy(ys_hbm.at[pl.ds(0, SUB * SLOTS), :], ybuf.at[slot], sem.at[slot]).wait()
    lane_t = lax.broadcasted_iota(I32, (TM, V7X_LANES), 1).astype(F32)
    for sb in range(SUB):
        t0 = TM * sb
        acc = h_ref[t0:t0 + TM, :]
        for c in range(SLOTS // V7X_MXU_DIM):
            halves = []
            for hc in range(V7X_MXU_DIM // V7X_LANES):
                sio = lane_t + float(V7X_MXU_DIM * c + V7X_LANES * hc)
                wm = jnp.where(sio == sl_b[0, t0:t0 + TM], w_b[0, t0:t0 + TM], 0.0)
                for k in range(1, TOP_K):
                    wm = jnp.where(sio == sl_b[k, t0:t0 + TM], w_b[k, t0:t0 + TM], wm)
                halves.append(wm.astype(BF16))
            r0 = SLOTS * sb + V7X_MXU_DIM * c
            acc = acc + _dot(jnp.concatenate(halves, axis=1), ybuf[slot, r0:r0 + V7X_MXU_DIM, :].astype(BF16))
        o_ref[t0:t0 + TM, :] = acc


def _combine(csrc3, h2, tok_tab, ys):
    n_tok, D = h2.shape
    n_steps = n_tok // TL_PROJ
    tab = csrc3.shape[2]
    return pl.pallas_call(
        _combine_kernel,
        grid=(n_steps,),
        in_specs=[
            pl.BlockSpec((None, 1, tab), lambda i: (i, 0, 0), memory_space=pltpu.SMEM),
            pl.BlockSpec((None, 1, tab), lambda i: (jnp.minimum(i + 1, n_steps - 1), 0, 0),
                         memory_space=pltpu.SMEM),
            pl.BlockSpec((TL_PROJ, D), lambda i: (i, 0)),
            pl.BlockSpec((TL_PROJ, 2 * TOP_K), lambda i: (i, 0)),
            pl.BlockSpec(memory_space=pl.ANY),
        ],
        out_specs=pl.BlockSpec((TL_PROJ, D), lambda i: (i, 0)),
        out_shape=jax.ShapeDtypeStruct((n_tok, D), F32),
        scratch_shapes=[pltpu.VMEM((2, SUB * SLOTS, D), F32),
                        pltpu.VMEM((TOP_K, TL_PROJ, V7X_LANES), F32),
                        pltpu.VMEM((TOP_K, TL_PROJ, V7X_LANES), F32),
                        pltpu.SemaphoreType.DMA((2,))],
        compiler_params=_params(("arbitrary",)),
        name="combine",
    )(csrc3, csrc3, h2, tok_tab, ys)


def _s5_matrices(a_re, a_im, log_dt, b_re, b_im, c_re, c_im):
    G, P = a_re.shape
    dt = jnp.exp(log_dt.astype(F32))[:, None]
    lr = a_re.astype(F32)
    li = a_im.astype(F32)
    mag = jnp.exp(lr * dt)
    abar_r = mag * jnp.cos(li * dt)
    abar_i = mag * jnp.sin(li * dt)
    den = lr * lr + li * li
    nr = abar_r - 1.0
    ni = abar_i
    coef_r = (nr * lr + ni * li) / den
    coef_i = (ni * lr - nr * li) / den
    br_ = b_re.astype(F32)
    bi_ = b_im.astype(F32)
    bbar_r = coef_r[..., None] * br_ - coef_i[..., None] * bi_
    bbar_i = coef_r[..., None] * bi_ + coef_i[..., None] * br_

    gp = G // S5_PARTS

    def block_diag(t):
        n, g, a, b = t.shape
        rep = (jnp.arange(b)[:, None] == (jnp.arange(g * b) % b)[None, :]).astype(F32)
        tiled = jnp.einsum('xrb,bc->xrc', t.reshape(n, g * a, b), rep, precision=lax.Precision.HIGHEST)
        keep = (jnp.arange(g * a) // a)[:, None] == (jnp.arange(g * b) // b)[None, :]
        return jnp.where(keep[None], tiled, 0.0)

    def in_mat(bb):
        return block_diag(jnp.swapaxes(bb.reshape(S5_PARTS, gp, P, SSM_GROUP), 2, 3))

    def out_mat(cc):
        return block_diag(jnp.swapaxes(cc.reshape(S5_PARTS, gp, SSM_GROUP, P), 2, 3))

    bmat = jnp.concatenate([in_mat(bbar_r), in_mat(bbar_i)], axis=-1).astype(BF16)
    cmat = jnp.concatenate([out_mat(c_re.astype(F32)), -out_mat(c_im.astype(F32))], axis=1).astype(BF16)
    return bmat, abar_r.reshape(1, G * P), abar_i.reshape(1, G * P), cmat


def _pad_lanes(t):
    n, w = t.shape
    wp = ((w + V7X_LANES - 1) // V7X_LANES) * V7X_LANES
    return jnp.pad(t, ((0, 0), (0, wp - w))).reshape(n, 1, wp)


def kernel(x, norm1_g, w_in, b_gate, ssm_a_re, ssm_a_im, ssm_log_dt, ssm_b_re, ssm_b_im, ssm_c_re, ssm_c_im, ssm_d, ssm_w_glu, ssm_b_glu, q_norm_g, k_norm_g, lambda_q1, lambda_k1, lambda_q2, lambda_k2, subln_g, w_proj_ssm, w_proj_att, w_out, norm2_g, w_router, b_router, w_up, b_up, w_down, b_down):
    B, L, D = x.shape
    assert B == V7X_SUBLANES and D == 1024 and L % TL_PROJ == 0 and w_in.shape[0] == 1
    n_tok = B * L
    n_t = n_tok // TM
    assert n_t <= V7X_LANES
    l = 0

    scale = ATT_SUB_DIM ** -0.5
    qg = (jnp.tile(q_norm_g[l].astype(F32), 2 * ATT_HEADS) * (scale * math.log2(math.e))).reshape(1, 512)
    kg = jnp.tile(k_norm_g[l].astype(F32), 2 * ATT_HEADS).reshape(1, 512)
    blk = jnp.arange(256) // ATT_SUB_DIM
    ones_bd = (blk[:, None] == blk[None, :]).astype(BF16)
    u_tm, q, k, vt, gates = _inproj(x, norm1_g[l].reshape(1, D), w_in[l],
                                    b_gate[l].reshape(1, 2 * D), qg, kg, ones_bd)

    bmat, ar, ai, cmat = _s5_matrices(ssm_a_re[l], ssm_a_im[l], ssm_log_dt[l], ssm_b_re[l], ssm_b_im[l],
                                      ssm_c_re[l], ssm_c_im[l])
    y_ssm = _s5(u_tm, bmat, ar, ai, cmat, ssm_d[l].reshape(1, 512).astype(F32),
                ssm_w_glu[l].astype(BF16), ssm_b_glu[l].reshape(1, 512).astype(F32))

    lam4 = jnp.stack([lambda_q1[l], lambda_k1[l], lambda_q2[l], lambda_k2[l]]).astype(F32)
    sg = jnp.tile(subln_g[l].astype(F32), ATT_HEADS).reshape(1, 512)
    y_att = _attn(q, k, vt, lam4, sg)

    tpos = jnp.arange(TM)
    sut = (tpos[:, None] < tpos[None, :]).astype(BF16)
    epos = jnp.arange(N_EXPERTS)
    lt = (epos[None, :] < epos[:, None]).astype(BF16)
    h, xs, tok_tab, oct_tab = _mix(
        x, y_ssm, y_att, gates,
        w_proj_ssm[l].astype(BF16), w_proj_att[l].astype(BF16), w_out[l].astype(BF16),
        norm2_g[l].reshape(1, D), w_router[l].T.astype(BF16), b_router[l].reshape(N_EXPERTS, 1).astype(F32),
        sut, lt)

    n_blocks = (n_tok * TOP_K + n_t * N_EXPERTS * (OCT - 1) + MOE_ROWS - 1) // MOE_ROWS + N_EXPERTS
    meta, src, csrc = _moe_tables(oct_tab, n_t, n_blocks)

    de2 = w_up.shape[-1]
    b_up_p = b_up[l].reshape(N_EXPERTS, de2 // 256, 128, 2).transpose(0, 1, 3, 2).reshape(N_EXPERTS, 1, de2)
    pr = jnp.arange(256)
    col_src = jnp.where(pr < 128, 2 * pr, 2 * (pr - 128) + 1)
    perm = (jnp.arange(256)[:, None] == col_src[None, :]).astype(BF16)
    ys = _experts(meta, _pad_lanes(src), xs,
                  w_up[l], b_up_p, w_down[l], b_down[l].reshape(N_EXPERTS, 1, D), perm)

    out = _combine(_pad_lanes(csrc.reshape(n_t // SUB, SUB * SL_OCT)), h.reshape(n_tok, D), tok_tab, ys)
    return out.reshape(B, L, D)
```

```python
import math

import jax
import jax.numpy as jnp
import numpy as np
from jax import lax
from jax.experimental import pallas as pl
from jax.experimental.pallas import tpu as pltpu

F32 = jnp.float32
BF16 = jnp.bfloat16
I32 = jnp.int32

RMS_EPS = 1e-6
CHUNK = 64
SSM_GROUP = 16
SSM_STATE = 64
ATT_HEADS = 4
ATT_SUB_DIM = 64
ATT_V_DIM = 128
N_EXPERTS = 32
TOP_K = 4
SWIGLU_ALPHA = 1.702
SWIGLU_LIMIT = 7.0
LAMBDA_INIT = 0.8 - 0.6 * math.exp(-0.3 * 0)

V7X_SUBLANES = 8
V7X_LANES = 128
V7X_MXU_DIM = 256

TL_IN = 1024
TL_PROJ = 512
TM = 512
SUB = TL_PROJ // TM
S5_STEPS = 128
S5_PARTS = 4
TQ = 256
TK = 256
MOE_ROWS = 1024
MOE_ROW_VARIANTS = (1024, 768, 512, 384, 256, 128)
OCT = V7X_SUBLANES
BLK_OCT = MOE_ROWS // OCT
SLOTS = ((TM * TOP_K + N_EXPERTS * (OCT - 1) + V7X_MXU_DIM - 1) // V7X_MXU_DIM) * V7X_MXU_DIM
SL_OCT = SLOTS // OCT
NEG = -1e30
VMEM_LIMIT = 56 * 1024 * 1024


def _dot(a, b):
    return jnp.dot(a, b, preferred_element_type=F32)


def _dot_nt(a, b):
    return lax.dot_general(a, b, (((1,), (1,)), ((), ())), preferred_element_type=F32)


def _params(sem, vmem=VMEM_LIMIT):
    return pltpu.CompilerParams(dimension_semantics=sem, vmem_limit_bytes=vmem)


def _inproj_kernel(x_ref, g1_ref, wf_ref, bg_ref, qg_ref, kg_ref, ones_ref,
                   u_ref, q_ref, k_ref, vt_ref, gate_ref, w_ref):
    @pl.when(jnp.logical_and(pl.program_id(0) == 0, pl.program_id(1) == 0))
    def _():
        for c in range(wf_ref.shape[1] // 512):
            w_ref[:, 512 * c:512 * (c + 1)] = wf_ref[:, 512 * c:512 * (c + 1)].astype(BF16)

    x = x_ref[...]
    ms = jnp.mean(x * x, axis=-1, keepdims=True)
    xn = (x * lax.rsqrt(ms + RMS_EPS) * g1_ref[...]).astype(BF16)

    u_ref[...] = _dot(xn, w_ref[:, 0:512])

    def head_norm(z, g):
        sq = (z * z).astype(BF16)
        ss = jnp.concatenate([_dot(sq[:, 0:256], ones_ref[...]),
                              _dot(sq[:, 256:512], ones_ref[...])], axis=1)
        return z * lax.rsqrt(ss * (1.0 / ATT_SUB_DIM) + RMS_EPS) * g

    q_ref[...] = head_norm(_dot(xn, w_ref[:, 512:1024]), qg_ref[...]).astype(BF16)
    k_ref[...] = head_norm(_dot(xn, w_ref[:, 1024:1536]), kg_ref[...]).astype(BF16)
    vt_ref[...] = _dot(xn, w_ref[:, 1536:2048]).T.astype(BF16)
    for c in range(4):
        lo = 2048 + 512 * c
        z = _dot(xn, w_ref[:, lo:lo + 512]) + bg_ref[:, 512 * c:512 * (c + 1)]
        gate_ref[:, 512 * c:512 * (c + 1)] = jax.nn.sigmoid(z).astype(BF16)


def _inproj(x, g1, w_in, b_gate, qg, kg, ones_bd):
    B, L, D = x.shape
    nl = L // TL_IN
    full = lambda shape: pl.BlockSpec(shape, lambda b, l: (0,) * len(shape))
    return pl.pallas_call(
        _inproj_kernel,
        grid=(B, nl),
        in_specs=[
            pl.BlockSpec((None, TL_IN, D), lambda b, l: (b, l, 0)),
            full((1, D)),
            pl.BlockSpec(w_in.shape, lambda b, l: (0, 0), pipeline_mode=pl.Buffered(1)),
            full((1, 2 * D)),
            full((1, 512)), full((1, 512)), full((256, 256)),
        ],
        out_specs=[
            pl.BlockSpec((None, TL_IN, 512), lambda b, l: (b, l, 0)),
            pl.BlockSpec((None, TL_IN, 512), lambda b, l: (b, l, 0)),
            pl.BlockSpec((None, TL_IN, 512), lambda b, l: (b, l, 0)),
            pl.BlockSpec((None, 512, TL_IN), lambda b, l: (b, 0, l)),
            pl.BlockSpec((None, TL_IN, 2 * D), lambda b, l: (b, l, 0)),
        ],
        out_shape=[
            jax.ShapeDtypeStruct((B, L, 512), F32),
            jax.ShapeDtypeStruct((B, L, 512), BF16),
            jax.ShapeDtypeStruct((B, L, 512), BF16),
            jax.ShapeDtypeStruct((B, 512, L), BF16),
            jax.ShapeDtypeStruct((B, L, 2 * D), BF16),
        ],
        scratch_shapes=[pltpu.VMEM(w_in.shape, BF16)],
        compiler_params=_params(("arbitrary", "arbitrary")),
        name="inproj",
    )(x, g1, w_in, b_gate, qg, kg, ones_bd)


def _s5_kernel(u_hbm, bm_ref, ar_ref, ai_ref, cm_ref, d_ref, wg_ref, bgl_ref,
               y_hbm, ubuf, ybuf, hs_ref, h_ref, usem, ysem):
    i = pl.program_id(0)
    n = pl.num_programs(0)
    slot = lax.rem(i, 2)
    nb = ubuf.shape[2]

    def u_copies(step, s):
        t0 = pl.multiple_of(step * S5_STEPS, S5_STEPS)
        return [pltpu.make_async_copy(u_hbm.at[b, pl.ds(t0, S5_STEPS), :], ubuf.at[s, :, b, :], usem.at[s])
                for b in range(nb)]

    def y_copies(step, s):
        t0 = pl.multiple_of(step * S5_STEPS, S5_STEPS)
        return [pltpu.make_async_copy(ybuf.at[s, :, b, :], y_hbm.at[b, pl.ds(t0, S5_STEPS), :], ysem.at[s])
                for b in range(nb)]

    @pl.when(i == 0)
    def _():
        h_ref[...] = jnp.zeros_like(h_ref)
        for cp in u_copies(0, 0):
            cp.start()

    @pl.when(i + 1 < n)
    def _():
        for cp in u_copies(i + 1, 1 - slot):
            cp.start()

    for cp in u_copies(i, slot):
        cp.wait()
    uf = ubuf[slot].reshape(S5_STEPS * nb, ubuf.shape[3])
    u = uf.astype(BF16)
    n_part = bm_ref.shape[0]
    cw = u.shape[1] // n_part
    sw = bm_ref.shape[2] // 2
    for pt in range(n_part):
        hs_ref[:, 2 * sw * pt:2 * sw * (pt + 1)] = _dot(u[:, cw * pt:cw * (pt + 1)], bm_ref[pt])

    ys = []
    for pt in range(n_part):
        re0 = 2 * sw * pt
        im0 = re0 + sw
        ar = jnp.broadcast_to(ar_ref[:, sw * pt:sw * (pt + 1)], (V7X_SUBLANES, sw))
        ai = jnp.broadcast_to(ai_ref[:, sw * pt:sw * (pt + 1)], (V7X_SUBLANES, sw))
        hr = h_ref[:, re0:re0 + sw]
        hi = h_ref[:, im0:im0 + sw]
        for t in range(S5_STEPS):
            r0 = t * V7X_SUBLANES
            xr = hs_ref[r0:r0 + V7X_SUBLANES, re0:re0 + sw]
            xi = hs_ref[r0:r0 + V7X_SUBLANES, im0:im0 + sw]
            hr, hi = ar * hr - ai * hi + xr, ar * hi + ai * hr + xi
            hs_ref[r0:r0 + V7X_SUBLANES, re0:re0 + sw] = hr
            hs_ref[r0:r0 + V7X_SUBLANES, im0:im0 + sw] = hi
        h_ref[:, re0:re0 + sw] = hr
        h_ref[:, im0:im0 + sw] = hi
        ys.append(_dot(hs_ref[:, re0:re0 + 2 * sw].astype(BF16), cm_ref[pt]))

    y = jnp.concatenate(ys, axis=1)
    y = y + d_ref[...] * uf
    y = 0.5 * y * (1.0 + jnp.tanh(math.sqrt(2.0 / math.pi) * (y + 0.044715 * (y * y * y))))
    z = _dot(y.astype(BF16), wg_ref[...]) + bgl_ref[...]

    @pl.when(i >= 2)
    def _():
        for cp in y_copies(i - 2, slot):
            cp.wait()

    ybuf[slot] = (y * jax.nn.sigmoid(z)).reshape(S5_STEPS, nb, ybuf.shape[3])
    for cp in y_copies(i, slot):
        cp.start()

    @pl.when(i == n - 1)
    def _():
        for cp in y_copies(i, slot):
            cp.wait()

    @pl.when(jnp.logical_and(i == n - 1, i >= 1))
    def _():
        for cp in y_copies(i - 1, 1 - slot):
            cp.wait()


def _s5(u, bmat, ar, ai, cmat, d_skip, w_glu_bf, b_glu):
    B, L, C = u.shape
    R = S5_STEPS * B
    full = lambda shape: pl.BlockSpec(shape, lambda i: (0,) * len(shape))
    return pl.pallas_call(
        _s5_kernel,
        grid=(L // S5_STEPS,),
        in_specs=[
            pl.BlockSpec(memory_space=pl.ANY),
            full(bmat.shape), full(ar.shape), full(ai.shape), full(cmat.shape),
            full(d_skip.shape), full(w_glu_bf.shape), full(b_glu.shape),
        ],
        out_specs=pl.BlockSpec(memory_space=pl.ANY),
        out_shape=jax.ShapeDtypeStruct((B, L, C), F32),
        scratch_shapes=[pltpu.VMEM((2, S5_STEPS, B, C), F32), pltpu.VMEM((2, S5_STEPS, B, C), F32),
                        pltpu.VMEM((R, 4096), F32), pltpu.VMEM((B, 4096), F32),
                        pltpu.SemaphoreType.DMA((2,)), pltpu.SemaphoreType.DMA((2,))],
        compiler_params=_params(("arbitrary",)),
        name="s5",
    )(u, bmat, ar, ai, cmat, d_skip, w_glu_bf, b_glu)


def _attn_kernel(q_ref, k_ref, vt_ref, lam_ref, sg_ref, o_ref, qm_ref, acc_ref, ml_ref, s_ref):
    i = pl.program_id(1)
    n_chain = 2 * ATT_HEADS
    lamv = lam_ref[...]
    lam = (jnp.exp(jnp.sum(lamv[0:1] * lamv[1:2], axis=-1, keepdims=True))
           - jnp.exp(jnp.sum(lamv[2:3] * lamv[3:4], axis=-1, keepdims=True)) + LAMBDA_INIT)
    lane = lax.broadcasted_iota(I32, (TQ, 128), 1)
    key_chunk = lax.broadcasted_iota(I32, (TK, TQ), 0) // CHUNK
    qry_chunk = lax.broadcasted_iota(I32, (TK, TQ), 1) // CHUNK
    diag_ok = key_chunk <= qry_chunk

    for hd in range(ATT_HEADS):
        qh = q_ref[:, 128 * hd:128 * (hd + 1)]
        zero = jnp.zeros_like(qh)
        qm_ref[2 * hd] = jnp.where(lane < ATT_SUB_DIM, qh, zero)
        qm_ref[2 * hd + 1] = jnp.where(lane >= ATT_SUB_DIM, qh, zero)
    acc_ref[...] = jnp.zeros_like(acc_ref)
    row = lax.broadcasted_iota(I32, (2 * n_chain, TQ), 0)
    ml_ref[...] = jnp.where(row % 2 == 0, NEG, 0.0)

    def scores(j, par):
        off = pl.multiple_of(j * TK, TK)
        for hd in range(ATT_HEADS):
            ks = k_ref[pl.ds(off, TK), 128 * hd:128 * (hd + 1)]
            for s in range(2):
                s_ref[par, 2 * hd + s] = _dot_nt(ks, qm_ref[2 * hd + s])

    def consume(j, par, masked):
        off = pl.multiple_of(j * TK, TK)
        ps, alphas = [], []
        for c in range(n_chain):
            st = s_ref[par, c]
            if masked:
                st = jnp.where(diag_ok, st, NEG)
            m = ml_ref[2 * c:2 * c + 1, :]
            l = ml_ref[2 * c + 1:2 * c + 2, :]
            mn = jnp.maximum(m, jnp.max(st, axis=0, keepdims=True))
            alpha = jnp.exp2(m - mn)
            p = jnp.exp2(st - mn)
            ml_ref[2 * c:2 * c + 1, :] = mn
            ml_ref[2 * c + 1:2 * c + 2, :] = alpha * l + jnp.sum(p, axis=0, keepdims=True)
            ps.append(p.astype(BF16))
            alphas.append(alpha)
        for c in range(n_chain):
            hd = c // 2
            vts = vt_ref[128 * hd:128 * (hd + 1), pl.ds(off, TK)]
            acc_ref[c] = acc_ref[c] * alphas[c] + _dot(vts, ps[c])

    scores(0, 0)

    def body(jj, carry):
        j = 2 * jj
        scores(j + 1, 1)
        consume(j, 0, False)
        scores(j + 2, 0)
        consume(j + 1, 1, False)
        return carry

    lax.fori_loop(0, i // 2, body, 0)

    @pl.when(i % 2 == 1)
    def _():
        scores(i, 1)
        consume(i - 1, 0, False)
        consume(i, 1, True)

    @pl.when(i % 2 == 0)
    def _():
        consume(i, 0, True)

    for hd in range(ATT_HEADS):
        c0 = 128 * hd
        l1 = ml_ref[4 * hd + 1:4 * hd + 2, :]
        l2 = ml_ref[4 * hd + 3:4 * hd + 4, :]
        o = acc_ref[2 * hd] / l1 - lam * (acc_ref[2 * hd + 1] / l2)
        ms = jnp.mean(o * o, axis=0, keepdims=True)
        on = (o * lax.rsqrt(ms + RMS_EPS)).T
        o_ref[:, c0:c0 + 128] = (on * sg_ref[:, c0:c0 + 128] * (1.0 - LAMBDA_INIT)).astype(BF16)


def _attn(q, k, vt, lam4, sg):
    B, L, W = q.shape
    n_chain = 2 * ATT_HEADS
    return pl.pallas_call(
        _attn_kernel,
        grid=(B, L // TQ),
        in_specs=[
            pl.BlockSpec((None, TQ, W), lambda b, i: (b, i, 0)),
            pl.BlockSpec((None, L, W), lambda b, i: (b, 0, 0)),
            pl.BlockSpec((None, W, L), lambda b, i: (b, 0, 0)),
            pl.BlockSpec((4, ATT_SUB_DIM), lambda b, i: (0, 0)),
            pl.BlockSpec((1, W), lambda b, i: (0, 0)),
        ],
        out_specs=pl.BlockSpec((None, TQ, W), lambda b, i: (b, i, 0)),
        out_shape=jax.ShapeDtypeStruct((B, L, W), BF16),
        scratch_shapes=[pltpu.VMEM((n_chain, TQ, 128), BF16),
                        pltpu.VMEM((n_chain, ATT_V_DIM, TQ), F32),
                        pltpu.VMEM((2 * n_chain, TQ), F32),
                        pltpu.VMEM((2, n_chain, TK, TQ), F32)],
        compiler_params=_params(("arbitrary", "arbitrary")),
        name="diff_attn",
    )(q, k, vt, lam4, sg)


def _mix_kernel(x_ref, ys_ref, ya_ref, gate_ref, wps_ref, wpa_ref, wo_ref, g2_ref, wrt_ref, br_ref,
                sut_ref, lt_ref, h_ref, xs_ref, tok_ref, oct_ref):
    tile = pl.program_id(0) * pl.num_programs(1) + pl.program_id(1)

    D = x_ref.shape[-1]
    tl = x_ref.shape[0]
    gs = gate_ref[:, 0:D].astype(F32)
    ga = gate_ref[:, D:2 * D].astype(F32)
    mixed = gs * _dot(ys_ref[...].astype(BF16), wps_ref[...]) + ga * _dot(ya_ref[...], wpa_ref[...])
    h = x_ref[...] + _dot(mixed.astype(BF16), wo_ref[...])
    h_ref[...] = h
    ms = jnp.mean(h * h, axis=-1, keepdims=True)
    hb = (h * lax.rsqrt(ms + RMS_EPS) * g2_ref[...]).astype(BF16)

    lg = _dot_nt(wrt_ref[...], hb) + br_ref[...]
    eio = lax.broadcasted_iota(I32, (N_EXPERTS, tl), 0)
    vals, idxs = [], []
    for k in range(TOP_K):
        m = jnp.max(lg, axis=0, keepdims=True)
        idx = jnp.min(jnp.where(lg == m, eio, N_EXPERTS), axis=0, keepdims=True)
        vals.append(m)
        idxs.append(idx)
        lg = jnp.where(eio == idx, -jnp.inf, lg)
    ex = [jnp.exp(v - vals[0]) for v in vals]
    den = ex[0] + ex[1] + ex[2] + ex[3]
    wts = [e / den for e in ex]

    lane = lax.broadcasted_iota(I32, (N_EXPERTS, V7X_LANES), 1)

    @pl.when(tile == 0)
    def _():
        oct_ref[...] = jnp.zeros_like(oct_ref)

    slot_rows = [[] for _ in range(TOP_K)]
    for sb in range(SUB):
        t0 = TM * sb
        eio_t = lax.broadcasted_iota(I32, (N_EXPERTS, TM), 0)
        sel_t = [eio_t == idxs[k][:, t0:t0 + TM] for k in range(TOP_K)]
        mtot = jnp.zeros((N_EXPERTS, TM), F32)
        for k in range(TOP_K):
            mtot = mtot + jnp.where(sel_t[k], 1.0, 0.0)
        pre = _dot(mtot.astype(BF16), sut_ref[...])
        cnt = jnp.sum(mtot, axis=1, keepdims=True)
        seg_oct = jnp.floor((cnt + (OCT - 1.0)) * (1.0 / OCT))
        seg_b = jnp.broadcast_to(seg_oct, (N_EXPERTS, V7X_LANES))
        start_oct = _dot(lt_ref[...], seg_b.astype(BF16))
        base = pre + start_oct[:, 0:1] * float(OCT)
        slots_f = [jnp.sum(jnp.where(sel_t[k], base, 0.0), axis=0, keepdims=True) for k in range(TOP_K)]
        slots = [s.astype(I32) for s in slots_f]
        for k in range(TOP_K):
            slot_rows[k].append(slots_f[k])

        hb_t = hb[t0:t0 + TM, :]
        for c in range(SLOTS // V7X_MXU_DIM):
            sio = lax.broadcasted_iota(I32, (V7X_MXU_DIM, TM), 0) + V7X_MXU_DIM * c
            p = jnp.where(sio == slots[0], 1.0, 0.0)
            for k in range(1, TOP_K):
                p = jnp.where(sio == slots[k], 1.0, p)
            r0 = SLOTS * sb + V7X_MXU_DIM * c
            xs_ref[r0:r0 + V7X_MXU_DIM, :] = _dot(p.astype(BF16), hb_t)

        oct_ref[...] = jnp.where(lane == SUB * tile + sb, seg_b, oct_ref[...])

    slot_full = [jnp.concatenate(slot_rows[k], axis=1) for k in range(TOP_K)]
    tok_ref[...] = jnp.concatenate(slot_full + wts, axis=0).T


def _mix(x, ys_tm, ya, gates, wps, wpa, wo, g2, wrt, br, sut, lt):
    B, L, D = x.shape
    nl = L // TL_PROJ
    n_tok = B * L
    full = lambda shape: pl.BlockSpec(shape, lambda b, l: (0,) * len(shape))
    return pl.pallas_call(
        _mix_kernel,
        grid=(B, nl),
        in_specs=[
            pl.BlockSpec((None, TL_PROJ, D), lambda b, l: (b, l, 0)),
            pl.BlockSpec((None, TL_PROJ, 512), lambda b, l: (b, l, 0)),
            pl.BlockSpec((None, TL_PROJ, 512), lambda b, l: (b, l, 0)),
            pl.BlockSpec((None, TL_PROJ, 2 * D), lambda b, l: (b, l, 0)),
            full(wps.shape), full(wpa.shape), full(wo.shape), full((1, D)),
            full(wrt.shape), full(br.shape), full(sut.shape), full(lt.shape),
        ],
        out_specs=[
            pl.BlockSpec((None, TL_PROJ, D), lambda b, l: (b, l, 0)),
            pl.BlockSpec((SUB * SLOTS, D), lambda b, l: (b * nl + l, 0)),
            pl.BlockSpec((TL_PROJ, 2 * TOP_K), lambda b, l: (b * nl + l, 0)),
            full((N_EXPERTS, V7X_LANES)),
        ],
        out_shape=[
            jax.ShapeDtypeStruct((B, L, D), F32),
            jax.ShapeDtypeStruct((B * nl * SUB * SLOTS, D), F32),
            jax.ShapeDtypeStruct((n_tok, 2 * TOP_K), F32),
            jax.ShapeDtypeStruct((N_EXPERTS, V7X_LANES), F32),
        ],
        compiler_params=_params(("arbitrary", "arbitrary")),
        name="mix_router",
    )(x, ys_tm, ya, gates, wps, wpa, wo, g2, wrt, br, sut, lt)


def _moe_tables(oct_tab, n_t, n_blocks):
    E = N_EXPERTS
    O = oct_tab[:, :n_t].astype(I32)
    lstart = jnp.cumsum(O, axis=0) - O
    ecum = jnp.cumsum(O, axis=1) - O
    tot = jnp.sum(O, axis=1)
    nb = (tot + BLK_OCT - 1) // BLK_OCT
    bend = jnp.cumsum(nb)
    bstart = bend - nb
    n_valid = bend[-1]
    bi = jnp.arange(n_blocks, dtype=I32)
    er = jnp.arange(E, dtype=I32)
    tr = jnp.arange(n_t, dtype=I32)
    be = jnp.minimum(jnp.sum((bi[:, None] >= bend[None, :]).astype(I32), axis=1), E - 1)
    last_e = jnp.sum(jnp.where(bi == jnp.maximum(n_valid - 1, 0), be, 0))
    be = jnp.where(bi < n_valid, be, last_e)
    oh_e = be[:, None] == er[None, :]
    pick = lambda tab: jnp.sum(jnp.where(oh_e[:, :, None], tab[None, :, :], 0), axis=1)
    ecum_i, o_i, lst_i = pick(ecum), pick(O), pick(lstart)
    tot_i = jnp.sum(jnp.where(oh_e, tot[None, :], 0), axis=1)
    bst_i = jnp.sum(jnp.where(oh_e, bstart[None, :], 0), axis=1)
    g = (bi - bst_i)[:, None] * BLK_OCT + jnp.arange(BLK_OCT, dtype=I32)[None, :]
    tau = jnp.minimum(jnp.sum((g[:, :, None] >= (ecum_i + o_i)[:, None, :]).astype(I32), axis=2), n_t - 1)
    off = tr[None, :] * SL_OCT + lst_i - ecum_i
    src = jnp.sum(jnp.where(tau[:, :, None] == tr[None, None, :], off[:, None, :], 0), axis=2) + g
    src = jnp.where((g < tot_i[:, None]) & (bi[:, None] < n_valid), src, 0)
    s = jnp.arange(SL_OCT, dtype=I32)
    lend_t = (lstart + O).T
    e_s = jnp.minimum(jnp.sum((s[None, :, None] >= lend_t[:, None, :]).astype(I32), axis=2), E - 1)
    offc = (bstart[:, None] * BLK_OCT + ecum - lstart).T
    csrc = jnp.sum(jnp.where(e_s[:, :, None] == er[None, None, :], offc[:, None, :], 0), axis=2) + s[None, :]
    csrc = jnp.where(s[None, :] < jnp.sum(O, axis=0)[:, None], csrc, 0)
    prev_e = jnp.concatenate([jnp.full((1,), -1, I32), be[:-1]])
    first = ((be != prev_e) & (bi < n_valid)).astype(I32)
    par = (jnp.cumsum(first) - 1) % 2
    later = (er[None, :] > er[:, None]) & (nb[None, :] > 0)
    nxt_of_e = jnp.min(jnp.where(later, er[None, :], E), axis=1)
    nxt_of_e = jnp.where(nxt_of_e < E, nxt_of_e, -1)
    nxt = jnp.sum(jnp.where(oh_e, nxt_of_e[None, :], 0), axis=1)
    rows_i = jnp.clip(tot_i * OCT - (bi - bst_i) * MOE_ROWS, 0, MOE_ROWS)
    var = jnp.zeros_like(bi)
    for vi, m in enumerate(MOE_ROW_VARIANTS):
        var = jnp.where(rows_i <= m, vi, var)
    meta = (be, first, par.astype(I32), nxt.astype(I32), var.astype(I32), n_valid.astype(I32).reshape(1))
    return meta, src, csrc


def _gather_octets(src_ref, n_oct, src_hbm, dst, sem, first=0, priorities=(0, 1)):
    for q in range(first, first + n_oct):
        row = pl.multiple_of(src_ref[0, q] * OCT, OCT)
        pltpu.make_async_copy(src_hbm.at[pl.ds(row, OCT), :], dst.at[pl.ds(OCT * q, OCT), :], sem).start(
            priority=priorities[q % len(priorities)])


def _expert_kernel(be_ref, first_ref, par_ref, nxt_ref, var_ref, nv_ref, src_cur_ref, src_nxt_ref, xs_hbm,
                   wup_hbm, bup_ref, wdn_hbm, bdn_ref, perm_ref, y_ref,
                   xbuf, wup_st, wdn_st, wup_bf, wdn_bf, act_ref, sem, wsem):
    i = pl.program_id(0)
    nv = nv_ref[0]
    slot = lax.rem(i, 2)

    def weight_copies(e, s):
        return (pltpu.make_async_copy(wup_hbm.at[e], wup_st.at[s], wsem.at[0, s]),
                pltpu.make_async_copy(wdn_hbm.at[e], wdn_st.at[s], wsem.at[1, s]))

    def gather_rows(src_ref, variant, s):
        for vi, m in enumerate(MOE_ROW_VARIANTS):
            @pl.when(variant == vi)
            def _():
                _gather_octets(src_ref, m // OCT, xs_hbm, xbuf.at[s], sem.at[s], priorities=(0,))

    @pl.when(i == 0)
    def _():
        gather_rows(src_cur_ref, var_ref[0], 0)
        for cp in weight_copies(be_ref[0], par_ref[0]):
            cp.start()

    @pl.when(i + 1 < nv)
    def _():
        gather_rows(src_nxt_ref, var_ref[jnp.minimum(i + 1, pl.num_programs(0) - 1)], 1 - slot)

    def block(m):
        pltpu.make_async_copy(xs_hbm.at[pl.ds(0, m), :], xbuf.at[slot, pl.ds(0, m), :], sem.at[slot]).wait()
        x = xbuf[slot, 0:m, :].astype(BF16)
        for cb in range(8):
            hp = _dot(x, wup_bf[:, 256 * cb:256 * (cb + 1)]) + bup_ref[:, 256 * cb:256 * (cb + 1)]
            glu = jnp.minimum(hp[:, 0:128], SWIGLU_LIMIT)
            lin = jnp.clip(hp[:, 128:256], -SWIGLU_LIMIT, SWIGLU_LIMIT)
            act = glu * jax.nn.sigmoid(SWIGLU_ALPHA * glu) * (lin + 1.0)
            act_ref[0:m, 128 * cb:128 * (cb + 1)] = act.astype(BF16)
        y_ref[0:m, :] = _dot(act_ref[0:m, :], wdn_bf[...]) + bdn_ref[...]
        if m < MOE_ROWS:
            y_ref[m:MOE_ROWS, :] = jnp.zeros((MOE_ROWS - m, y_ref.shape[1]), F32)

    @pl.when(i < nv)
    def _():
        @pl.when(first_ref[i] == 1)
        def _():
            s = par_ref[i]
            for cp in weight_copies(be_ref[i], s):
                cp.wait()

            @pl.when(nxt_ref[i] >= 0)
            def _():
                for cp in weight_copies(nxt_ref[i], 1 - s):
                    cp.start()

            for cb in range(8):
                blk = wup_st[s, :, 256 * cb:256 * (cb + 1)].astype(BF16)
                wup_bf[:, 256 * cb:256 * (cb + 1)] = _dot(blk, perm_ref[...]).astype(BF16)
            wdn_bf[...] = wdn_st[s].astype(BF16)

        for vi, m in enumerate(MOE_ROW_VARIANTS):
            @pl.when(var_ref[i] == vi)
            def _():
                block(m)

    @pl.when(i >= nv)
    def _():
        y_ref[...] = jnp.zeros_like(y_ref)


def _experts(meta, src3, xs, w_up, b_up_p, w_down, b_down, perm):
    n_blocks = src3.shape[0]
    DE2 = w_up.shape[2]
    D = w_up.shape[1]
    tab = src3.shape[2]
    be_map = lambda i, be, *_: (be[i], 0, 0)
    grid_spec = pltpu.PrefetchScalarGridSpec(
        num_scalar_prefetch=len(meta),
        grid=(n_blocks,),
        in_specs=[
            pl.BlockSpec((None, 1, tab), lambda i, *_: (i, 0, 0), memory_space=pltpu.SMEM),
            pl.BlockSpec((None, 1, tab), lambda i, *_: (jnp.minimum(i + 1, n_blocks - 1), 0, 0),
                         memory_space=pltpu.SMEM),
            pl.BlockSpec(memory_space=pl.ANY),
            pl.BlockSpec(memory_space=pl.ANY),
            pl.BlockSpec((None, 1, DE2), be_map),
            pl.BlockSpec(memory_space=pl.ANY),
            pl.BlockSpec((None, 1, D), be_map),
            pl.BlockSpec((256, 256), lambda i, *_: (0, 0)),
        ],
        out_specs=pl.BlockSpec((MOE_ROWS, D), lambda i, *_: (i, 0)),
        scratch_shapes=[
            pltpu.VMEM((2, MOE_ROWS, D), F32),
            pltpu.VMEM((2, D, DE2), F32),
            pltpu.VMEM((2, DE2 // 2, D), F32),
            pltpu.VMEM((D, DE2), BF16),
            pltpu.VMEM((DE2 // 2, D), BF16),
            pltpu.VMEM((MOE_ROWS, DE2 // 2), BF16),
            pltpu.SemaphoreType.DMA((2,)),
            pltpu.SemaphoreType.DMA((2, 2)),
        ],
    )
    return pl.pallas_call(
        _expert_kernel,
        grid_spec=grid_spec,
        out_shape=jax.ShapeDtypeStruct((n_blocks * MOE_ROWS, D), F32),
        compiler_params=_params(("arbitrary",)),
        name="experts",
    )(*meta, src3, src3, xs, w_up, b_up_p, w_down, b_down, perm)


def _combine_kernel(src_cur_ref, src_nxt_ref, h_ref, tok_ref, ys_hbm, o_ref,
                    ybuf, sl_b, w_b, sem):
    i = pl.program_id(0)
    n = pl.num_programs(0)
    slot = lax.rem(i, 2)

    @pl.when(i == 0)
    def _():
        _gather_octets(src_cur_ref, SUB * SL_OCT, ys_hbm, ybuf.at[0], sem.at[0])

    @pl.when(i + 1 < n)
    def _():
        _gather_octets(src_nxt_ref, SUB * SL_OCT, ys_hbm, ybuf.at[1 - slot], sem.at[1 - slot])

    tl = h_ref.shape[0]
    for k in range(TOP_K):
        sl_b[k] = jnp.broadcast_to(tok_ref[:, k:k + 1], (tl, V7X_LANES))
        w_b[k] = jnp.broadcast_to(tok_ref[:, TOP_K + k:TOP_K + k + 1], (tl, V7X_LANES))
    lane = lax.broadcasted_iota(I32, (tl, V7X_LANES), 1).astype(F32)
    pltpu.make_async_copy(ys_hbm.at[pl.ds(0, SUB * SLOTS), :], ybuf.at[slot], sem.at[slot]).wait()
    lane_t = lax.broadcasted_iota(I32, (TM, V7X_LANES), 1).astype(F32)
    for sb in range(SUB):
        t0 = TM * sb
        acc = h_ref[t0:t0 + TM, :]
        for c in range(SLOTS // V7X_MXU_DIM):
            halves = []
            for hc in range(V7X_MXU_DIM // V7X_LANES):
                sio = lane_t + float(V7X_MXU_DIM * c + V7X_LANES * hc)
                wm = jnp.where(sio == sl_b[0, t0:t0 + TM], w_b[0, t0:t0 + TM], 0.0)
                for k in range(1, TOP_K):
                    wm = jnp.where(sio == sl_b[k, t0:t0 + TM], w_b[k, t0:t0 + TM], wm)
                halves.append(wm.astype(BF16))
            r0 = SLOTS * sb + V7X_MXU_DIM * c
            acc = acc + _dot(jnp.concatenate(halves, axis=1), ybuf[slot, r0:r0 + V7X_MXU_DIM, :].astype(BF16))
        o_ref[t0:t0 + TM, :] = acc


def _combine(csrc3, h2, tok_tab, ys):
    n_tok, D = h2.shape
    n_steps = n_tok // TL_PROJ
    tab = csrc3.shape[2]
    return pl.pallas_call(
        _combine_kernel,
        grid=(n_steps,),
        in_specs=[
            pl.BlockSpec((None, 1, tab), lambda i: (i, 0, 0), memory_space=pltpu.SMEM),
            pl.BlockSpec((None, 1, tab), lambda i: (jnp.minimum(i + 1, n_steps - 1), 0, 0),
                         memory_space=pltpu.SMEM),
            pl.BlockSpec((TL_PROJ, D), lambda i: (i, 0)),
            pl.BlockSpec((TL_PROJ, 2 * TOP_K), lambda i: (i, 0)),
            pl.BlockSpec(memory_space=pl.ANY),
        ],
        out_specs=pl.BlockSpec((TL_PROJ, D), lambda i: (i, 0)),
        out_shape=jax.ShapeDtypeStruct((n_tok, D), F32),
        scratch_shapes=[pltpu.VMEM((2, SUB * SLOTS, D), F32),
                        pltpu.VMEM((TOP_K, TL_PROJ, V7X_LANES), F32),
                        pltpu.VMEM((TOP_K, TL_PROJ, V7X_LANES), F32),
                        pltpu.SemaphoreType.DMA((2,))],
        compiler_params=_params(("arbitrary",)),
        name="combine",
    )(csrc3, csrc3, h2, tok_tab, ys)


def _s5_matrices(a_re, a_im, log_dt, b_re, b_im, c_re, c_im):
    G, P = a_re.shape
    dt = jnp.exp(log_dt.astype(F32))[:, None]
    lr = a_re.astype(F32)
    li = a_im.astype(F32)
    mag = jnp.exp(lr * dt)
    abar_r = mag * jnp.cos(li * dt)
    abar_i = mag * jnp.sin(li * dt)
    den = lr * lr + li * li
    nr = abar_r - 1.0
    ni = abar_i
    coef_r = (nr * lr + ni * li) / den
    coef_i = (ni * lr - nr * li) / den
    br_ = b_re.astype(F32)
    bi_ = b_im.astype(F32)
    bbar_r = coef_r[..., None] * br_ - coef_i[..., None] * bi_
    bbar_i = coef_r[..., None] * bi_ + coef_i[..., None] * br_

    gp = G // S5_PARTS

    def block_diag(t):
        n, g, a, b = t.shape
        rep = (jnp.arange(b)[:, None] == (jnp.arange(g * b) % b)[None, :]).astype(F32)
        tiled = jnp.einsum('xrb,bc->xrc', t.reshape(n, g * a, b), rep, precision=lax.Precision.HIGHEST)
        keep = (jnp.arange(g * a) // a)[:, None] == (jnp.arange(g * b) // b)[None, :]
        return jnp.where(keep[None], tiled, 0.0)

    def in_mat(bb):
        return block_diag(jnp.swapaxes(bb.reshape(S5_PARTS, gp, P, SSM_GROUP), 2, 3))

    def out_mat(cc):
        return block_diag(jnp.swapaxes(cc.reshape(S5_PARTS, gp, SSM_GROUP, P), 2, 3))

    bmat = jnp.concatenate([in_mat(bbar_r), in_mat(bbar_i)], axis=-1).astype(BF16)
    cmat = jnp.concatenate([out_mat(c_re.astype(F32)), -out_mat(c_im.astype(F32))], axis=1).astype(BF16)
    return bmat, abar_r.reshape(1, G * P), abar_i.reshape(1, G * P), cmat


def _pad_lanes(t):
    n, w = t.shape
    wp = ((w + V7X_LANES - 1) // V7X_LANES) * V7X_LANES
    return jnp.pad(t, ((0, 0), (0, wp - w))).reshape(n, 1, wp)


def kernel(x, norm1_g, w_in, b_gate, ssm_a_re, ssm_a_im, ssm_log_dt, ssm_b_re, ssm_b_im, ssm_c_re, ssm_c_im, ssm_d, ssm_w_glu, ssm_b_glu, q_norm_g, k_norm_g, lambda_q1, lambda_k1, lambda_q2, lambda_k2, subln_g, w_proj_ssm, w_proj_att, w_out, norm2_g, w_router, b_router, w_up, b_up, w_down, b_down):
    B, L, D = x.shape
    assert B == V7X_SUBLANES and D == 1024 and L % TL_PROJ == 0 and w_in.shape[0] == 1
    n_tok = B * L
    n_t = n_tok // TM
    assert n_t <= V7X_LANES
    l = 0

    scale = ATT_SUB_DIM ** -0.5
    qg = (jnp.tile(q_norm_g[l].astype(F32), 2 * ATT_HEADS) * (scale * math.log2(math.e))).reshape(1, 512)
    kg = jnp.tile(k_norm_g[l].astype(F32), 2 * ATT_HEADS).reshape(1, 512)
    blk = np.arange(256) // ATT_SUB_DIM
    ones_bd = jnp.asarray(blk[:, None] == blk[None, :], dtype=BF16)
    u_tm, q, k, vt, gates = _inproj(x, norm1_g[l].reshape(1, D), w_in[l],
                                    b_gate[l].reshape(1, 2 * D), qg, kg, ones_bd)

    bmat, ar, ai, cmat = _s5_matrices(ssm_a_re[l], ssm_a_im[l], ssm_log_dt[l], ssm_b_re[l], ssm_b_im[l],
                                      ssm_c_re[l], ssm_c_im[l])
    y_ssm = _s5(u_tm, bmat, ar, ai, cmat, ssm_d[l].reshape(1, 512).astype(F32),
                ssm_w_glu[l].astype(BF16), ssm_b_glu[l].reshape(1, 512).astype(F32))

    lam4 = jnp.stack([lambda_q1[l], lambda_k1[l], lambda_q2[l], lambda_k2[l]]).astype(F32)
    sg = jnp.tile(subln_g[l].astype(F32), ATT_HEADS).reshape(1, 512)
    y_att = _attn(q, k, vt, lam4, sg)

    tpos = np.arange(TM)
    sut = jnp.asarray(tpos[:, None] < tpos[None, :], dtype=BF16)
    epos = np.arange(N_EXPERTS)
    lt = jnp.asarray(epos[None, :] < epos[:, None], dtype=BF16)
    h, xs, tok_tab, oct_tab = _mix(
        x, y_ssm, y_att, gates,
        w_proj_ssm[l].astype(BF16), w_proj_att[l].astype(BF16), w_out[l].astype(BF16),
        norm2_g[l].reshape(1, D), w_router[l].T.astype(BF16), b_router[l].reshape(N_EXPERTS, 1).astype(F32),
        sut, lt)

    n_blocks = (n_tok * TOP_K + n_t * N_EXPERTS * (OCT - 1) + MOE_ROWS - 1) // MOE_ROWS + N_EXPERTS
    meta, src, csrc = _moe_tables(oct_tab, n_t, n_blocks)

    de2 = w_up.shape[-1]
    b_up_p = b_up[l].reshape(N_EXPERTS, de2 // 256, 128, 2).transpose(0, 1, 3, 2).reshape(N_EXPERTS, 1, de2)
    pr = np.arange(256)
    col_src = np.where(pr < 128, 2 * pr, 2 * (pr - 128) + 1)
    perm = jnp.asarray(pr[:, None] == col_src[None, :], dtype=BF16)
    ys = _experts(meta, _pad_lanes(src), xs,
                  w_up[l], b_up_p, w_down[l], b_down[l].reshape(N_EXPERTS, 1, D), perm)

    out = _combine(_pad_lanes(csrc.reshape(n_t // SUB, SUB * SL_OCT)), h.reshape(n_tok, D), tok_tab, ys)
    return out.reshape(B, L, D)
```

```python
import math

import jax
import jax.numpy as jnp
import numpy as np
from jax import lax
from jax.experimental import pallas as pl
from jax.experimental.pallas import tpu as pltpu

F32 = jnp.float32
BF16 = jnp.bfloat16
I32 = jnp.int32

RMS_EPS = 1e-6
CHUNK = 64
SSM_GROUP = 16
SSM_STATE = 64
ATT_HEADS = 4
ATT_SUB_DIM = 64
ATT_V_DIM = 128
N_EXPERTS = 32
TOP_K = 4
SWIGLU_ALPHA = 1.702
SWIGLU_LIMIT = 7.0
LAMBDA_INIT = 0.8 - 0.6 * math.exp(-0.3 * 0)

V7X_SUBLANES = 8
V7X_LANES = 128
V7X_MXU_DIM = 256

TL_IN = 1024
TL_PROJ = 512
TM = 512
SUB = TL_PROJ // TM
S5_STEPS = 128
S5_PARTS = 4
TQ = 256
TK = 256
MOE_ROWS = 1024
MOE_ROW_VARIANTS = (1024, 768, 512, 384, 256, 128)
OCT = V7X_SUBLANES
BLK_OCT = MOE_ROWS // OCT
SLOTS = ((TM * TOP_K + N_EXPERTS * (OCT - 1) + V7X_MXU_DIM - 1) // V7X_MXU_DIM) * V7X_MXU_DIM
SL_OCT = SLOTS // OCT
NEG = -1e30
VMEM_LIMIT = 56 * 1024 * 1024


def _dot(a, b):
    return jnp.dot(a, b, preferred_element_type=F32)


def _dot_nt(a, b):
    return lax.dot_general(a, b, (((1,), (1,)), ((), ())), preferred_element_type=F32)


def _params(sem, vmem=VMEM_LIMIT):
    return pltpu.CompilerParams(dimension_semantics=sem, vmem_limit_bytes=vmem)


def _inproj_kernel(x_ref, g1_ref, wf_ref, bg_ref, qg_ref, kg_ref, ones_ref,
                   u_ref, q_ref, k_ref, vt_ref, gate_ref, w_ref):
    @pl.when(jnp.logical_and(pl.program_id(0) == 0, pl.program_id(1) == 0))
    def _():
        for c in range(wf_ref.shape[1] // 512):
            w_ref[:, 512 * c:512 * (c + 1)] = wf_ref[:, 512 * c:512 * (c + 1)].astype(BF16)

    x = x_ref[...]
    ms = jnp.mean(x * x, axis=-1, keepdims=True)
    xn = (x * lax.rsqrt(ms + RMS_EPS) * g1_ref[...]).astype(BF16)

    u_ref[...] = _dot(xn, w_ref[:, 0:512])

    def head_norm(z, g):
        sq = (z * z).astype(BF16)
        ss = jnp.concatenate([_dot(sq[:, 0:256], ones_ref[...]),
                              _dot(sq[:, 256:512], ones_ref[...])], axis=1)
        return z * lax.rsqrt(ss * (1.0 / ATT_SUB_DIM) + RMS_EPS) * g

    q_ref[...] = head_norm(_dot(xn, w_ref[:, 512:1024]), qg_ref[...]).astype(BF16)
    k_ref[...] = head_norm(_dot(xn, w_ref[:, 1024:1536]), kg_ref[...]).astype(BF16)
    vt_ref[...] = _dot(xn, w_ref[:, 1536:2048]).T.astype(BF16)
    for c in range(4):
        lo = 2048 + 512 * c
        z = _dot(xn, w_ref[:, lo:lo + 512]) + bg_ref[:, 512 * c:512 * (c + 1)]
        gate_ref[:, 512 * c:512 * (c + 1)] = jax.nn.sigmoid(z).astype(BF16)


def _inproj(x, g1, w_in, b_gate, qg, kg, ones_bd):
    B, L, D = x.shape
    nl = L // TL_IN
    full = lambda shape: pl.BlockSpec(shape, lambda b, l: (0,) * len(shape))
    return pl.pallas_call(
        _inproj_kernel,
        grid=(B, nl),
        in_specs=[
            pl.BlockSpec((None, TL_IN, D), lambda b, l: (b, l, 0)),
            full((1, D)),
            pl.BlockSpec(w_in.shape, lambda b, l: (0, 0), pipeline_mode=pl.Buffered(1)),
            full((1, 2 * D)),
            full((1, 512)), full((1, 512)), full((256, 256)),
        ],
        out_specs=[
            pl.BlockSpec((None, TL_IN, 512), lambda b, l: (b, l, 0)),
            pl.BlockSpec((None, TL_IN, 512), lambda b, l: (b, l, 0)),
            pl.BlockSpec((None, TL_IN, 512), lambda b, l: (b, l, 0)),
            pl.BlockSpec((None, 512, TL_IN), lambda b, l: (b, 0, l)),
            pl.BlockSpec((None, TL_IN, 2 * D), lambda b, l: (b, l, 0)),
        ],
        out_shape=[
            jax.ShapeDtypeStruct((B, L, 512), F32),
            jax.ShapeDtypeStruct((B, L, 512), BF16),
            jax.ShapeDtypeStruct((B, L, 512), BF16),
            jax.ShapeDtypeStruct((B, 512, L), BF16),
            jax.ShapeDtypeStruct((B, L, 2 * D), BF16),
        ],
        scratch_shapes=[pltpu.VMEM(w_in.shape, BF16)],
        compiler_params=_params(("arbitrary", "arbitrary")),
        name="inproj",
    )(x, g1, w_in, b_gate, qg, kg, ones_bd)


def _s5_kernel(u_hbm, bm_ref, ar_ref, ai_ref, cm_ref, d_ref, wg_ref, bgl_ref,
               y_hbm, ubuf, ybuf, hs_ref, h_ref, usem, ysem):
    i = pl.program_id(0)
    n = pl.num_programs(0)
    slot = lax.rem(i, 2)
    nb = ubuf.shape[2]

    def u_copies(step, s):
        t0 = pl.multiple_of(step * S5_STEPS, S5_STEPS)
        return [pltpu.make_async_copy(u_hbm.at[b, pl.ds(t0, S5_STEPS), :], ubuf.at[s, :, b, :], usem.at[s])
                for b in range(nb)]

    def y_copies(step, s):
        t0 = pl.multiple_of(step * S5_STEPS, S5_STEPS)
        return [pltpu.make_async_copy(ybuf.at[s, :, b, :], y_hbm.at[b, pl.ds(t0, S5_STEPS), :], ysem.at[s])
                for b in range(nb)]

    @pl.when(i == 0)
    def _():
        h_ref[...] = jnp.zeros_like(h_ref)
        for cp in u_copies(0, 0):
            cp.start()

    @pl.when(i + 1 < n)
    def _():
        for b, cp in enumerate(u_copies(i + 1, 1 - slot)):
            cp.start(priority=b % 2)

    for cp in u_copies(i, slot):
        cp.wait()
    uf = ubuf[slot].reshape(S5_STEPS * nb, ubuf.shape[3])
    u = uf.astype(BF16)
    n_part = bm_ref.shape[0]
    cw = u.shape[1] // n_part
    sw = bm_ref.shape[2] // 2
    for pt in range(n_part):
        hs_ref[:, 2 * sw * pt:2 * sw * (pt + 1)] = _dot(u[:, cw * pt:cw * (pt + 1)], bm_ref[pt])

    ys = []
    for pt in range(n_part):
        re0 = 2 * sw * pt
        im0 = re0 + sw
        ar = jnp.broadcast_to(ar_ref[:, sw * pt:sw * (pt + 1)], (V7X_SUBLANES, sw))
        ai = jnp.broadcast_to(ai_ref[:, sw * pt:sw * (pt + 1)], (V7X_SUBLANES, sw))
        hr = h_ref[:, re0:re0 + sw]
        hi = h_ref[:, im0:im0 + sw]
        for t in range(S5_STEPS):
            r0 = t * V7X_SUBLANES
            xr = hs_ref[r0:r0 + V7X_SUBLANES, re0:re0 + sw]
            xi = hs_ref[r0:r0 + V7X_SUBLANES, im0:im0 + sw]
            hr, hi = ar * hr - ai * hi + xr, ar * hi + ai * hr + xi
            hs_ref[r0:r0 + V7X_SUBLANES, re0:re0 + sw] = hr
            hs_ref[r0:r0 + V7X_SUBLANES, im0:im0 + sw] = hi
        h_ref[:, re0:re0 + sw] = hr
        h_ref[:, im0:im0 + sw] = hi
        ys.append(_dot(hs_ref[:, re0:re0 + 2 * sw].astype(BF16), cm_ref[pt]))

    y = jnp.concatenate(ys, axis=1)
    y = y + d_ref[...] * uf
    y = 0.5 * y * (1.0 + jnp.tanh(math.sqrt(2.0 / math.pi) * (y + 0.044715 * (y * y * y))))
    z = _dot(y.astype(BF16), wg_ref[...]) + bgl_ref[...]

    @pl.when(i >= 2)
    def _():
        for cp in y_copies(i - 2, slot):
            cp.wait()

    ybuf[slot] = (y * jax.nn.sigmoid(z)).reshape(S5_STEPS, nb, ybuf.shape[3])
    for b, cp in enumerate(y_copies(i, slot)):
        cp.start(priority=b % 2)

    @pl.when(i == n - 1)
    def _():
        for cp in y_copies(i, slot):
            cp.wait()

    @pl.when(jnp.logical_and(i == n - 1, i >= 1))
    def _():
        for cp in y_copies(i - 1, 1 - slot):
            cp.wait()


def _s5(u, bmat, ar, ai, cmat, d_skip, w_glu_bf, b_glu):
    B, L, C = u.shape
    R = S5_STEPS * B
    full = lambda shape: pl.BlockSpec(shape, lambda i: (0,) * len(shape))
    return pl.pallas_call(
        _s5_kernel,
        grid=(L // S5_STEPS,),
        in_specs=[
            pl.BlockSpec(memory_space=pl.ANY),
            full(bmat.shape), full(ar.shape), full(ai.shape), full(cmat.shape),
            full(d_skip.shape), full(w_glu_bf.shape), full(b_glu.shape),
        ],
        out_specs=pl.BlockSpec(memory_space=pl.ANY),
        out_shape=jax.ShapeDtypeStruct((B, L, C), F32),
        scratch_shapes=[pltpu.VMEM((2, S5_STEPS, B, C), F32), pltpu.VMEM((2, S5_STEPS, B, C), F32),
                        pltpu.VMEM((R, 4096), F32), pltpu.VMEM((B, 4096), F32),
                        pltpu.SemaphoreType.DMA((2,)), pltpu.SemaphoreType.DMA((2,))],
        compiler_params=_params(("arbitrary",)),
        name="s5",
    )(u, bmat, ar, ai, cmat, d_skip, w_glu_bf, b_glu)


def _attn_kernel(q_ref, k_ref, vt_ref, lam_ref, sg_ref, o_ref, qm_ref, acc_ref, ml_ref, s_ref):
    i = pl.program_id(1)
    n_chain = 2 * ATT_HEADS
    lamv = lam_ref[...]
    lam = (jnp.exp(jnp.sum(lamv[0:1] * lamv[1:2], axis=-1, keepdims=True))
           - jnp.exp(jnp.sum(lamv[2:3] * lamv[3:4], axis=-1, keepdims=True)) + LAMBDA_INIT)
    lane = lax.broadcasted_iota(I32, (TQ, 128), 1)
    key_chunk = lax.broadcasted_iota(I32, (TK, TQ), 0) // CHUNK
    qry_chunk = lax.broadcasted_iota(I32, (TK, TQ), 1) // CHUNK
    diag_ok = key_chunk <= qry_chunk

    for hd in range(ATT_HEADS):
        qh = q_ref[:, 128 * hd:128 * (hd + 1)]
        zero = jnp.zeros_like(qh)
        qm_ref[2 * hd] = jnp.where(lane < ATT_SUB_DIM, qh, zero)
        qm_ref[2 * hd + 1] = jnp.where(lane >= ATT_SUB_DIM, qh, zero)
    acc_ref[...] = jnp.zeros_like(acc_ref)
    row = lax.broadcasted_iota(I32, (2 * n_chain, TQ), 0)
    ml_ref[...] = jnp.where(row % 2 == 0, NEG, 0.0)

    def scores(j, par):
        off = pl.multiple_of(j * TK, TK)
        for hd in range(ATT_HEADS):
            ks = k_ref[pl.ds(off, TK), 128 * hd:128 * (hd + 1)]
            for s in range(2):
                s_ref[par, 2 * hd + s] = _dot_nt(ks, qm_ref[2 * hd + s])

    def consume(j, par, masked):
        off = pl.multiple_of(j * TK, TK)
        ps, alphas = [], []
        for c in range(n_chain):
            st = s_ref[par, c]
            if masked:
                st = jnp.where(diag_ok, st, NEG)
            m = ml_ref[2 * c:2 * c + 1, :]
            l = ml_ref[2 * c + 1:2 * c + 2, :]
            mn = jnp.maximum(m, jnp.max(st, axis=0, keepdims=True))
            alpha = jnp.exp2(m - mn)
            p = jnp.exp2(st - mn)
            ml_ref[2 * c:2 * c + 1, :] = mn
            ml_ref[2 * c + 1:2 * c + 2, :] = alpha * l + jnp.sum(p, axis=0, keepdims=True)
            ps.append(p.astype(BF16))
            alphas.append(alpha)
        for c in range(n_chain):
            hd = c // 2
            vts = vt_ref[128 * hd:128 * (hd + 1), pl.ds(off, TK)]
            acc_ref[c] = acc_ref[c] * alphas[c] + _dot(vts, ps[c])

    scores(0, 0)

    def body(jj, carry):
        j = 2 * jj
        scores(j + 1, 1)
        consume(j, 0, False)
        scores(j + 2, 0)
        consume(j + 1, 1, False)
        return carry

    lax.fori_loop(0, i // 2, body, 0)

    @pl.when(i % 2 == 1)
    def _():
        scores(i, 1)
        consume(i - 1, 0, False)
        consume(i, 1, True)

    @pl.when(i % 2 == 0)
    def _():
        consume(i, 0, True)

    for hd in range(ATT_HEADS):
        c0 = 128 * hd
        l1 = ml_ref[4 * hd + 1:4 * hd + 2, :]
        l2 = ml_ref[4 * hd + 3:4 * hd + 4, :]
        o = acc_ref[2 * hd] / l1 - lam * (acc_ref[2 * hd + 1] / l2)
        ms = jnp.mean(o * o, axis=0, keepdims=True)
        on = (o * lax.rsqrt(ms + RMS_EPS)).T
        o_ref[:, c0:c0 + 128] = (on * sg_ref[:, c0:c0 + 128] * (1.0 - LAMBDA_INIT)).astype(BF16)


def _attn(q, k, vt, lam4, sg):
    B, L, W = q.shape
    n_chain = 2 * ATT_HEADS
    return pl.pallas_call(
        _attn_kernel,
        grid=(B, L // TQ),
        in_specs=[
            pl.BlockSpec((None, TQ, W), lambda b, i: (b, i, 0)),
            pl.BlockSpec((None, L, W), lambda b, i: (b, 0, 0)),
            pl.BlockSpec((None, W, L), lambda b, i: (b, 0, 0)),
            pl.BlockSpec((4, ATT_SUB_DIM), lambda b, i: (0, 0)),
            pl.BlockSpec((1, W), lambda b, i: (0, 0)),
        ],
        out_specs=pl.BlockSpec((None, TQ, W), lambda b, i: (b, i, 0)),
        out_shape=jax.ShapeDtypeStruct((B, L, W), BF16),
        scratch_shapes=[pltpu.VMEM((n_chain, TQ, 128), BF16),
                        pltpu.VMEM((n_chain, ATT_V_DIM, TQ), F32),
                        pltpu.VMEM((2 * n_chain, TQ), F32),
                        pltpu.VMEM((2, n_chain, TK, TQ), F32)],
        compiler_params=_params(("arbitrary", "arbitrary")),
        name="diff_attn",
    )(q, k, vt, lam4, sg)


def _mix_kernel(x_ref, ys_ref, ya_ref, gate_ref, wps_ref, wpa_ref, wo_ref, g2_ref, wrt_ref, br_ref,
                sut_ref, lt_ref, h_ref, xs_ref, tok_ref, oct_ref):
    tile = pl.program_id(0) * pl.num_programs(1) + pl.program_id(1)

    D = x_ref.shape[-1]
    tl = x_ref.shape[0]
    gs = gate_ref[:, 0:D].astype(F32)
    ga = gate_ref[:, D:2 * D].astype(F32)
    mixed = gs * _dot(ys_ref[...].astype(BF16), wps_ref[...]) + ga * _dot(ya_ref[...], wpa_ref[...])
    h = x_ref[...] + _dot(mixed.astype(BF16), wo_ref[...])
    h_ref[...] = h
    ms = jnp.mean(h * h, axis=-1, keepdims=True)
    hb = (h * lax.rsqrt(ms + RMS_EPS) * g2_ref[...]).astype(BF16)

    lg = _dot_nt(wrt_ref[...], hb) + br_ref[...]
    eio = lax.broadcasted_iota(I32, (N_EXPERTS, tl), 0)
    vals, idxs = [], []
    for k in range(TOP_K):
        m = jnp.max(lg, axis=0, keepdims=True)
        idx = jnp.min(jnp.where(lg == m, eio, N_EXPERTS), axis=0, keepdims=True)
        vals.append(m)
        idxs.append(idx)
        lg = jnp.where(eio == idx, -jnp.inf, lg)
    ex = [jnp.exp(v - vals[0]) for v in vals]
    den = ex[0] + ex[1] + ex[2] + ex[3]
    wts = [e / den for e in ex]

    lane = lax.broadcasted_iota(I32, (N_EXPERTS, V7X_LANES), 1)

    @pl.when(tile == 0)
    def _():
        oct_ref[...] = jnp.zeros_like(oct_ref)

    slot_rows = [[] for _ in range(TOP_K)]
    for sb in range(SUB):
        t0 = TM * sb
        eio_t = lax.broadcasted_iota(I32, (N_EXPERTS, TM), 0)
        sel_t = [eio_t == idxs[k][:, t0:t0 + TM] for k in range(TOP_K)]
        mtot = jnp.zeros((N_EXPERTS, TM), F32)
        for k in range(TOP_K):
            mtot = mtot + jnp.where(sel_t[k], 1.0, 0.0)
        pre = _dot(mtot.astype(BF16), sut_ref[...])
        cnt = jnp.sum(mtot, axis=1, keepdims=True)
        seg_oct = jnp.floor((cnt + (OCT - 1.0)) * (1.0 / OCT))
        seg_b = jnp.broadcast_to(seg_oct, (N_EXPERTS, V7X_LANES))
        start_oct = _dot(lt_ref[...], seg_b.astype(BF16))
        base = pre + start_oct[:, 0:1] * float(OCT)
        slots_f = [jnp.sum(jnp.where(sel_t[k], base, 0.0), axis=0, keepdims=True) for k in range(TOP_K)]
        slots = [s.astype(I32) for s in slots_f]
        for k in range(TOP_K):
            slot_rows[k].append(slots_f[k])

        hb_t = hb[t0:t0 + TM, :]
        for c in range(SLOTS // V7X_MXU_DIM):
            sio = lax.broadcasted_iota(I32, (V7X_MXU_DIM, TM), 0) + V7X_MXU_DIM * c
            p = jnp.where(sio == slots[0], 1.0, 0.0)
            for k in range(1, TOP_K):
                p = jnp.where(sio == slots[k], 1.0, p)
            r0 = SLOTS * sb + V7X_MXU_DIM * c
            xs_ref[r0:r0 + V7X_MXU_DIM, :] = _dot(p.astype(BF16), hb_t)

        oct_ref[...] = jnp.where(lane == SUB * tile + sb, seg_b, oct_ref[...])

    slot_full = [jnp.concatenate(slot_rows[k], axis=1) for k in range(TOP_K)]
    tok_ref[...] = jnp.concatenate(slot_full + wts, axis=0).T


def _mix(x, ys_tm, ya, gates, wps, wpa, wo, g2, wrt, br, sut, lt):
    B, L, D = x.shape
    nl = L // TL_PROJ
    n_tok = B * L
    full = lambda shape: pl.BlockSpec(shape, lambda b, l: (0,) * len(shape))
    return pl.pallas_call(
        _mix_kernel,
        grid=(B, nl),
        in_specs=[
            pl.BlockSpec((None, TL_PROJ, D), lambda b, l: (b, l, 0)),
            pl.BlockSpec((None, TL_PROJ, 512), lambda b, l: (b, l, 0)),
            pl.BlockSpec((None, TL_PROJ, 512), lambda b, l: (b, l, 0)),
            pl.BlockSpec((None, TL_PROJ, 2 * D), lambda b, l: (b, l, 0)),
            full(wps.shape), full(wpa.shape), full(wo.shape), full((1, D)),
            full(wrt.shape), full(br.shape), full(sut.shape), full(lt.shape),
        ],
        out_specs=[
            pl.BlockSpec((None, TL_PROJ, D), lambda b, l: (b, l, 0)),
            pl.BlockSpec((SUB * SLOTS, D), lambda b, l: (b * nl + l, 0)),
            pl.BlockSpec((TL_PROJ, 2 * TOP_K), lambda b, l: (b * nl + l, 0)),
            full((N_EXPERTS, V7X_LANES)),
        ],
        out_shape=[
            jax.ShapeDtypeStruct((B, L, D), F32),
            jax.ShapeDtypeStruct((B * nl * SUB * SLOTS, D), F32),
            jax.ShapeDtypeStruct((n_tok, 2 * TOP_K), F32),
            jax.ShapeDtypeStruct((N_EXPERTS, V7X_LANES), F32),
        ],
        compiler_params=_params(("arbitrary", "arbitrary")),
        name="mix_router",
    )(x, ys_tm, ya, gates, wps, wpa, wo, g2, wrt, br, sut, lt)


def _moe_tables(oct_tab, n_t, n_blocks):
    E = N_EXPERTS
    O = oct_tab[:, :n_t].astype(I32)
    lstart = jnp.cumsum(O, axis=0) - O
    ecum = jnp.cumsum(O, axis=1) - O
    tot = jnp.sum(O, axis=1)
    nb = (tot + BLK_OCT - 1) // BLK_OCT
    bend = jnp.cumsum(nb)
    bstart = bend - nb
    n_valid = bend[-1]
    bi = jnp.arange(n_blocks, dtype=I32)
    er = jnp.arange(E, dtype=I32)
    tr = jnp.arange(n_t, dtype=I32)
    be = jnp.minimum(jnp.sum((bi[:, None] >= bend[None, :]).astype(I32), axis=1), E - 1)
    last_e = jnp.sum(jnp.where(bi == jnp.maximum(n_valid - 1, 0), be, 0))
    be = jnp.where(bi < n_valid, be, last_e)
    oh_e = be[:, None] == er[None, :]
    pick = lambda tab: jnp.sum(jnp.where(oh_e[:, :, None], tab[None, :, :], 0), axis=1)
    ecum_i, o_i, lst_i = pick(ecum), pick(O), pick(lstart)
    tot_i = jnp.sum(jnp.where(oh_e, tot[None, :], 0), axis=1)
    bst_i = jnp.sum(jnp.where(oh_e, bstart[None, :], 0), axis=1)
    g = (bi - bst_i)[:, None] * BLK_OCT + jnp.arange(BLK_OCT, dtype=I32)[None, :]
    tau = jnp.minimum(jnp.sum((g[:, :, None] >= (ecum_i + o_i)[:, None, :]).astype(I32), axis=2), n_t - 1)
    off = tr[None, :] * SL_OCT + lst_i - ecum_i
    src = jnp.sum(jnp.where(tau[:, :, None] == tr[None, None, :], off[:, None, :], 0), axis=2) + g
    src = jnp.where((g < tot_i[:, None]) & (bi[:, None] < n_valid), src, 0)
    s = jnp.arange(SL_OCT, dtype=I32)
    lend_t = (lstart + O).T
    e_s = jnp.minimum(jnp.sum((s[None, :, None] >= lend_t[:, None, :]).astype(I32), axis=2), E - 1)
    offc = (bstart[:, None] * BLK_OCT + ecum - lstart).T
    csrc = jnp.sum(jnp.where(e_s[:, :, None] == er[None, None, :], offc[:, None, :], 0), axis=2) + s[None, :]
    csrc = jnp.where(s[None, :] < jnp.sum(O, axis=0)[:, None], csrc, 0)
    prev_e = jnp.concatenate([jnp.full((1,), -1, I32), be[:-1]])
    first = ((be != prev_e) & (bi < n_valid)).astype(I32)
    par = (jnp.cumsum(first) - 1) % 2
    later = (er[None, :] > er[:, None]) & (nb[None, :] > 0)
    nxt_of_e = jnp.min(jnp.where(later, er[None, :], E), axis=1)
    nxt_of_e = jnp.where(nxt_of_e < E, nxt_of_e, -1)
    nxt = jnp.sum(jnp.where(oh_e, nxt_of_e[None, :], 0), axis=1)
    rows_i = jnp.clip(tot_i * OCT - (bi - bst_i) * MOE_ROWS, 0, MOE_ROWS)
    var = jnp.zeros_like(bi)
    for vi, m in enumerate(MOE_ROW_VARIANTS):
        var = jnp.where(rows_i <= m, vi, var)
    meta = (be, first, par.astype(I32), nxt.astype(I32), var.astype(I32), n_valid.astype(I32).reshape(1))
    return meta, src, csrc


def _gather_octets(src_ref, n_oct, src_hbm, dst, sem, first=0, priorities=(0, 1)):
    for q in range(first, first + n_oct):
        row = pl.multiple_of(src_ref[0, q] * OCT, OCT)
        pltpu.make_async_copy(src_hbm.at[pl.ds(row, OCT), :], dst.at[pl.ds(OCT * q, OCT), :], sem).start(
            priority=priorities[q % len(priorities)])


def _expert_kernel(be_ref, first_ref, par_ref, nxt_ref, var_ref, nv_ref, src_cur_ref, src_nxt_ref, xs_hbm,
                   wup_hbm, bup_ref, wdn_hbm, bdn_ref, perm_ref, y_ref,
                   xbuf, wup_st, wdn_st, wup_bf, wdn_bf, act_ref, sem, wsem):
    i = pl.program_id(0)
    nv = nv_ref[0]
    slot = lax.rem(i, 2)

    def weight_copies(e, s):
        return (pltpu.make_async_copy(wup_hbm.at[e], wup_st.at[s], wsem.at[0, s]),
                pltpu.make_async_copy(wdn_hbm.at[e], wdn_st.at[s], wsem.at[1, s]))

    def gather_rows(src_ref, variant, s):
        for vi, m in enumerate(MOE_ROW_VARIANTS):
            @pl.when(variant == vi)
            def _():
                _gather_octets(src_ref, m // OCT, xs_hbm, xbuf.at[s], sem.at[s], priorities=(0,))

    @pl.when(i == 0)
    def _():
        gather_rows(src_cur_ref, var_ref[0], 0)
        for cp in weight_copies(be_ref[0], par_ref[0]):
            cp.start()

    @pl.when(i + 1 < nv)
    def _():
        gather_rows(src_nxt_ref, var_ref[jnp.minimum(i + 1, pl.num_programs(0) - 1)], 1 - slot)

    def block(m):
        pltpu.make_async_copy(xs_hbm.at[pl.ds(0, m), :], xbuf.at[slot, pl.ds(0, m), :], sem.at[slot]).wait()
        x = xbuf[slot, 0:m, :].astype(BF16)
        for cb in range(8):
            hp = _dot(x, wup_bf[:, 256 * cb:256 * (cb + 1)]) + bup_ref[:, 256 * cb:256 * (cb + 1)]
            glu = jnp.minimum(hp[:, 0:128], SWIGLU_LIMIT)
            lin = jnp.clip(hp[:, 128:256], -SWIGLU_LIMIT, SWIGLU_LIMIT)
            act = glu * jax.nn.sigmoid(SWIGLU_ALPHA * glu) * (lin + 1.0)
            act_ref[0:m, 128 * cb:128 * (cb + 1)] = act.astype(BF16)
        y_ref[0:m, :] = _dot(act_ref[0:m, :], wdn_bf[...]) + bdn_ref[...]
        if m < MOE_ROWS:
            y_ref[m:MOE_ROWS, :] = jnp.zeros((MOE_ROWS - m, y_ref.shape[1]), F32)

    @pl.when(i < nv)
    def _():
        @pl.when(first_ref[i] == 1)
        def _():
            s = par_ref[i]
            for cp in weight_copies(be_ref[i], s):
                cp.wait()

            @pl.when(nxt_ref[i] >= 0)
            def _():
                for cp in weight_copies(nxt_ref[i], 1 - s):
                    cp.start()

            for cb in range(8):
                blk = wup_st[s, :, 256 * cb:256 * (cb + 1)].astype(BF16)
                wup_bf[:, 256 * cb:256 * (cb + 1)] = _dot(blk, perm_ref[...]).astype(BF16)
            wdn_bf[...] = wdn_st[s].astype(BF16)

        for vi, m in enumerate(MOE_ROW_VARIANTS):
            @pl.when(var_ref[i] == vi)
            def _():
                block(m)

    @pl.when(i >= nv)
    def _():
        y_ref[...] = jnp.zeros_like(y_ref)


def _experts(meta, src3, xs, w_up, b_up_p, w_down, b_down, perm):
    n_blocks = src3.shape[0]
    DE2 = w_up.shape[2]
    D = w_up.shape[1]
    tab = src3.shape[2]
    be_map = lambda i, be, *_: (be[i], 0, 0)
    grid_spec = pltpu.PrefetchScalarGridSpec(
        num_scalar_prefetch=len(meta),
        grid=(n_blocks,),
        in_specs=[
            pl.BlockSpec((None, 1, tab), lambda i, *_: (i, 0, 0), memory_space=pltpu.SMEM),
            pl.BlockSpec((None, 1, tab), lambda i, *_: (jnp.minimum(i + 1, n_blocks - 1), 0, 0),
                         memory_space=pltpu.SMEM),
            pl.BlockSpec(memory_space=pl.ANY),
            pl.BlockSpec(memory_space=pl.ANY),
            pl.BlockSpec((None, 1, DE2), be_map),
            pl.BlockSpec(memory_space=pl.ANY),
            pl.BlockSpec((None, 1, D), be_map),
            pl.BlockSpec((256, 256), lambda i, *_: (0, 0)),
        ],
        out_specs=pl.BlockSpec((MOE_ROWS, D), lambda i, *_: (i, 0)),
        scratch_shapes=[
            pltpu.VMEM((2, MOE_ROWS, D), F32),
            pltpu.VMEM((2, D, DE2), F32),
            pltpu.VMEM((2, DE2 // 2, D), F32),
            pltpu.VMEM((D, DE2), BF16),
            pltpu.VMEM((DE2 // 2, D), BF16),
            pltpu.VMEM((MOE_ROWS, DE2 // 2), BF16),
            pltpu.SemaphoreType.DMA((2,)),
            pltpu.SemaphoreType.DMA((2, 2)),
        ],
    )
    return pl.pallas_call(
        _expert_kernel,
        grid_spec=grid_spec,
        out_shape=jax.ShapeDtypeStruct((n_blocks * MOE_ROWS, D), F32),
        compiler_params=_params(("arbitrary",)),
        name="experts",
    )(*meta, src3, src3, xs, w_up, b_up_p, w_down, b_down, perm)


def _combine_kernel(src_cur_ref, src_nxt_ref, h_ref, tok_ref, ys_hbm, o_ref,
                    ybuf, sl_b, w_b, sem):
    i = pl.program_id(0)
    n = pl.num_programs(0)
    slot = lax.rem(i, 2)

    @pl.when(i == 0)
    def _():
        _gather_octets(src_cur_ref, SUB * SL_OCT, ys_hbm, ybuf.at[0], sem.at[0])

    @pl.when(i + 1 < n)
    def _():
        _gather_octets(src_nxt_ref, SUB * SL_OCT, ys_hbm, ybuf.at[1 - slot], sem.at[1 - slot])

    tl = h_ref.shape[0]
    for k in range(TOP_K):
        sl_b[k] = jnp.broadcast_to(tok_ref[:, k:k + 1], (tl, V7X_LANES))
        w_b[k] = jnp.broadcast_to(tok_ref[:, TOP_K + k:TOP_K + k + 1], (tl, V7X_LANES))
    lane = lax.broadcasted_iota(I32, (tl, V7X_LANES), 1).astype(F32)
    pltpu.make_async_copy(ys_hbm.at[pl.ds(0, SUB * SLOTS), :], ybuf.at[slot], sem.at[slot]).wait()
    lane_t = lax.broadcasted_iota(I32, (TM, V7X_LANES), 1).astype(F32)
    for sb in range(SUB):
        t0 = TM * sb
        acc = h_ref[t0:t0 + TM, :]
        for c in range(SLOTS // V7X_MXU_DIM):
            halves = []
            for hc in range(V7X_MXU_DIM // V7X_LANES):
                sio = lane_t + float(V7X_MXU_DIM * c + V7X_LANES * hc)
                wm = jnp.where(sio == sl_b[0, t0:t0 + TM], w_b[0, t0:t0 + TM], 0.0)
                for k in range(1, TOP_K):
                    wm = jnp.where(sio == sl_b[k, t0:t0 + TM], w_b[k, t0:t0 + TM], wm)
                halves.append(wm.astype(BF16))
            r0 = SLOTS * sb + V7X_MXU_DIM * c
            acc = acc + _dot(jnp.concatenate(halves, axis=1), ybuf[slot, r0:r0 + V7X_MXU_DIM, :].astype(BF16))
        o_ref[t0:t0 + TM, :] = acc


def _combine(csrc3, h2, tok_tab, ys):
    n_tok, D = h2.shape
    n_steps = n_tok // TL_PROJ
    tab = csrc3.shape[2]
    return pl.pallas_call(
        _combine_kernel,
        grid=(n_steps,),
        in_specs=[
            pl.BlockSpec((None, 1, tab), lambda i: (i, 0, 0), memory_space=pltpu.SMEM),
            pl.BlockSpec((None, 1, tab), lambda i: (jnp.minimum(i + 1, n_steps - 1), 0, 0),
                         memory_space=pltpu.SMEM),
            pl.BlockSpec((TL_PROJ, D), lambda i: (i, 0)),
            pl.BlockSpec((TL_PROJ, 2 * TOP_K), lambda i: (i, 0)),
            pl.BlockSpec(memory_space=pl.ANY),
        ],
        out_specs=pl.BlockSpec((TL_PROJ, D), lambda i: (i, 0)),
        out_shape=jax.ShapeDtypeStruct((n_tok, D), F32),
        scratch_shapes=[pltpu.VMEM((2, SUB * SLOTS, D), F32),
                        pltpu.VMEM((TOP_K, TL_PROJ, V7X_LANES), F32),
                        pltpu.VMEM((TOP_K, TL_PROJ, V7X_LANES), F32),
                        pltpu.SemaphoreType.DMA((2,))],
        compiler_params=_params(("arbitrary",)),
        name="combine",
    )(csrc3, csrc3, h2, tok_tab, ys)


def _s5_matrices(a_re, a_im, log_dt, b_re, b_im, c_re, c_im):
    G, P = a_re.shape
    dt = jnp.exp(log_dt.astype(F32))[:, None]
    lr = a_re.astype(F32)
    li = a_im.astype(F32)
    mag = jnp.exp(lr * dt)
    abar_r = mag * jnp.cos(li * dt)
    abar_i = mag * jnp.sin(li * dt)
    den = lr * lr + li * li
    nr = abar_r - 1.0
    ni = abar_i
    coef_r = (nr * lr + ni * li) / den
    coef_i = (ni * lr - nr * li) / den
    br_ = b_re.astype(F32)
    bi_ = b_im.astype(F32)
    bbar_r = coef_r[..., None] * br_ - coef_i[..., None] * bi_
    bbar_i = coef_r[..., None] * bi_ + coef_i[..., None] * br_

    gp = G // S5_PARTS

    def block_diag(t):
        n, g, a, b = t.shape
        rep = (jnp.arange(b)[:, None] == (jnp.arange(g * b) % b)[None, :]).astype(F32)
        tiled = jnp.einsum('xrb,bc->xrc', t.reshape(n, g * a, b), rep, precision=lax.Precision.HIGHEST)
        keep = (jnp.arange(g * a) // a)[:, None] == (jnp.arange(g * b) // b)[None, :]
        return jnp.where(keep[None], tiled, 0.0)

    def in_mat(bb):
        return block_diag(jnp.swapaxes(bb.reshape(S5_PARTS, gp, P, SSM_GROUP), 2, 3))

    def out_mat(cc):
        return block_diag(jnp.swapaxes(cc.reshape(S5_PARTS, gp, SSM_GROUP, P), 2, 3))

    bmat = jnp.concatenate([in_mat(bbar_r), in_mat(bbar_i)], axis=-1).astype(BF16)
    cmat = jnp.concatenate([out_mat(c_re.astype(F32)), -out_mat(c_im.astype(F32))], axis=1).astype(BF16)
    return bmat, abar_r.reshape(1, G * P), abar_i.reshape(1, G * P), cmat


def _pad_lanes(t):
    n, w = t.shape
    wp = ((w + V7X_LANES - 1) // V7X_LANES) * V7X_LANES
    return jnp.pad(t, ((0, 0), (0, wp - w))).reshape(n, 1, wp)


def kernel(x, norm1_g, w_in, b_gate, ssm_a_re, ssm_a_im, ssm_log_dt, ssm_b_re, ssm_b_im, ssm_c_re, ssm_c_im, ssm_d, ssm_w_glu, ssm_b_glu, q_norm_g, k_norm_g, lambda_q1, lambda_k1, lambda_q2, lambda_k2, subln_g, w_proj_ssm, w_proj_att, w_out, norm2_g, w_router, b_router, w_up, b_up, w_down, b_down):
    B, L, D = x.shape
    assert B == V7X_SUBLANES and D == 1024 and L % TL_PROJ == 0 and w_in.shape[0] == 1
    n_tok = B * L
    n_t = n_tok // TM
    assert n_t <= V7X_LANES
    l = 0

    scale = ATT_SUB_DIM ** -0.5
    qg = (jnp.tile(q_norm_g[l].astype(F32), 2 * ATT_HEADS) * (scale * math.log2(math.e))).reshape(1, 512)
    kg = jnp.tile(k_norm_g[l].astype(F32), 2 * ATT_HEADS).reshape(1, 512)
    blk = np.arange(256) // ATT_SUB_DIM
    ones_bd = jnp.asarray(blk[:, None] == blk[None, :], dtype=BF16)
    u_tm, q, k, vt, gates = _inproj(x, norm1_g[l].reshape(1, D), w_in[l],
                                    b_gate[l].reshape(1, 2 * D), qg, kg, ones_bd)

    bmat, ar, ai, cmat = _s5_matrices(ssm_a_re[l], ssm_a_im[l], ssm_log_dt[l], ssm_b_re[l], ssm_b_im[l],
                                      ssm_c_re[l], ssm_c_im[l])
    y_ssm = _s5(u_tm, bmat, ar, ai, cmat, ssm_d[l].reshape(1, 512).astype(F32),
                ssm_w_glu[l].astype(BF16), ssm_b_glu[l].reshape(1, 512).astype(F32))

    lam4 = jnp.stack([lambda_q1[l], lambda_k1[l], lambda_q2[l], lambda_k2[l]]).astype(F32)
    sg = jnp.tile(subln_g[l].astype(F32), ATT_HEADS).reshape(1, 512)
    y_att = _attn(q, k, vt, lam4, sg)

    tpos = np.arange(TM)
    sut = jnp.asarray(tpos[:, None] < tpos[None, :], dtype=BF16)
    epos = np.arange(N_EXPERTS)
    lt = jnp.asarray(epos[None, :] < epos[:, None], dtype=BF16)
    h, xs, tok_tab, oct_tab = _mix(
        x, y_ssm, y_att, gates,
        w_proj_ssm[l].astype(BF16), w_proj_att[l].astype(BF16), w_out[l].astype(BF16),
        norm2_g[l].reshape(1, D), w_router[l].T.astype(BF16), b_router[l].reshape(N_EXPERTS, 1).astype(F32),
        sut, lt)

    n_blocks = (n_tok * TOP_K + n_t * N_EXPERTS * (OCT - 1) + MOE_ROWS - 1) // MOE_ROWS + N_EXPERTS
    meta, src, csrc = _moe_tables(oct_tab, n_t, n_blocks)

    de2 = w_up.shape[-1]
    b_up_p = b_up[l].reshape(N_EXPERTS, de2 // 256, 128, 2).transpose(0, 1, 3, 2).reshape(N_EXPERTS, 1, de2)
    pr = np.arange(256)
    col_src = np.where(pr < 128, 2 * pr, 2 * (pr - 128) + 1)
    perm = jnp.asarray(pr[:, None] == col_src[None, :], dtype=BF16)
    ys = _experts(meta, _pad_lanes(src), xs,
                  w_up[l], b_up_p, w_down[l], b_down[l].reshape(N_EXPERTS, 1, D), perm)

    out = _combine(_pad_lanes(csrc.reshape(n_t // SUB, SUB * SL_OCT)), h.reshape(n_tok, D), tok_tab, ys)
    return out.reshape(B, L, D)
```
